```python
import math
import jax, jax.numpy as jnp
from jax import lax
import numpy as np

D_MODEL = 1024
BATCH = 8
SEQ = 4096
DEPTH = 4

F32 = jnp.float32
GRID_W = 64
CTX_LEN = 256
N_MIXERS = 4
NORM_EPS = 1e-6

HG_HEADS = 8
HG_KEY = D_MODEL // HG_HEADS
HG_VAL = D_MODEL // HG_HEADS
HG_CHUNK = 32

HY_ORDER = 2
HY_SHORT = 3
HY_EMB = 33
HY_FILTER_W = 64
HY_DECAY_TARGET = 1e-2
HY_DECAY_HI_PCT = 0.3
HY_DECAY_LO_PCT = 1.5

MB_INNER = 2 * D_MODEL
MB_HEADDIM = 64
MB_HEADS = MB_INNER // MB_HEADDIM
MB_GROUPS = 8
MB_HPG = MB_HEADS // MB_GROUPS
MB_STATE = 128
MB_CONV = 5
MB_CHUNK = 64
MB_CONV_DIM = MB_INNER + 2 * MB_GROUPS * MB_STATE
MB_IN = MB_INNER + MB_CONV_DIM + 2 * MB_HEADS

GLA_HEADS = 4
GLA_KEY_DIM = D_MODEL // 2
GLA_VAL_DIM = D_MODEL
GLA_HK = GLA_KEY_DIM // GLA_HEADS
GLA_HV = GLA_VAL_DIM // GLA_HEADS
GLA_GATE_RANK = 16
GLA_GATE_NORM = 16.0
GLA_CHUNK = 32
GLA_IN = 2 * GLA_KEY_DIM + 2 * GLA_VAL_DIM + 2 * GLA_GATE_RANK

N_EXPERTS = 16
EC_CAPACITY_FACTOR = 2
EXPERT_FF = 2048

kernel_name = 'hybrid_interleaved_diffusion_trunk'


def rmsnorm(x, g):
    xf = x.astype(F32)
    return xf * lax.rsqrt(jnp.mean(xf * xf, axis=-1, keepdims=True) + NORM_EPS) * g


def adaln(cond, w, b):
    m = jax.nn.silu(cond.astype(F32)) @ w + b
    return jnp.split(m[..., None, :], 6, axis=-1)


def modulate(h, shift, scale):
    return h * (1.0 + scale) + shift


def heads(t, n):
    b, l, _ = t.shape
    return t.reshape(b, l, n, -1).transpose(0, 2, 1, 3)


def dwconv_centered(x, w, b):
    k_w = w.shape[0]
    pad = k_w // 2
    length = x.shape[1]
    xp = jnp.pad(x, ((0, 0), (pad, pad), (0, 0)))
    return sum(xp[:, j:j + length] * w[j] for j in range(k_w)) + b


def raster_to_colmajor(h, rows):
    b, l, d = h.shape
    return h.reshape(b, rows, GRID_W, d).transpose(0, 2, 1, 3).reshape(b, l, d)


def colmajor_to_raster(h, rows):
    b, l, d = h.shape
    return h.reshape(b, GRID_W, rows, d).transpose(0, 2, 1, 3).reshape(b, l, d)


def chunk_gla(q, k, v, g, s0, chunk):
    bsz, nh, length, kd = q.shape
    vd = v.shape[-1]
    n = length // chunk

    def split_chunks(t):
        return jnp.moveaxis(t.astype(F32).reshape(bsz, nh, n, chunk, t.shape[-1]), 2, 0)

    qs, ks, vs, gs = split_chunks(q), split_chunks(k), split_chunks(v), split_chunks(g)
    cum = jnp.cumsum(gs, axis=3)
    lower = jnp.tril(jnp.ones((chunk, chunk), bool))[:, :, None]

    def step(state, inp):
        qc, kc, vc, gc = inp
        rel = jnp.where(lower, gc[:, :, :, None, :] - gc[:, :, None, :, :], -jnp.inf)
        att = jnp.einsum('bhtk,bhsk,bhtsk->bhts', qc, kc, jnp.exp(rel))
        g_end = gc[:, :, -1:, :]
        o = att @ vc + jnp.einsum('bhtk,bhkv->bhtv', qc * jnp.exp(gc), state)
        state = jnp.exp(g_end[:, :, 0, :, None]) * state + jnp.einsum('bhsk,bhsv->bhkv', kc * jnp.exp(g_end - gc), vc)
        return state, o

    state, o = lax.scan(step, s0.astype(F32), (qs, ks, vs, cum))
    return jnp.moveaxis(o, 0, 2).reshape(bsz, nh, length, vd), state


def bidir_gla(lat, ctx, chunk):
    q, kf, kb, v, gf, gb = lat
    qc, kfc, kbc, vc, gfc, gbc = ctx
    s0 = jnp.zeros(q.shape[:2] + (q.shape[-1], v.shape[-1]), F32)
    oc_f, s_f = chunk_gla(qc, kfc, vc, gfc, s0, chunk)
    ox_f, _ = chunk_gla(q, kf, v, gf, s_f, chunk)
    r = lambda t: jnp.flip(t, axis=2)
    oc_b, s_b = chunk_gla(r(qc), r(kbc), r(vc), r(gbc), s0, chunk)
    ox_b, _ = chunk_gla(r(q), r(kb), r(v), r(gb), s_b, chunk)
    return ox_f + r(ox_b), oc_f + r(oc_b)


def gated_head_out(o, og, gain, w_out):
    b, nh, l, vd = o.shape
    o = rmsnorm(o.transpose(0, 2, 1, 3), gain).reshape(b, l, nh * vd)
    return (o * jax.nn.silu(og)) @ w_out


def hgrn_lower_bound(lb_param, layer):
    return jnp.cumsum(jax.nn.softmax(lb_param.astype(F32), axis=0), axis=0)[layer]


def hgrn2_mixer(hx, hc, w_in, lb, onorm_g, w_out):
    def project(h):
        q, ff, fb, inp, og = jnp.split(h @ w_in, 5, axis=-1)
        f_fwd = lb + (1.0 - lb) * jax.nn.sigmoid(ff.astype(F32))
        f_bwd = lb + (1.0 - lb) * jax.nn.sigmoid(fb.astype(F32))
        hd = lambda t: heads(t, HG_HEADS)
        streams = (hd(jax.nn.silu(q) * HG_KEY ** -0.5), hd(1.0 - f_fwd), hd(1.0 - f_bwd), hd(inp),
                   hd(jnp.log(f_fwd)), hd(jnp.log(f_bwd)))
        return streams, og

    lat, ogx = project(hx)
    ctxs, ogc = project(hc)
    ox, oc = bidir_gla(lat, ctxs, HG_CHUNK)
    return gated_head_out(ox, ogx, onorm_g, w_out), gated_head_out(oc, ogc, onorm_g, w_out)


def hyena_pos_features(length):
    t = jnp.linspace(0.0, 1.0, length, dtype=F32)[:, None]
    bands = (HY_EMB - 1) // 2
    f = jnp.linspace(1e-4, bands - 1, bands, dtype=F32)[None, :]
    w = 2.0 * math.pi * jnp.arange(length, dtype=F32)[:, None] / length
    return jnp.concatenate([t, jnp.cos(f * w), -jnp.sin(f * w)], axis=-1), t


def hyena_window(t):
    max_decay = math.log(HY_DECAY_TARGET) / HY_DECAY_HI_PCT
    min_decay = math.log(HY_DECAY_TARGET) / HY_DECAY_LO_PCT
    deltas = jnp.abs(jnp.linspace(min_decay, max_decay, D_MODEL, dtype=F32))
    return jnp.exp(-t * deltas[None, :])


def hyena_filters(length, w1, b1, w2, b2, w3, b3, w4, freq):
    z, t = hyena_pos_features(length)
    a = jnp.sin(freq * (z @ w1 + b1))
    a = jnp.sin(freq * (a @ w2 + b2))
    a = jnp.sin(freq * (a @ w3 + b3))
    h = (a @ w4).astype(F32).reshape(length, HY_ORDER, 2, D_MODEL) * hyena_window(t)[:, None, None, :]
    return h / jnp.sum(jnp.abs(h), axis=(0, 2), keepdims=True)


def twosided_fftconv(u, h_fwd, h_bwd):
    length = u.shape[1]
    k2 = jnp.concatenate([h_fwd, jnp.zeros_like(h_fwd[:1]), h_bwd[:0:-1]], axis=0)
    spec = jnp.fft.rfft(u, n=2 * length, axis=1) * jnp.fft.rfft(k2, axis=0)[None]
    return jnp.fft.irfft(spec, n=2 * length, axis=1)[:, :length]


def hyena_mixer(hx, hc, w_in, b_in, short_w, short_b, f_w1, f_b1, f_w2, f_b2, f_w3, f_b3, f_w4,
                f_freq, f_bias, w_out, b_out):
    def run(h):
        length = h.shape[1]
        u = dwconv_centered(h @ w_in + b_in, short_w, short_b).astype(F32)
        x1, x2, v = jnp.split(u, 3, axis=-1)
        filt = hyena_filters(length, f_w1, f_b1, f_w2, f_b2, f_w3, f_b3, f_w4, f_freq)
        z = v
        for o, gate in enumerate((x1, x2)):
            z = gate * (twosided_fftconv(z, filt[:, o, 0], filt[:, o, 1]) + z * f_bias[o])
        return z @ w_out + b_out

    return run(hx), run(hc)


def chunk_ssd(xs, bm, cm, dt, a, s0, chunk):
    bsz, length = xs.shape[:2]
    n = length // chunk

    def split_chunks(t):
        return jnp.moveaxis(t.astype(F32).reshape((bsz, n, chunk) + t.shape[2:]), 1, 0)

    xcs, bcs, ccs, dts = split_chunks(xs), split_chunks(bm), split_chunks(cm), split_chunks(dt)
    cum = jnp.cumsum(dts * a.astype(F32), axis=2)
    lower = jnp.tril(jnp.ones((chunk, chunk), bool))[:, :, None, None]

    def step(state, inp):
        xc, bc, cc, dtc, lc = inp
        seg = jnp.exp(jnp.where(lower, lc[:, :, None] - lc[:, None, :], -jnp.inf))
        w = jnp.einsum('btgn,bsgn->btsg', cc, bc)[..., None] * seg * dtc[:, None]
        y = jnp.einsum('btsgh,bsghp->btghp', w, xc)
        y = y + jnp.einsum('btgn,bghnp->btghp', cc, state) * jnp.exp(lc)[..., None]
        l_end = lc[:, -1]
        state = jnp.exp(l_end)[..., None, None] * state + jnp.einsum(
            'bsgn,bsgh,bsghp->bghnp', bc, jnp.exp(l_end[:, None] - lc) * dtc, xc)
        return state, y

    state, y = lax.scan(step, s0.astype(F32), (xcs, bcs, ccs, dts, cum))
    return jnp.moveaxis(y, 0, 1).reshape(xs.shape), state


def mamba2_mixer(hx, hc, w_in, conv_w, conv_b, dt_bias, a_log, d_skip, norm_g, w_out):
    a = -jnp.exp(a_log.astype(F32)).reshape(2, MB_GROUPS, MB_HPG)

    def project(h):
        bsz, length, _ = h.shape
        z, xbc, dt_raw = jnp.split(h @ w_in, [MB_INNER, MB_INNER + MB_CONV_DIM], axis=-1)
        xbc = jax.nn.silu(dwconv_centered(xbc, conv_w, conv_b))
        xs, bm, cm = jnp.split(xbc, [MB_INNER, MB_INNER + MB_GROUPS * MB_STATE], axis=-1)
        xs = xs.reshape(bsz, length, MB_GROUPS, MB_HPG, MB_HEADDIM)
        bm = bm.reshape(bsz, length, MB_GROUPS, MB_STATE)
        cm = cm.reshape(bsz, length, MB_GROUPS, MB_STATE)
        dt = jax.nn.softplus(dt_raw.astype(F32).reshape(bsz, length, 2, MB_GROUPS, MB_HPG)
                             + dt_bias.reshape(2, MB_GROUPS, MB_HPG))
        return z, xs, bm, cm, dt

    zx, xx, bx, cx, dtx = project(hx)
    zc, xc, bc, cc, dtc = project(hc)
    s0 = jnp.zeros((hx.shape[0], MB_GROUPS, MB_HPG, MB_STATE, MB_HEADDIM), F32)
    yc_f, s_f = chunk_ssd(xc, bc, cc, dtc[:, :, 0], a[0], s0, MB_CHUNK)
    yx_f, _ = chunk_ssd(xx, bx, cx, dtx[:, :, 0], a[0], s_f, MB_CHUNK)
    r = lambda t: jnp.flip(t, axis=1)
    yc_b, s_b = chunk_ssd(r(xc), r(bc), r(cc), r(dtc[:, :, 1]), a[1], s0, MB_CHUNK)
    yx_b, _ = chunk_ssd(r(xx), r(bx), r(cx), r(dtx[:, :, 1]), a[1], s_b, MB_CHUNK)

    def finish(y, xs, z):
        bsz, length = y.shape[:2]
        y = y + d_skip.reshape(MB_GROUPS, MB_HPG)[..., None] * xs
        y = y.reshape(bsz, length, MB_INNER) * jax.nn.silu(z)
        y = rmsnorm(y.reshape(bsz, length, MB_GROUPS, MB_INNER // MB_GROUPS),
                    norm_g.reshape(MB_GROUPS, MB_INNER // MB_GROUPS))
        return y.reshape(bsz, length, MB_INNER) @ w_out

    return finish(yx_f + r(yx_b), xx, zx), finish(yc_f + r(yc_b), xc, zc)


def gla_mixer(hx, hc, w_in, gk_w2, gk_b, onorm_g, w_out):
    rows = hx.shape[1] // GRID_W
    kd, vd, rk = GLA_KEY_DIM, GLA_VAL_DIM, GLA_GATE_RANK
    cuts = [kd, 2 * kd, 2 * kd + vd, 2 * kd + 2 * vd, 2 * kd + 2 * vd + rk]

    def project(h):
        q, k, v, og, rf, rb = jnp.split(h @ w_in, cuts, axis=-1)
        gf = jax.nn.log_sigmoid((rf @ gk_w2[0] + gk_b[0]).astype(F32)) / GLA_GATE_NORM
        gb = jax.nn.log_sigmoid((rb @ gk_w2[1] + gk_b[1]).astype(F32)) / GLA_GATE_NORM
        hd = lambda t: heads(t, GLA_HEADS)
        kh = hd(k)
        return (hd(q * GLA_HK ** -0.5), kh, kh, hd(v), hd(gf), hd(gb)), og

    lat, ogx = project(raster_to_colmajor(hx, rows))
    ctxs, ogc = project(hc)
    ox, oc = bidir_gla(lat, ctxs, GLA_CHUNK)
    yx = colmajor_to_raster(gated_head_out(ox, ogx, onorm_g, w_out), rows)
    return yx, gated_head_out(oc, ogc, onorm_g, w_out)


def expert_choice_ffn(h, router_w, w_gate, w_up, w_down):
    bsz, t_len, d = h.shape
    cap = max(1, EC_CAPACITY_FACTOR * t_len // N_EXPERTS)
    aff = jax.nn.softmax(h.astype(F32) @ router_w.astype(F32), axis=-1)
    gate, idx = lax.top_k(jnp.swapaxes(aff, 1, 2), cap)
    xg = jax.vmap(lambda hb, ib: hb[ib])(h, idx)
    hid = jax.nn.silu(jnp.einsum('becd,edf->becf', xg, w_gate)) * jnp.einsum('becd,edf->becf', xg, w_up)
    ye = jnp.einsum('becf,efd->becd', hid, w_down) * gate[..., None]
    return jax.vmap(lambda ib, yb: jnp.zeros((t_len, d), yb.dtype).at[ib.reshape(-1)].add(yb.reshape(-1, d)))(idx, ye)


def setup_inputs(seed: int = 0) -> dict:
    key = jax.random.key(seed)
    ks = iter(jax.random.split(key, 64))

    def nrm(shape, scale):
        return scale * jax.random.normal(next(ks), shape, F32)

    def lin(*shape):
        return nrm(shape, shape[-2] ** -0.5)

    def gain(*shape):
        return 1.0 + nrm(shape, 0.05)

    n_a, n_b, n_c, n_d = (len(range(m, DEPTH, N_MIXERS)) for m in range(N_MIXERS))
    d = D_MODEL
    dt0 = jnp.exp(jax.random.uniform(next(ks), (n_c, 2, MB_HEADS), F32, math.log(1e-3), math.log(1e-1)))
    a0 = jax.random.uniform(next(ks), (n_c, 2, MB_HEADS), F32, 1.0, 16.0)
    return dict(
        x=nrm((BATCH, SEQ, d), 1.0),
        c=nrm((BATCH, d), 1.0),
        ctx=nrm((BATCH, CTX_LEN, d), 1.0),
        c_ctx=nrm((d,), 1.0),
        ada_w=nrm((DEPTH, d, 6 * d), 0.5 * d ** -0.5),
        ada_b=nrm((DEPTH, 6 * d), 0.02),
        norm1_g=gain(DEPTH, d),
        norm2_g=gain(DEPTH, d),
        hg_w_in=lin(n_a, d, 5 * d),
        hg_lb=nrm((DEPTH + 1, d), 0.1),
        hg_onorm_g=gain(n_a, HG_VAL),
        hg_w_out=lin(n_a, d, d),
        hy_w_in=lin(n_b, d, 3 * d),
        hy_b_in=nrm((n_b, 3 * d), 0.02),
        hy_short_w=nrm((n_b, HY_SHORT, 3 * d), HY_SHORT ** -0.5),
        hy_short_b=nrm((n_b, 3 * d), 0.02),
        hy_f_w1=lin(n_b, HY_EMB, HY_FILTER_W),
        hy_f_b1=nrm((n_b, HY_FILTER_W), 0.1),
        hy_f_w2=lin(n_b, HY_FILTER_W, HY_FILTER_W),
        hy_f_b2=nrm((n_b, HY_FILTER_W), 0.1),
        hy_f_w3=lin(n_b, HY_FILTER_W, HY_FILTER_W),
        hy_f_b3=nrm((n_b, HY_FILTER_W), 0.1),
        hy_f_w4=lin(n_b, HY_FILTER_W, HY_ORDER * 2 * d),
        hy_f_freq=gain(n_b, HY_FILTER_W),
        hy_f_bias=nrm((n_b, HY_ORDER, d), 0.5),
        hy_w_out=lin(n_b, d, d),
        hy_b_out=nrm((n_b, d), 0.02),
        mb_w_in=lin(n_c, d, MB_IN),
        mb_conv_w=nrm((n_c, MB_CONV, MB_CONV_DIM), MB_CONV ** -0.5),
        mb_conv_b=nrm((n_c, MB_CONV_DIM), 0.02),
        mb_dt_bias=dt0 + jnp.log(-jnp.expm1(-dt0)),
        mb_a_log=jnp.log(a0),
        mb_d=gain(n_c, MB_HEADS),
        mb_norm_g=gain(n_c, MB_INNER),
        mb_w_out=lin(n_c, MB_INNER, d),
        gla_w_in=lin(n_d, d, GLA_IN),
        gla_gk_w2=lin(n_d, 2, GLA_GATE_RANK, GLA_KEY_DIM),
        gla_gk_b=nrm((n_d, 2, GLA_KEY_DIM), 0.1),
        gla_onorm_g=gain(n_d, GLA_HV),
        gla_w_out=lin(n_d, GLA_VAL_DIM, d),
        router_w=lin(DEPTH, d, N_EXPERTS),
        moe_w_gate=lin(DEPTH, N_EXPERTS, d, EXPERT_FF),
        moe_w_up=lin(DEPTH, N_EXPERTS, d, EXPERT_FF),
        moe_w_down=lin(DEPTH, N_EXPERTS, EXPERT_FF, d),
        final_norm_g=gain(d),
    )


def reference(x, c, ctx, c_ctx, ada_w, ada_b, norm1_g, norm2_g,
              hg_w_in, hg_lb, hg_onorm_g, hg_w_out,
              hy_w_in, hy_b_in, hy_short_w, hy_short_b, hy_f_w1, hy_f_b1, hy_f_w2, hy_f_b2,
              hy_f_w3, hy_f_b3, hy_f_w4, hy_f_freq, hy_f_bias, hy_w_out, hy_b_out,
              mb_w_in, mb_conv_w, mb_conv_b, mb_dt_bias, mb_a_log, mb_d, mb_norm_g, mb_w_out,
              gla_w_in, gla_gk_w2, gla_gk_b, gla_onorm_g, gla_w_out,
              router_w, moe_w_gate, moe_w_up, moe_w_down, final_norm_g):
    for i in range(DEPTH):
        last = i == DEPTH - 1
        sh1, sc1, g1, sh2, sc2, g2 = adaln(c, ada_w[i], ada_b[i])
        csh1, csc1, cg1, csh2, csc2, cg2 = adaln(c_ctx, ada_w[i], ada_b[i])
        hx = modulate(rmsnorm(x, norm1_g[i]), sh1, sc1)
        hc = modulate(rmsnorm(ctx, norm1_g[i]), csh1, csc1)
        kind, j = i % N_MIXERS, i // N_MIXERS
        if kind == 0:
            yx, yc = hgrn2_mixer(hx, hc, hg_w_in[j], hgrn_lower_bound(hg_lb, i), hg_onorm_g[j], hg_w_out[j])
        elif kind == 1:
            yx, yc = hyena_mixer(hx, hc, hy_w_in[j], hy_b_in[j], hy_short_w[j], hy_short_b[j],
                                 hy_f_w1[j], hy_f_b1[j], hy_f_w2[j], hy_f_b2[j], hy_f_w3[j], hy_f_b3[j],
                                 hy_f_w4[j], hy_f_freq[j], hy_f_bias[j], hy_w_out[j], hy_b_out[j])
        elif kind == 2:
            yx, yc = mamba2_mixer(hx, hc, mb_w_in[j], mb_conv_w[j], mb_conv_b[j], mb_dt_bias[j],
                                  mb_a_log[j], mb_d[j], mb_norm_g[j], mb_w_out[j])
        else:
            yx, yc = gla_mixer(hx, hc, gla_w_in[j], gla_gk_w2[j], gla_gk_b[j], gla_onorm_g[j], gla_w_out[j])
        x = x + g1 * yx
        x = x + g2 * expert_choice_ffn(modulate(rmsnorm(x, norm2_g[i]), sh2, sc2),
                                       router_w[i], moe_w_gate[i], moe_w_up[i], moe_w_down[i])
        if not last:
            ctx = ctx + cg1 * yc
            ctx = ctx + cg2 * expert_choice_ffn(modulate(rmsnorm(ctx, norm2_g[i]), csh2, csc2),
                                                router_w[i], moe_w_gate[i], moe_w_up[i], moe_w_down[i])
    return rmsnorm(x, final_norm_g)
```

```python
import math
import jax, jax.numpy as jnp
from jax import lax
from jax.experimental import pallas as pl
from jax.experimental.pallas import tpu as pltpu

D_MODEL = 1024
BATCH = 8
SEQ = 4096
DEPTH = 4

F32 = jnp.float32
GRID_W = 64
CTX_LEN = 256
N_MIXERS = 4
NORM_EPS = 1e-6

HG_HEADS = 8
HG_KEY = D_MODEL // HG_HEADS
HG_VAL = D_MODEL // HG_HEADS
HG_CHUNK = 32

HY_ORDER = 2
HY_SHORT = 3
HY_EMB = 33
HY_FILTER_W = 64
HY_DECAY_TARGET = 1e-2
HY_DECAY_HI_PCT = 0.3
HY_DECAY_LO_PCT = 1.5

MB_INNER = 2 * D_MODEL
MB_HEADDIM = 64
MB_HEADS = MB_INNER // MB_HEADDIM
MB_GROUPS = 8
MB_HPG = MB_HEADS // MB_GROUPS
MB_STATE = 128
MB_CONV = 5
MB_CHUNK = 64
MB_CONV_DIM = MB_INNER + 2 * MB_GROUPS * MB_STATE
MB_IN = MB_INNER + MB_CONV_DIM + 2 * MB_HEADS

GLA_HEADS = 4
GLA_KEY_DIM = D_MODEL // 2
GLA_VAL_DIM = D_MODEL
GLA_HK = GLA_KEY_DIM // GLA_HEADS
GLA_HV = GLA_VAL_DIM // GLA_HEADS
GLA_GATE_RANK = 16
GLA_GATE_NORM = 16.0
GLA_CHUNK = 32
GLA_IN = 2 * GLA_KEY_DIM + 2 * GLA_VAL_DIM + 2 * GLA_GATE_RANK

N_EXPERTS = 16
EC_CAPACITY_FACTOR = 2
EXPERT_FF = 2048


def rmsnorm(x, g):
    xf = x.astype(F32)
    return xf * lax.rsqrt(jnp.mean(xf * xf, axis=-1, keepdims=True) + NORM_EPS) * g


def adaln(cond, w, b):
    m = jax.nn.silu(cond.astype(F32)) @ w + b
    return jnp.split(m[..., None, :], 6, axis=-1)


def modulate(h, shift, scale):
    return h * (1.0 + scale) + shift


def heads(t, n):
    b, l, _ = t.shape
    return t.reshape(b, l, n, -1).transpose(0, 2, 1, 3)


def dwconv_centered(x, w, b):
    k_w = w.shape[0]
    pad = k_w // 2
    length = x.shape[1]
    xp = jnp.pad(x, ((0, 0), (pad, pad), (0, 0)))
    return sum(xp[:, j:j + length] * w[j] for j in range(k_w)) + b


def raster_to_colmajor(h, rows):
    b, l, d = h.shape
    return h.reshape(b, rows, GRID_W, d).transpose(0, 2, 1, 3).reshape(b, l, d)


def colmajor_to_raster(h, rows):
    b, l, d = h.shape
    return h.reshape(b, GRID_W, rows, d).transpose(0, 2, 1, 3).reshape(b, l, d)


def chunk_gla(q, k, v, g, s0, chunk):
    bsz, nh, length, kd = q.shape
    vd = v.shape[-1]
    n = length // chunk

    def split_chunks(t):
        return jnp.moveaxis(t.astype(F32).reshape(bsz, nh, n, chunk, t.shape[-1]), 2, 0)

    qs, ks, vs, gs = split_chunks(q), split_chunks(k), split_chunks(v), split_chunks(g)
    cum = jnp.cumsum(gs, axis=3)
    lower = jnp.tril(jnp.ones((chunk, chunk), bool))[:, :, None]

    def step(state, inp):
        qc, kc, vc, gc = inp
        rel = jnp.where(lower, gc[:, :, :, None, :] - gc[:, :, None, :, :], -jnp.inf)
        att = jnp.einsum('bhtk,bhsk,bhtsk->bhts', qc, kc, jnp.exp(rel))
        g_end = gc[:, :, -1:, :]
        o = att @ vc + jnp.einsum('bhtk,bhkv->bhtv', qc * jnp.exp(gc), state)
        state = jnp.exp(g_end[:, :, 0, :, None]) * state + jnp.einsum('bhsk,bhsv->bhkv', kc * jnp.exp(g_end - gc), vc)
        return state, o

    state, o = lax.scan(step, s0.astype(F32), (qs, ks, vs, cum))
    return jnp.moveaxis(o, 0, 2).reshape(bsz, nh, length, vd), state


def bidir_gla(lat, ctx, chunk):
    q, kf, kb, v, gf, gb = lat
    qc, kfc, kbc, vc, gfc, gbc = ctx
    s0 = jnp.zeros(q.shape[:2] + (q.shape[-1], v.shape[-1]), F32)
    oc_f, s_f = chunk_gla(qc, kfc, vc, gfc, s0, chunk)
    ox_f, _ = chunk_gla(q, kf, v, gf, s_f, chunk)
    r = lambda t: jnp.flip(t, axis=2)
    oc_b, s_b = chunk_gla(r(qc), r(kbc), r(vc), r(gbc), s0, chunk)
    ox_b, _ = chunk_gla(r(q), r(kb), r(v), r(gb), s_b, chunk)
    return ox_f + r(ox_b), oc_f + r(oc_b)


def gated_head_out(o, og, gain, w_out):
    b, nh, l, vd = o.shape
    o = rmsnorm(o.transpose(0, 2, 1, 3), gain).reshape(b, l, nh * vd)
    return (o * jax.nn.silu(og)) @ w_out


def hgrn_lower_bound(lb_param, layer):
    return jnp.cumsum(jax.nn.softmax(lb_param.astype(F32), axis=0), axis=0)[layer]


def hgrn2_mixer(hx, hc, w_in, lb, onorm_g, w_out):
    def project(h):
        q, ff, fb, inp, og = jnp.split(h @ w_in, 5, axis=-1)
        f_fwd = lb + (1.0 - lb) * jax.nn.sigmoid(ff.astype(F32))
        f_bwd = lb + (1.0 - lb) * jax.nn.sigmoid(fb.astype(F32))
        hd = lambda t: heads(t, HG_HEADS)
        streams = (hd(jax.nn.silu(q) * HG_KEY ** -0.5), hd(1.0 - f_fwd), hd(1.0 - f_bwd), hd(inp),
                   hd(jnp.log(f_fwd)), hd(jnp.log(f_bwd)))
        return streams, og

    lat, ogx = project(hx)
    ctxs, ogc = project(hc)
    ox, oc = bidir_gla(lat, ctxs, HG_CHUNK)
    return gated_head_out(ox, ogx, onorm_g, w_out), gated_head_out(oc, ogc, onorm_g, w_out)


def hyena_pos_features(length):
    t = jnp.linspace(0.0, 1.0, length, dtype=F32)[:, None]
    bands = (HY_EMB - 1) // 2
    f = jnp.linspace(1e-4, bands - 1, bands, dtype=F32)[None, :]
    w = 2.0 * math.pi * jnp.arange(length, dtype=F32)[:, None] / length
    return jnp.concatenate([t, jnp.cos(f * w), -jnp.sin(f * w)], axis=-1), t


def hyena_window(t):
    max_decay = math.log(HY_DECAY_TARGET) / HY_DECAY_HI_PCT
    min_decay = math.log(HY_DECAY_TARGET) / HY_DECAY_LO_PCT
    deltas = jnp.abs(jnp.linspace(min_decay, max_decay, D_MODEL, dtype=F32))
    return jnp.exp(-t * deltas[None, :])


def hyena_filters(length, w1, b1, w2, b2, w3, b3, w4, freq):
    z, t = hyena_pos_features(length)
    a = jnp.sin(freq * (z @ w1 + b1))
    a = jnp.sin(freq * (a @ w2 + b2))
    a = jnp.sin(freq * (a @ w3 + b3))
    h = (a @ w4).astype(F32).reshape(length, HY_ORDER, 2, D_MODEL) * hyena_window(t)[:, None, None, :]
    return h / jnp.sum(jnp.abs(h), axis=(0, 2), keepdims=True)


def twosided_fftconv(u, h_fwd, h_bwd):
    length = u.shape[1]
    k2 = jnp.concatenate([h_fwd, jnp.zeros_like(h_fwd[:1]), h_bwd[:0:-1]], axis=0)
    spec = jnp.fft.rfft(u, n=2 * length, axis=1) * jnp.fft.rfft(k2, axis=0)[None]
    return jnp.fft.irfft(spec, n=2 * length, axis=1)[:, :length]


def hyena_mixer(hx, hc, w_in, b_in, short_w, short_b, f_w1, f_b1, f_w2, f_b2, f_w3, f_b3, f_w4,
                f_freq, f_bias, w_out, b_out):
    def run(h):
        length = h.shape[1]
        u = dwconv_centered(h @ w_in + b_in, short_w, short_b).astype(F32)
        x1, x2, v = jnp.split(u, 3, axis=-1)
        filt = hyena_filters(length, f_w1, f_b1, f_w2, f_b2, f_w3, f_b3, f_w4, f_freq)
        z = v
        for o, gate in enumerate((x1, x2)):
            z = gate * (twosided_fftconv(z, filt[:, o, 0], filt[:, o, 1]) + z * f_bias[o])
        return z @ w_out + b_out

    return run(hx), run(hc)


def chunk_ssd(xs, bm, cm, dt, a, s0, chunk):
    bsz, length = xs.shape[:2]
    n = length // chunk

    def split_chunks(t):
        return jnp.moveaxis(t.astype(F32).reshape((bsz, n, chunk) + t.shape[2:]), 1, 0)

    xcs, bcs, ccs, dts = split_chunks(xs), split_chunks(bm), split_chunks(cm), split_chunks(dt)
    cum = jnp.cumsum(dts * a.astype(F32), axis=2)
    lower = jnp.tril(jnp.ones((chunk, chunk), bool))[:, :, None, None]

    def step(state, inp):
        xc, bc, cc, dtc, lc = inp
        seg = jnp.exp(jnp.where(lower, lc[:, :, None] - lc[:, None, :], -jnp.inf))
        w = jnp.einsum('btgn,bsgn->btsg', cc, bc)[..., None] * seg * dtc[:, None]
        y = jnp.einsum('btsgh,bsghp->btghp', w, xc)
        y = y + jnp.einsum('btgn,bghnp->btghp', cc, state) * jnp.exp(lc)[..., None]
        l_end = lc[:, -1]
        state = jnp.exp(l_end)[..., None, None] * state + jnp.einsum(
            'bsgn,bsgh,bsghp->bghnp', bc, jnp.exp(l_end[:, None] - lc) * dtc, xc)
        return state, y

    state, y = lax.scan(step, s0.astype(F32), (xcs, bcs, ccs, dts, cum))
    return jnp.moveaxis(y, 0, 1).reshape(xs.shape), state


def mamba2_mixer(hx, hc, w_in, conv_w, conv_b, dt_bias, a_log, d_skip, norm_g, w_out):
    a = -jnp.exp(a_log.astype(F32)).reshape(2, MB_GROUPS, MB_HPG)

    def project(h):
        bsz, length, _ = h.shape
        z, xbc, dt_raw = jnp.split(h @ w_in, [MB_INNER, MB_INNER + MB_CONV_DIM], axis=-1)
        xbc = jax.nn.silu(dwconv_centered(xbc, conv_w, conv_b))
        xs, bm, cm = jnp.split(xbc, [MB_INNER, MB_INNER + MB_GROUPS * MB_STATE], axis=-1)
        xs = xs.reshape(bsz, length, MB_GROUPS, MB_HPG, MB_HEADDIM)
        bm = bm.reshape(bsz, length, MB_GROUPS, MB_STATE)
        cm = cm.reshape(bsz, length, MB_GROUPS, MB_STATE)
        dt = jax.nn.softplus(dt_raw.astype(F32).reshape(bsz, length, 2, MB_GROUPS, MB_HPG)
                             + dt_bias.reshape(2, MB_GROUPS, MB_HPG))
        return z, xs, bm, cm, dt

    zx, xx, bx, cx, dtx = project(hx)
    zc, xc, bc, cc, dtc = project(hc)
    s0 = jnp.zeros((hx.shape[0], MB_GROUPS, MB_HPG, MB_STATE, MB_HEADDIM), F32)
    yc_f, s_f = chunk_ssd(xc, bc, cc, dtc[:, :, 0], a[0], s0, MB_CHUNK)
    yx_f, _ = chunk_ssd(xx, bx, cx, dtx[:, :, 0], a[0], s_f, MB_CHUNK)
    r = lambda t: jnp.flip(t, axis=1)
    yc_b, s_b = chunk_ssd(r(xc), r(bc), r(cc), r(dtc[:, :, 1]), a[1], s0, MB_CHUNK)
    yx_b, _ = chunk_ssd(r(xx), r(bx), r(cx), r(dtx[:, :, 1]), a[1], s_b, MB_CHUNK)

    def finish(y, xs, z):
        bsz, length = y.shape[:2]
        y = y + d_skip.reshape(MB_GROUPS, MB_HPG)[..., None] * xs
        y = y.reshape(bsz, length, MB_INNER) * jax.nn.silu(z)
        y = rmsnorm(y.reshape(bsz, length, MB_GROUPS, MB_INNER // MB_GROUPS),
                    norm_g.reshape(MB_GROUPS, MB_INNER // MB_GROUPS))
        return y.reshape(bsz, length, MB_INNER) @ w_out

    return finish(yx_f + r(yx_b), xx, zx), finish(yc_f + r(yc_b), xc, zc)


def gla_mixer(hx, hc, w_in, gk_w2, gk_b, onorm_g, w_out):
    rows = hx.shape[1] // GRID_W
    kd, vd, rk = GLA_KEY_DIM, GLA_VAL_DIM, GLA_GATE_RANK
    cuts = [kd, 2 * kd, 2 * kd + vd, 2 * kd + 2 * vd, 2 * kd + 2 * vd + rk]

    def project(h):
        q, k, v, og, rf, rb = jnp.split(h @ w_in, cuts, axis=-1)
        gf = jax.nn.log_sigmoid((rf @ gk_w2[0] + gk_b[0]).astype(F32)) / GLA_GATE_NORM
        gb = jax.nn.log_sigmoid((rb @ gk_w2[1] + gk_b[1]).astype(F32)) / GLA_GATE_NORM
        hd = lambda t: heads(t, GLA_HEADS)
        kh = hd(k)
        return (hd(q * GLA_HK ** -0.5), kh, kh, hd(v), hd(gf), hd(gb)), og

    lat, ogx = project(raster_to_colmajor(hx, rows))
    ctxs, ogc = project(hc)
    ox, oc = bidir_gla(lat, ctxs, GLA_CHUNK)
    yx = colmajor_to_raster(gated_head_out(ox, ogx, onorm_g, w_out), rows)
    return yx, gated_head_out(oc, ogc, onorm_g, w_out)


def expert_choice_ffn(h, router_w, w_gate, w_up, w_down):
    bsz, t_len, d = h.shape
    cap = max(1, EC_CAPACITY_FACTOR * t_len // N_EXPERTS)
    aff = jax.nn.softmax(h.astype(F32) @ router_w.astype(F32), axis=-1)
    gate, idx = lax.top_k(jnp.swapaxes(aff, 1, 2), cap)
    xg = jax.vmap(lambda hb, ib: hb[ib])(h, idx)
    hid = jax.nn.silu(jnp.einsum('becd,edf->becf', xg, w_gate)) * jnp.einsum('becd,edf->becf', xg, w_up)
    ye = jnp.einsum('becf,efd->becd', hid, w_down) * gate[..., None]
    return jax.vmap(lambda ib, yb: jnp.zeros((t_len, d), yb.dtype).at[ib.reshape(-1)].add(yb.reshape(-1, d)))(idx, ye)


def _final_norm_body(x_ref, g_ref, o_ref):
    xf = x_ref[...]
    o_ref[...] = xf * lax.rsqrt(jnp.mean(xf * xf, axis=-1, keepdims=True) + NORM_EPS) * g_ref[...]


def final_rmsnorm(x, g):
    b, l, d = x.shape
    rows = b * l
    tm = 512
    out = pl.pallas_call(
        _final_norm_body,
        grid=(rows // tm,),
        in_specs=[pl.BlockSpec((tm, d), lambda i: (i, 0)), pl.BlockSpec((1, d), lambda i: (0, 0))],
        out_specs=pl.BlockSpec((tm, d), lambda i: (i, 0)),
        out_shape=jax.ShapeDtypeStruct((rows, d), F32),
    )(x.reshape(rows, d), g.reshape(1, d))
    return out.reshape(b, l, d)


def kernel(x, c, ctx, c_ctx, ada_w, ada_b, norm1_g, norm2_g,
           hg_w_in, hg_lb, hg_onorm_g, hg_w_out,
           hy_w_in, hy_b_in, hy_short_w, hy_short_b, hy_f_w1, hy_f_b1, hy_f_w2, hy_f_b2,
           hy_f_w3, hy_f_b3, hy_f_w4, hy_f_freq, hy_f_bias, hy_w_out, hy_b_out,
           mb_w_in, mb_conv_w, mb_conv_b, mb_dt_bias, mb_a_log, mb_d, mb_norm_g, mb_w_out,
           gla_w_in, gla_gk_w2, gla_gk_b, gla_onorm_g, gla_w_out,
           router_w, moe_w_gate, moe_w_up, moe_w_down, final_norm_g):
    for i in range(DEPTH):
        last = i == DEPTH - 1
        sh1, sc1, g1, sh2, sc2, g2 = adaln(c, ada_w[i], ada_b[i])
        csh1, csc1, cg1, csh2, csc2, cg2 = adaln(c_ctx, ada_w[i], ada_b[i])
        hx = modulate(rmsnorm(x, norm1_g[i]), sh1, sc1)
        hc = modulate(rmsnorm(ctx, norm1_g[i]), csh1, csc1)
        kind, j = i % N_MIXERS, i // N_MIXERS
        if kind == 0:
            yx, yc = hgrn2_mixer(hx, hc, hg_w_in[j], hgrn_lower_bound(hg_lb, i), hg_onorm_g[j], hg_w_out[j])
        elif kind == 1:
            yx, yc = hyena_mixer(hx, hc, hy_w_in[j], hy_b_in[j], hy_short_w[j], hy_short_b[j],
                                 hy_f_w1[j], hy_f_b1[j], hy_f_w2[j], hy_f_b2[j], hy_f_w3[j], hy_f_b3[j],
                                 hy_f_w4[j], hy_f_freq[j], hy_f_bias[j], hy_w_out[j], hy_b_out[j])
        elif kind == 2:
            yx, yc = mamba2_mixer(hx, hc, mb_w_in[j], mb_conv_w[j], mb_conv_b[j], mb_dt_bias[j],
                                  mb_a_log[j], mb_d[j], mb_norm_g[j], mb_w_out[j])
        else:
            yx, yc = gla_mixer(hx, hc, gla_w_in[j], gla_gk_w2[j], gla_gk_b[j], gla_onorm_g[j], gla_w_out[j])
        x = x + g1 * yx
        x = x + g2 * expert_choice_ffn(modulate(rmsnorm(x, norm2_g[i]), sh2, sc2),
                                       router_w[i], moe_w_gate[i], moe_w_up[i], moe_w_down[i])
        if not last:
            ctx = ctx + cg1 * yc
            ctx = ctx + cg2 * expert_choice_ffn(modulate(rmsnorm(ctx, norm2_g[i]), csh2, csc2),
                                                router_w[i], moe_w_gate[i], moe_w_up[i], moe_w_down[i])
    return final_rmsnorm(x, final_norm_g)
```

```python
import math
import jax, jax.numpy as jnp
from jax import lax
from jax.experimental import pallas as pl
from jax.experimental.pallas import tpu as pltpu

D_MODEL = 1024
BATCH = 8
SEQ = 4096
DEPTH = 4

F32 = jnp.float32
GRID_W = 64
CTX_LEN = 256
N_MIXERS = 4
NORM_EPS = 1e-6

HG_HEADS = 8
HG_KEY = D_MODEL // HG_HEADS
HG_VAL = D_MODEL // HG_HEADS
HG_CHUNK = 32

HY_ORDER = 2
HY_SHORT = 3
HY_EMB = 33
HY_FILTER_W = 64
HY_DECAY_TARGET = 1e-2
HY_DECAY_HI_PCT = 0.3
HY_DECAY_LO_PCT = 1.5

MB_INNER = 2 * D_MODEL
MB_HEADDIM = 64
MB_HEADS = MB_INNER // MB_HEADDIM
MB_GROUPS = 8
MB_HPG = MB_HEADS // MB_GROUPS
MB_STATE = 128
MB_CONV = 5
MB_CHUNK = 64
MB_CONV_DIM = MB_INNER + 2 * MB_GROUPS * MB_STATE
MB_IN = MB_INNER + MB_CONV_DIM + 2 * MB_HEADS

GLA_HEADS = 4
GLA_KEY_DIM = D_MODEL // 2
GLA_VAL_DIM = D_MODEL
GLA_HK = GLA_KEY_DIM // GLA_HEADS
GLA_HV = GLA_VAL_DIM // GLA_HEADS
GLA_GATE_RANK = 16
GLA_GATE_NORM = 16.0
GLA_CHUNK = 32
GLA_IN = 2 * GLA_KEY_DIM + 2 * GLA_VAL_DIM + 2 * GLA_GATE_RANK

N_EXPERTS = 16
EC_CAPACITY_FACTOR = 2
EXPERT_FF = 2048


def rmsnorm(x, g):
    xf = x.astype(F32)
    return xf * lax.rsqrt(jnp.mean(xf * xf, axis=-1, keepdims=True) + NORM_EPS) * g


def adaln(cond, w, b):
    m = jax.nn.silu(cond.astype(F32)) @ w + b
    return jnp.split(m[..., None, :], 6, axis=-1)


def modulate(h, shift, scale):
    return h * (1.0 + scale) + shift


def heads(t, n):
    b, l, _ = t.shape
    return t.reshape(b, l, n, -1).transpose(0, 2, 1, 3)


def dwconv_centered(x, w, b):
    k_w = w.shape[0]
    pad = k_w // 2
    length = x.shape[1]
    xp = jnp.pad(x, ((0, 0), (pad, pad), (0, 0)))
    return sum(xp[:, j:j + length] * w[j] for j in range(k_w)) + b


def raster_to_colmajor(h, rows):
    b, l, d = h.shape
    return h.reshape(b, rows, GRID_W, d).transpose(0, 2, 1, 3).reshape(b, l, d)


def colmajor_to_raster(h, rows):
    b, l, d = h.shape
    return h.reshape(b, GRID_W, rows, d).transpose(0, 2, 1, 3).reshape(b, l, d)


def chunk_gla(q, k, v, g, s0, chunk):
    bsz, nh, length, kd = q.shape
    vd = v.shape[-1]
    n = length // chunk

    def split_chunks(t):
        return jnp.moveaxis(t.astype(F32).reshape(bsz, nh, n, chunk, t.shape[-1]), 2, 0)

    qs, ks, vs, gs = split_chunks(q), split_chunks(k), split_chunks(v), split_chunks(g)
    cum = jnp.cumsum(gs, axis=3)
    lower = jnp.tril(jnp.ones((chunk, chunk), bool))[:, :, None]

    def step(state, inp):
        qc, kc, vc, gc = inp
        rel = jnp.where(lower, gc[:, :, :, None, :] - gc[:, :, None, :, :], -jnp.inf)
        att = jnp.einsum('bhtk,bhsk,bhtsk->bhts', qc, kc, jnp.exp(rel))
        g_end = gc[:, :, -1:, :]
        o = att @ vc + jnp.einsum('bhtk,bhkv->bhtv', qc * jnp.exp(gc), state)
        state = jnp.exp(g_end[:, :, 0, :, None]) * state + jnp.einsum('bhsk,bhsv->bhkv', kc * jnp.exp(g_end - gc), vc)
        return state, o

    state, o = lax.scan(step, s0.astype(F32), (qs, ks, vs, cum))
    return jnp.moveaxis(o, 0, 2).reshape(bsz, nh, length, vd), state


def bidir_gla(lat, ctx, chunk):
    q, kf, kb, v, gf, gb = lat
    qc, kfc, kbc, vc, gfc, gbc = ctx
    s0 = jnp.zeros(q.shape[:2] + (q.shape[-1], v.shape[-1]), F32)
    oc_f, s_f = chunk_gla(qc, kfc, vc, gfc, s0, chunk)
    ox_f, _ = chunk_gla(q, kf, v, gf, s_f, chunk)
    r = lambda t: jnp.flip(t, axis=2)
    oc_b, s_b = chunk_gla(r(qc), r(kbc), r(vc), r(gbc), s0, chunk)
    ox_b, _ = chunk_gla(r(q), r(kb), r(v), r(gb), s_b, chunk)
    return ox_f + r(ox_b), oc_f + r(oc_b)


def gated_head_out(o, og, gain, w_out):
    b, nh, l, vd = o.shape
    o = rmsnorm(o.transpose(0, 2, 1, 3), gain).reshape(b, l, nh * vd)
    return (o * jax.nn.silu(og)) @ w_out


def hgrn_lower_bound(lb_param, layer):
    return jnp.cumsum(jax.nn.softmax(lb_param.astype(F32), axis=0), axis=0)[layer]


def hgrn2_mixer(hx, hc, w_in, lb, onorm_g, w_out):
    def project(h):
        q, ff, fb, inp, og = jnp.split(h @ w_in, 5, axis=-1)
        f_fwd = lb + (1.0 - lb) * jax.nn.sigmoid(ff.astype(F32))
        f_bwd = lb + (1.0 - lb) * jax.nn.sigmoid(fb.astype(F32))
        hd = lambda t: heads(t, HG_HEADS)
        streams = (hd(jax.nn.silu(q) * HG_KEY ** -0.5), hd(1.0 - f_fwd), hd(1.0 - f_bwd), hd(inp),
                   hd(jnp.log(f_fwd)), hd(jnp.log(f_bwd)))
        return streams, og

    lat, ogx = project(hx)
    ctxs, ogc = project(hc)
    ox, oc = bidir_gla(lat, ctxs, HG_CHUNK)
    return gated_head_out(ox, ogx, onorm_g, w_out), gated_head_out(oc, ogc, onorm_g, w_out)


def hyena_pos_features(length):
    t = jnp.linspace(0.0, 1.0, length, dtype=F32)[:, None]
    bands = (HY_EMB - 1) // 2
    f = jnp.linspace(1e-4, bands - 1, bands, dtype=F32)[None, :]
    w = 2.0 * math.pi * jnp.arange(length, dtype=F32)[:, None] / length
    return jnp.concatenate([t, jnp.cos(f * w), -jnp.sin(f * w)], axis=-1), t


def hyena_window(t):
    max_decay = math.log(HY_DECAY_TARGET) / HY_DECAY_HI_PCT
    min_decay = math.log(HY_DECAY_TARGET) / HY_DECAY_LO_PCT
    deltas = jnp.abs(jnp.linspace(min_decay, max_decay, D_MODEL, dtype=F32))
    return jnp.exp(-t * deltas[None, :])


def hyena_filters(length, w1, b1, w2, b2, w3, b3, w4, freq):
    z, t = hyena_pos_features(length)
    a = jnp.sin(freq * (z @ w1 + b1))
    a = jnp.sin(freq * (a @ w2 + b2))
    a = jnp.sin(freq * (a @ w3 + b3))
    h = (a @ w4).astype(F32).reshape(length, HY_ORDER, 2, D_MODEL) * hyena_window(t)[:, None, None, :]
    return h / jnp.sum(jnp.abs(h), axis=(0, 2), keepdims=True)


def twosided_fftconv(u, h_fwd, h_bwd):
    length = u.shape[1]
    k2 = jnp.concatenate([h_fwd, jnp.zeros_like(h_fwd[:1]), h_bwd[:0:-1]], axis=0)
    spec = jnp.fft.rfft(u, n=2 * length, axis=1) * jnp.fft.rfft(k2, axis=0)[None]
    return jnp.fft.irfft(spec, n=2 * length, axis=1)[:, :length]


def hyena_mixer(hx, hc, w_in, b_in, short_w, short_b, f_w1, f_b1, f_w2, f_b2, f_w3, f_b3, f_w4,
                f_freq, f_bias, w_out, b_out):
    def run(h):
        length = h.shape[1]
        u = dwconv_centered(h @ w_in + b_in, short_w, short_b).astype(F32)
        x1, x2, v = jnp.split(u, 3, axis=-1)
        filt = hyena_filters(length, f_w1, f_b1, f_w2, f_b2, f_w3, f_b3, f_w4, f_freq)
        z = v
        for o, gate in enumerate((x1, x2)):
            z = gate * (twosided_fftconv(z, filt[:, o, 0], filt[:, o, 1]) + z * f_bias[o])
        return z @ w_out + b_out

    return run(hx), run(hc)


def chunk_ssd(xs, bm, cm, dt, a, s0, chunk):
    bsz, length = xs.shape[:2]
    n = length // chunk

    def split_chunks(t):
        return jnp.moveaxis(t.astype(F32).reshape((bsz, n, chunk) + t.shape[2:]), 1, 0)

    xcs, bcs, ccs, dts = split_chunks(xs), split_chunks(bm), split_chunks(cm), split_chunks(dt)
    cum = jnp.cumsum(dts * a.astype(F32), axis=2)
    lower = jnp.tril(jnp.ones((chunk, chunk), bool))[:, :, None, None]

    def step(state, inp):
        xc, bc, cc, dtc, lc = inp
        seg = jnp.exp(jnp.where(lower, lc[:, :, None] - lc[:, None, :], -jnp.inf))
        w = jnp.einsum('btgn,bsgn->btsg', cc, bc)[..., None] * seg * dtc[:, None]
        y = jnp.einsum('btsgh,bsghp->btghp', w, xc)
        y = y + jnp.einsum('btgn,bghnp->btghp', cc, state) * jnp.exp(lc)[..., None]
        l_end = lc[:, -1]
        state = jnp.exp(l_end)[..., None, None] * state + jnp.einsum(
            'bsgn,bsgh,bsghp->bghnp', bc, jnp.exp(l_end[:, None] - lc) * dtc, xc)
        return state, y

    state, y = lax.scan(step, s0.astype(F32), (xcs, bcs, ccs, dts, cum))
    return jnp.moveaxis(y, 0, 1).reshape(xs.shape), state


def mamba2_mixer(hx, hc, w_in, conv_w, conv_b, dt_bias, a_log, d_skip, norm_g, w_out):
    a = -jnp.exp(a_log.astype(F32)).reshape(2, MB_GROUPS, MB_HPG)

    def project(h):
        bsz, length, _ = h.shape
        z, xbc, dt_raw = jnp.split(h @ w_in, [MB_INNER, MB_INNER + MB_CONV_DIM], axis=-1)
        xbc = jax.nn.silu(dwconv_centered(xbc, conv_w, conv_b))
        xs, bm, cm = jnp.split(xbc, [MB_INNER, MB_INNER + MB_GROUPS * MB_STATE], axis=-1)
        xs = xs.reshape(bsz, length, MB_GROUPS, MB_HPG, MB_HEADDIM)
        bm = bm.reshape(bsz, length, MB_GROUPS, MB_STATE)
        cm = cm.reshape(bsz, length, MB_GROUPS, MB_STATE)
        dt = jax.nn.softplus(dt_raw.astype(F32).reshape(bsz, length, 2, MB_GROUPS, MB_HPG)
                             + dt_bias.reshape(2, MB_GROUPS, MB_HPG))
        return z, xs, bm, cm, dt

    zx, xx, bx, cx, dtx = project(hx)
    zc, xc, bc, cc, dtc = project(hc)
    s0 = jnp.zeros((hx.shape[0], MB_GROUPS, MB_HPG, MB_STATE, MB_HEADDIM), F32)
    yc_f, s_f = chunk_ssd(xc, bc, cc, dtc[:, :, 0], a[0], s0, MB_CHUNK)
    yx_f, _ = chunk_ssd(xx, bx, cx, dtx[:, :, 0], a[0], s_f, MB_CHUNK)
    r = lambda t: jnp.flip(t, axis=1)
    yc_b, s_b = chunk_ssd(r(xc), r(bc), r(cc), r(dtc[:, :, 1]), a[1], s0, MB_CHUNK)
    yx_b, _ = chunk_ssd(r(xx), r(bx), r(cx), r(dtx[:, :, 1]), a[1], s_b, MB_CHUNK)

    def finish(y, xs, z):
        bsz, length = y.shape[:2]
        y = y + d_skip.reshape(MB_GROUPS, MB_HPG)[..., None] * xs
        y = y.reshape(bsz, length, MB_INNER) * jax.nn.silu(z)
        y = rmsnorm(y.reshape(bsz, length, MB_GROUPS, MB_INNER // MB_GROUPS),
                    norm_g.reshape(MB_GROUPS, MB_INNER // MB_GROUPS))
        return y.reshape(bsz, length, MB_INNER) @ w_out

    return finish(yx_f + r(yx_b), xx, zx), finish(yc_f + r(yc_b), xc, zc)


def gla_mixer(hx, hc, w_in, gk_w2, gk_b, onorm_g, w_out):
    rows = hx.shape[1] // GRID_W
    kd, vd, rk = GLA_KEY_DIM, GLA_VAL_DIM, GLA_GATE_RANK
    cuts = [kd, 2 * kd, 2 * kd + vd, 2 * kd + 2 * vd, 2 * kd + 2 * vd + rk]

    def project(h):
        q, k, v, og, rf, rb = jnp.split(h @ w_in, cuts, axis=-1)
        gf = jax.nn.log_sigmoid((rf @ gk_w2[0] + gk_b[0]).astype(F32)) / GLA_GATE_NORM
        gb = jax.nn.log_sigmoid((rb @ gk_w2[1] + gk_b[1]).astype(F32)) / GLA_GATE_NORM
        hd = lambda t: heads(t, GLA_HEADS)
        kh = hd(k)
        return (hd(q * GLA_HK ** -0.5), kh, kh, hd(v), hd(gf), hd(gb)), og

    lat, ogx = project(raster_to_colmajor(hx, rows))
    ctxs, ogc = project(hc)
    ox, oc = bidir_gla(lat, ctxs, GLA_CHUNK)
    yx = colmajor_to_raster(gated_head_out(ox, ogx, onorm_g, w_out), rows)
    return yx, gated_head_out(oc, ogc, onorm_g, w_out)


def expert_choice_ffn(h, router_w, w_gate, w_up, w_down):
    bsz, t_len, d = h.shape
    cap = max(1, EC_CAPACITY_FACTOR * t_len // N_EXPERTS)
    aff = jax.nn.softmax(h.astype(F32) @ router_w.astype(F32), axis=-1)
    gate, idx = lax.top_k(jnp.swapaxes(aff, 1, 2), cap)
    xg = jax.vmap(lambda hb, ib: hb[ib])(h, idx)
    hid = jax.nn.silu(jnp.einsum('becd,edf->becf', xg, w_gate)) * jnp.einsum('becd,edf->becf', xg, w_up)
    ye = jnp.einsum('becf,efd->becd', hid, w_down) * gate[..., None]
    return jax.vmap(lambda ib, yb: jnp.zeros((t_len, d), yb.dtype).at[ib.reshape(-1)].add(yb.reshape(-1, d)))(idx, ye)


BF16 = jnp.bfloat16
LANES = 128
VMEM_LIMIT_BYTES = 56 * 2 ** 20
SCAN_CHUNK = 64
SCAN_FAST_RANGE = 80.0
SCAN_UNROLL = 4


def _cumsum_rows(tri_bf16, g):
    hi = g.astype(BF16)
    r1 = g - hi.astype(F32)
    mid = r1.astype(BF16)
    lo = (r1 - mid.astype(F32)).astype(BF16)
    d = lambda a: jnp.dot(tri_bf16, a, preferred_element_type=F32)
    return d(hi) + d(mid) + d(lo)


def _log_sigmoid(z):
    return jnp.minimum(z, 0.0) - jnp.log1p(jnp.exp(-jnp.abs(z)))


def _dot_nt(a, b):
    return lax.dot_general(a, b, (((1,), (1,)), ((), ())), preferred_element_type=F32)


def _dot_tn(a, b):
    return lax.dot_general(a, b, (((0,), (0,)), ((), ())), preferred_element_type=F32)


def _gla_scan_body(mode, seq, ctx_len, kdim, vdim, *refs):
    chunk = SCAN_CHUNK
    if mode == 'hgrn2':
        (q_x, ff_x, fb_x, v_x, og_x, q_c, ff_c, fb_c, v_c, og_c, lb_ref, gain_ref,
         o_x, o_c, cf_s, cb_s, sf_s, sb_s, kt_s, gt_s, qf_s, qb_s, of_s, ob_s, kf_s, kb_s) = refs
    else:
        (q_x, k_x, v_x, og_x, r_x, q_c, k_c, v_c, og_c, r_c, w2f_ref, w2b_ref, bf_ref, bb_ref, gain_ref,
         o_x, o_c, cf_s, cb_s, sf_s, sb_s, kt_s, gt_s, qf_s, qb_s, of_s, ob_s) = refs
    if mode == 'hgrn2':
        segs = (((ff_c, fb_c), q_c, None, v_c, og_c, o_c, ctx_len, 0),
                ((ff_x, fb_x), q_x, None, v_x, og_x, o_x, seq, ctx_len))
    else:
        segs = (((r_c,), q_c, k_c, v_c, og_c, o_c, ctx_len, 0),
                ((r_x,), q_x, k_x, v_x, og_x, o_x, seq, ctx_len))

    row = lax.broadcasted_iota(jnp.int32, (chunk, chunk), 0)
    col = lax.broadcasted_iota(jnp.int32, (chunk, chunk), 1)
    masks = (row >= col, row <= col)
    tris = tuple(m.astype(BF16) for m in masks)
    cum_s, state_s, qin_s, kout_s = (cf_s, cb_s), (sf_s, sb_s), (qf_s, qb_s), (of_s, ob_s)
    end_row = (chunk - 1, 0)

    def rows_of(i, off=0):
        return pl.ds(pl.multiple_of(off + i * chunk, chunk), chunk)

    def gate_pass(seg, bound):
        gsrc, _, _, _, _, _, length, off = seg

        def body(i, bound):
            rows, srows = rows_of(i), rows_of(i, off)
            if mode == 'hgrn2':
                lb = lb_ref[...]
                f_f = lb + (1.0 - lb) * jax.nn.sigmoid(gsrc[0][rows, :])
                f_b = lb + (1.0 - lb) * jax.nn.sigmoid(gsrc[1][rows, :])
                g_f, g_b = jnp.log(f_f), jnp.log(f_b)
                kf_s[srows, :] = 1.0 - f_f
                kb_s[srows, :] = 1.0 - f_b
            else:
                r = gsrc[0][rows, :].astype(BF16)
                z_f = jnp.dot(r, w2f_ref[...].astype(BF16), preferred_element_type=F32) + bf_ref[...]
                z_b = jnp.dot(r, w2b_ref[...].astype(BF16), preferred_element_type=F32) + bb_ref[...]
                g_f = _log_sigmoid(z_f) / GLA_GATE_NORM
                g_b = _log_sigmoid(z_b) / GLA_GATE_NORM
            c_f = _cumsum_rows(tris[0], g_f)
            c_b = _cumsum_rows(tris[1], g_b)
            cf_s[srows, :] = c_f
            cb_s[srows, :] = c_b
            tot = jnp.maximum(-c_f[end_row[0]:end_row[0] + 1, :], -c_b[end_row[1]:end_row[1] + 1, :])
            return jnp.maximum(bound, tot)

        return lax.fori_loop(0, length // chunk, body, bound, unroll=SCAN_UNROLL)

    def intra(seg, direction, i, exact):
        _, q_r, k_r, v_r, _, _, _, off = seg
        rows, srows = rows_of(i), rows_of(i, off)
        if mode == 'hgrn2':
            qq = jax.nn.silu(q_r[rows, :]) * HG_KEY ** -0.5
            k = (kf_s if direction == 0 else kb_s)[srows, :]
        else:
            qq = q_r[rows, :] * GLA_HK ** -0.5
            k = k_r[rows, :]
        cum = cum_s[direction][srows, :]
        e = end_row[direction]
        cum_end = cum[e:e + 1, :]
        q_in = (qq * jnp.exp(cum)).astype(BF16)
        qin_s[direction][srows, :] = q_in
        kout_s[direction][srows, :] = (k * jnp.exp(cum_end - cum)).astype(BF16)
        if exact:
            kt_s[...] = k
            gt_s[...] = cum

            def col_body(s, att):
                ks = kt_s[pl.ds(s, 1), :]
                gs = gt_s[pl.ds(s, 1), :]
                p = qq * ks * jnp.exp(jnp.minimum(cum - gs, 0.0))
                return att + jnp.where(col == s, jnp.sum(p, axis=-1, keepdims=True), 0.0)

            att = lax.fori_loop(0, chunk, col_body, jnp.zeros((chunk, chunk), F32))
        else:
            att = _dot_nt(q_in, (k * jnp.exp(-cum)).astype(BF16))
        att = jnp.where(masks[direction], att, 0.0).astype(BF16)
        return jnp.dot(att, v_r[rows, :].astype(BF16), preferred_element_type=F32)

    def intra_pass(exact):
        for seg in segs:
            o_r, length = seg[5], seg[6]

            def body(i, carry):
                o_r[rows_of(i), :] = intra(seg, 0, i, exact) + intra(seg, 1, i, exact)
                return carry

            lax.fori_loop(0, length // chunk, body, 0, unroll=1 if exact else SCAN_UNROLL)

    def inter(seg, direction, i):
        v_r, off = seg[3], seg[7]
        rows, srows = rows_of(i), rows_of(i, off)
        s_ref = state_s[direction]
        state = s_ref[...]
        o = _dot_nt(qin_s[direction][srows, :], state.astype(BF16))
        e = pl.ds(pl.multiple_of(off + i * chunk, chunk) + end_row[direction], 1)
        decay = jnp.exp(cum_s[direction][e, :])
        s_ref[...] = state * decay + _dot_tn(v_r[rows, :].astype(BF16), kout_s[direction][srows, :])
        return o

    def finish(seg, i, o):
        o = o * lax.rsqrt(jnp.mean(o * o, axis=-1, keepdims=True) + NORM_EPS) * gain_ref[...]
        return o * jax.nn.silu(seg[4][rows_of(i), :])

    def inter_pass():
        sf_s[...] = jnp.zeros_like(sf_s)
        sb_s[...] = jnp.zeros_like(sb_s)
        for seg in segs:
            o_r, length = seg[5], seg[6]
            n = length // chunk

            def first_half(i, carry):
                j = n - 1 - i
                o_r[rows_of(i), :] += inter(seg, 0, i)
                o_r[rows_of(j), :] += inter(seg, 1, j)
                return carry

            def second_half(i, carry):
                j = n - 1 - i
                o_r[rows_of(i), :] = finish(seg, i, o_r[rows_of(i), :] + inter(seg, 0, i))
                o_r[rows_of(j), :] = finish(seg, j, o_r[rows_of(j), :] + inter(seg, 1, j))
                return carry

            lax.fori_loop(0, n // 2, first_half, 0, unroll=min(SCAN_UNROLL, n // 2))
            lax.fori_loop(n // 2, n, second_half, 0, unroll=min(SCAN_UNROLL, n // 2))

    bound = jnp.zeros((1, kdim), F32)
    for seg in segs:
        bound = gate_pass(seg, bound)
    fast = jnp.max(bound) <= SCAN_FAST_RANGE

    @pl.when(fast)
    def _():
        intra_pass(False)

    @pl.when(jnp.logical_not(fast))
    def _():
        intra_pass(True)

    inter_pass()


def _gla_scan_call(mode, nheads, kdim, vdim, lat_in, ctx_in, small_in, bsz, seq, ctx_len):
    def stream_spec(length, width, first, stride):
        return pl.BlockSpec((None, length, width), lambda b, h: (b, 0, first + stride * h))

    in_specs = [stream_spec(seq, w, f, s) for _, w, f, s in lat_in]
    in_specs += [stream_spec(ctx_len, w, f, s) for _, w, f, s in ctx_in]
    in_specs += [pl.BlockSpec(blk, imap) for _, blk, imap in small_in]
    args = [a for a, _, _, _ in lat_in] + [a for a, _, _, _ in ctx_in] + [a for a, _, _ in small_in]
    tot = seq + ctx_len
    scratch = [pltpu.VMEM((tot, kdim), F32), pltpu.VMEM((tot, kdim), F32),
               pltpu.VMEM((vdim, kdim), F32), pltpu.VMEM((vdim, kdim), F32),
               pltpu.VMEM((SCAN_CHUNK, kdim), F32), pltpu.VMEM((SCAN_CHUNK, kdim), F32),
               pltpu.VMEM((tot, kdim), BF16), pltpu.VMEM((tot, kdim), BF16),
               pltpu.VMEM((tot, kdim), BF16), pltpu.VMEM((tot, kdim), BF16)]
    if mode == 'hgrn2':
        scratch += [pltpu.VMEM((tot, kdim), F32), pltpu.VMEM((tot, kdim), F32)]
    return pl.pallas_call(
        lambda *refs: _gla_scan_body(mode, seq, ctx_len, kdim, vdim, *refs),
        grid=(bsz, nheads),
        in_specs=in_specs,
        out_specs=[pl.BlockSpec((None, seq, vdim), lambda b, h: (b, 0, h)),
                   pl.BlockSpec((None, ctx_len, vdim), lambda b, h: (b, 0, h))],
        out_shape=[jax.ShapeDtypeStruct((bsz, seq, nheads * vdim), F32),
                   jax.ShapeDtypeStruct((bsz, ctx_len, nheads * vdim), F32)],
        scratch_shapes=scratch,
        compiler_params=pltpu.CompilerParams(dimension_semantics=("arbitrary", "arbitrary"),
                                             vmem_limit_bytes=VMEM_LIMIT_BYTES),
        name=f"{mode}_scan",
    )(*args)


def hgrn2_mixer_pallas(hx, hc, w_in, lb, onorm_g, w_out):
    bsz, seq, _ = hx.shape
    ctx_len = hc.shape[1]
    yx, yc = hx @ w_in, hc @ w_in
    nh = HG_HEADS
    streams = lambda y: [(y, HG_KEY, j * nh, 1) for j in range(5)]
    small = [(lb.reshape(1, D_MODEL), (1, HG_KEY), lambda b, h: (0, h)),
             (onorm_g.reshape(1, HG_VAL), (1, HG_VAL), lambda b, h: (0, 0))]
    ox, oc = _gla_scan_call('hgrn2', nh, HG_KEY, HG_VAL, streams(yx), streams(yc), small, bsz, seq, ctx_len)
    return ox @ w_out, oc @ w_out


def gla_mixer_pallas(hx, hc, w_in, gk_w2, gk_b, onorm_g, w_out):
    bsz, seq, _ = hx.shape
    ctx_len = hc.shape[1]
    rows = seq // GRID_W
    kd, vd, rk = GLA_KEY_DIM, GLA_VAL_DIM, GLA_GATE_RANK
    nh = GLA_HEADS
    w_in_p = jnp.pad(w_in, ((0, 0), (0, LANES - 2 * rk)))
    yx = raster_to_colmajor(hx, rows) @ w_in_p
    yc = hc @ w_in_p
    w2f = jnp.pad(gk_w2[0], ((0, LANES - rk), (0, 0)))
    w2b = jnp.pad(gk_w2[1], ((rk, LANES - 2 * rk), (0, 0)))
    streams = lambda y: [(y, GLA_HK, 0, 1), (y, GLA_HK, kd // GLA_HK, 1), (y, GLA_HV, 2 * kd // GLA_HV, 1),
                         (y, GLA_HV, (2 * kd + vd) // GLA_HV, 1), (y, LANES, (2 * kd + 2 * vd) // LANES, 0)]
    small = [(w2f, (LANES, GLA_HK), lambda b, h: (0, h)), (w2b, (LANES, GLA_HK), lambda b, h: (0, h)),
             (gk_b[0].reshape(1, kd), (1, GLA_HK), lambda b, h: (0, h)),
             (gk_b[1].reshape(1, kd), (1, GLA_HK), lambda b, h: (0, h)),
             (onorm_g.reshape(1, GLA_HV), (1, GLA_HV), lambda b, h: (0, 0))]
    ox, oc = _gla_scan_call('gla', nh, GLA_HK, GLA_HV, streams(yx), streams(yc), small, bsz, seq, ctx_len)
    return colmajor_to_raster(ox @ w_out, rows), oc @ w_out


def _final_norm_body(x_ref, g_ref, o_ref):
    xf = x_ref[...]
    o_ref[...] = xf * lax.rsqrt(jnp.mean(xf * xf, axis=-1, keepdims=True) + NORM_EPS) * g_ref[...]


def final_rmsnorm(x, g):
    b, l, d = x.shape
    rows = b * l
    tm = 512
    out = pl.pallas_call(
        _final_norm_body,
        grid=(rows // tm,),
        in_specs=[pl.BlockSpec((tm, d), lambda i: (i, 0)), pl.BlockSpec((1, d), lambda i: (0, 0))],
        out_specs=pl.BlockSpec((tm, d), lambda i: (i, 0)),
        out_shape=jax.ShapeDtypeStruct((rows, d), F32),
    )(x.reshape(rows, d), g.reshape(1, d))
    return out.reshape(b, l, d)


def kernel(x, c, ctx, c_ctx, ada_w, ada_b, norm1_g, norm2_g,
           hg_w_in, hg_lb, hg_onorm_g, hg_w_out,
           hy_w_in, hy_b_in, hy_short_w, hy_short_b, hy_f_w1, hy_f_b1, hy_f_w2, hy_f_b2,
           hy_f_w3, hy_f_b3, hy_f_w4, hy_f_freq, hy_f_bias, hy_w_out, hy_b_out,
           mb_w_in, mb_conv_w, mb_conv_b, mb_dt_bias, mb_a_log, mb_d, mb_norm_g, mb_w_out,
           gla_w_in, gla_gk_w2, gla_gk_b, gla_onorm_g, gla_w_out,
           router_w, moe_w_gate, moe_w_up, moe_w_down, final_norm_g):
    for i in range(DEPTH):
        last = i == DEPTH - 1
        sh1, sc1, g1, sh2, sc2, g2 = adaln(c, ada_w[i], ada_b[i])
        csh1, csc1, cg1, csh2, csc2, cg2 = adaln(c_ctx, ada_w[i], ada_b[i])
        hx = modulate(rmsnorm(x, norm1_g[i]), sh1, sc1)
        hc = modulate(rmsnorm(ctx, norm1_g[i]), csh1, csc1)
        kind, j = i % N_MIXERS, i // N_MIXERS
        if kind == 0:
            yx, yc = hgrn2_mixer_pallas(hx, hc, hg_w_in[j], hgrn_lower_bound(hg_lb, i), hg_onorm_g[j], hg_w_out[j])
        elif kind == 1:
            yx, yc = hyena_mixer(hx, hc, hy_w_in[j], hy_b_in[j], hy_short_w[j], hy_short_b[j],
                                 hy_f_w1[j], hy_f_b1[j], hy_f_w2[j], hy_f_b2[j], hy_f_w3[j], hy_f_b3[j],
                                 hy_f_w4[j], hy_f_freq[j], hy_f_bias[j], hy_w_out[j], hy_b_out[j])
        elif kind == 2:
            yx, yc = mamba2_mixer(hx, hc, mb_w_in[j], mb_conv_w[j], mb_conv_b[j], mb_dt_bias[j],
                                  mb_a_log[j], mb_d[j], mb_norm_g[j], mb_w_out[j])
        else:
            yx, yc = gla_mixer_pallas(hx, hc, gla_w_in[j], gla_gk_w2[j], gla_gk_b[j], gla_onorm_g[j], gla_w_out[j])
        x = x + g1 * yx
        x = x + g2 * expert_choice_ffn(modulate(rmsnorm(x, norm2_g[i]), sh2, sc2),
                                       router_w[i], moe_w_gate[i], moe_w_up[i], moe_w_down[i])
        if not last:
            ctx = ctx + cg1 * yc
            ctx = ctx + cg2 * expert_choice_ffn(modulate(rmsnorm(ctx, norm2_g[i]), csh2, csc2),
                                                router_w[i], moe_w_gate[i], moe_w_up[i], moe_w_down[i])
    return final_rmsnorm(x, final_norm_g)
```

```python
import math
import jax, jax.numpy as jnp
from jax import lax
from jax.experimental import pallas as pl
from jax.experimental.pallas import tpu as pltpu

D_MODEL = 1024
BATCH = 8
SEQ = 4096
DEPTH = 4

F32 = jnp.float32
GRID_W = 64
CTX_LEN = 256
N_MIXERS = 4
NORM_EPS = 1e-6

HG_HEADS = 8
HG_KEY = D_MODEL // HG_HEADS
HG_VAL = D_MODEL // HG_HEADS
HG_CHUNK = 32

HY_ORDER = 2
HY_SHORT = 3
HY_EMB = 33
HY_FILTER_W = 64
HY_DECAY_TARGET = 1e-2
HY_DECAY_HI_PCT = 0.3
HY_DECAY_LO_PCT = 1.5

MB_INNER = 2 * D_MODEL
MB_HEADDIM = 64
MB_HEADS = MB_INNER // MB_HEADDIM
MB_GROUPS = 8
MB_HPG = MB_HEADS // MB_GROUPS
MB_STATE = 128
MB_CONV = 5
MB_CHUNK = 64
MB_CONV_DIM = MB_INNER + 2 * MB_GROUPS * MB_STATE
MB_IN = MB_INNER + MB_CONV_DIM + 2 * MB_HEADS

GLA_HEADS = 4
GLA_KEY_DIM = D_MODEL // 2
GLA_VAL_DIM = D_MODEL
GLA_HK = GLA_KEY_DIM // GLA_HEADS
GLA_HV = GLA_VAL_DIM // GLA_HEADS
GLA_GATE_RANK = 16
GLA_GATE_NORM = 16.0
GLA_CHUNK = 32
GLA_IN = 2 * GLA_KEY_DIM + 2 * GLA_VAL_DIM + 2 * GLA_GATE_RANK

N_EXPERTS = 16
EC_CAPACITY_FACTOR = 2
EXPERT_FF = 2048


def rmsnorm(x, g):
    xf = x.astype(F32)
    return xf * lax.rsqrt(jnp.mean(xf * xf, axis=-1, keepdims=True) + NORM_EPS) * g


def adaln(cond, w, b):
    m = jax.nn.silu(cond.astype(F32)) @ w + b
    return jnp.split(m[..., None, :], 6, axis=-1)


def modulate(h, shift, scale):
    return h * (1.0 + scale) + shift


def heads(t, n):
    b, l, _ = t.shape
    return t.reshape(b, l, n, -1).transpose(0, 2, 1, 3)


def dwconv_centered(x, w, b):
    k_w = w.shape[0]
    pad = k_w // 2
    length = x.shape[1]
    xp = jnp.pad(x, ((0, 0), (pad, pad), (0, 0)))
    return sum(xp[:, j:j + length] * w[j] for j in range(k_w)) + b


def raster_to_colmajor(h, rows):
    b, l, d = h.shape
    return h.reshape(b, rows, GRID_W, d).transpose(0, 2, 1, 3).reshape(b, l, d)


def colmajor_to_raster(h, rows):
    b, l, d = h.shape
    return h.reshape(b, GRID_W, rows, d).transpose(0, 2, 1, 3).reshape(b, l, d)


def chunk_gla(q, k, v, g, s0, chunk):
    bsz, nh, length, kd = q.shape
    vd = v.shape[-1]
    n = length // chunk

    def split_chunks(t):
        return jnp.moveaxis(t.astype(F32).reshape(bsz, nh, n, chunk, t.shape[-1]), 2, 0)

    qs, ks, vs, gs = split_chunks(q), split_chunks(k), split_chunks(v), split_chunks(g)
    cum = jnp.cumsum(gs, axis=3)
    lower = jnp.tril(jnp.ones((chunk, chunk), bool))[:, :, None]

    def step(state, inp):
        qc, kc, vc, gc = inp
        rel = jnp.where(lower, gc[:, :, :, None, :] - gc[:, :, None, :, :], -jnp.inf)
        att = jnp.einsum('bhtk,bhsk,bhtsk->bhts', qc, kc, jnp.exp(rel))
        g_end = gc[:, :, -1:, :]
        o = att @ vc + jnp.einsum('bhtk,bhkv->bhtv', qc * jnp.exp(gc), state)
        state = jnp.exp(g_end[:, :, 0, :, None]) * state + jnp.einsum('bhsk,bhsv->bhkv', kc * jnp.exp(g_end - gc), vc)
        return state, o

    state, o = lax.scan(step, s0.astype(F32), (qs, ks, vs, cum))
    return jnp.moveaxis(o, 0, 2).reshape(bsz, nh, length, vd), state


def bidir_gla(lat, ctx, chunk):
    q, kf, kb, v, gf, gb = lat
    qc, kfc, kbc, vc, gfc, gbc = ctx
    s0 = jnp.zeros(q.shape[:2] + (q.shape[-1], v.shape[-1]), F32)
    oc_f, s_f = chunk_gla(qc, kfc, vc, gfc, s0, chunk)
    ox_f, _ = chunk_gla(q, kf, v, gf, s_f, chunk)
    r = lambda t: jnp.flip(t, axis=2)
    oc_b, s_b = chunk_gla(r(qc), r(kbc), r(vc), r(gbc), s0, chunk)
    ox_b, _ = chunk_gla(r(q), r(kb), r(v), r(gb), s_b, chunk)
    return ox_f + r(ox_b), oc_f + r(oc_b)


def gated_head_out(o, og, gain, w_out):
    b, nh, l, vd = o.shape
    o = rmsnorm(o.transpose(0, 2, 1, 3), gain).reshape(b, l, nh * vd)
    return (o * jax.nn.silu(og)) @ w_out


def hgrn_lower_bound(lb_param, layer):
    return jnp.cumsum(jax.nn.softmax(lb_param.astype(F32), axis=0), axis=0)[layer]


def hgrn2_mixer(hx, hc, w_in, lb, onorm_g, w_out):
    def project(h):
        q, ff, fb, inp, og = jnp.split(h @ w_in, 5, axis=-1)
        f_fwd = lb + (1.0 - lb) * jax.nn.sigmoid(ff.astype(F32))
        f_bwd = lb + (1.0 - lb) * jax.nn.sigmoid(fb.astype(F32))
        hd = lambda t: heads(t, HG_HEADS)
        streams = (hd(jax.nn.silu(q) * HG_KEY ** -0.5), hd(1.0 - f_fwd), hd(1.0 - f_bwd), hd(inp),
                   hd(jnp.log(f_fwd)), hd(jnp.log(f_bwd)))
        return streams, og

    lat, ogx = project(hx)
    ctxs, ogc = project(hc)
    ox, oc = bidir_gla(lat, ctxs, HG_CHUNK)
    return gated_head_out(ox, ogx, onorm_g, w_out), gated_head_out(oc, ogc, onorm_g, w_out)


def hyena_pos_features(length):
    t = jnp.linspace(0.0, 1.0, length, dtype=F32)[:, None]
    bands = (HY_EMB - 1) // 2
    f = jnp.linspace(1e-4, bands - 1, bands, dtype=F32)[None, :]
    w = 2.0 * math.pi * jnp.arange(length, dtype=F32)[:, None] / length
    return jnp.concatenate([t, jnp.cos(f * w), -jnp.sin(f * w)], axis=-1), t


def hyena_window(t):
    max_decay = math.log(HY_DECAY_TARGET) / HY_DECAY_HI_PCT
    min_decay = math.log(HY_DECAY_TARGET) / HY_DECAY_LO_PCT
    deltas = jnp.abs(jnp.linspace(min_decay, max_decay, D_MODEL, dtype=F32))
    return jnp.exp(-t * deltas[None, :])


def hyena_filters(length, w1, b1, w2, b2, w3, b3, w4, freq):
    z, t = hyena_pos_features(length)
    a = jnp.sin(freq * (z @ w1 + b1))
    a = jnp.sin(freq * (a @ w2 + b2))
    a = jnp.sin(freq * (a @ w3 + b3))
    h = (a @ w4).astype(F32).reshape(length, HY_ORDER, 2, D_MODEL) * hyena_window(t)[:, None, None, :]
    return h / jnp.sum(jnp.abs(h), axis=(0, 2), keepdims=True)


def twosided_fftconv(u, h_fwd, h_bwd):
    length = u.shape[1]
    k2 = jnp.concatenate([h_fwd, jnp.zeros_like(h_fwd[:1]), h_bwd[:0:-1]], axis=0)
    spec = jnp.fft.rfft(u, n=2 * length, axis=1) * jnp.fft.rfft(k2, axis=0)[None]
    return jnp.fft.irfft(spec, n=2 * length, axis=1)[:, :length]


def hyena_mixer(hx, hc, w_in, b_in, short_w, short_b, f_w1, f_b1, f_w2, f_b2, f_w3, f_b3, f_w4,
                f_freq, f_bias, w_out, b_out):
    def run(h):
        length = h.shape[1]
        u = dwconv_centered(h @ w_in + b_in, short_w, short_b).astype(F32)
        x1, x2, v = jnp.split(u, 3, axis=-1)
        filt = hyena_filters(length, f_w1, f_b1, f_w2, f_b2, f_w3, f_b3, f_w4, f_freq)
        z = v
        for o, gate in enumerate((x1, x2)):
            z = gate * (twosided_fftconv(z, filt[:, o, 0], filt[:, o, 1]) + z * f_bias[o])
        return z @ w_out + b_out

    return run(hx), run(hc)


def chunk_ssd(xs, bm, cm, dt, a, s0, chunk):
    bsz, length = xs.shape[:2]
    n = length // chunk

    def split_chunks(t):
        return jnp.moveaxis(t.astype(F32).reshape((bsz, n, chunk) + t.shape[2:]), 1, 0)

    xcs, bcs, ccs, dts = split_chunks(xs), split_chunks(bm), split_chunks(cm), split_chunks(dt)
    cum = jnp.cumsum(dts * a.astype(F32), axis=2)
    lower = jnp.tril(jnp.ones((chunk, chunk), bool))[:, :, None, None]

    def step(state, inp):
        xc, bc, cc, dtc, lc = inp
        seg = jnp.exp(jnp.where(lower, lc[:, :, None] - lc[:, None, :], -jnp.inf))
        w = jnp.einsum('btgn,bsgn->btsg', cc, bc)[..., None] * seg * dtc[:, None]
        y = jnp.einsum('btsgh,bsghp->btghp', w, xc)
        y = y + jnp.einsum('btgn,bghnp->btghp', cc, state) * jnp.exp(lc)[..., None]
        l_end = lc[:, -1]
        state = jnp.exp(l_end)[..., None, None] * state + jnp.einsum(
            'bsgn,bsgh,bsghp->bghnp', bc, jnp.exp(l_end[:, None] - lc) * dtc, xc)
        return state, y

    state, y = lax.scan(step, s0.astype(F32), (xcs, bcs, ccs, dts, cum))
    return jnp.moveaxis(y, 0, 1).reshape(xs.shape), state


def mamba2_mixer(hx, hc, w_in, conv_w, conv_b, dt_bias, a_log, d_skip, norm_g, w_out):
    a = -jnp.exp(a_log.astype(F32)).reshape(2, MB_GROUPS, MB_HPG)

    def project(h):
        bsz, length, _ = h.shape
        z, xbc, dt_raw = jnp.split(h @ w_in, [MB_INNER, MB_INNER + MB_CONV_DIM], axis=-1)
        xbc = jax.nn.silu(dwconv_centered(xbc, conv_w, conv_b))
        xs, bm, cm = jnp.split(xbc, [MB_INNER, MB_INNER + MB_GROUPS * MB_STATE], axis=-1)
        xs = xs.reshape(bsz, length, MB_GROUPS, MB_HPG, MB_HEADDIM)
        bm = bm.reshape(bsz, length, MB_GROUPS, MB_STATE)
        cm = cm.reshape(bsz, length, MB_GROUPS, MB_STATE)
        dt = jax.nn.softplus(dt_raw.astype(F32).reshape(bsz, length, 2, MB_GROUPS, MB_HPG)
                             + dt_bias.reshape(2, MB_GROUPS, MB_HPG))
        return z, xs, bm, cm, dt

    zx, xx, bx, cx, dtx = project(hx)
    zc, xc, bc, cc, dtc = project(hc)
    s0 = jnp.zeros((hx.shape[0], MB_GROUPS, MB_HPG, MB_STATE, MB_HEADDIM), F32)
    yc_f, s_f = chunk_ssd(xc, bc, cc, dtc[:, :, 0], a[0], s0, MB_CHUNK)
    yx_f, _ = chunk_ssd(xx, bx, cx, dtx[:, :, 0], a[0], s_f, MB_CHUNK)
    r = lambda t: jnp.flip(t, axis=1)
    yc_b, s_b = chunk_ssd(r(xc), r(bc), r(cc), r(dtc[:, :, 1]), a[1], s0, MB_CHUNK)
    yx_b, _ = chunk_ssd(r(xx), r(bx), r(cx), r(dtx[:, :, 1]), a[1], s_b, MB_CHUNK)

    def finish(y, xs, z):
        bsz, length = y.shape[:2]
        y = y + d_skip.reshape(MB_GROUPS, MB_HPG)[..., None] * xs
        y = y.reshape(bsz, length, MB_INNER) * jax.nn.silu(z)
        y = rmsnorm(y.reshape(bsz, length, MB_GROUPS, MB_INNER // MB_GROUPS),
                    norm_g.reshape(MB_GROUPS, MB_INNER // MB_GROUPS))
        return y.reshape(bsz, length, MB_INNER) @ w_out

    return finish(yx_f + r(yx_b), xx, zx), finish(yc_f + r(yc_b), xc, zc)


def gla_mixer(hx, hc, w_in, gk_w2, gk_b, onorm_g, w_out):
    rows = hx.shape[1] // GRID_W
    kd, vd, rk = GLA_KEY_DIM, GLA_VAL_DIM, GLA_GATE_RANK
    cuts = [kd, 2 * kd, 2 * kd + vd, 2 * kd + 2 * vd, 2 * kd + 2 * vd + rk]

    def project(h):
        q, k, v, og, rf, rb = jnp.split(h @ w_in, cuts, axis=-1)
        gf = jax.nn.log_sigmoid((rf @ gk_w2[0] + gk_b[0]).astype(F32)) / GLA_GATE_NORM
        gb = jax.nn.log_sigmoid((rb @ gk_w2[1] + gk_b[1]).astype(F32)) / GLA_GATE_NORM
        hd = lambda t: heads(t, GLA_HEADS)
        kh = hd(k)
        return (hd(q * GLA_HK ** -0.5), kh, kh, hd(v), hd(gf), hd(gb)), og

    lat, ogx = project(raster_to_colmajor(hx, rows))
    ctxs, ogc = project(hc)
    ox, oc = bidir_gla(lat, ctxs, GLA_CHUNK)
    yx = colmajor_to_raster(gated_head_out(ox, ogx, onorm_g, w_out), rows)
    return yx, gated_head_out(oc, ogc, onorm_g, w_out)


def expert_choice_ffn(h, router_w, w_gate, w_up, w_down):
    bsz, t_len, d = h.shape
    cap = max(1, EC_CAPACITY_FACTOR * t_len // N_EXPERTS)
    aff = jax.nn.softmax(h.astype(F32) @ router_w.astype(F32), axis=-1)
    gate, idx = lax.top_k(jnp.swapaxes(aff, 1, 2), cap)
    xg = jax.vmap(lambda hb, ib: hb[ib])(h, idx)
    hid = jax.nn.silu(jnp.einsum('becd,edf->becf', xg, w_gate)) * jnp.einsum('becd,edf->becf', xg, w_up)
    ye = jnp.einsum('becf,efd->becd', hid, w_down) * gate[..., None]
    return jax.vmap(lambda ib, yb: jnp.zeros((t_len, d), yb.dtype).at[ib.reshape(-1)].add(yb.reshape(-1, d)))(idx, ye)


BF16 = jnp.bfloat16
LANES = 128
VMEM_LIMIT_BYTES = 56 * 2 ** 20
SCAN_CHUNK = 64
SCAN_FAST_RANGE = 80.0
SCAN_UNROLL = 4


def _cumsum_rows(tri_bf16, g):
    hi = g.astype(BF16)
    r1 = g - hi.astype(F32)
    mid = r1.astype(BF16)
    lo = (r1 - mid.astype(F32)).astype(BF16)
    d = lambda a: jnp.dot(tri_bf16, a, preferred_element_type=F32)
    return d(hi) + d(mid) + d(lo)


def _log_sigmoid(z):
    return jnp.minimum(z, 0.0) - jnp.log1p(jnp.exp(-jnp.abs(z)))


def _dot_nt(a, b):
    return lax.dot_general(a, b, (((1,), (1,)), ((), ())), preferred_element_type=F32)


def _dot_tn(a, b):
    return lax.dot_general(a, b, (((0,), (0,)), ((), ())), preferred_element_type=F32)


def _gla_scan_body(mode, seq, ctx_len, kdim, vdim, *refs):
    chunk = SCAN_CHUNK
    if mode == 'hgrn2':
        (q_x, ff_x, fb_x, v_x, og_x, q_c, ff_c, fb_c, v_c, og_c, lb_ref, gain_ref,
         o_x, o_c, cf_s, cb_s, sf_s, sb_s, kt_s, gt_s, qf_s, qb_s, of_s, ob_s, kf_s, kb_s) = refs
    else:
        (q_x, k_x, v_x, og_x, r_x, q_c, k_c, v_c, og_c, r_c, w2f_ref, w2b_ref, bf_ref, bb_ref, gain_ref,
         o_x, o_c, cf_s, cb_s, sf_s, sb_s, kt_s, gt_s, qf_s, qb_s, of_s, ob_s) = refs
    if mode == 'hgrn2':
        segs = (((ff_c, fb_c), q_c, None, v_c, og_c, o_c, ctx_len, 0),
                ((ff_x, fb_x), q_x, None, v_x, og_x, o_x, seq, ctx_len))
    else:
        segs = (((r_c,), q_c, k_c, v_c, og_c, o_c, ctx_len, 0),
                ((r_x,), q_x, k_x, v_x, og_x, o_x, seq, ctx_len))

    row = lax.broadcasted_iota(jnp.int32, (chunk, chunk), 0)
    col = lax.broadcasted_iota(jnp.int32, (chunk, chunk), 1)
    masks = (row >= col, row <= col)
    tris = tuple(m.astype(BF16) for m in masks)
    cum_s, state_s, qin_s, kout_s = (cf_s, cb_s), (sf_s, sb_s), (qf_s, qb_s), (of_s, ob_s)
    end_row = (chunk - 1, 0)

    def rows_of(i, off=0):
        return pl.ds(pl.multiple_of(off + i * chunk, chunk), chunk)

    def gate_pass(seg, bound):
        gsrc, _, _, _, _, _, length, off = seg

        def body(i, bound):
            rows, srows = rows_of(i), rows_of(i, off)
            if mode == 'hgrn2':
                lb = lb_ref[...]
                f_f = lb + (1.0 - lb) * jax.nn.sigmoid(gsrc[0][rows, :])
                f_b = lb + (1.0 - lb) * jax.nn.sigmoid(gsrc[1][rows, :])
                g_f, g_b = jnp.log(f_f), jnp.log(f_b)
                kf_s[srows, :] = 1.0 - f_f
                kb_s[srows, :] = 1.0 - f_b
            else:
                r = gsrc[0][rows, :].astype(BF16)
                z_f = jnp.dot(r, w2f_ref[...].astype(BF16), preferred_element_type=F32) + bf_ref[...]
                z_b = jnp.dot(r, w2b_ref[...].astype(BF16), preferred_element_type=F32) + bb_ref[...]
                g_f = _log_sigmoid(z_f) / GLA_GATE_NORM
                g_b = _log_sigmoid(z_b) / GLA_GATE_NORM
            c_f = _cumsum_rows(tris[0], g_f)
            c_b = _cumsum_rows(tris[1], g_b)
            cf_s[srows, :] = c_f
            cb_s[srows, :] = c_b
            tot = jnp.maximum(-c_f[end_row[0]:end_row[0] + 1, :], -c_b[end_row[1]:end_row[1] + 1, :])
            return jnp.maximum(bound, tot)

        return lax.fori_loop(0, length // chunk, body, bound, unroll=SCAN_UNROLL)

    def intra(seg, direction, i, exact):
        _, q_r, k_r, v_r, _, _, _, off = seg
        rows, srows = rows_of(i), rows_of(i, off)
        if mode == 'hgrn2':
            qq = jax.nn.silu(q_r[rows, :]) * HG_KEY ** -0.5
            k = (kf_s if direction == 0 else kb_s)[srows, :]
        else:
            qq = q_r[rows, :] * GLA_HK ** -0.5
            k = k_r[rows, :]
        cum = cum_s[direction][srows, :]
        e = end_row[direction]
        cum_end = cum[e:e + 1, :]
        q_in = (qq * jnp.exp(cum)).astype(BF16)
        qin_s[direction][srows, :] = q_in
        kout_s[direction][srows, :] = (k * jnp.exp(cum_end - cum)).astype(BF16)
        if exact:
            kt_s[...] = k
            gt_s[...] = cum

            def col_body(s, att):
                ks = kt_s[pl.ds(s, 1), :]
                gs = gt_s[pl.ds(s, 1), :]
                p = qq * ks * jnp.exp(jnp.minimum(cum - gs, 0.0))
                return att + jnp.where(col == s, jnp.sum(p, axis=-1, keepdims=True), 0.0)

            att = lax.fori_loop(0, chunk, col_body, jnp.zeros((chunk, chunk), F32))
        else:
            att = _dot_nt(q_in, (k * jnp.exp(-cum)).astype(BF16))
        att = jnp.where(masks[direction], att, 0.0).astype(BF16)
        return jnp.dot(att, v_r[rows, :].astype(BF16), preferred_element_type=F32)

    def intra_pass(exact):
        for seg in segs:
            o_r, length = seg[5], seg[6]

            def body(i, carry):
                o_r[rows_of(i), :] = intra(seg, 0, i, exact) + intra(seg, 1, i, exact)
                return carry

            lax.fori_loop(0, length // chunk, body, 0, unroll=1 if exact else SCAN_UNROLL)

    def inter(seg, direction, i):
        v_r, off = seg[3], seg[7]
        rows, srows = rows_of(i), rows_of(i, off)
        s_ref = state_s[direction]
        state = s_ref[...]
        o = _dot_nt(qin_s[direction][srows, :], state.astype(BF16))
        e = pl.ds(pl.multiple_of(off + i * chunk, chunk) + end_row[direction], 1)
        decay = jnp.exp(cum_s[direction][e, :])
        s_ref[...] = state * decay + _dot_tn(v_r[rows, :].astype(BF16), kout_s[direction][srows, :])
        return o

    def finish(seg, i, o):
        o = o * lax.rsqrt(jnp.mean(o * o, axis=-1, keepdims=True) + NORM_EPS) * gain_ref[...]
        return o * jax.nn.silu(seg[4][rows_of(i), :])

    def inter_pass():
        sf_s[...] = jnp.zeros_like(sf_s)
        sb_s[...] = jnp.zeros_like(sb_s)
        for seg in segs:
            o_r, length = seg[5], seg[6]
            n = length // chunk

            def first_half(i, carry):
                j = n - 1 - i
                o_r[rows_of(i), :] += inter(seg, 0, i)
                o_r[rows_of(j), :] += inter(seg, 1, j)
                return carry

            def second_half(i, carry):
                j = n - 1 - i
                o_r[rows_of(i), :] = finish(seg, i, o_r[rows_of(i), :] + inter(seg, 0, i))
                o_r[rows_of(j), :] = finish(seg, j, o_r[rows_of(j), :] + inter(seg, 1, j))
                return carry

            lax.fori_loop(0, n // 2, first_half, 0, unroll=min(SCAN_UNROLL, n // 2))
            lax.fori_loop(n // 2, n, second_half, 0, unroll=min(SCAN_UNROLL, n // 2))

    bound = jnp.zeros((1, kdim), F32)
    for seg in segs:
        bound = gate_pass(seg, bound)
    fast = jnp.max(bound) <= SCAN_FAST_RANGE

    @pl.when(fast)
    def _():
        intra_pass(False)

    @pl.when(jnp.logical_not(fast))
    def _():
        intra_pass(True)

    inter_pass()


def _gla_scan_call(mode, nheads, kdim, vdim, lat_in, ctx_in, small_in, bsz, seq, ctx_len):
    def stream_spec(length, width, first, stride):
        return pl.BlockSpec((None, length, width), lambda b, h: (b, 0, first + stride * h))

    in_specs = [stream_spec(seq, w, f, s) for _, w, f, s in lat_in]
    in_specs += [stream_spec(ctx_len, w, f, s) for _, w, f, s in ctx_in]
    in_specs += [pl.BlockSpec(blk, imap) for _, blk, imap in small_in]
    args = [a for a, _, _, _ in lat_in] + [a for a, _, _, _ in ctx_in] + [a for a, _, _ in small_in]
    tot = seq + ctx_len
    scratch = [pltpu.VMEM((tot, kdim), F32), pltpu.VMEM((tot, kdim), F32),
               pltpu.VMEM((vdim, kdim), F32), pltpu.VMEM((vdim, kdim), F32),
               pltpu.VMEM((SCAN_CHUNK, kdim), F32), pltpu.VMEM((SCAN_CHUNK, kdim), F32),
               pltpu.VMEM((tot, kdim), BF16), pltpu.VMEM((tot, kdim), BF16),
               pltpu.VMEM((tot, kdim), BF16), pltpu.VMEM((tot, kdim), BF16)]
    if mode == 'hgrn2':
        scratch += [pltpu.VMEM((tot, kdim), F32), pltpu.VMEM((tot, kdim), F32)]
    return pl.pallas_call(
        lambda *refs: _gla_scan_body(mode, seq, ctx_len, kdim, vdim, *refs),
        grid=(bsz, nheads),
        in_specs=in_specs,
        out_specs=[pl.BlockSpec((None, seq, vdim), lambda b, h: (b, 0, h)),
                   pl.BlockSpec((None, ctx_len, vdim), lambda b, h: (b, 0, h))],
        out_shape=[jax.ShapeDtypeStruct((bsz, seq, nheads * vdim), F32),
                   jax.ShapeDtypeStruct((bsz, ctx_len, nheads * vdim), F32)],
        scratch_shapes=scratch,
        compiler_params=pltpu.CompilerParams(dimension_semantics=("arbitrary", "arbitrary"),
                                             vmem_limit_bytes=VMEM_LIMIT_BYTES),
        name=f"{mode}_scan",
    )(*args)


def hgrn2_mixer_pallas(hx, hc, w_in, lb, onorm_g, w_out):
    bsz, seq, _ = hx.shape
    ctx_len = hc.shape[1]
    yx, yc = hx @ w_in, hc @ w_in
    nh = HG_HEADS
    streams = lambda y: [(y, HG_KEY, j * nh, 1) for j in range(5)]
    small = [(lb.reshape(1, D_MODEL), (1, HG_KEY), lambda b, h: (0, h)),
             (onorm_g.reshape(1, HG_VAL), (1, HG_VAL), lambda b, h: (0, 0))]
    ox, oc = _gla_scan_call('hgrn2', nh, HG_KEY, HG_VAL, streams(yx), streams(yc), small, bsz, seq, ctx_len)
    return ox @ w_out, oc @ w_out


def gla_mixer_pallas(hx, hc, w_in, gk_w2, gk_b, onorm_g, w_out):
    bsz, seq, _ = hx.shape
    ctx_len = hc.shape[1]
    rows = seq // GRID_W
    kd, vd, rk = GLA_KEY_DIM, GLA_VAL_DIM, GLA_GATE_RANK
    nh = GLA_HEADS
    w_in_p = jnp.pad(w_in, ((0, 0), (0, LANES - 2 * rk)))
    yx = raster_to_colmajor(hx, rows) @ w_in_p
    yc = hc @ w_in_p
    w2f = jnp.pad(gk_w2[0], ((0, LANES - rk), (0, 0)))
    w2b = jnp.pad(gk_w2[1], ((rk, LANES - 2 * rk), (0, 0)))
    streams = lambda y: [(y, GLA_HK, 0, 1), (y, GLA_HK, kd // GLA_HK, 1), (y, GLA_HV, 2 * kd // GLA_HV, 1),
                         (y, GLA_HV, (2 * kd + vd) // GLA_HV, 1), (y, LANES, (2 * kd + 2 * vd) // LANES, 0)]
    small = [(w2f, (LANES, GLA_HK), lambda b, h: (0, h)), (w2b, (LANES, GLA_HK), lambda b, h: (0, h)),
             (gk_b[0].reshape(1, kd), (1, GLA_HK), lambda b, h: (0, h)),
             (gk_b[1].reshape(1, kd), (1, GLA_HK), lambda b, h: (0, h)),
             (onorm_g.reshape(1, GLA_HV), (1, GLA_HV), lambda b, h: (0, 0))]
    ox, oc = _gla_scan_call('gla', nh, GLA_HK, GLA_HV, streams(yx), streams(yc), small, bsz, seq, ctx_len)
    return colmajor_to_raster(ox @ w_out, rows), oc @ w_out


MB_GROUP_W = MB_INNER // MB_GROUPS


def _ssd_scan_body(seq, ctx_len, *refs):
    chunk = MB_CHUNK
    gw, hpg = MB_GROUP_W, MB_HPG
    (x_x, b_x, c_x, z_x, cl_x, rt_x, x_c, b_c, c_c, z_c, cl_c, rt_c, dskip_ref, gain_ref,
     o_x, o_c, xf_s, xb_s, sf_s, sb_s) = refs
    segs = ((x_c, b_c, c_c, z_c, cl_c, rt_c, o_c, ctx_len, 0),
            (x_x, b_x, c_x, z_x, cl_x, rt_x, o_x, seq, ctx_len))
    xs_s, state_s = (xf_s, xb_s), (sf_s, sb_s)
    end_row = (chunk - 1, 0)

    row = lax.broadcasted_iota(jnp.int32, (chunk, chunk), 0)
    col = lax.broadcasted_iota(jnp.int32, (chunk, chunk), 1)
    masks = (row >= col, row <= col)
    head_of_lane = lax.broadcasted_iota(jnp.int32, (1, gw), 1) // MB_HEADDIM

    def rows_of(i, off=0):
        return pl.ds(pl.multiple_of(off + i * chunk, chunk), chunk)

    def bcast_heads(c4):
        out = c4[:, hpg - 1:hpg]
        for hh in range(hpg - 2, -1, -1):
            out = jnp.where(head_of_lane == hh, c4[:, hh:hh + 1], out)
        return out

    def intra_pass():
        for seg in segs:
            x_r, b_r, c_r, _, cl_r, rt_r, o_r, length, off = seg

            def body(i, carry):
                rows, srows = rows_of(i), rows_of(i, off)
                cols, rt = cl_r[i], rt_r[i]
                x = x_r[rows, :]
                cb = _dot_nt(c_r[rows, :].astype(BF16), b_r[rows, :].astype(BF16))
                xh = [jnp.where(head_of_lane == hh, x, 0.0).astype(BF16) for hh in range(hpg)]
                y = jnp.zeros((chunk, gw), F32)
                for d in range(2):
                    cum = cols[:, d * hpg:(d + 1) * hpg]
                    dt = cols[:, (2 + d) * hpg:(3 + d) * hpg]
                    e = end_row[d]
                    scale = jnp.exp(cum[e:e + 1, :] - cum) * dt
                    xs_s[d][srows, :] = (x * bcast_heads(scale)).astype(BF16)
                    for hh in range(hpg):
                        j = d * hpg + hh
                        seg_decay = jnp.exp(jnp.minimum(cols[:, j:j + 1] - rt[j:j + 1, :], 0.0))
                        w = jnp.where(masks[d], cb * seg_decay * rt[2 * hpg + j:2 * hpg + j + 1, :], 0.0)
                        y = y + jnp.dot(w.astype(BF16), xh[hh], preferred_element_type=F32)
                o_r[rows, :] = y
                return carry

            lax.fori_loop(0, length // chunk, body, 0)

    def inter(seg, d, i):
        _, b_r, c_r, _, cl_r, _, _, _, off = seg
        rows, srows = rows_of(i), rows_of(i, off)
        cum = cl_r[i][:, d * hpg:(d + 1) * hpg]
        state = state_s[d][...]
        y = jnp.dot(c_r[rows, :].astype(BF16), state.astype(BF16), preferred_element_type=F32)
        e = end_row[d]
        decay = bcast_heads(jnp.exp(cum[e:e + 1, :]))
        state_s[d][...] = state * decay + _dot_tn(b_r[rows, :].astype(BF16), xs_s[d][srows, :])
        return y * bcast_heads(jnp.exp(cum))

    def finish(seg, i, y):
        x_r, z_r = seg[0], seg[3]
        rows = rows_of(i)
        y = (y + dskip_ref[...] * x_r[rows, :]) * jax.nn.silu(z_r[rows, :])
        return y * lax.rsqrt(jnp.mean(y * y, axis=-1, keepdims=True) + NORM_EPS) * gain_ref[...]

    def inter_pass():
        sf_s[...] = jnp.zeros_like(sf_s)
        sb_s[...] = jnp.zeros_like(sb_s)
        for seg in segs:
            o_r, length = seg[6], seg[7]
            n = length // chunk

            def first_half(i, carry):
                j = n - 1 - i
                o_r[rows_of(i), :] += inter(seg, 0, i)
                o_r[rows_of(j), :] += inter(seg, 1, j)
                return carry

            def second_half(i, carry):
                j = n - 1 - i
                o_r[rows_of(i), :] = finish(seg, i, o_r[rows_of(i), :] + inter(seg, 0, i))
                o_r[rows_of(j), :] = finish(seg, j, o_r[rows_of(j), :] + inter(seg, 1, j))
                return carry

            lax.fori_loop(0, n // 2, first_half, 0, unroll=2)
            lax.fori_loop(n // 2, n, second_half, 0, unroll=2)

    intra_pass()
    inter_pass()


def _ssd_head_tables(dt_raw, dt_bias, a_log):
    bsz, length, _ = dt_raw.shape
    n = length // MB_CHUNK
    a = -jnp.exp(a_log.astype(F32)).reshape(2, MB_GROUPS, MB_HPG)
    dt = jax.nn.softplus(dt_raw.astype(F32).reshape(bsz, length, 2, MB_GROUPS, MB_HPG)
                         + dt_bias.reshape(2, MB_GROUPS, MB_HPG))
    dta = (dt * a).reshape(bsz, n, MB_CHUNK, 2, MB_GROUPS, MB_HPG)
    dtc = dt.reshape(bsz, n, MB_CHUNK, 2, MB_GROUPS, MB_HPG)
    cum_f = jnp.cumsum(dta[:, :, :, 0], axis=2)
    cum_b = jnp.flip(jnp.cumsum(jnp.flip(dta[:, :, :, 1], axis=2), axis=2), axis=2)
    cols = jnp.concatenate([cum_f, cum_b, dtc[:, :, :, 0], dtc[:, :, :, 1]], axis=-1)
    cols = cols.transpose(0, 3, 1, 2, 4)
    return cols, jnp.swapaxes(cols, -1, -2)


def mamba2_mixer_pallas(hx, hc, w_in, conv_w, conv_b, dt_bias, a_log, d_skip, norm_g, w_out):
    bsz, seq, _ = hx.shape
    ctx_len = hc.shape[1]
    chunk, gw = MB_CHUNK, MB_GROUP_W

    def project(h):
        y = h @ w_in
        xbc = jax.nn.silu(dwconv_centered(y[..., MB_INNER:MB_INNER + MB_CONV_DIM], conv_w, conv_b))
        cols, rows_t = _ssd_head_tables(y[..., MB_INNER + MB_CONV_DIM:], dt_bias, a_log)
        return y, xbc, cols, rows_t

    yx, xbcx, clx, rtx = project(hx)
    yc, xbcc, clc, rtc = project(hc)

    def stream_specs(length):
        n = length // chunk
        bw = MB_INNER // MB_STATE
        return [pl.BlockSpec((None, length, gw), lambda b, g: (b, 0, g)),
                pl.BlockSpec((None, length, MB_STATE), lambda b, g: (b, 0, bw + g)),
                pl.BlockSpec((None, length, MB_STATE), lambda b, g: (b, 0, bw + MB_GROUPS + g)),
                pl.BlockSpec((None, length, gw), lambda b, g: (b, 0, g)),
                pl.BlockSpec((None, None, n, chunk, 4 * MB_HPG), lambda b, g: (b, g, 0, 0, 0)),
                pl.BlockSpec((None, None, n, 4 * MB_HPG, chunk), lambda b, g: (b, g, 0, 0, 0))]

    tot = seq + ctx_len
    ox, oc = pl.pallas_call(
        lambda *refs: _ssd_scan_body(seq, ctx_len, *refs),
        grid=(bsz, MB_GROUPS),
        in_specs=stream_specs(seq) + stream_specs(ctx_len) + [
            pl.BlockSpec((1, gw), lambda b, g: (0, g)), pl.BlockSpec((1, gw), lambda b, g: (0, g))],
        out_specs=[pl.BlockSpec((None, seq, gw), lambda b, g: (b, 0, g)),
                   pl.BlockSpec((None, ctx_len, gw), lambda b, g: (b, 0, g))],
        out_shape=[jax.ShapeDtypeStruct((bsz, seq, MB_INNER), F32),
                   jax.ShapeDtypeStruct((bsz, ctx_len, MB_INNER), F32)],
        scratch_shapes=[pltpu.VMEM((tot, gw), BF16), pltpu.VMEM((tot, gw), BF16),
                        pltpu.VMEM((MB_STATE, gw), F32), pltpu.VMEM((MB_STATE, gw), F32)],
        compiler_params=pltpu.CompilerParams(dimension_semantics=("arbitrary", "arbitrary"),
                                             vmem_limit_bytes=VMEM_LIMIT_BYTES),
        name="ssd_scan",
    )(xbcx, xbcx, xbcx, yx, clx, rtx, xbcc, xbcc, xbcc, yc, clc, rtc,
      jnp.repeat(d_skip, MB_HEADDIM).reshape(1, MB_INNER), norm_g.reshape(1, MB_INNER))
    return ox @ w_out, oc @ w_out


def _final_norm_body(x_ref, g_ref, o_ref):
    xf = x_ref[...]
    o_ref[...] = xf * lax.rsqrt(jnp.mean(xf * xf, axis=-1, keepdims=True) + NORM_EPS) * g_ref[...]


def final_rmsnorm(x, g):
    b, l, d = x.shape
    rows = b * l
    tm = 512
    out = pl.pallas_call(
        _final_norm_body,
        grid=(rows // tm,),
        in_specs=[pl.BlockSpec((tm, d), lambda i: (i, 0)), pl.BlockSpec((1, d), lambda i: (0, 0))],
        out_specs=pl.BlockSpec((tm, d), lambda i: (i, 0)),
        out_shape=jax.ShapeDtypeStruct((rows, d), F32),
    )(x.reshape(rows, d), g.reshape(1, d))
    return out.reshape(b, l, d)


def kernel(x, c, ctx, c_ctx, ada_w, ada_b, norm1_g, norm2_g,
           hg_w_in, hg_lb, hg_onorm_g, hg_w_out,
           hy_w_in, hy_b_in, hy_short_w, hy_short_b, hy_f_w1, hy_f_b1, hy_f_w2, hy_f_b2,
           hy_f_w3, hy_f_b3, hy_f_w4, hy_f_freq, hy_f_bias, hy_w_out, hy_b_out,
           mb_w_in, mb_conv_w, mb_conv_b, mb_dt_bias, mb_a_log, mb_d, mb_norm_g, mb_w_out,
           gla_w_in, gla_gk_w2, gla_gk_b, gla_onorm_g, gla_w_out,
           router_w, moe_w_gate, moe_w_up, moe_w_down, final_norm_g):
    for i in range(DEPTH):
        last = i == DEPTH - 1
        sh1, sc1, g1, sh2, sc2, g2 = adaln(c, ada_w[i], ada_b[i])
        csh1, csc1, cg1, csh2, csc2, cg2 = adaln(c_ctx, ada_w[i], ada_b[i])
        hx = modulate(rmsnorm(x, norm1_g[i]), sh1, sc1)
        hc = modulate(rmsnorm(ctx, norm1_g[i]), csh1, csc1)
        kind, j = i % N_MIXERS, i // N_MIXERS
        if kind == 0:
            yx, yc = hgrn2_mixer_pallas(hx, hc, hg_w_in[j], hgrn_lower_bound(hg_lb, i), hg_onorm_g[j], hg_w_out[j])
        elif kind == 1:
            yx, yc = hyena_mixer(hx, hc, hy_w_in[j], hy_b_in[j], hy_short_w[j], hy_short_b[j],
                                 hy_f_w1[j], hy_f_b1[j], hy_f_w2[j], hy_f_b2[j], hy_f_w3[j], hy_f_b3[j],
                                 hy_f_w4[j], hy_f_freq[j], hy_f_bias[j], hy_w_out[j], hy_b_out[j])
        elif kind == 2:
            yx, yc = mamba2_mixer_pallas(hx, hc, mb_w_in[j], mb_conv_w[j], mb_conv_b[j], mb_dt_bias[j],
                                  mb_a_log[j], mb_d[j], mb_norm_g[j], mb_w_out[j])
        else:
            yx, yc = gla_mixer_pallas(hx, hc, gla_w_in[j], gla_gk_w2[j], gla_gk_b[j], gla_onorm_g[j], gla_w_out[j])
        x = x + g1 * yx
        x = x + g2 * expert_choice_ffn(modulate(rmsnorm(x, norm2_g[i]), sh2, sc2),
                                       router_w[i], moe_w_gate[i], moe_w_up[i], moe_w_down[i])
        if not last:
            ctx = ctx + cg1 * yc
            ctx = ctx + cg2 * expert_choice_ffn(modulate(rmsnorm(ctx, norm2_g[i]), csh2, csc2),
                                                router_w[i], moe_w_gate[i], moe_w_up[i], moe_w_down[i])
    return final_rmsnorm(x, final_norm_g)
```

```python
import math
import jax, jax.numpy as jnp
from jax import lax
from jax.experimental import pallas as pl
from jax.experimental.pallas import tpu as pltpu

D_MODEL = 1024
BATCH = 8
SEQ = 4096
DEPTH = 4

F32 = jnp.float32
GRID_W = 64
CTX_LEN = 256
N_MIXERS = 4
NORM_EPS = 1e-6

HG_HEADS = 8
HG_KEY = D_MODEL // HG_HEADS
HG_VAL = D_MODEL // HG_HEADS
HG_CHUNK = 32

HY_ORDER = 2
HY_SHORT = 3
HY_EMB = 33
HY_FILTER_W = 64
HY_DECAY_TARGET = 1e-2
HY_DECAY_HI_PCT = 0.3
HY_DECAY_LO_PCT = 1.5

MB_INNER = 2 * D_MODEL
MB_HEADDIM = 64
MB_HEADS = MB_INNER // MB_HEADDIM
MB_GROUPS = 8
MB_HPG = MB_HEADS // MB_GROUPS
MB_STATE = 128
MB_CONV = 5
MB_CHUNK = 64
MB_CONV_DIM = MB_INNER + 2 * MB_GROUPS * MB_STATE
MB_IN = MB_INNER + MB_CONV_DIM + 2 * MB_HEADS

GLA_HEADS = 4
GLA_KEY_DIM = D_MODEL // 2
GLA_VAL_DIM = D_MODEL
GLA_HK = GLA_KEY_DIM // GLA_HEADS
GLA_HV = GLA_VAL_DIM // GLA_HEADS
GLA_GATE_RANK = 16
GLA_GATE_NORM = 16.0
GLA_CHUNK = 32
GLA_IN = 2 * GLA_KEY_DIM + 2 * GLA_VAL_DIM + 2 * GLA_GATE_RANK

N_EXPERTS = 16
EC_CAPACITY_FACTOR = 2
EXPERT_FF = 2048


def rmsnorm(x, g):
    xf = x.astype(F32)
    return xf * lax.rsqrt(jnp.mean(xf * xf, axis=-1, keepdims=True) + NORM_EPS) * g


def adaln(cond, w, b):
    m = jax.nn.silu(cond.astype(F32)) @ w + b
    return jnp.split(m[..., None, :], 6, axis=-1)


def modulate(h, shift, scale):
    return h * (1.0 + scale) + shift


def heads(t, n):
    b, l, _ = t.shape
    return t.reshape(b, l, n, -1).transpose(0, 2, 1, 3)


def dwconv_centered(x, w, b):
    k_w = w.shape[0]
    pad = k_w // 2
    length = x.shape[1]
    xp = jnp.pad(x, ((0, 0), (pad, pad), (0, 0)))
    return sum(xp[:, j:j + length] * w[j] for j in range(k_w)) + b


def raster_to_colmajor(h, rows):
    b, l, d = h.shape
    return h.reshape(b, rows, GRID_W, d).transpose(0, 2, 1, 3).reshape(b, l, d)


def colmajor_to_raster(h, rows):
    b, l, d = h.shape
    return h.reshape(b, GRID_W, rows, d).transpose(0, 2, 1, 3).reshape(b, l, d)


def chunk_gla(q, k, v, g, s0, chunk):
    bsz, nh, length, kd = q.shape
    vd = v.shape[-1]
    n = length // chunk

    def split_chunks(t):
        return jnp.moveaxis(t.astype(F32).reshape(bsz, nh, n, chunk, t.shape[-1]), 2, 0)

    qs, ks, vs, gs = split_chunks(q), split_chunks(k), split_chunks(v), split_chunks(g)
    cum = jnp.cumsum(gs, axis=3)
    lower = jnp.tril(jnp.ones((chunk, chunk), bool))[:, :, None]

    def step(state, inp):
        qc, kc, vc, gc = inp
        rel = jnp.where(lower, gc[:, :, :, None, :] - gc[:, :, None, :, :], -jnp.inf)
        att = jnp.einsum('bhtk,bhsk,bhtsk->bhts', qc, kc, jnp.exp(rel))
        g_end = gc[:, :, -1:, :]
        o = att @ vc + jnp.einsum('bhtk,bhkv->bhtv', qc * jnp.exp(gc), state)
        state = jnp.exp(g_end[:, :, 0, :, None]) * state + jnp.einsum('bhsk,bhsv->bhkv', kc * jnp.exp(g_end - gc), vc)
        return state, o

    state, o = lax.scan(step, s0.astype(F32), (qs, ks, vs, cum))
    return jnp.moveaxis(o, 0, 2).reshape(bsz, nh, length, vd), state


def bidir_gla(lat, ctx, chunk):
    q, kf, kb, v, gf, gb = lat
    qc, kfc, kbc, vc, gfc, gbc = ctx
    s0 = jnp.zeros(q.shape[:2] + (q.shape[-1], v.shape[-1]), F32)
    oc_f, s_f = chunk_gla(qc, kfc, vc, gfc, s0, chunk)
    ox_f, _ = chunk_gla(q, kf, v, gf, s_f, chunk)
    r = lambda t: jnp.flip(t, axis=2)
    oc_b, s_b = chunk_gla(r(qc), r(kbc), r(vc), r(gbc), s0, chunk)
    ox_b, _ = chunk_gla(r(q), r(kb), r(v), r(gb), s_b, chunk)
    return ox_f + r(ox_b), oc_f + r(oc_b)


def gated_head_out(o, og, gain, w_out):
    b, nh, l, vd = o.shape
    o = rmsnorm(o.transpose(0, 2, 1, 3), gain).reshape(b, l, nh * vd)
    return (o * jax.nn.silu(og)) @ w_out


def hgrn_lower_bound(lb_param, layer):
    return jnp.cumsum(jax.nn.softmax(lb_param.astype(F32), axis=0), axis=0)[layer]


def hgrn2_mixer(hx, hc, w_in, lb, onorm_g, w_out):
    def project(h):
        q, ff, fb, inp, og = jnp.split(h @ w_in, 5, axis=-1)
        f_fwd = lb + (1.0 - lb) * jax.nn.sigmoid(ff.astype(F32))
        f_bwd = lb + (1.0 - lb) * jax.nn.sigmoid(fb.astype(F32))
        hd = lambda t: heads(t, HG_HEADS)
        streams = (hd(jax.nn.silu(q) * HG_KEY ** -0.5), hd(1.0 - f_fwd), hd(1.0 - f_bwd), hd(inp),
                   hd(jnp.log(f_fwd)), hd(jnp.log(f_bwd)))
        return streams, og

    lat, ogx = project(hx)
    ctxs, ogc = project(hc)
    ox, oc = bidir_gla(lat, ctxs, HG_CHUNK)
    return gated_head_out(ox, ogx, onorm_g, w_out), gated_head_out(oc, ogc, onorm_g, w_out)


def hyena_pos_features(length):
    t = jnp.linspace(0.0, 1.0, length, dtype=F32)[:, None]
    bands = (HY_EMB - 1) // 2
    f = jnp.linspace(1e-4, bands - 1, bands, dtype=F32)[None, :]
    w = 2.0 * math.pi * jnp.arange(length, dtype=F32)[:, None] / length
    return jnp.concatenate([t, jnp.cos(f * w), -jnp.sin(f * w)], axis=-1), t


def hyena_window(t):
    max_decay = math.log(HY_DECAY_TARGET) / HY_DECAY_HI_PCT
    min_decay = math.log(HY_DECAY_TARGET) / HY_DECAY_LO_PCT
    deltas = jnp.abs(jnp.linspace(min_decay, max_decay, D_MODEL, dtype=F32))
    return jnp.exp(-t * deltas[None, :])


def hyena_filters(length, w1, b1, w2, b2, w3, b3, w4, freq):
    z, t = hyena_pos_features(length)
    a = jnp.sin(freq * (z @ w1 + b1))
    a = jnp.sin(freq * (a @ w2 + b2))
    a = jnp.sin(freq * (a @ w3 + b3))
    h = (a @ w4).astype(F32).reshape(length, HY_ORDER, 2, D_MODEL) * hyena_window(t)[:, None, None, :]
    return h / jnp.sum(jnp.abs(h), axis=(0, 2), keepdims=True)


def twosided_fftconv(u, h_fwd, h_bwd):
    length = u.shape[1]
    k2 = jnp.concatenate([h_fwd, jnp.zeros_like(h_fwd[:1]), h_bwd[:0:-1]], axis=0)
    spec = jnp.fft.rfft(u, n=2 * length, axis=1) * jnp.fft.rfft(k2, axis=0)[None]
    return jnp.fft.irfft(spec, n=2 * length, axis=1)[:, :length]


def hyena_mixer(hx, hc, w_in, b_in, short_w, short_b, f_w1, f_b1, f_w2, f_b2, f_w3, f_b3, f_w4,
                f_freq, f_bias, w_out, b_out):
    def run(h):
        length = h.shape[1]
        u = dwconv_centered(h @ w_in + b_in, short_w, short_b).astype(F32)
        x1, x2, v = jnp.split(u, 3, axis=-1)
        filt = hyena_filters(length, f_w1, f_b1, f_w2, f_b2, f_w3, f_b3, f_w4, f_freq)
        z = v
        for o, gate in enumerate((x1, x2)):
            z = gate * (twosided_fftconv(z, filt[:, o, 0], filt[:, o, 1]) + z * f_bias[o])
        return z @ w_out + b_out

    return run(hx), run(hc)


def chunk_ssd(xs, bm, cm, dt, a, s0, chunk):
    bsz, length = xs.shape[:2]
    n = length // chunk

    def split_chunks(t):
        return jnp.moveaxis(t.astype(F32).reshape((bsz, n, chunk) + t.shape[2:]), 1, 0)

    xcs, bcs, ccs, dts = split_chunks(xs), split_chunks(bm), split_chunks(cm), split_chunks(dt)
    cum = jnp.cumsum(dts * a.astype(F32), axis=2)
    lower = jnp.tril(jnp.ones((chunk, chunk), bool))[:, :, None, None]

    def step(state, inp):
        xc, bc, cc, dtc, lc = inp
        seg = jnp.exp(jnp.where(lower, lc[:, :, None] - lc[:, None, :], -jnp.inf))
        w = jnp.einsum('btgn,bsgn->btsg', cc, bc)[..., None] * seg * dtc[:, None]
        y = jnp.einsum('btsgh,bsghp->btghp', w, xc)
        y = y + jnp.einsum('btgn,bghnp->btghp', cc, state) * jnp.exp(lc)[..., None]
        l_end = lc[:, -1]
        state = jnp.exp(l_end)[..., None, None] * state + jnp.einsum(
            'bsgn,bsgh,bsghp->bghnp', bc, jnp.exp(l_end[:, None] - lc) * dtc, xc)
        return state, y

    state, y = lax.scan(step, s0.astype(F32), (xcs, bcs, ccs, dts, cum))
    return jnp.moveaxis(y, 0, 1).reshape(xs.shape), state


def mamba2_mixer(hx, hc, w_in, conv_w, conv_b, dt_bias, a_log, d_skip, norm_g, w_out):
    a = -jnp.exp(a_log.astype(F32)).reshape(2, MB_GROUPS, MB_HPG)

    def project(h):
        bsz, length, _ = h.shape
        z, xbc, dt_raw = jnp.split(h @ w_in, [MB_INNER, MB_INNER + MB_CONV_DIM], axis=-1)
        xbc = jax.nn.silu(dwconv_centered(xbc, conv_w, conv_b))
        xs, bm, cm = jnp.split(xbc, [MB_INNER, MB_INNER + MB_GROUPS * MB_STATE], axis=-1)
        xs = xs.reshape(bsz, length, MB_GROUPS, MB_HPG, MB_HEADDIM)
        bm = bm.reshape(bsz, length, MB_GROUPS, MB_STATE)
        cm = cm.reshape(bsz, length, MB_GROUPS, MB_STATE)
        dt = jax.nn.softplus(dt_raw.astype(F32).reshape(bsz, length, 2, MB_GROUPS, MB_HPG)
                             + dt_bias.reshape(2, MB_GROUPS, MB_HPG))
        return z, xs, bm, cm, dt

    zx, xx, bx, cx, dtx = project(hx)
    zc, xc, bc, cc, dtc = project(hc)
    s0 = jnp.zeros((hx.shape[0], MB_GROUPS, MB_HPG, MB_STATE, MB_HEADDIM), F32)
    yc_f, s_f = chunk_ssd(xc, bc, cc, dtc[:, :, 0], a[0], s0, MB_CHUNK)
    yx_f, _ = chunk_ssd(xx, bx, cx, dtx[:, :, 0], a[0], s_f, MB_CHUNK)
    r = lambda t: jnp.flip(t, axis=1)
    yc_b, s_b = chunk_ssd(r(xc), r(bc), r(cc), r(dtc[:, :, 1]), a[1], s0, MB_CHUNK)
    yx_b, _ = chunk_ssd(r(xx), r(bx), r(cx), r(dtx[:, :, 1]), a[1], s_b, MB_CHUNK)

    def finish(y, xs, z):
        bsz, length = y.shape[:2]
        y = y + d_skip.reshape(MB_GROUPS, MB_HPG)[..., None] * xs
        y = y.reshape(bsz, length, MB_INNER) * jax.nn.silu(z)
        y = rmsnorm(y.reshape(bsz, length, MB_GROUPS, MB_INNER // MB_GROUPS),
                    norm_g.reshape(MB_GROUPS, MB_INNER // MB_GROUPS))
        return y.reshape(bsz, length, MB_INNER) @ w_out

    return finish(yx_f + r(yx_b), xx, zx), finish(yc_f + r(yc_b), xc, zc)


def gla_mixer(hx, hc, w_in, gk_w2, gk_b, onorm_g, w_out):
    rows = hx.shape[1] // GRID_W
    kd, vd, rk = GLA_KEY_DIM, GLA_VAL_DIM, GLA_GATE_RANK
    cuts = [kd, 2 * kd, 2 * kd + vd, 2 * kd + 2 * vd, 2 * kd + 2 * vd + rk]

    def project(h):
        q, k, v, og, rf, rb = jnp.split(h @ w_in, cuts, axis=-1)
        gf = jax.nn.log_sigmoid((rf @ gk_w2[0] + gk_b[0]).astype(F32)) / GLA_GATE_NORM
        gb = jax.nn.log_sigmoid((rb @ gk_w2[1] + gk_b[1]).astype(F32)) / GLA_GATE_NORM
        hd = lambda t: heads(t, GLA_HEADS)
        kh = hd(k)
        return (hd(q * GLA_HK ** -0.5), kh, kh, hd(v), hd(gf), hd(gb)), og

    lat, ogx = project(raster_to_colmajor(hx, rows))
    ctxs, ogc = project(hc)
    ox, oc = bidir_gla(lat, ctxs, GLA_CHUNK)
    yx = colmajor_to_raster(gated_head_out(ox, ogx, onorm_g, w_out), rows)
    return yx, gated_head_out(oc, ogc, onorm_g, w_out)


def expert_choice_ffn(h, router_w, w_gate, w_up, w_down):
    bsz, t_len, d = h.shape
    cap = max(1, EC_CAPACITY_FACTOR * t_len // N_EXPERTS)
    aff = jax.nn.softmax(h.astype(F32) @ router_w.astype(F32), axis=-1)
    gate, idx = lax.top_k(jnp.swapaxes(aff, 1, 2), cap)
    xg = jax.vmap(lambda hb, ib: hb[ib])(h, idx)
    hid = jax.nn.silu(jnp.einsum('becd,edf->becf', xg, w_gate)) * jnp.einsum('becd,edf->becf', xg, w_up)
    ye = jnp.einsum('becf,efd->becd', hid, w_down) * gate[..., None]
    return jax.vmap(lambda ib, yb: jnp.zeros((t_len, d), yb.dtype).at[ib.reshape(-1)].add(yb.reshape(-1, d)))(idx, ye)


BF16 = jnp.bfloat16
LANES = 128
VMEM_LIMIT_BYTES = 56 * 2 ** 20
SCAN_CHUNK = 64
SCAN_FAST_RANGE = 80.0
SCAN_UNROLL = 4


def _cumsum_rows(tri_bf16, g):
    hi = g.astype(BF16)
    r1 = g - hi.astype(F32)
    mid = r1.astype(BF16)
    lo = (r1 - mid.astype(F32)).astype(BF16)
    d = lambda a: jnp.dot(tri_bf16, a, preferred_element_type=F32)
    return d(hi) + d(mid) + d(lo)


def _log_sigmoid(z):
    return jnp.minimum(z, 0.0) - jnp.log1p(jnp.exp(-jnp.abs(z)))


def _dot_nt(a, b):
    return lax.dot_general(a, b, (((1,), (1,)), ((), ())), preferred_element_type=F32)


def _dot_tn(a, b):
    return lax.dot_general(a, b, (((0,), (0,)), ((), ())), preferred_element_type=F32)


def _gla_scan_body(mode, seq, ctx_len, kdim, vdim, *refs):
    chunk = SCAN_CHUNK
    if mode == 'hgrn2':
        (q_x, ff_x, fb_x, v_x, og_x, q_c, ff_c, fb_c, v_c, og_c, lb_ref, gain_ref,
         o_x, o_c, cf_s, cb_s, sf_s, sb_s, kt_s, gt_s, qf_s, qb_s, of_s, ob_s, kf_s, kb_s) = refs
    else:
        (q_x, k_x, v_x, og_x, r_x, q_c, k_c, v_c, og_c, r_c, w2f_ref, w2b_ref, bf_ref, bb_ref, gain_ref,
         o_x, o_c, cf_s, cb_s, sf_s, sb_s, kt_s, gt_s, qf_s, qb_s, of_s, ob_s) = refs
    if mode == 'hgrn2':
        segs = (((ff_c, fb_c), q_c, None, v_c, og_c, o_c, ctx_len, 0),
                ((ff_x, fb_x), q_x, None, v_x, og_x, o_x, seq, ctx_len))
    else:
        segs = (((r_c,), q_c, k_c, v_c, og_c, o_c, ctx_len, 0),
                ((r_x,), q_x, k_x, v_x, og_x, o_x, seq, ctx_len))

    row = lax.broadcasted_iota(jnp.int32, (chunk, chunk), 0)
    col = lax.broadcasted_iota(jnp.int32, (chunk, chunk), 1)
    masks = (row >= col, row <= col)
    tris = tuple(m.astype(BF16) for m in masks)
    cum_s, state_s, qin_s, kout_s = (cf_s, cb_s), (sf_s, sb_s), (qf_s, qb_s), (of_s, ob_s)
    end_row = (chunk - 1, 0)

    def rows_of(i, off=0):
        return pl.ds(pl.multiple_of(off + i * chunk, chunk), chunk)

    def gate_pass(seg, bound):
        gsrc, _, _, _, _, _, length, off = seg

        def body(i, bound):
            rows, srows = rows_of(i), rows_of(i, off)
            if mode == 'hgrn2':
                lb = lb_ref[...]
                f_f = lb + (1.0 - lb) * jax.nn.sigmoid(gsrc[0][rows, :])
                f_b = lb + (1.0 - lb) * jax.nn.sigmoid(gsrc[1][rows, :])
                g_f, g_b = jnp.log(f_f), jnp.log(f_b)
                kf_s[srows, :] = 1.0 - f_f
                kb_s[srows, :] = 1.0 - f_b
            else:
                r = gsrc[0][rows, :].astype(BF16)
                z_f = jnp.dot(r, w2f_ref[...].astype(BF16), preferred_element_type=F32) + bf_ref[...]
                z_b = jnp.dot(r, w2b_ref[...].astype(BF16), preferred_element_type=F32) + bb_ref[...]
                g_f = _log_sigmoid(z_f) / GLA_GATE_NORM
                g_b = _log_sigmoid(z_b) / GLA_GATE_NORM
            c_f = _cumsum_rows(tris[0], g_f)
            c_b = _cumsum_rows(tris[1], g_b)
            cf_s[srows, :] = c_f
            cb_s[srows, :] = c_b
            tot = jnp.maximum(-c_f[end_row[0]:end_row[0] + 1, :], -c_b[end_row[1]:end_row[1] + 1, :])
            return jnp.maximum(bound, tot)

        return lax.fori_loop(0, length // chunk, body, bound, unroll=SCAN_UNROLL)

    def intra(seg, direction, i, exact):
        _, q_r, k_r, v_r, _, _, _, off = seg
        rows, srows = rows_of(i), rows_of(i, off)
        if mode == 'hgrn2':
            qq = jax.nn.silu(q_r[rows, :]) * HG_KEY ** -0.5
            k = (kf_s if direction == 0 else kb_s)[srows, :]
        else:
            qq = q_r[rows, :] * GLA_HK ** -0.5
            k = k_r[rows, :]
        cum = cum_s[direction][srows, :]
        e = end_row[direction]
        cum_end = cum[e:e + 1, :]
        q_in = (qq * jnp.exp(cum)).astype(BF16)
        qin_s[direction][srows, :] = q_in
        kout_s[direction][srows, :] = (k * jnp.exp(cum_end - cum)).astype(BF16)
        if exact:
            kt_s[...] = k
            gt_s[...] = cum

            def col_body(s, att):
                ks = kt_s[pl.ds(s, 1), :]
                gs = gt_s[pl.ds(s, 1), :]
                p = qq * ks * jnp.exp(jnp.minimum(cum - gs, 0.0))
                return att + jnp.where(col == s, jnp.sum(p, axis=-1, keepdims=True), 0.0)

            att = lax.fori_loop(0, chunk, col_body, jnp.zeros((chunk, chunk), F32))
        else:
            att = _dot_nt(q_in, (k * jnp.exp(-cum)).astype(BF16))
        att = jnp.where(masks[direction], att, 0.0).astype(BF16)
        return jnp.dot(att, v_r[rows, :].astype(BF16), preferred_element_type=F32)

    def intra_pass(exact):
        for seg in segs:
            o_r, length = seg[5], seg[6]

            def body(i, carry):
                o_r[rows_of(i), :] = intra(seg, 0, i, exact) + intra(seg, 1, i, exact)
                return carry

            lax.fori_loop(0, length // chunk, body, 0, unroll=1 if exact else SCAN_UNROLL)

    def inter(seg, direction, i):
        v_r, off = seg[3], seg[7]
        rows, srows = rows_of(i), rows_of(i, off)
        s_ref = state_s[direction]
        state = s_ref[...]
        o = _dot_nt(qin_s[direction][srows, :], state.astype(BF16))
        e = pl.ds(pl.multiple_of(off + i * chunk, chunk) + end_row[direction], 1)
        decay = jnp.exp(cum_s[direction][e, :])
        s_ref[...] = state * decay + _dot_tn(v_r[rows, :].astype(BF16), kout_s[direction][srows, :])
        return o

    def finish(seg, i, o):
        o = o * lax.rsqrt(jnp.mean(o * o, axis=-1, keepdims=True) + NORM_EPS) * gain_ref[...]
        return o * jax.nn.silu(seg[4][rows_of(i), :])

    def inter_pass():
        sf_s[...] = jnp.zeros_like(sf_s)
        sb_s[...] = jnp.zeros_like(sb_s)
        for seg in segs:
            o_r, length = seg[5], seg[6]
            n = length // chunk

            def first_half(i, carry):
                j = n - 1 - i
                o_r[rows_of(i), :] += inter(seg, 0, i)
                o_r[rows_of(j), :] += inter(seg, 1, j)
                return carry

            def second_half(i, carry):
                j = n - 1 - i
                o_r[rows_of(i), :] = finish(seg, i, o_r[rows_of(i), :] + inter(seg, 0, i))
                o_r[rows_of(j), :] = finish(seg, j, o_r[rows_of(j), :] + inter(seg, 1, j))
                return carry

            lax.fori_loop(0, n // 2, first_half, 0, unroll=min(SCAN_UNROLL, n // 2))
            lax.fori_loop(n // 2, n, second_half, 0, unroll=min(SCAN_UNROLL, n // 2))

    bound = jnp.zeros((1, kdim), F32)
    for seg in segs:
        bound = gate_pass(seg, bound)
    fast = jnp.max(bound) <= SCAN_FAST_RANGE

    @pl.when(fast)
    def _():
        intra_pass(False)

    @pl.when(jnp.logical_not(fast))
    def _():
        intra_pass(True)

    inter_pass()


def _gla_scan_call(mode, nheads, kdim, vdim, lat_in, ctx_in, small_in, bsz, seq, ctx_len):
    def stream_spec(length, width, first, stride):
        return pl.BlockSpec((None, length, width), lambda b, h: (b, 0, first + stride * h))

    in_specs = [stream_spec(seq, w, f, s) for _, w, f, s in lat_in]
    in_specs += [stream_spec(ctx_len, w, f, s) for _, w, f, s in ctx_in]
    in_specs += [pl.BlockSpec(blk, imap) for _, blk, imap in small_in]
    args = [a for a, _, _, _ in lat_in] + [a for a, _, _, _ in ctx_in] + [a for a, _, _ in small_in]
    tot = seq + ctx_len
    scratch = [pltpu.VMEM((tot, kdim), F32), pltpu.VMEM((tot, kdim), F32),
               pltpu.VMEM((vdim, kdim), F32), pltpu.VMEM((vdim, kdim), F32),
               pltpu.VMEM((SCAN_CHUNK, kdim), F32), pltpu.VMEM((SCAN_CHUNK, kdim), F32),
               pltpu.VMEM((tot, kdim), BF16), pltpu.VMEM((tot, kdim), BF16),
               pltpu.VMEM((tot, kdim), BF16), pltpu.VMEM((tot, kdim), BF16)]
    if mode == 'hgrn2':
        scratch += [pltpu.VMEM((tot, kdim), F32), pltpu.VMEM((tot, kdim), F32)]
    return pl.pallas_call(
        lambda *refs: _gla_scan_body(mode, seq, ctx_len, kdim, vdim, *refs),
        grid=(bsz, nheads),
        in_specs=in_specs,
        out_specs=[pl.BlockSpec((None, seq, vdim), lambda b, h: (b, 0, h)),
                   pl.BlockSpec((None, ctx_len, vdim), lambda b, h: (b, 0, h))],
        out_shape=[jax.ShapeDtypeStruct((bsz, seq, nheads * vdim), F32),
                   jax.ShapeDtypeStruct((bsz, ctx_len, nheads * vdim), F32)],
        scratch_shapes=scratch,
        compiler_params=pltpu.CompilerParams(dimension_semantics=("arbitrary", "arbitrary"),
                                             vmem_limit_bytes=VMEM_LIMIT_BYTES),
        name=f"{mode}_scan",
    )(*args)


def hgrn2_mixer_pallas(hx, hc, w_in, lb, onorm_g, w_out):
    bsz, seq, _ = hx.shape
    ctx_len = hc.shape[1]
    yx, yc = hx @ w_in, hc @ w_in
    nh = HG_HEADS
    streams = lambda y: [(y, HG_KEY, j * nh, 1) for j in range(5)]
    small = [(lb.reshape(1, D_MODEL), (1, HG_KEY), lambda b, h: (0, h)),
             (onorm_g.reshape(1, HG_VAL), (1, HG_VAL), lambda b, h: (0, 0))]
    ox, oc = _gla_scan_call('hgrn2', nh, HG_KEY, HG_VAL, streams(yx), streams(yc), small, bsz, seq, ctx_len)
    return ox @ w_out, oc @ w_out


def gla_mixer_pallas(hx, hc, w_in, gk_w2, gk_b, onorm_g, w_out):
    bsz, seq, _ = hx.shape
    ctx_len = hc.shape[1]
    rows = seq // GRID_W
    kd, vd, rk = GLA_KEY_DIM, GLA_VAL_DIM, GLA_GATE_RANK
    nh = GLA_HEADS
    w_in_p = jnp.pad(w_in, ((0, 0), (0, LANES - 2 * rk)))
    yx = raster_to_colmajor(hx, rows) @ w_in_p
    yc = hc @ w_in_p
    w2f = jnp.pad(gk_w2[0], ((0, LANES - rk), (0, 0)))
    w2b = jnp.pad(gk_w2[1], ((rk, LANES - 2 * rk), (0, 0)))
    streams = lambda y: [(y, GLA_HK, 0, 1), (y, GLA_HK, kd // GLA_HK, 1), (y, GLA_HV, 2 * kd // GLA_HV, 1),
                         (y, GLA_HV, (2 * kd + vd) // GLA_HV, 1), (y, LANES, (2 * kd + 2 * vd) // LANES, 0)]
    small = [(w2f, (LANES, GLA_HK), lambda b, h: (0, h)), (w2b, (LANES, GLA_HK), lambda b, h: (0, h)),
             (gk_b[0].reshape(1, kd), (1, GLA_HK), lambda b, h: (0, h)),
             (gk_b[1].reshape(1, kd), (1, GLA_HK), lambda b, h: (0, h)),
             (onorm_g.reshape(1, GLA_HV), (1, GLA_HV), lambda b, h: (0, 0))]
    ox, oc = _gla_scan_call('gla', nh, GLA_HK, GLA_HV, streams(yx), streams(yc), small, bsz, seq, ctx_len)
    return colmajor_to_raster(ox @ w_out, rows), oc @ w_out


MB_GROUP_W = MB_INNER // MB_GROUPS


def _ssd_scan_body(seq, ctx_len, *refs):
    chunk = MB_CHUNK
    gw, hpg = MB_GROUP_W, MB_HPG
    (x_x, b_x, c_x, z_x, cl_x, rt_x, x_c, b_c, c_c, z_c, cl_c, rt_c, dskip_ref, gain_ref,
     o_x, o_c, xf_s, xb_s, sf_s, sb_s) = refs
    segs = ((x_c, b_c, c_c, z_c, cl_c, rt_c, o_c, ctx_len, 0),
            (x_x, b_x, c_x, z_x, cl_x, rt_x, o_x, seq, ctx_len))
    xs_s, state_s = (xf_s, xb_s), (sf_s, sb_s)
    end_row = (chunk - 1, 0)

    row = lax.broadcasted_iota(jnp.int32, (chunk, chunk), 0)
    col = lax.broadcasted_iota(jnp.int32, (chunk, chunk), 1)
    masks = (row >= col, row <= col)
    head_of_lane = lax.broadcasted_iota(jnp.int32, (1, gw), 1) // MB_HEADDIM

    def rows_of(i, off=0):
        return pl.ds(pl.multiple_of(off + i * chunk, chunk), chunk)

    def bcast_heads(c4):
        out = c4[:, hpg - 1:hpg]
        for hh in range(hpg - 2, -1, -1):
            out = jnp.where(head_of_lane == hh, c4[:, hh:hh + 1], out)
        return out

    def intra_pass():
        for seg in segs:
            x_r, b_r, c_r, _, cl_r, rt_r, o_r, length, off = seg

            def body(i, carry):
                rows, srows = rows_of(i), rows_of(i, off)
                cols, rt = cl_r[i], rt_r[i]
                x = x_r[rows, :]
                cb = _dot_nt(c_r[rows, :].astype(BF16), b_r[rows, :].astype(BF16))
                xh = [jnp.where(head_of_lane == hh, x, 0.0).astype(BF16) for hh in range(hpg)]
                y = jnp.zeros((chunk, gw), F32)
                for d in range(2):
                    cum = cols[:, d * hpg:(d + 1) * hpg]
                    dt = cols[:, (2 + d) * hpg:(3 + d) * hpg]
                    e = end_row[d]
                    scale = jnp.exp(cum[e:e + 1, :] - cum) * dt
                    xs_s[d][srows, :] = (x * bcast_heads(scale)).astype(BF16)
                    for hh in range(hpg):
                        j = d * hpg + hh
                        seg_decay = jnp.exp(jnp.minimum(cols[:, j:j + 1] - rt[j:j + 1, :], 0.0))
                        w = jnp.where(masks[d], cb * seg_decay * rt[2 * hpg + j:2 * hpg + j + 1, :], 0.0)
                        y = y + jnp.dot(w.astype(BF16), xh[hh], preferred_element_type=F32)
                o_r[rows, :] = y
                return carry

            lax.fori_loop(0, length // chunk, body, 0)

    def inter(seg, d, i):
        _, b_r, c_r, _, cl_r, _, _, _, off = seg
        rows, srows = rows_of(i), rows_of(i, off)
        cum = cl_r[i][:, d * hpg:(d + 1) * hpg]
        state = state_s[d][...]
        y = jnp.dot(c_r[rows, :].astype(BF16), state.astype(BF16), preferred_element_type=F32)
        e = end_row[d]
        decay = bcast_heads(jnp.exp(cum[e:e + 1, :]))
        state_s[d][...] = state * decay + _dot_tn(b_r[rows, :].astype(BF16), xs_s[d][srows, :])
        return y * bcast_heads(jnp.exp(cum))

    def finish(seg, i, y):
        x_r, z_r = seg[0], seg[3]
        rows = rows_of(i)
        y = (y + dskip_ref[...] * x_r[rows, :]) * jax.nn.silu(z_r[rows, :])
        return y * lax.rsqrt(jnp.mean(y * y, axis=-1, keepdims=True) + NORM_EPS) * gain_ref[...]

    def inter_pass():
        sf_s[...] = jnp.zeros_like(sf_s)
        sb_s[...] = jnp.zeros_like(sb_s)
        for seg in segs:
            o_r, length = seg[6], seg[7]
            n = length // chunk

            def first_half(i, carry):
                j = n - 1 - i
                o_r[rows_of(i), :] += inter(seg, 0, i)
                o_r[rows_of(j), :] += inter(seg, 1, j)
                return carry

            def second_half(i, carry):
                j = n - 1 - i
                o_r[rows_of(i), :] = finish(seg, i, o_r[rows_of(i), :] + inter(seg, 0, i))
                o_r[rows_of(j), :] = finish(seg, j, o_r[rows_of(j), :] + inter(seg, 1, j))
                return carry

            lax.fori_loop(0, n // 2, first_half, 0, unroll=2)
            lax.fori_loop(n // 2, n, second_half, 0, unroll=2)

    intra_pass()
    inter_pass()


def _ssd_head_tables(dt_raw, dt_bias, a_log):
    bsz, length, _ = dt_raw.shape
    n = length // MB_CHUNK
    a = -jnp.exp(a_log.astype(F32)).reshape(2, MB_GROUPS, MB_HPG)
    dt = jax.nn.softplus(dt_raw.astype(F32).reshape(bsz, length, 2, MB_GROUPS, MB_HPG)
                         + dt_bias.reshape(2, MB_GROUPS, MB_HPG))
    dta = (dt * a).reshape(bsz, n, MB_CHUNK, 2, MB_GROUPS, MB_HPG)
    dtc = dt.reshape(bsz, n, MB_CHUNK, 2, MB_GROUPS, MB_HPG)
    cum_f = jnp.cumsum(dta[:, :, :, 0], axis=2)
    cum_b = jnp.flip(jnp.cumsum(jnp.flip(dta[:, :, :, 1], axis=2), axis=2), axis=2)
    cols = jnp.concatenate([cum_f, cum_b, dtc[:, :, :, 0], dtc[:, :, :, 1]], axis=-1)
    cols = cols.transpose(0, 3, 1, 2, 4)
    return cols, jnp.swapaxes(cols, -1, -2)


def mamba2_mixer_pallas(hx, hc, w_in, conv_w, conv_b, dt_bias, a_log, d_skip, norm_g, w_out):
    bsz, seq, _ = hx.shape
    ctx_len = hc.shape[1]
    chunk, gw = MB_CHUNK, MB_GROUP_W

    def project(h):
        y = h @ w_in
        xbc = jax.nn.silu(dwconv_centered(y[..., MB_INNER:MB_INNER + MB_CONV_DIM], conv_w, conv_b))
        cols, rows_t = _ssd_head_tables(y[..., MB_INNER + MB_CONV_DIM:], dt_bias, a_log)
        return y, xbc, cols, rows_t

    yx, xbcx, clx, rtx = project(hx)
    yc, xbcc, clc, rtc = project(hc)

    def stream_specs(length):
        n = length // chunk
        bw = MB_INNER // MB_STATE
        return [pl.BlockSpec((None, length, gw), lambda b, g: (b, 0, g)),
                pl.BlockSpec((None, length, MB_STATE), lambda b, g: (b, 0, bw + g)),
                pl.BlockSpec((None, length, MB_STATE), lambda b, g: (b, 0, bw + MB_GROUPS + g)),
                pl.BlockSpec((None, length, gw), lambda b, g: (b, 0, g)),
                pl.BlockSpec((None, None, n, chunk, 4 * MB_HPG), lambda b, g: (b, g, 0, 0, 0)),
                pl.BlockSpec((None, None, n, 4 * MB_HPG, chunk), lambda b, g: (b, g, 0, 0, 0))]

    tot = seq + ctx_len
    ox, oc = pl.pallas_call(
        lambda *refs: _ssd_scan_body(seq, ctx_len, *refs),
        grid=(bsz, MB_GROUPS),
        in_specs=stream_specs(seq) + stream_specs(ctx_len) + [
            pl.BlockSpec((1, gw), lambda b, g: (0, g)), pl.BlockSpec((1, gw), lambda b, g: (0, g))],
        out_specs=[pl.BlockSpec((None, seq, gw), lambda b, g: (b, 0, g)),
                   pl.BlockSpec((None, ctx_len, gw), lambda b, g: (b, 0, g))],
        out_shape=[jax.ShapeDtypeStruct((bsz, seq, MB_INNER), F32),
                   jax.ShapeDtypeStruct((bsz, ctx_len, MB_INNER), F32)],
        scratch_shapes=[pltpu.VMEM((tot, gw), BF16), pltpu.VMEM((tot, gw), BF16),
                        pltpu.VMEM((MB_STATE, gw), F32), pltpu.VMEM((MB_STATE, gw), F32)],
        compiler_params=pltpu.CompilerParams(dimension_semantics=("arbitrary", "arbitrary"),
                                             vmem_limit_bytes=VMEM_LIMIT_BYTES),
        name="ssd_scan",
    )(xbcx, xbcx, xbcx, yx, clx, rtx, xbcc, xbcc, xbcc, yc, clc, rtc,
      jnp.repeat(d_skip, MB_HEADDIM).reshape(1, MB_INNER), norm_g.reshape(1, MB_INNER))
    return ox @ w_out, oc @ w_out


HY_BLOCK = 256
HY_FREQ_ROWS = 64


def _split_bf16(a):
    hi = a.astype(BF16)
    return hi, (a - hi.astype(F32)).astype(BF16)


def _dot_split(w_hi, w_lo, x):
    x_hi, x_lo = _split_bf16(x)
    d = lambda a, b: jnp.dot(a, b, preferred_element_type=F32)
    return d(w_hi, x_hi) + d(w_hi, x_lo) + d(w_lo, x_hi)


def _hyena_dft_matrices():
    p = HY_BLOCK
    f = jnp.arange(p, dtype=F32)[:, None] + 0.5
    b = jnp.arange(p, dtype=F32)[None, :]
    k = jnp.round(f * b * 2.0).astype(jnp.int32) % (4 * p)
    ang = k.astype(F32) * (2.0 * math.pi / (4 * p))
    fwd = jnp.concatenate([jnp.cos(ang), -jnp.sin(ang)], axis=0)
    inv = jnp.concatenate([jnp.cos(ang).T, -jnp.sin(ang).T], axis=1) / p
    return fwd, inv


def _hyena_filter_spectra(hf, hb, fwd):
    length, d = hf.shape
    p = HY_BLOCK
    nb = length // p
    h2 = jnp.concatenate([jnp.zeros((1, d), F32), jnp.flip(hb[1:], axis=0), hf], axis=0)
    blocks = h2.reshape(2 * nb, p, d)
    spec = jnp.einsum('fj,kjd->kfd', fwd, blocks, precision=lax.Precision.HIGHEST)
    s_re, s_im = spec[:, :p], spec[:, p:]
    sign = jnp.where(jnp.arange(p) % 2 == 0, 1.0, -1.0).astype(F32)[None, :, None]
    return s_re[1:] - sign * s_im[:-1], s_im[1:] + sign * s_re[:-1]


def _hyena_conv_body(nb, u_ref, x_ref, bias_ref, gre_ref, gim_ref, wfh_ref, wfl_ref, wih_ref, wil_ref,
                     o_ref, ure_s, uim_s, y_s):
    p, fr = HY_BLOCK, HY_FREQ_ROWS

    def rows_of(i):
        return pl.ds(pl.multiple_of(i * p, p), p)

    def forward(j, carry):
        spec = _dot_split(wfh_ref[...], wfl_ref[...], u_ref[rows_of(j), :])
        ure_s[j] = spec[:p]
        uim_s[j] = spec[p:]
        return carry

    lax.fori_loop(0, nb, forward, 0)

    def out_block(i, carry):
        for t in range(p // fr):
            r = pl.ds(t * fr, fr)

            def acc_body(j, acc):
                a_re, a_im = acc
                k = i - j + nb - 1
                g_re, g_im = gre_ref[k, r, :], gim_ref[k, r, :]
                u_re, u_im = ure_s[j, r, :], uim_s[j, r, :]
                return a_re + g_re * u_re - g_im * u_im, a_im + g_re * u_im + g_im * u_re

            zero = jnp.zeros((fr, u_ref.shape[-1]), F32)
            a_re, a_im = lax.fori_loop(0, nb, acc_body, (zero, zero), unroll=min(nb, 4))
            y_s[pl.ds(t * fr, fr), :] = a_re
            y_s[pl.ds(p + t * fr, fr), :] = a_im
        y = _dot_split(wih_ref[...], wil_ref[...], y_s[...])
        u = u_ref[rows_of(i), :]
        o_ref[rows_of(i), :] = x_ref[rows_of(i), :] * (y + u * bias_ref[...])
        return carry

    lax.fori_loop(0, nb, out_block, 0)


def _hyena_conv(u_arr, u_blk, x_arr, x_blk, bias, g_re, g_im, mats):
    bsz, length, _ = u_arr.shape
    d = bias.shape[-1]
    p = HY_BLOCK
    nb = length // p
    nseg = 2 * nb - 1
    const = lambda shape: pl.BlockSpec(shape, lambda c, b: (0, 0))
    return pl.pallas_call(
        lambda *refs: _hyena_conv_body(nb, *refs),
        grid=(d // LANES, bsz),
        in_specs=[pl.BlockSpec((None, length, LANES), lambda c, b: (b, 0, u_blk + c)),
                  pl.BlockSpec((None, length, LANES), lambda c, b: (b, 0, x_blk + c)),
                  pl.BlockSpec((1, LANES), lambda c, b: (0, c)),
                  pl.BlockSpec((nseg, p, LANES), lambda c, b: (0, 0, c)),
                  pl.BlockSpec((nseg, p, LANES), lambda c, b: (0, 0, c)),
                  const((2 * p, p)), const((2 * p, p)), const((p, 2 * p)), const((p, 2 * p))],
        out_specs=pl.BlockSpec((None, length, LANES), lambda c, b: (b, 0, c)),
        out_shape=jax.ShapeDtypeStruct((bsz, length, d), F32),
        scratch_shapes=[pltpu.VMEM((nb, p, LANES), F32), pltpu.VMEM((nb, p, LANES), F32),
                        pltpu.VMEM((2 * p, LANES), F32)],
        compiler_params=pltpu.CompilerParams(dimension_semantics=("arbitrary", "arbitrary"),
                                             vmem_limit_bytes=VMEM_LIMIT_BYTES),
        name="hyena_conv",
    )(u_arr, x_arr, bias.reshape(1, d), g_re, g_im, *mats)


def hyena_mixer_pallas(hx, hc, w_in, b_in, short_w, short_b, f_w1, f_b1, f_w2, f_b2, f_w3, f_b3, f_w4,
                       f_freq, f_bias, w_out, b_out):
    fwd, inv = _hyena_dft_matrices()
    mats = _split_bf16(fwd) + _split_bf16(inv)
    nblk = D_MODEL // LANES

    def run(h):
        length = h.shape[1]
        u = dwconv_centered(h @ w_in + b_in, short_w, short_b).astype(F32)
        filt = hyena_filters(length, f_w1, f_b1, f_w2, f_b2, f_w3, f_b3, f_w4, f_freq)
        g0 = _hyena_filter_spectra(filt[:, 0, 0], filt[:, 0, 1], fwd)
        g1 = _hyena_filter_spectra(filt[:, 1, 0], filt[:, 1, 1], fwd)
        z = _hyena_conv(u, 2 * nblk, u, 0, f_bias[0], g0[0], g0[1], mats)
        z = _hyena_conv(z, 0, u, nblk, f_bias[1], g1[0], g1[1], mats)
        return z @ w_out + b_out

    return run(hx), run(hc)


def _final_norm_body(x_ref, g_ref, o_ref):
    xf = x_ref[...]
    o_ref[...] = xf * lax.rsqrt(jnp.mean(xf * xf, axis=-1, keepdims=True) + NORM_EPS) * g_ref[...]


def final_rmsnorm(x, g):
    b, l, d = x.shape
    rows = b * l
    tm = 512
    out = pl.pallas_call(
        _final_norm_body,
        grid=(rows // tm,),
        in_specs=[pl.BlockSpec((tm, d), lambda i: (i, 0)), pl.BlockSpec((1, d), lambda i: (0, 0))],
        out_specs=pl.BlockSpec((tm, d), lambda i: (i, 0)),
        out_shape=jax.ShapeDtypeStruct((rows, d), F32),
    )(x.reshape(rows, d), g.reshape(1, d))
    return out.reshape(b, l, d)


def kernel(x, c, ctx, c_ctx, ada_w, ada_b, norm1_g, norm2_g,
           hg_w_in, hg_lb, hg_onorm_g, hg_w_out,
           hy_w_in, hy_b_in, hy_short_w, hy_short_b, hy_f_w1, hy_f_b1, hy_f_w2, hy_f_b2,
           hy_f_w3, hy_f_b3, hy_f_w4, hy_f_freq, hy_f_bias, hy_w_out, hy_b_out,
           mb_w_in, mb_conv_w, mb_conv_b, mb_dt_bias, mb_a_log, mb_d, mb_norm_g, mb_w_out,
           gla_w_in, gla_gk_w2, gla_gk_b, gla_onorm_g, gla_w_out,
           router_w, moe_w_gate, moe_w_up, moe_w_down, final_norm_g):
    for i in range(DEPTH):
        last = i == DEPTH - 1
        sh1, sc1, g1, sh2, sc2, g2 = adaln(c, ada_w[i], ada_b[i])
        csh1, csc1, cg1, csh2, csc2, cg2 = adaln(c_ctx, ada_w[i], ada_b[i])
        hx = modulate(rmsnorm(x, norm1_g[i]), sh1, sc1)
        hc = modulate(rmsnorm(ctx, norm1_g[i]), csh1, csc1)
        kind, j = i % N_MIXERS, i // N_MIXERS
        if kind == 0:
            yx, yc = hgrn2_mixer_pallas(hx, hc, hg_w_in[j], hgrn_lower_bound(hg_lb, i), hg_onorm_g[j], hg_w_out[j])
        elif kind == 1:
            yx, yc = hyena_mixer_pallas(hx, hc, hy_w_in[j], hy_b_in[j], hy_short_w[j], hy_short_b[j],
                                 hy_f_w1[j], hy_f_b1[j], hy_f_w2[j], hy_f_b2[j], hy_f_w3[j], hy_f_b3[j],
                                 hy_f_w4[j], hy_f_freq[j], hy_f_bias[j], hy_w_out[j], hy_b_out[j])
        elif kind == 2:
            yx, yc = mamba2_mixer_pallas(hx, hc, mb_w_in[j], mb_conv_w[j], mb_conv_b[j], mb_dt_bias[j],
                                  mb_a_log[j], mb_d[j], mb_norm_g[j], mb_w_out[j])
        else:
            yx, yc = gla_mixer_pallas(hx, hc, gla_w_in[j], gla_gk_w2[j], gla_gk_b[j], gla_onorm_g[j], gla_w_out[j])
        x = x + g1 * yx
        x = x + g2 * expert_choice_ffn(modulate(rmsnorm(x, norm2_g[i]), sh2, sc2),
                                       router_w[i], moe_w_gate[i], moe_w_up[i], moe_w_down[i])
        if not last:
            ctx = ctx + cg1 * yc
            ctx = ctx + cg2 * expert_choice_ffn(modulate(rmsnorm(ctx, norm2_g[i]), csh2, csc2),
                                                router_w[i], moe_w_gate[i], moe_w_up[i], moe_w_down[i])
    return final_rmsnorm(x, final_norm_g)
```

```python
import math
import jax, jax.numpy as jnp
from jax import lax
from jax.experimental import pallas as pl
from jax.experimental.pallas import tpu as pltpu

D_MODEL = 1024
BATCH = 8
SEQ = 4096
DEPTH = 4

F32 = jnp.float32
GRID_W = 64
CTX_LEN = 256
N_MIXERS = 4
NORM_EPS = 1e-6

HG_HEADS = 8
HG_KEY = D_MODEL // HG_HEADS
HG_VAL = D_MODEL // HG_HEADS
HG_CHUNK = 32

HY_ORDER = 2
HY_SHORT = 3
HY_EMB = 33
HY_FILTER_W = 64
HY_DECAY_TARGET = 1e-2
HY_DECAY_HI_PCT = 0.3
HY_DECAY_LO_PCT = 1.5

MB_INNER = 2 * D_MODEL
MB_HEADDIM = 64
MB_HEADS = MB_INNER // MB_HEADDIM
MB_GROUPS = 8
MB_HPG = MB_HEADS // MB_GROUPS
MB_STATE = 128
MB_CONV = 5
MB_CHUNK = 64
MB_CONV_DIM = MB_INNER + 2 * MB_GROUPS * MB_STATE
MB_IN = MB_INNER + MB_CONV_DIM + 2 * MB_HEADS

GLA_HEADS = 4
GLA_KEY_DIM = D_MODEL // 2
GLA_VAL_DIM = D_MODEL
GLA_HK = GLA_KEY_DIM // GLA_HEADS
GLA_HV = GLA_VAL_DIM // GLA_HEADS
GLA_GATE_RANK = 16
GLA_GATE_NORM = 16.0
GLA_CHUNK = 32
GLA_IN = 2 * GLA_KEY_DIM + 2 * GLA_VAL_DIM + 2 * GLA_GATE_RANK

N_EXPERTS = 16
EC_CAPACITY_FACTOR = 2
EXPERT_FF = 2048


def rmsnorm(x, g):
    xf = x.astype(F32)
    return xf * lax.rsqrt(jnp.mean(xf * xf, axis=-1, keepdims=True) + NORM_EPS) * g


def adaln(cond, w, b):
    m = jax.nn.silu(cond.astype(F32)) @ w + b
    return jnp.split(m[..., None, :], 6, axis=-1)


def modulate(h, shift, scale):
    return h * (1.0 + scale) + shift


def heads(t, n):
    b, l, _ = t.shape
    return t.reshape(b, l, n, -1).transpose(0, 2, 1, 3)


def dwconv_centered(x, w, b):
    k_w = w.shape[0]
    pad = k_w // 2
    length = x.shape[1]
    xp = jnp.pad(x, ((0, 0), (pad, pad), (0, 0)))
    return sum(xp[:, j:j + length] * w[j] for j in range(k_w)) + b


def raster_to_colmajor(h, rows):
    b, l, d = h.shape
    return h.reshape(b, rows, GRID_W, d).transpose(0, 2, 1, 3).reshape(b, l, d)


def colmajor_to_raster(h, rows):
    b, l, d = h.shape
    return h.reshape(b, GRID_W, rows, d).transpose(0, 2, 1, 3).reshape(b, l, d)


def chunk_gla(q, k, v, g, s0, chunk):
    bsz, nh, length, kd = q.shape
    vd = v.shape[-1]
    n = length // chunk

    def split_chunks(t):
        return jnp.moveaxis(t.astype(F32).reshape(bsz, nh, n, chunk, t.shape[-1]), 2, 0)

    qs, ks, vs, gs = split_chunks(q), split_chunks(k), split_chunks(v), split_chunks(g)
    cum = jnp.cumsum(gs, axis=3)
    lower = jnp.tril(jnp.ones((chunk, chunk), bool))[:, :, None]

    def step(state, inp):
        qc, kc, vc, gc = inp
        rel = jnp.where(lower, gc[:, :, :, None, :] - gc[:, :, None, :, :], -jnp.inf)
        att = jnp.einsum('bhtk,bhsk,bhtsk->bhts', qc, kc, jnp.exp(rel))
        g_end = gc[:, :, -1:, :]
        o = att @ vc + jnp.einsum('bhtk,bhkv->bhtv', qc * jnp.exp(gc), state)
        state = jnp.exp(g_end[:, :, 0, :, None]) * state + jnp.einsum('bhsk,bhsv->bhkv', kc * jnp.exp(g_end - gc), vc)
        return state, o

    state, o = lax.scan(step, s0.astype(F32), (qs, ks, vs, cum))
    return jnp.moveaxis(o, 0, 2).reshape(bsz, nh, length, vd), state


def bidir_gla(lat, ctx, chunk):
    q, kf, kb, v, gf, gb = lat
    qc, kfc, kbc, vc, gfc, gbc = ctx
    s0 = jnp.zeros(q.shape[:2] + (q.shape[-1], v.shape[-1]), F32)
    oc_f, s_f = chunk_gla(qc, kfc, vc, gfc, s0, chunk)
    ox_f, _ = chunk_gla(q, kf, v, gf, s_f, chunk)
    r = lambda t: jnp.flip(t, axis=2)
    oc_b, s_b = chunk_gla(r(qc), r(kbc), r(vc), r(gbc), s0, chunk)
    ox_b, _ = chunk_gla(r(q), r(kb), r(v), r(gb), s_b, chunk)
    return ox_f + r(ox_b), oc_f + r(oc_b)


def gated_head_out(o, og, gain, w_out):
    b, nh, l, vd = o.shape
    o = rmsnorm(o.transpose(0, 2, 1, 3), gain).reshape(b, l, nh * vd)
    return (o * jax.nn.silu(og)) @ w_out


def hgrn_lower_bound(lb_param, layer):
    return jnp.cumsum(jax.nn.softmax(lb_param.astype(F32), axis=0), axis=0)[layer]


def hgrn2_mixer(hx, hc, w_in, lb, onorm_g, w_out):
    def project(h):
        q, ff, fb, inp, og = jnp.split(h @ w_in, 5, axis=-1)
        f_fwd = lb + (1.0 - lb) * jax.nn.sigmoid(ff.astype(F32))
        f_bwd = lb + (1.0 - lb) * jax.nn.sigmoid(fb.astype(F32))
        hd = lambda t: heads(t, HG_HEADS)
        streams = (hd(jax.nn.silu(q) * HG_KEY ** -0.5), hd(1.0 - f_fwd), hd(1.0 - f_bwd), hd(inp),
                   hd(jnp.log(f_fwd)), hd(jnp.log(f_bwd)))
        return streams, og

    lat, ogx = project(hx)
    ctxs, ogc = project(hc)
    ox, oc = bidir_gla(lat, ctxs, HG_CHUNK)
    return gated_head_out(ox, ogx, onorm_g, w_out), gated_head_out(oc, ogc, onorm_g, w_out)


def hyena_pos_features(length):
    t = jnp.linspace(0.0, 1.0, length, dtype=F32)[:, None]
    bands = (HY_EMB - 1) // 2
    f = jnp.linspace(1e-4, bands - 1, bands, dtype=F32)[None, :]
    w = 2.0 * math.pi * jnp.arange(length, dtype=F32)[:, None] / length
    return jnp.concatenate([t, jnp.cos(f * w), -jnp.sin(f * w)], axis=-1), t


def hyena_window(t):
    max_decay = math.log(HY_DECAY_TARGET) / HY_DECAY_HI_PCT
    min_decay = math.log(HY_DECAY_TARGET) / HY_DECAY_LO_PCT
    deltas = jnp.abs(jnp.linspace(min_decay, max_decay, D_MODEL, dtype=F32))
    return jnp.exp(-t * deltas[None, :])


def hyena_filters(length, w1, b1, w2, b2, w3, b3, w4, freq):
    z, t = hyena_pos_features(length)
    a = jnp.sin(freq * (z @ w1 + b1))
    a = jnp.sin(freq * (a @ w2 + b2))
    a = jnp.sin(freq * (a @ w3 + b3))
    h = (a @ w4).astype(F32).reshape(length, HY_ORDER, 2, D_MODEL) * hyena_window(t)[:, None, None, :]
    return h / jnp.sum(jnp.abs(h), axis=(0, 2), keepdims=True)


def twosided_fftconv(u, h_fwd, h_bwd):
    length = u.shape[1]
    k2 = jnp.concatenate([h_fwd, jnp.zeros_like(h_fwd[:1]), h_bwd[:0:-1]], axis=0)
    spec = jnp.fft.rfft(u, n=2 * length, axis=1) * jnp.fft.rfft(k2, axis=0)[None]
    return jnp.fft.irfft(spec, n=2 * length, axis=1)[:, :length]


def hyena_mixer(hx, hc, w_in, b_in, short_w, short_b, f_w1, f_b1, f_w2, f_b2, f_w3, f_b3, f_w4,
                f_freq, f_bias, w_out, b_out):
    def run(h):
        length = h.shape[1]
        u = dwconv_centered(h @ w_in + b_in, short_w, short_b).astype(F32)
        x1, x2, v = jnp.split(u, 3, axis=-1)
        filt = hyena_filters(length, f_w1, f_b1, f_w2, f_b2, f_w3, f_b3, f_w4, f_freq)
        z = v
        for o, gate in enumerate((x1, x2)):
            z = gate * (twosided_fftconv(z, filt[:, o, 0], filt[:, o, 1]) + z * f_bias[o])
        return z @ w_out + b_out

    return run(hx), run(hc)


def chunk_ssd(xs, bm, cm, dt, a, s0, chunk):
    bsz, length = xs.shape[:2]
    n = length // chunk

    def split_chunks(t):
        return jnp.moveaxis(t.astype(F32).reshape((bsz, n, chunk) + t.shape[2:]), 1, 0)

    xcs, bcs, ccs, dts = split_chunks(xs), split_chunks(bm), split_chunks(cm), split_chunks(dt)
    cum = jnp.cumsum(dts * a.astype(F32), axis=2)
    lower = jnp.tril(jnp.ones((chunk, chunk), bool))[:, :, None, None]

    def step(state, inp):
        xc, bc, cc, dtc, lc = inp
        seg = jnp.exp(jnp.where(lower, lc[:, :, None] - lc[:, None, :], -jnp.inf))
        w = jnp.einsum('btgn,bsgn->btsg', cc, bc)[..., None] * seg * dtc[:, None]
        y = jnp.einsum('btsgh,bsghp->btghp', w, xc)
        y = y + jnp.einsum('btgn,bghnp->btghp', cc, state) * jnp.exp(lc)[..., None]
        l_end = lc[:, -1]
        state = jnp.exp(l_end)[..., None, None] * state + jnp.einsum(
            'bsgn,bsgh,bsghp->bghnp', bc, jnp.exp(l_end[:, None] - lc) * dtc, xc)
        return state, y

    state, y = lax.scan(step, s0.astype(F32), (xcs, bcs, ccs, dts, cum))
    return jnp.moveaxis(y, 0, 1).reshape(xs.shape), state


def mamba2_mixer(hx, hc, w_in, conv_w, conv_b, dt_bias, a_log, d_skip, norm_g, w_out):
    a = -jnp.exp(a_log.astype(F32)).reshape(2, MB_GROUPS, MB_HPG)

    def project(h):
        bsz, length, _ = h.shape
        z, xbc, dt_raw = jnp.split(h @ w_in, [MB_INNER, MB_INNER + MB_CONV_DIM], axis=-1)
        xbc = jax.nn.silu(dwconv_centered(xbc, conv_w, conv_b))
        xs, bm, cm = jnp.split(xbc, [MB_INNER, MB_INNER + MB_GROUPS * MB_STATE], axis=-1)
        xs = xs.reshape(bsz, length, MB_GROUPS, MB_HPG, MB_HEADDIM)
        bm = bm.reshape(bsz, length, MB_GROUPS, MB_STATE)
        cm = cm.reshape(bsz, length, MB_GROUPS, MB_STATE)
        dt = jax.nn.softplus(dt_raw.astype(F32).reshape(bsz, length, 2, MB_GROUPS, MB_HPG)
                             + dt_bias.reshape(2, MB_GROUPS, MB_HPG))
        return z, xs, bm, cm, dt

    zx, xx, bx, cx, dtx = project(hx)
    zc, xc, bc, cc, dtc = project(hc)
    s0 = jnp.zeros((hx.shape[0], MB_GROUPS, MB_HPG, MB_STATE, MB_HEADDIM), F32)
    yc_f, s_f = chunk_ssd(xc, bc, cc, dtc[:, :, 0], a[0], s0, MB_CHUNK)
    yx_f, _ = chunk_ssd(xx, bx, cx, dtx[:, :, 0], a[0], s_f, MB_CHUNK)
    r = lambda t: jnp.flip(t, axis=1)
    yc_b, s_b = chunk_ssd(r(xc), r(bc), r(cc), r(dtc[:, :, 1]), a[1], s0, MB_CHUNK)
    yx_b, _ = chunk_ssd(r(xx), r(bx), r(cx), r(dtx[:, :, 1]), a[1], s_b, MB_CHUNK)

    def finish(y, xs, z):
        bsz, length = y.shape[:2]
        y = y + d_skip.reshape(MB_GROUPS, MB_HPG)[..., None] * xs
        y = y.reshape(bsz, length, MB_INNER) * jax.nn.silu(z)
        y = rmsnorm(y.reshape(bsz, length, MB_GROUPS, MB_INNER // MB_GROUPS),
                    norm_g.reshape(MB_GROUPS, MB_INNER // MB_GROUPS))
        return y.reshape(bsz, length, MB_INNER) @ w_out

    return finish(yx_f + r(yx_b), xx, zx), finish(yc_f + r(yc_b), xc, zc)


def gla_mixer(hx, hc, w_in, gk_w2, gk_b, onorm_g, w_out):
    rows = hx.shape[1] // GRID_W
    kd, vd, rk = GLA_KEY_DIM, GLA_VAL_DIM, GLA_GATE_RANK
    cuts = [kd, 2 * kd, 2 * kd + vd, 2 * kd + 2 * vd, 2 * kd + 2 * vd + rk]

    def project(h):
        q, k, v, og, rf, rb = jnp.split(h @ w_in, cuts, axis=-1)
        gf = jax.nn.log_sigmoid((rf @ gk_w2[0] + gk_b[0]).astype(F32)) / GLA_GATE_NORM
        gb = jax.nn.log_sigmoid((rb @ gk_w2[1] + gk_b[1]).astype(F32)) / GLA_GATE_NORM
        hd = lambda t: heads(t, GLA_HEADS)
        kh = hd(k)
        return (hd(q * GLA_HK ** -0.5), kh, kh, hd(v), hd(gf), hd(gb)), og

    lat, ogx = project(raster_to_colmajor(hx, rows))
    ctxs, ogc = project(hc)
    ox, oc = bidir_gla(lat, ctxs, GLA_CHUNK)
    yx = colmajor_to_raster(gated_head_out(ox, ogx, onorm_g, w_out), rows)
    return yx, gated_head_out(oc, ogc, onorm_g, w_out)


def expert_choice_ffn(h, router_w, w_gate, w_up, w_down):
    bsz, t_len, d = h.shape
    cap = max(1, EC_CAPACITY_FACTOR * t_len // N_EXPERTS)
    aff = jax.nn.softmax(h.astype(F32) @ router_w.astype(F32), axis=-1)
    gate, idx = lax.top_k(jnp.swapaxes(aff, 1, 2), cap)
    xg = jax.vmap(lambda hb, ib: hb[ib])(h, idx)
    hid = jax.nn.silu(jnp.einsum('becd,edf->becf', xg, w_gate)) * jnp.einsum('becd,edf->becf', xg, w_up)
    ye = jnp.einsum('becf,efd->becd', hid, w_down) * gate[..., None]
    return jax.vmap(lambda ib, yb: jnp.zeros((t_len, d), yb.dtype).at[ib.reshape(-1)].add(yb.reshape(-1, d)))(idx, ye)


BF16 = jnp.bfloat16
LANES = 128
VMEM_LIMIT_BYTES = 56 * 2 ** 20
SCAN_CHUNK = 64
SCAN_FAST_RANGE = 80.0
SCAN_UNROLL = 4


def _cumsum_rows(tri_bf16, g):
    hi = g.astype(BF16)
    r1 = g - hi.astype(F32)
    mid = r1.astype(BF16)
    lo = (r1 - mid.astype(F32)).astype(BF16)
    d = lambda a: jnp.dot(tri_bf16, a, preferred_element_type=F32)
    return d(hi) + d(mid) + d(lo)


def _log_sigmoid(z):
    return jnp.minimum(z, 0.0) - jnp.log1p(jnp.exp(-jnp.abs(z)))


def _dot_nt(a, b):
    return lax.dot_general(a, b, (((1,), (1,)), ((), ())), preferred_element_type=F32)


def _dot_tn(a, b):
    return lax.dot_general(a, b, (((0,), (0,)), ((), ())), preferred_element_type=F32)


def _gla_scan_body(mode, seq, ctx_len, kdim, vdim, *refs):
    chunk = SCAN_CHUNK
    if mode == 'hgrn2':
        (q_x, ff_x, fb_x, v_x, og_x, q_c, ff_c, fb_c, v_c, og_c, lb_ref, gain_ref,
         o_x, o_c, cf_s, cb_s, sf_s, sb_s, kt_s, gt_s, qf_s, qb_s, of_s, ob_s, kf_s, kb_s) = refs
    else:
        (q_x, k_x, v_x, og_x, r_x, q_c, k_c, v_c, og_c, r_c, w2f_ref, w2b_ref, bf_ref, bb_ref, gain_ref,
         o_x, o_c, cf_s, cb_s, sf_s, sb_s, kt_s, gt_s, qf_s, qb_s, of_s, ob_s) = refs
    if mode == 'hgrn2':
        segs = (((ff_c, fb_c), q_c, None, v_c, og_c, o_c, ctx_len, 0),
                ((ff_x, fb_x), q_x, None, v_x, og_x, o_x, seq, ctx_len))
    else:
        segs = (((r_c,), q_c, k_c, v_c, og_c, o_c, ctx_len, 0),
                ((r_x,), q_x, k_x, v_x, og_x, o_x, seq, ctx_len))

    row = lax.broadcasted_iota(jnp.int32, (chunk, chunk), 0)
    col = lax.broadcasted_iota(jnp.int32, (chunk, chunk), 1)
    masks = (row >= col, row <= col)
    tris = tuple(m.astype(BF16) for m in masks)
    cum_s, state_s, qin_s, kout_s = (cf_s, cb_s), (sf_s, sb_s), (qf_s, qb_s), (of_s, ob_s)
    end_row = (chunk - 1, 0)

    def rows_of(i, off=0):
        return pl.ds(pl.multiple_of(off + i * chunk, chunk), chunk)

    def gate_pass(seg, bound):
        gsrc, _, _, _, _, _, length, off = seg

        def body(i, bound):
            rows, srows = rows_of(i), rows_of(i, off)
            if mode == 'hgrn2':
                lb = lb_ref[...]
                f_f = lb + (1.0 - lb) * jax.nn.sigmoid(gsrc[0][rows, :])
                f_b = lb + (1.0 - lb) * jax.nn.sigmoid(gsrc[1][rows, :])
                g_f, g_b = jnp.log(f_f), jnp.log(f_b)
                kf_s[srows, :] = 1.0 - f_f
                kb_s[srows, :] = 1.0 - f_b
            else:
                r = gsrc[0][rows, :].astype(BF16)
                z_f = jnp.dot(r, w2f_ref[...].astype(BF16), preferred_element_type=F32) + bf_ref[...]
                z_b = jnp.dot(r, w2b_ref[...].astype(BF16), preferred_element_type=F32) + bb_ref[...]
                g_f = _log_sigmoid(z_f) / GLA_GATE_NORM
                g_b = _log_sigmoid(z_b) / GLA_GATE_NORM
            c_f = _cumsum_rows(tris[0], g_f)
            c_b = _cumsum_rows(tris[1], g_b)
            cf_s[srows, :] = c_f
            cb_s[srows, :] = c_b
            tot = jnp.maximum(-c_f[end_row[0]:end_row[0] + 1, :], -c_b[end_row[1]:end_row[1] + 1, :])
            return jnp.maximum(bound, tot)

        return lax.fori_loop(0, length // chunk, body, bound, unroll=SCAN_UNROLL)

    def intra(seg, direction, i, exact):
        _, q_r, k_r, v_r, _, _, _, off = seg
        rows, srows = rows_of(i), rows_of(i, off)
        if mode == 'hgrn2':
            qq = jax.nn.silu(q_r[rows, :]) * HG_KEY ** -0.5
            k = (kf_s if direction == 0 else kb_s)[srows, :]
        else:
            qq = q_r[rows, :] * GLA_HK ** -0.5
            k = k_r[rows, :]
        cum = cum_s[direction][srows, :]
        e = end_row[direction]
        cum_end = cum[e:e + 1, :]
        q_in = (qq * jnp.exp(cum)).astype(BF16)
        qin_s[direction][srows, :] = q_in
        kout_s[direction][srows, :] = (k * jnp.exp(cum_end - cum)).astype(BF16)
        if exact:
            kt_s[...] = k
            gt_s[...] = cum

            def col_body(s, att):
                ks = kt_s[pl.ds(s, 1), :]
                gs = gt_s[pl.ds(s, 1), :]
                p = qq * ks * jnp.exp(jnp.minimum(cum - gs, 0.0))
                return att + jnp.where(col == s, jnp.sum(p, axis=-1, keepdims=True), 0.0)

            att = lax.fori_loop(0, chunk, col_body, jnp.zeros((chunk, chunk), F32))
        else:
            att = _dot_nt(q_in, (k * jnp.exp(-cum)).astype(BF16))
        att = jnp.where(masks[direction], att, 0.0).astype(BF16)
        return jnp.dot(att, v_r[rows, :].astype(BF16), preferred_element_type=F32)

    def intra_pass(exact):
        for seg in segs:
            o_r, length = seg[5], seg[6]

            def body(i, carry):
                o_r[rows_of(i), :] = intra(seg, 0, i, exact) + intra(seg, 1, i, exact)
                return carry

            lax.fori_loop(0, length // chunk, body, 0, unroll=1 if exact else SCAN_UNROLL)

    def inter(seg, direction, i):
        v_r, off = seg[3], seg[7]
        rows, srows = rows_of(i), rows_of(i, off)
        s_ref = state_s[direction]
        state = s_ref[...]
        o = _dot_nt(qin_s[direction][srows, :], state.astype(BF16))
        e = pl.ds(pl.multiple_of(off + i * chunk, chunk) + end_row[direction], 1)
        decay = jnp.exp(cum_s[direction][e, :])
        s_ref[...] = state * decay + _dot_tn(v_r[rows, :].astype(BF16), kout_s[direction][srows, :])
        return o

    def finish(seg, i, o):
        o = o * lax.rsqrt(jnp.mean(o * o, axis=-1, keepdims=True) + NORM_EPS) * gain_ref[...]
        return o * jax.nn.silu(seg[4][rows_of(i), :])

    def inter_pass():
        sf_s[...] = jnp.zeros_like(sf_s)
        sb_s[...] = jnp.zeros_like(sb_s)
        for seg in segs:
            o_r, length = seg[5], seg[6]
            n = length // chunk

            def first_half(i, carry):
                j = n - 1 - i
                o_r[rows_of(i), :] += inter(seg, 0, i)
                o_r[rows_of(j), :] += inter(seg, 1, j)
                return carry

            def second_half(i, carry):
                j = n - 1 - i
                o_r[rows_of(i), :] = finish(seg, i, o_r[rows_of(i), :] + inter(seg, 0, i))
                o_r[rows_of(j), :] = finish(seg, j, o_r[rows_of(j), :] + inter(seg, 1, j))
                return carry

            lax.fori_loop(0, n // 2, first_half, 0, unroll=min(SCAN_UNROLL, n // 2))
            lax.fori_loop(n // 2, n, second_half, 0, unroll=min(SCAN_UNROLL, n // 2))

    bound = jnp.zeros((1, kdim), F32)
    for seg in segs:
        bound = gate_pass(seg, bound)
    fast = jnp.max(bound) <= SCAN_FAST_RANGE

    @pl.when(fast)
    def _():
        intra_pass(False)

    @pl.when(jnp.logical_not(fast))
    def _():
        intra_pass(True)

    inter_pass()


def _gla_scan_call(mode, nheads, kdim, vdim, lat_in, ctx_in, small_in, bsz, seq, ctx_len):
    def stream_spec(length, width, first, stride):
        return pl.BlockSpec((None, length, width), lambda b, h: (b, 0, first + stride * h))

    in_specs = [stream_spec(seq, w, f, s) for _, w, f, s in lat_in]
    in_specs += [stream_spec(ctx_len, w, f, s) for _, w, f, s in ctx_in]
    in_specs += [pl.BlockSpec(blk, imap) for _, blk, imap in small_in]
    args = [a for a, _, _, _ in lat_in] + [a for a, _, _, _ in ctx_in] + [a for a, _, _ in small_in]
    tot = seq + ctx_len
    scratch = [pltpu.VMEM((tot, kdim), F32), pltpu.VMEM((tot, kdim), F32),
               pltpu.VMEM((vdim, kdim), F32), pltpu.VMEM((vdim, kdim), F32),
               pltpu.VMEM((SCAN_CHUNK, kdim), F32), pltpu.VMEM((SCAN_CHUNK, kdim), F32),
               pltpu.VMEM((tot, kdim), BF16), pltpu.VMEM((tot, kdim), BF16),
               pltpu.VMEM((tot, kdim), BF16), pltpu.VMEM((tot, kdim), BF16)]
    if mode == 'hgrn2':
        scratch += [pltpu.VMEM((tot, kdim), F32), pltpu.VMEM((tot, kdim), F32)]
    return pl.pallas_call(
        lambda *refs: _gla_scan_body(mode, seq, ctx_len, kdim, vdim, *refs),
        grid=(bsz, nheads),
        in_specs=in_specs,
        out_specs=[pl.BlockSpec((None, seq, vdim), lambda b, h: (b, 0, h)),
                   pl.BlockSpec((None, ctx_len, vdim), lambda b, h: (b, 0, h))],
        out_shape=[jax.ShapeDtypeStruct((bsz, seq, nheads * vdim), F32),
                   jax.ShapeDtypeStruct((bsz, ctx_len, nheads * vdim), F32)],
        scratch_shapes=scratch,
        compiler_params=pltpu.CompilerParams(dimension_semantics=("arbitrary", "arbitrary"),
                                             vmem_limit_bytes=VMEM_LIMIT_BYTES),
        name=f"{mode}_scan",
    )(*args)


def hgrn2_mixer_pallas(hx, hc, w_in, lb, onorm_g, w_out):
    bsz, seq, _ = hx.shape
    ctx_len = hc.shape[1]
    yx, yc = hx @ w_in, hc @ w_in
    nh = HG_HEADS
    streams = lambda y: [(y, HG_KEY, j * nh, 1) for j in range(5)]
    small = [(lb.reshape(1, D_MODEL), (1, HG_KEY), lambda b, h: (0, h)),
             (onorm_g.reshape(1, HG_VAL), (1, HG_VAL), lambda b, h: (0, 0))]
    ox, oc = _gla_scan_call('hgrn2', nh, HG_KEY, HG_VAL, streams(yx), streams(yc), small, bsz, seq, ctx_len)
    return ox @ w_out, oc @ w_out


def gla_mixer_pallas(hx, hc, w_in, gk_w2, gk_b, onorm_g, w_out):
    bsz, seq, _ = hx.shape
    ctx_len = hc.shape[1]
    rows = seq // GRID_W
    kd, vd, rk = GLA_KEY_DIM, GLA_VAL_DIM, GLA_GATE_RANK
    nh = GLA_HEADS
    w_in_p = jnp.pad(w_in, ((0, 0), (0, LANES - 2 * rk)))
    yx = raster_to_colmajor(hx, rows) @ w_in_p
    yc = hc @ w_in_p
    w2f = jnp.pad(gk_w2[0], ((0, LANES - rk), (0, 0)))
    w2b = jnp.pad(gk_w2[1], ((rk, LANES - 2 * rk), (0, 0)))
    streams = lambda y: [(y, GLA_HK, 0, 1), (y, GLA_HK, kd // GLA_HK, 1), (y, GLA_HV, 2 * kd // GLA_HV, 1),
                         (y, GLA_HV, (2 * kd + vd) // GLA_HV, 1), (y, LANES, (2 * kd + 2 * vd) // LANES, 0)]
    small = [(w2f, (LANES, GLA_HK), lambda b, h: (0, h)), (w2b, (LANES, GLA_HK), lambda b, h: (0, h)),
             (gk_b[0].reshape(1, kd), (1, GLA_HK), lambda b, h: (0, h)),
             (gk_b[1].reshape(1, kd), (1, GLA_HK), lambda b, h: (0, h)),
             (onorm_g.reshape(1, GLA_HV), (1, GLA_HV), lambda b, h: (0, 0))]
    ox, oc = _gla_scan_call('gla', nh, GLA_HK, GLA_HV, streams(yx), streams(yc), small, bsz, seq, ctx_len)
    return colmajor_to_raster(ox @ w_out, rows), oc @ w_out


MB_GROUP_W = MB_INNER // MB_GROUPS


def _ssd_scan_body(seq, ctx_len, *refs):
    chunk = MB_CHUNK
    gw, hpg = MB_GROUP_W, MB_HPG
    (x_x, b_x, c_x, z_x, cl_x, rt_x, x_c, b_c, c_c, z_c, cl_c, rt_c, dskip_ref, gain_ref,
     o_x, o_c, xf_s, xb_s, sf_s, sb_s) = refs
    segs = ((x_c, b_c, c_c, z_c, cl_c, rt_c, o_c, ctx_len, 0),
            (x_x, b_x, c_x, z_x, cl_x, rt_x, o_x, seq, ctx_len))
    xs_s, state_s = (xf_s, xb_s), (sf_s, sb_s)
    end_row = (chunk - 1, 0)

    row = lax.broadcasted_iota(jnp.int32, (chunk, chunk), 0)
    col = lax.broadcasted_iota(jnp.int32, (chunk, chunk), 1)
    masks = (row >= col, row <= col)
    head_of_lane = lax.broadcasted_iota(jnp.int32, (1, gw), 1) // MB_HEADDIM

    def rows_of(i, off=0):
        return pl.ds(pl.multiple_of(off + i * chunk, chunk), chunk)

    def bcast_heads(c4):
        out = c4[:, hpg - 1:hpg]
        for hh in range(hpg - 2, -1, -1):
            out = jnp.where(head_of_lane == hh, c4[:, hh:hh + 1], out)
        return out

    def intra_pass():
        for seg in segs:
            x_r, b_r, c_r, _, cl_r, rt_r, o_r, length, off = seg

            def body(i, carry):
                rows, srows = rows_of(i), rows_of(i, off)
                cols, rt = cl_r[i], rt_r[i]
                x = x_r[rows, :]
                cb = _dot_nt(c_r[rows, :].astype(BF16), b_r[rows, :].astype(BF16))
                xh = [jnp.where(head_of_lane == hh, x, 0.0).astype(BF16) for hh in range(hpg)]
                y = jnp.zeros((chunk, gw), F32)
                for d in range(2):
                    cum = cols[:, d * hpg:(d + 1) * hpg]
                    dt = cols[:, (2 + d) * hpg:(3 + d) * hpg]
                    e = end_row[d]
                    scale = jnp.exp(cum[e:e + 1, :] - cum) * dt
                    xs_s[d][srows, :] = (x * bcast_heads(scale)).astype(BF16)
                    for hh in range(hpg):
                        j = d * hpg + hh
                        seg_decay = jnp.exp(jnp.minimum(cols[:, j:j + 1] - rt[j:j + 1, :], 0.0))
                        w = jnp.where(masks[d], cb * seg_decay * rt[2 * hpg + j:2 * hpg + j + 1, :], 0.0)
                        y = y + jnp.dot(w.astype(BF16), xh[hh], preferred_element_type=F32)
                o_r[rows, :] = y
                return carry

            lax.fori_loop(0, length // chunk, body, 0)

    def inter(seg, d, i):
        _, b_r, c_r, _, cl_r, _, _, _, off = seg
        rows, srows = rows_of(i), rows_of(i, off)
        cum = cl_r[i][:, d * hpg:(d + 1) * hpg]
        state = state_s[d][...]
        y = jnp.dot(c_r[rows, :].astype(BF16), state.astype(BF16), preferred_element_type=F32)
        e = end_row[d]
        decay = bcast_heads(jnp.exp(cum[e:e + 1, :]))
        state_s[d][...] = state * decay + _dot_tn(b_r[rows, :].astype(BF16), xs_s[d][srows, :])
        return y * bcast_heads(jnp.exp(cum))

    def finish(seg, i, y):
        x_r, z_r = seg[0], seg[3]
        rows = rows_of(i)
        y = (y + dskip_ref[...] * x_r[rows, :]) * jax.nn.silu(z_r[rows, :])
        return y * lax.rsqrt(jnp.mean(y * y, axis=-1, keepdims=True) + NORM_EPS) * gain_ref[...]

    def inter_pass():
        sf_s[...] = jnp.zeros_like(sf_s)
        sb_s[...] = jnp.zeros_like(sb_s)
        for seg in segs:
            o_r, length = seg[6], seg[7]
            n = length // chunk

            def first_half(i, carry):
                j = n - 1 - i
                o_r[rows_of(i), :] += inter(seg, 0, i)
                o_r[rows_of(j), :] += inter(seg, 1, j)
                return carry

            def second_half(i, carry):
                j = n - 1 - i
                o_r[rows_of(i), :] = finish(seg, i, o_r[rows_of(i), :] + inter(seg, 0, i))
                o_r[rows_of(j), :] = finish(seg, j, o_r[rows_of(j), :] + inter(seg, 1, j))
                return carry

            lax.fori_loop(0, n // 2, first_half, 0, unroll=2)
            lax.fori_loop(n // 2, n, second_half, 0, unroll=2)

    intra_pass()
    inter_pass()


def _ssd_head_tables(dt_raw, dt_bias, a_log):
    bsz, length, _ = dt_raw.shape
    n = length // MB_CHUNK
    a = -jnp.exp(a_log.astype(F32)).reshape(2, MB_GROUPS, MB_HPG)
    dt = jax.nn.softplus(dt_raw.astype(F32).reshape(bsz, length, 2, MB_GROUPS, MB_HPG)
                         + dt_bias.reshape(2, MB_GROUPS, MB_HPG))
    dta = (dt * a).reshape(bsz, n, MB_CHUNK, 2, MB_GROUPS, MB_HPG)
    dtc = dt.reshape(bsz, n, MB_CHUNK, 2, MB_GROUPS, MB_HPG)
    cum_f = jnp.cumsum(dta[:, :, :, 0], axis=2)
    cum_b = jnp.flip(jnp.cumsum(jnp.flip(dta[:, :, :, 1], axis=2), axis=2), axis=2)
    cols = jnp.concatenate([cum_f, cum_b, dtc[:, :, :, 0], dtc[:, :, :, 1]], axis=-1)
    cols = cols.transpose(0, 3, 1, 2, 4)
    return cols, jnp.swapaxes(cols, -1, -2)


def mamba2_mixer_pallas(hx, hc, w_in, conv_w, conv_b, dt_bias, a_log, d_skip, norm_g, w_out):
    bsz, seq, _ = hx.shape
    ctx_len = hc.shape[1]
    chunk, gw = MB_CHUNK, MB_GROUP_W

    def project(h):
        y = h @ w_in
        xbc = jax.nn.silu(dwconv_centered(y[..., MB_INNER:MB_INNER + MB_CONV_DIM], conv_w, conv_b))
        cols, rows_t = _ssd_head_tables(y[..., MB_INNER + MB_CONV_DIM:], dt_bias, a_log)
        return y, xbc, cols, rows_t

    yx, xbcx, clx, rtx = project(hx)
    yc, xbcc, clc, rtc = project(hc)

    def stream_specs(length):
        n = length // chunk
        bw = MB_INNER // MB_STATE
        return [pl.BlockSpec((None, length, gw), lambda b, g: (b, 0, g)),
                pl.BlockSpec((None, length, MB_STATE), lambda b, g: (b, 0, bw + g)),
                pl.BlockSpec((None, length, MB_STATE), lambda b, g: (b, 0, bw + MB_GROUPS + g)),
                pl.BlockSpec((None, length, gw), lambda b, g: (b, 0, g)),
                pl.BlockSpec((None, None, n, chunk, 4 * MB_HPG), lambda b, g: (b, g, 0, 0, 0)),
                pl.BlockSpec((None, None, n, 4 * MB_HPG, chunk), lambda b, g: (b, g, 0, 0, 0))]

    tot = seq + ctx_len
    ox, oc = pl.pallas_call(
        lambda *refs: _ssd_scan_body(seq, ctx_len, *refs),
        grid=(bsz, MB_GROUPS),
        in_specs=stream_specs(seq) + stream_specs(ctx_len) + [
            pl.BlockSpec((1, gw), lambda b, g: (0, g)), pl.BlockSpec((1, gw), lambda b, g: (0, g))],
        out_specs=[pl.BlockSpec((None, seq, gw), lambda b, g: (b, 0, g)),
                   pl.BlockSpec((None, ctx_len, gw), lambda b, g: (b, 0, g))],
        out_shape=[jax.ShapeDtypeStruct((bsz, seq, MB_INNER), F32),
                   jax.ShapeDtypeStruct((bsz, ctx_len, MB_INNER), F32)],
        scratch_shapes=[pltpu.VMEM((tot, gw), BF16), pltpu.VMEM((tot, gw), BF16),
                        pltpu.VMEM((MB_STATE, gw), F32), pltpu.VMEM((MB_STATE, gw), F32)],
        compiler_params=pltpu.CompilerParams(dimension_semantics=("arbitrary", "arbitrary"),
                                             vmem_limit_bytes=VMEM_LIMIT_BYTES),
        name="ssd_scan",
    )(xbcx, xbcx, xbcx, yx, clx, rtx, xbcc, xbcc, xbcc, yc, clc, rtc,
      jnp.repeat(d_skip, MB_HEADDIM).reshape(1, MB_INNER), norm_g.reshape(1, MB_INNER))
    return ox @ w_out, oc @ w_out


HY_BLOCK = 256
HY_FREQ_ROWS = 64


def _split_bf16(a):
    hi = a.astype(BF16)
    return hi, (a - hi.astype(F32)).astype(BF16)


def _dot_split(w_hi, w_lo, x):
    x_hi, x_lo = _split_bf16(x)
    d = lambda a, b: jnp.dot(a, b, preferred_element_type=F32)
    return d(w_hi, x_hi) + d(w_hi, x_lo) + d(w_lo, x_hi)


def _hyena_dft_matrices():
    p = HY_BLOCK
    f = jnp.arange(p, dtype=F32)[:, None] + 0.5
    b = jnp.arange(p, dtype=F32)[None, :]
    k = jnp.round(f * b * 2.0).astype(jnp.int32) % (4 * p)
    ang = k.astype(F32) * (2.0 * math.pi / (4 * p))
    fwd = jnp.concatenate([jnp.cos(ang), -jnp.sin(ang)], axis=0)
    inv = jnp.concatenate([jnp.cos(ang).T, -jnp.sin(ang).T], axis=1) / p
    return fwd, inv


def _hyena_filter_spectra(hf, hb, fwd):
    length, d = hf.shape
    p = HY_BLOCK
    nb = length // p
    h2 = jnp.concatenate([jnp.zeros((1, d), F32), jnp.flip(hb[1:], axis=0), hf], axis=0)
    blocks = h2.reshape(2 * nb, p, d)
    spec = jnp.einsum('fj,kjd->kfd', fwd, blocks, precision=lax.Precision.HIGHEST)
    s_re, s_im = spec[:, :p], spec[:, p:]
    sign = jnp.where(jnp.arange(p) % 2 == 0, 1.0, -1.0).astype(F32)[None, :, None]
    return s_re[1:] - sign * s_im[:-1], s_im[1:] + sign * s_re[:-1]


def _hyena_conv_body(nb, u_ref, x_ref, bias_ref, gre_ref, gim_ref, wfh_ref, wfl_ref, wih_ref, wil_ref,
                     o_ref, ure_s, uim_s, y_s):
    p, fr = HY_BLOCK, HY_FREQ_ROWS

    def rows_of(i):
        return pl.ds(pl.multiple_of(i * p, p), p)

    def forward(j, carry):
        spec = _dot_split(wfh_ref[...], wfl_ref[...], u_ref[rows_of(j), :])
        ure_s[j] = spec[:p]
        uim_s[j] = spec[p:]
        return carry

    lax.fori_loop(0, nb, forward, 0)

    def out_block(i, carry):
        for t in range(p // fr):
            r = pl.ds(t * fr, fr)

            def acc_body(j, acc):
                a_re, a_im = acc
                k = i - j + nb - 1
                g_re, g_im = gre_ref[k, r, :], gim_ref[k, r, :]
                u_re, u_im = ure_s[j, r, :], uim_s[j, r, :]
                return a_re + g_re * u_re - g_im * u_im, a_im + g_re * u_im + g_im * u_re

            zero = jnp.zeros((fr, u_ref.shape[-1]), F32)
            a_re, a_im = lax.fori_loop(0, nb, acc_body, (zero, zero), unroll=min(nb, 4))
            y_s[pl.ds(t * fr, fr), :] = a_re
            y_s[pl.ds(p + t * fr, fr), :] = a_im
        y = _dot_split(wih_ref[...], wil_ref[...], y_s[...])
        u = u_ref[rows_of(i), :]
        o_ref[rows_of(i), :] = x_ref[rows_of(i), :] * (y + u * bias_ref[...])
        return carry

    lax.fori_loop(0, nb, out_block, 0)


def _hyena_conv(u_arr, u_blk, x_arr, x_blk, bias, g_re, g_im, mats):
    bsz, length, _ = u_arr.shape
    d = bias.shape[-1]
    p = HY_BLOCK
    nb = length // p
    nseg = 2 * nb - 1
    const = lambda shape: pl.BlockSpec(shape, lambda c, b: (0, 0))
    return pl.pallas_call(
        lambda *refs: _hyena_conv_body(nb, *refs),
        grid=(d // LANES, bsz),
        in_specs=[pl.BlockSpec((None, length, LANES), lambda c, b: (b, 0, u_blk + c)),
                  pl.BlockSpec((None, length, LANES), lambda c, b: (b, 0, x_blk + c)),
                  pl.BlockSpec((1, LANES), lambda c, b: (0, c)),
                  pl.BlockSpec((nseg, p, LANES), lambda c, b: (0, 0, c)),
                  pl.BlockSpec((nseg, p, LANES), lambda c, b: (0, 0, c)),
                  const((2 * p, p)), const((2 * p, p)), const((p, 2 * p)), const((p, 2 * p))],
        out_specs=pl.BlockSpec((None, length, LANES), lambda c, b: (b, 0, c)),
        out_shape=jax.ShapeDtypeStruct((bsz, length, d), F32),
        scratch_shapes=[pltpu.VMEM((nb, p, LANES), F32), pltpu.VMEM((nb, p, LANES), F32),
                        pltpu.VMEM((2 * p, LANES), F32)],
        compiler_params=pltpu.CompilerParams(dimension_semantics=("arbitrary", "arbitrary"),
                                             vmem_limit_bytes=VMEM_LIMIT_BYTES),
        name="hyena_conv",
    )(u_arr, x_arr, bias.reshape(1, d), g_re, g_im, *mats)


def hyena_mixer_pallas(hx, hc, w_in, b_in, short_w, short_b, f_w1, f_b1, f_w2, f_b2, f_w3, f_b3, f_w4,
                       f_freq, f_bias, w_out, b_out):
    fwd, inv = _hyena_dft_matrices()
    mats = _split_bf16(fwd) + _split_bf16(inv)
    nblk = D_MODEL // LANES

    def run(h):
        length = h.shape[1]
        u = dwconv_centered(h @ w_in + b_in, short_w, short_b).astype(F32)
        filt = hyena_filters(length, f_w1, f_b1, f_w2, f_b2, f_w3, f_b3, f_w4, f_freq)
        g0 = _hyena_filter_spectra(filt[:, 0, 0], filt[:, 0, 1], fwd)
        g1 = _hyena_filter_spectra(filt[:, 1, 0], filt[:, 1, 1], fwd)
        z = _hyena_conv(u, 2 * nblk, u, 0, f_bias[0], g0[0], g0[1], mats)
        z = _hyena_conv(z, 0, u, nblk, f_bias[1], g1[0], g1[1], mats)
        return z @ w_out + b_out

    return run(hx), run(hc)


MOE_ROW_TILE = 1024
MOE_COL_TILE = 512


def _moe_combine_body(t_len, idx_ref, gate_ref, ye_ref, x_ref, g2_ref, o_ref):
    e = pl.program_id(2)
    cap = idx_ref.shape[-1]
    rt = min(MOE_ROW_TILE, t_len)

    @pl.when(e == 0)
    def _():
        o_ref[...] = jnp.zeros_like(o_ref)

    yg = (ye_ref[...] * gate_ref[...]).astype(BF16)
    idx = idx_ref[...]

    def body(r, carry):
        r0 = pl.multiple_of(r * rt, rt)
        tok = lax.broadcasted_iota(jnp.int32, (rt, cap), 0) + r0
        onehot = jnp.where(tok == idx, 1.0, 0.0).astype(BF16)
        o_ref[pl.ds(r0, rt), :] += jnp.dot(onehot, yg, preferred_element_type=F32)
        return carry

    lax.fori_loop(0, t_len // rt, body, 0)

    @pl.when(e == pl.num_programs(2) - 1)
    def _():
        o_ref[...] = x_ref[...] + g2_ref[...] * o_ref[...]


def expert_choice_ffn_residual(x, g2, h, router_w, w_gate, w_up, w_down):
    bsz, t_len, d = h.shape
    cap = max(1, EC_CAPACITY_FACTOR * t_len // N_EXPERTS)
    aff = jax.nn.softmax(h.astype(F32) @ router_w.astype(F32), axis=-1)
    gate, idx = lax.top_k(jnp.swapaxes(aff, 1, 2), cap)
    xg = jax.vmap(lambda hb, ib: hb[ib])(h, idx)
    hid = jax.nn.silu(jnp.einsum('becd,edf->becf', xg, w_gate)) * jnp.einsum('becd,edf->becf', xg, w_up)
    ye = jnp.einsum('becf,efd->becd', hid, w_down)
    ct = MOE_COL_TILE
    g2 = jnp.broadcast_to(g2.reshape(-1, 1, d), (bsz, 1, d))
    return pl.pallas_call(
        lambda *refs: _moe_combine_body(t_len, *refs),
        grid=(bsz, d // ct, N_EXPERTS),
        in_specs=[pl.BlockSpec((None, None, 1, cap), lambda b, c, e: (b, e, 0, 0)),
                  pl.BlockSpec((None, None, cap, 1), lambda b, c, e: (b, e, 0, 0)),
                  pl.BlockSpec((None, None, cap, ct), lambda b, c, e: (b, e, 0, c)),
                  pl.BlockSpec((None, t_len, ct), lambda b, c, e: (b, 0, c)),
                  pl.BlockSpec((None, 1, ct), lambda b, c, e: (b, 0, c))],
        out_specs=pl.BlockSpec((None, t_len, ct), lambda b, c, e: (b, 0, c)),
        out_shape=jax.ShapeDtypeStruct((bsz, t_len, d), F32),
        compiler_params=pltpu.CompilerParams(dimension_semantics=("arbitrary", "arbitrary", "arbitrary"),
                                             vmem_limit_bytes=VMEM_LIMIT_BYTES),
        name="moe_combine",
    )(idx.reshape(bsz, N_EXPERTS, 1, cap), gate.reshape(bsz, N_EXPERTS, cap, 1), ye, x, g2)


def _final_norm_body(x_ref, g_ref, o_ref):
    xf = x_ref[...]
    o_ref[...] = xf * lax.rsqrt(jnp.mean(xf * xf, axis=-1, keepdims=True) + NORM_EPS) * g_ref[...]


def final_rmsnorm(x, g):
    b, l, d = x.shape
    rows = b * l
    tm = 512
    out = pl.pallas_call(
        _final_norm_body,
        grid=(rows // tm,),
        in_specs=[pl.BlockSpec((tm, d), lambda i: (i, 0)), pl.BlockSpec((1, d), lambda i: (0, 0))],
        out_specs=pl.BlockSpec((tm, d), lambda i: (i, 0)),
        out_shape=jax.ShapeDtypeStruct((rows, d), F32),
    )(x.reshape(rows, d), g.reshape(1, d))
    return out.reshape(b, l, d)


def kernel(x, c, ctx, c_ctx, ada_w, ada_b, norm1_g, norm2_g,
           hg_w_in, hg_lb, hg_onorm_g, hg_w_out,
           hy_w_in, hy_b_in, hy_short_w, hy_short_b, hy_f_w1, hy_f_b1, hy_f_w2, hy_f_b2,
           hy_f_w3, hy_f_b3, hy_f_w4, hy_f_freq, hy_f_bias, hy_w_out, hy_b_out,
           mb_w_in, mb_conv_w, mb_conv_b, mb_dt_bias, mb_a_log, mb_d, mb_norm_g, mb_w_out,
           gla_w_in, gla_gk_w2, gla_gk_b, gla_onorm_g, gla_w_out,
           router_w, moe_w_gate, moe_w_up, moe_w_down, final_norm_g):
    for i in range(DEPTH):
        last = i == DEPTH - 1
        sh1, sc1, g1, sh2, sc2, g2 = adaln(c, ada_w[i], ada_b[i])
        csh1, csc1, cg1, csh2, csc2, cg2 = adaln(c_ctx, ada_w[i], ada_b[i])
        hx = modulate(rmsnorm(x, norm1_g[i]), sh1, sc1)
        hc = modulate(rmsnorm(ctx, norm1_g[i]), csh1, csc1)
        kind, j = i % N_MIXERS, i // N_MIXERS
        if kind == 0:
            yx, yc = hgrn2_mixer_pallas(hx, hc, hg_w_in[j], hgrn_lower_bound(hg_lb, i), hg_onorm_g[j], hg_w_out[j])
        elif kind == 1:
            yx, yc = hyena_mixer_pallas(hx, hc, hy_w_in[j], hy_b_in[j], hy_short_w[j], hy_short_b[j],
                                 hy_f_w1[j], hy_f_b1[j], hy_f_w2[j], hy_f_b2[j], hy_f_w3[j], hy_f_b3[j],
                                 hy_f_w4[j], hy_f_freq[j], hy_f_bias[j], hy_w_out[j], hy_b_out[j])
        elif kind == 2:
            yx, yc = mamba2_mixer_pallas(hx, hc, mb_w_in[j], mb_conv_w[j], mb_conv_b[j], mb_dt_bias[j],
                                  mb_a_log[j], mb_d[j], mb_norm_g[j], mb_w_out[j])
        else:
            yx, yc = gla_mixer_pallas(hx, hc, gla_w_in[j], gla_gk_w2[j], gla_gk_b[j], gla_onorm_g[j], gla_w_out[j])
        x = x + g1 * yx
        x = expert_choice_ffn_residual(x, g2, modulate(rmsnorm(x, norm2_g[i]), sh2, sc2),
                                       router_w[i], moe_w_gate[i], moe_w_up[i], moe_w_down[i])
        if not last:
            ctx = ctx + cg1 * yc
            ctx = expert_choice_ffn_residual(ctx, cg2, modulate(rmsnorm(ctx, norm2_g[i]), csh2, csc2),
                                             router_w[i], moe_w_gate[i], moe_w_up[i], moe_w_down[i])
    return final_rmsnorm(x, final_norm_g)
```

```python
import math
import jax, jax.numpy as jnp
from jax import lax
from jax.experimental import pallas as pl
from jax.experimental.pallas import tpu as pltpu

D_MODEL = 1024
BATCH = 8
SEQ = 4096
DEPTH = 4

F32 = jnp.float32
GRID_W = 64
CTX_LEN = 256
N_MIXERS = 4
NORM_EPS = 1e-6

HG_HEADS = 8
HG_KEY = D_MODEL // HG_HEADS
HG_VAL = D_MODEL // HG_HEADS
HG_CHUNK = 32

HY_ORDER = 2
HY_SHORT = 3
HY_EMB = 33
HY_FILTER_W = 64
HY_DECAY_TARGET = 1e-2
HY_DECAY_HI_PCT = 0.3
HY_DECAY_LO_PCT = 1.5

MB_INNER = 2 * D_MODEL
MB_HEADDIM = 64
MB_HEADS = MB_INNER // MB_HEADDIM
MB_GROUPS = 8
MB_HPG = MB_HEADS // MB_GROUPS
MB_STATE = 128
MB_CONV = 5
MB_CHUNK = 64
MB_CONV_DIM = MB_INNER + 2 * MB_GROUPS * MB_STATE
MB_IN = MB_INNER + MB_CONV_DIM + 2 * MB_HEADS

GLA_HEADS = 4
GLA_KEY_DIM = D_MODEL // 2
GLA_VAL_DIM = D_MODEL
GLA_HK = GLA_KEY_DIM // GLA_HEADS
GLA_HV = GLA_VAL_DIM // GLA_HEADS
GLA_GATE_RANK = 16
GLA_GATE_NORM = 16.0
GLA_CHUNK = 32
GLA_IN = 2 * GLA_KEY_DIM + 2 * GLA_VAL_DIM + 2 * GLA_GATE_RANK

N_EXPERTS = 16
EC_CAPACITY_FACTOR = 2
EXPERT_FF = 2048


def rmsnorm(x, g):
    xf = x.astype(F32)
    return xf * lax.rsqrt(jnp.mean(xf * xf, axis=-1, keepdims=True) + NORM_EPS) * g


def adaln(cond, w, b):
    m = jax.nn.silu(cond.astype(F32)) @ w + b
    return jnp.split(m[..., None, :], 6, axis=-1)


def modulate(h, shift, scale):
    return h * (1.0 + scale) + shift


def heads(t, n):
    b, l, _ = t.shape
    return t.reshape(b, l, n, -1).transpose(0, 2, 1, 3)


def dwconv_centered(x, w, b):
    k_w = w.shape[0]
    pad = k_w // 2
    length = x.shape[1]
    xp = jnp.pad(x, ((0, 0), (pad, pad), (0, 0)))
    return sum(xp[:, j:j + length] * w[j] for j in range(k_w)) + b


def raster_to_colmajor(h, rows):
    b, l, d = h.shape
    return h.reshape(b, rows, GRID_W, d).transpose(0, 2, 1, 3).reshape(b, l, d)


def colmajor_to_raster(h, rows):
    b, l, d = h.shape
    return h.reshape(b, GRID_W, rows, d).transpose(0, 2, 1, 3).reshape(b, l, d)


def chunk_gla(q, k, v, g, s0, chunk):
    bsz, nh, length, kd = q.shape
    vd = v.shape[-1]
    n = length // chunk

    def split_chunks(t):
        return jnp.moveaxis(t.astype(F32).reshape(bsz, nh, n, chunk, t.shape[-1]), 2, 0)

    qs, ks, vs, gs = split_chunks(q), split_chunks(k), split_chunks(v), split_chunks(g)
    cum = jnp.cumsum(gs, axis=3)
    lower = jnp.tril(jnp.ones((chunk, chunk), bool))[:, :, None]

    def step(state, inp):
        qc, kc, vc, gc = inp
        rel = jnp.where(lower, gc[:, :, :, None, :] - gc[:, :, None, :, :], -jnp.inf)
        att = jnp.einsum('bhtk,bhsk,bhtsk->bhts', qc, kc, jnp.exp(rel))
        g_end = gc[:, :, -1:, :]
        o = att @ vc + jnp.einsum('bhtk,bhkv->bhtv', qc * jnp.exp(gc), state)
        state = jnp.exp(g_end[:, :, 0, :, None]) * state + jnp.einsum('bhsk,bhsv->bhkv', kc * jnp.exp(g_end - gc), vc)
        return state, o

    state, o = lax.scan(step, s0.astype(F32), (qs, ks, vs, cum))
    return jnp.moveaxis(o, 0, 2).reshape(bsz, nh, length, vd), state


def bidir_gla(lat, ctx, chunk):
    q, kf, kb, v, gf, gb = lat
    qc, kfc, kbc, vc, gfc, gbc = ctx
    s0 = jnp.zeros(q.shape[:2] + (q.shape[-1], v.shape[-1]), F32)
    oc_f, s_f = chunk_gla(qc, kfc, vc, gfc, s0, chunk)
    ox_f, _ = chunk_gla(q, kf, v, gf, s_f, chunk)
    r = lambda t: jnp.flip(t, axis=2)
    oc_b, s_b = chunk_gla(r(qc), r(kbc), r(vc), r(gbc), s0, chunk)
    ox_b, _ = chunk_gla(r(q), r(kb), r(v), r(gb), s_b, chunk)
    return ox_f + r(ox_b), oc_f + r(oc_b)


def gated_head_out(o, og, gain, w_out):
    b, nh, l, vd = o.shape
    o = rmsnorm(o.transpose(0, 2, 1, 3), gain).reshape(b, l, nh * vd)
    return (o * jax.nn.silu(og)) @ w_out


def hgrn_lower_bound(lb_param, layer):
    return jnp.cumsum(jax.nn.softmax(lb_param.astype(F32), axis=0), axis=0)[layer]


def hgrn2_mixer(hx, hc, w_in, lb, onorm_g, w_out):
    def project(h):
        q, ff, fb, inp, og = jnp.split(h @ w_in, 5, axis=-1)
        f_fwd = lb + (1.0 - lb) * jax.nn.sigmoid(ff.astype(F32))
        f_bwd = lb + (1.0 - lb) * jax.nn.sigmoid(fb.astype(F32))
        hd = lambda t: heads(t, HG_HEADS)
        streams = (hd(jax.nn.silu(q) * HG_KEY ** -0.5), hd(1.0 - f_fwd), hd(1.0 - f_bwd), hd(inp),
                   hd(jnp.log(f_fwd)), hd(jnp.log(f_bwd)))
        return streams, og

    lat, ogx = project(hx)
    ctxs, ogc = project(hc)
    ox, oc = bidir_gla(lat, ctxs, HG_CHUNK)
    return gated_head_out(ox, ogx, onorm_g, w_out), gated_head_out(oc, ogc, onorm_g, w_out)


def hyena_pos_features(length):
    t = jnp.linspace(0.0, 1.0, length, dtype=F32)[:, None]
    bands = (HY_EMB - 1) // 2
    f = jnp.linspace(1e-4, bands - 1, bands, dtype=F32)[None, :]
    w = 2.0 * math.pi * jnp.arange(length, dtype=F32)[:, None] / length
    return jnp.concatenate([t, jnp.cos(f * w), -jnp.sin(f * w)], axis=-1), t


def hyena_window(t):
    max_decay = math.log(HY_DECAY_TARGET) / HY_DECAY_HI_PCT
    min_decay = math.log(HY_DECAY_TARGET) / HY_DECAY_LO_PCT
    deltas = jnp.abs(jnp.linspace(min_decay, max_decay, D_MODEL, dtype=F32))
    return jnp.exp(-t * deltas[None, :])


def hyena_filters(length, w1, b1, w2, b2, w3, b3, w4, freq):
    z, t = hyena_pos_features(length)
    a = jnp.sin(freq * (z @ w1 + b1))
    a = jnp.sin(freq * (a @ w2 + b2))
    a = jnp.sin(freq * (a @ w3 + b3))
    h = (a @ w4).astype(F32).reshape(length, HY_ORDER, 2, D_MODEL) * hyena_window(t)[:, None, None, :]
    return h / jnp.sum(jnp.abs(h), axis=(0, 2), keepdims=True)


def twosided_fftconv(u, h_fwd, h_bwd):
    length = u.shape[1]
    k2 = jnp.concatenate([h_fwd, jnp.zeros_like(h_fwd[:1]), h_bwd[:0:-1]], axis=0)
    spec = jnp.fft.rfft(u, n=2 * length, axis=1) * jnp.fft.rfft(k2, axis=0)[None]
    return jnp.fft.irfft(spec, n=2 * length, axis=1)[:, :length]


def hyena_mixer(hx, hc, w_in, b_in, short_w, short_b, f_w1, f_b1, f_w2, f_b2, f_w3, f_b3, f_w4,
                f_freq, f_bias, w_out, b_out):
    def run(h):
        length = h.shape[1]
        u = dwconv_centered(h @ w_in + b_in, short_w, short_b).astype(F32)
        x1, x2, v = jnp.split(u, 3, axis=-1)
        filt = hyena_filters(length, f_w1, f_b1, f_w2, f_b2, f_w3, f_b3, f_w4, f_freq)
        z = v
        for o, gate in enumerate((x1, x2)):
            z = gate * (twosided_fftconv(z, filt[:, o, 0], filt[:, o, 1]) + z * f_bias[o])
        return z @ w_out + b_out

    return run(hx), run(hc)


def chunk_ssd(xs, bm, cm, dt, a, s0, chunk):
    bsz, length = xs.shape[:2]
    n = length // chunk

    def split_chunks(t):
        return jnp.moveaxis(t.astype(F32).reshape((bsz, n, chunk) + t.shape[2:]), 1, 0)

    xcs, bcs, ccs, dts = split_chunks(xs), split_chunks(bm), split_chunks(cm), split_chunks(dt)
    cum = jnp.cumsum(dts * a.astype(F32), axis=2)
    lower = jnp.tril(jnp.ones((chunk, chunk), bool))[:, :, None, None]

    def step(state, inp):
        xc, bc, cc, dtc, lc = inp
        seg = jnp.exp(jnp.where(lower, lc[:, :, None] - lc[:, None, :], -jnp.inf))
        w = jnp.einsum('btgn,bsgn->btsg', cc, bc)[..., None] * seg * dtc[:, None]
        y = jnp.einsum('btsgh,bsghp->btghp', w, xc)
        y = y + jnp.einsum('btgn,bghnp->btghp', cc, state) * jnp.exp(lc)[..., None]
        l_end = lc[:, -1]
        state = jnp.exp(l_end)[..., None, None] * state + jnp.einsum(
            'bsgn,bsgh,bsghp->bghnp', bc, jnp.exp(l_end[:, None] - lc) * dtc, xc)
        return state, y

    state, y = lax.scan(step, s0.astype(F32), (xcs, bcs, ccs, dts, cum))
    return jnp.moveaxis(y, 0, 1).reshape(xs.shape), state


def mamba2_mixer(hx, hc, w_in, conv_w, conv_b, dt_bias, a_log, d_skip, norm_g, w_out):
    a = -jnp.exp(a_log.astype(F32)).reshape(2, MB_GROUPS, MB_HPG)

    def project(h):
        bsz, length, _ = h.shape
        z, xbc, dt_raw = jnp.split(h @ w_in, [MB_INNER, MB_INNER + MB_CONV_DIM], axis=-1)
        xbc = jax.nn.silu(dwconv_centered(xbc, conv_w, conv_b))
        xs, bm, cm = jnp.split(xbc, [MB_INNER, MB_INNER + MB_GROUPS * MB_STATE], axis=-1)
        xs = xs.reshape(bsz, length, MB_GROUPS, MB_HPG, MB_HEADDIM)
        bm = bm.reshape(bsz, length, MB_GROUPS, MB_STATE)
        cm = cm.reshape(bsz, length, MB_GROUPS, MB_STATE)
        dt = jax.nn.softplus(dt_raw.astype(F32).reshape(bsz, length, 2, MB_GROUPS, MB_HPG)
                             + dt_bias.reshape(2, MB_GROUPS, MB_HPG))
        return z, xs, bm, cm, dt

    zx, xx, bx, cx, dtx = project(hx)
    zc, xc, bc, cc, dtc = project(hc)
    s0 = jnp.zeros((hx.shape[0], MB_GROUPS, MB_HPG, MB_STATE, MB_HEADDIM), F32)
    yc_f, s_f = chunk_ssd(xc, bc, cc, dtc[:, :, 0], a[0], s0, MB_CHUNK)
    yx_f, _ = chunk_ssd(xx, bx, cx, dtx[:, :, 0], a[0], s_f, MB_CHUNK)
    r = lambda t: jnp.flip(t, axis=1)
    yc_b, s_b = chunk_ssd(r(xc), r(bc), r(cc), r(dtc[:, :, 1]), a[1], s0, MB_CHUNK)
    yx_b, _ = chunk_ssd(r(xx), r(bx), r(cx), r(dtx[:, :, 1]), a[1], s_b, MB_CHUNK)

    def finish(y, xs, z):
        bsz, length = y.shape[:2]
        y = y + d_skip.reshape(MB_GROUPS, MB_HPG)[..., None] * xs
        y = y.reshape(bsz, length, MB_INNER) * jax.nn.silu(z)
        y = rmsnorm(y.reshape(bsz, length, MB_GROUPS, MB_INNER // MB_GROUPS),
                    norm_g.reshape(MB_GROUPS, MB_INNER // MB_GROUPS))
        return y.reshape(bsz, length, MB_INNER) @ w_out

    return finish(yx_f + r(yx_b), xx, zx), finish(yc_f + r(yc_b), xc, zc)


def gla_mixer(hx, hc, w_in, gk_w2, gk_b, onorm_g, w_out):
    rows = hx.shape[1] // GRID_W
    kd, vd, rk = GLA_KEY_DIM, GLA_VAL_DIM, GLA_GATE_RANK
    cuts = [kd, 2 * kd, 2 * kd + vd, 2 * kd + 2 * vd, 2 * kd + 2 * vd + rk]

    def project(h):
        q, k, v, og, rf, rb = jnp.split(h @ w_in, cuts, axis=-1)
        gf = jax.nn.log_sigmoid((rf @ gk_w2[0] + gk_b[0]).astype(F32)) / GLA_GATE_NORM
        gb = jax.nn.log_sigmoid((rb @ gk_w2[1] + gk_b[1]).astype(F32)) / GLA_GATE_NORM
        hd = lambda t: heads(t, GLA_HEADS)
        kh = hd(k)
        return (hd(q * GLA_HK ** -0.5), kh, kh, hd(v), hd(gf), hd(gb)), og

    lat, ogx = project(raster_to_colmajor(hx, rows))
    ctxs, ogc = project(hc)
    ox, oc = bidir_gla(lat, ctxs, GLA_CHUNK)
    yx = colmajor_to_raster(gated_head_out(ox, ogx, onorm_g, w_out), rows)
    return yx, gated_head_out(oc, ogc, onorm_g, w_out)


def expert_choice_ffn(h, router_w, w_gate, w_up, w_down):
    bsz, t_len, d = h.shape
    cap = max(1, EC_CAPACITY_FACTOR * t_len // N_EXPERTS)
    aff = jax.nn.softmax(h.astype(F32) @ router_w.astype(F32), axis=-1)
    gate, idx = lax.top_k(jnp.swapaxes(aff, 1, 2), cap)
    xg = jax.vmap(lambda hb, ib: hb[ib])(h, idx)
    hid = jax.nn.silu(jnp.einsum('becd,edf->becf', xg, w_gate)) * jnp.einsum('becd,edf->becf', xg, w_up)
    ye = jnp.einsum('becf,efd->becd', hid, w_down) * gate[..., None]
    return jax.vmap(lambda ib, yb: jnp.zeros((t_len, d), yb.dtype).at[ib.reshape(-1)].add(yb.reshape(-1, d)))(idx, ye)


BF16 = jnp.bfloat16
LANES = 128
VMEM_LIMIT_BYTES = 56 * 2 ** 20
SCAN_CHUNK = 128
SCAN_FAST_RANGE = 80.0
SCAN_UNROLL = 2


def _cumsum_rows(tri_bf16, g):
    hi = g.astype(BF16)
    r1 = g - hi.astype(F32)
    mid = r1.astype(BF16)
    lo = (r1 - mid.astype(F32)).astype(BF16)
    d = lambda a: jnp.dot(tri_bf16, a, preferred_element_type=F32)
    return d(hi) + d(mid) + d(lo)


def _log_sigmoid(z):
    return jnp.minimum(z, 0.0) - jnp.log1p(jnp.exp(-jnp.abs(z)))


def _dot_nt(a, b):
    return lax.dot_general(a, b, (((1,), (1,)), ((), ())), preferred_element_type=F32)


def _dot_tn(a, b):
    return lax.dot_general(a, b, (((0,), (0,)), ((), ())), preferred_element_type=F32)


def _gla_scan_body(mode, seq, ctx_len, kdim, vdim, *refs):
    chunk = SCAN_CHUNK
    if mode == 'hgrn2':
        (q_x, ff_x, fb_x, v_x, og_x, q_c, ff_c, fb_c, v_c, og_c, lb_ref, gain_ref,
         o_x, o_c, cf_s, cb_s, sf_s, sb_s, kt_s, gt_s, qf_s, qb_s, of_s, ob_s, kf_s, kb_s) = refs
    else:
        (q_x, k_x, v_x, og_x, r_x, q_c, k_c, v_c, og_c, r_c, w2f_ref, w2b_ref, bf_ref, bb_ref, gain_ref,
         o_x, o_c, cf_s, cb_s, sf_s, sb_s, kt_s, gt_s, qf_s, qb_s, of_s, ob_s) = refs
    if mode == 'hgrn2':
        segs = (((ff_c, fb_c), q_c, None, v_c, og_c, o_c, ctx_len, 0),
                ((ff_x, fb_x), q_x, None, v_x, og_x, o_x, seq, ctx_len))
    else:
        segs = (((r_c,), q_c, k_c, v_c, og_c, o_c, ctx_len, 0),
                ((r_x,), q_x, k_x, v_x, og_x, o_x, seq, ctx_len))

    row = lax.broadcasted_iota(jnp.int32, (chunk, chunk), 0)
    col = lax.broadcasted_iota(jnp.int32, (chunk, chunk), 1)
    masks = (row >= col, row <= col)
    tris = tuple(m.astype(BF16) for m in masks)
    cum_s, state_s, qin_s, kout_s = (cf_s, cb_s), (sf_s, sb_s), (qf_s, qb_s), (of_s, ob_s)
    end_row = (chunk - 1, 0)
    mid_row = chunk // 2

    def rows_of(i, off=0):
        return pl.ds(pl.multiple_of(off + i * chunk, chunk), chunk)

    def gate_pass(seg, bound):
        gsrc, _, _, _, _, _, length, off = seg

        def body(i, bound):
            rows, srows = rows_of(i), rows_of(i, off)
            if mode == 'hgrn2':
                lb = lb_ref[...]
                f_f = lb + (1.0 - lb) * jax.nn.sigmoid(gsrc[0][rows, :])
                f_b = lb + (1.0 - lb) * jax.nn.sigmoid(gsrc[1][rows, :])
                g_f, g_b = jnp.log(f_f), jnp.log(f_b)
                kf_s[srows, :] = 1.0 - f_f
                kb_s[srows, :] = 1.0 - f_b
            else:
                r = gsrc[0][rows, :].astype(BF16)
                z_f = jnp.dot(r, w2f_ref[...].astype(BF16), preferred_element_type=F32) + bf_ref[...]
                z_b = jnp.dot(r, w2b_ref[...].astype(BF16), preferred_element_type=F32) + bb_ref[...]
                g_f = _log_sigmoid(z_f) / GLA_GATE_NORM
                g_b = _log_sigmoid(z_b) / GLA_GATE_NORM
            c_f = _cumsum_rows(tris[0], g_f)
            c_b = _cumsum_rows(tris[1], g_b)
            cf_s[srows, :] = c_f
            cb_s[srows, :] = c_b
            for c in (c_f, c_b):
                mid = c[mid_row:mid_row + 1, :]
                spread = jnp.maximum(jnp.abs(c[0:1, :] - mid), jnp.abs(c[chunk - 1:chunk, :] - mid))
                bound = jnp.maximum(bound, spread)
            return bound

        return lax.fori_loop(0, length // chunk, body, bound, unroll=SCAN_UNROLL)

    def intra(seg, direction, i, exact):
        _, q_r, k_r, v_r, _, _, _, off = seg
        rows, srows = rows_of(i), rows_of(i, off)
        if mode == 'hgrn2':
            qq = jax.nn.silu(q_r[rows, :]) * HG_KEY ** -0.5
            k = (kf_s if direction == 0 else kb_s)[srows, :]
        else:
            qq = q_r[rows, :] * GLA_HK ** -0.5
            k = k_r[rows, :]
        cum = cum_s[direction][srows, :]
        e = end_row[direction]
        cum_end = cum[e:e + 1, :]
        q_in = (qq * jnp.exp(cum)).astype(BF16)
        qin_s[direction][srows, :] = q_in
        kout_s[direction][srows, :] = (k * jnp.exp(cum_end - cum)).astype(BF16)
        if exact:
            kt_s[...] = k
            gt_s[...] = cum

            def col_body(s, att):
                ks = kt_s[pl.ds(s, 1), :]
                gs = gt_s[pl.ds(s, 1), :]
                p = qq * ks * jnp.exp(jnp.minimum(cum - gs, 0.0))
                return att + jnp.where(col == s, jnp.sum(p, axis=-1, keepdims=True), 0.0)

            att = lax.fori_loop(0, chunk, col_body, jnp.zeros((chunk, chunk), F32))
        else:
            rel = cum - cum[mid_row:mid_row + 1, :]
            att = _dot_nt((qq * jnp.exp(rel)).astype(BF16), (k * jnp.exp(-rel)).astype(BF16))
        att = jnp.where(masks[direction], att, 0.0).astype(BF16)
        return jnp.dot(att, v_r[rows, :].astype(BF16), preferred_element_type=F32)

    def intra_pass(exact):
        for seg in segs:
            o_r, length = seg[5], seg[6]

            def body(i, carry):
                o_r[rows_of(i), :] = intra(seg, 0, i, exact) + intra(seg, 1, i, exact)
                return carry

            lax.fori_loop(0, length // chunk, body, 0, unroll=1 if exact else SCAN_UNROLL)

    def inter(seg, direction, i):
        v_r, off = seg[3], seg[7]
        rows, srows = rows_of(i), rows_of(i, off)
        s_ref = state_s[direction]
        state = s_ref[...]
        o = _dot_nt(qin_s[direction][srows, :], state.astype(BF16))
        e = pl.ds(pl.multiple_of(off + i * chunk, chunk) + end_row[direction], 1)
        decay = jnp.exp(cum_s[direction][e, :])
        s_ref[...] = state * decay + _dot_tn(v_r[rows, :].astype(BF16), kout_s[direction][srows, :])
        return o

    def finish(seg, i, o):
        o = o * lax.rsqrt(jnp.mean(o * o, axis=-1, keepdims=True) + NORM_EPS) * gain_ref[...]
        return o * jax.nn.silu(seg[4][rows_of(i), :])

    def inter_pass():
        sf_s[...] = jnp.zeros_like(sf_s)
        sb_s[...] = jnp.zeros_like(sb_s)
        for seg in segs:
            o_r, length = seg[5], seg[6]
            n = length // chunk

            def first_half(i, carry):
                j = n - 1 - i
                o_r[rows_of(i), :] += inter(seg, 0, i)
                o_r[rows_of(j), :] += inter(seg, 1, j)
                return carry

            def second_half(i, carry):
                j = n - 1 - i
                o_r[rows_of(i), :] = finish(seg, i, o_r[rows_of(i), :] + inter(seg, 0, i))
                o_r[rows_of(j), :] = finish(seg, j, o_r[rows_of(j), :] + inter(seg, 1, j))
                return carry

            lax.fori_loop(0, n // 2, first_half, 0, unroll=min(SCAN_UNROLL, n // 2))
            lax.fori_loop(n // 2, n, second_half, 0, unroll=min(SCAN_UNROLL, n // 2))

    bound = jnp.zeros((1, kdim), F32)
    for seg in segs:
        bound = gate_pass(seg, bound)
    fast = jnp.max(bound) <= SCAN_FAST_RANGE

    @pl.when(fast)
    def _():
        intra_pass(False)

    @pl.when(jnp.logical_not(fast))
    def _():
        intra_pass(True)

    inter_pass()


def _gla_scan_call(mode, nheads, kdim, vdim, lat_in, ctx_in, small_in, bsz, seq, ctx_len):
    def stream_spec(length, width, first, stride):
        return pl.BlockSpec((None, length, width), lambda b, h: (b, 0, first + stride * h))

    assert seq % (2 * SCAN_CHUNK) == 0 and ctx_len % (2 * SCAN_CHUNK) == 0
    in_specs = [stream_spec(seq, w, f, s) for _, w, f, s in lat_in]
    in_specs += [stream_spec(ctx_len, w, f, s) for _, w, f, s in ctx_in]
    in_specs += [pl.BlockSpec(blk, imap) for _, blk, imap in small_in]
    args = [a for a, _, _, _ in lat_in] + [a for a, _, _, _ in ctx_in] + [a for a, _, _ in small_in]
    tot = seq + ctx_len
    scratch = [pltpu.VMEM((tot, kdim), F32), pltpu.VMEM((tot, kdim), F32),
               pltpu.VMEM((vdim, kdim), F32), pltpu.VMEM((vdim, kdim), F32),
               pltpu.VMEM((SCAN_CHUNK, kdim), F32), pltpu.VMEM((SCAN_CHUNK, kdim), F32),
               pltpu.VMEM((tot, kdim), BF16), pltpu.VMEM((tot, kdim), BF16),
               pltpu.VMEM((tot, kdim), BF16), pltpu.VMEM((tot, kdim), BF16)]
    if mode == 'hgrn2':
        scratch += [pltpu.VMEM((tot, kdim), F32), pltpu.VMEM((tot, kdim), F32)]
    return pl.pallas_call(
        lambda *refs: _gla_scan_body(mode, seq, ctx_len, kdim, vdim, *refs),
        grid=(bsz, nheads),
        in_specs=in_specs,
        out_specs=[pl.BlockSpec((None, seq, vdim), lambda b, h: (b, 0, h)),
                   pl.BlockSpec((None, ctx_len, vdim), lambda b, h: (b, 0, h))],
        out_shape=[jax.ShapeDtypeStruct((bsz, seq, nheads * vdim), F32),
                   jax.ShapeDtypeStruct((bsz, ctx_len, nheads * vdim), F32)],
        scratch_shapes=scratch,
        compiler_params=pltpu.CompilerParams(dimension_semantics=("arbitrary", "arbitrary"),
                                             vmem_limit_bytes=VMEM_LIMIT_BYTES),
        name=f"{mode}_scan",
    )(*args)


def hgrn2_mixer_pallas(hx, hc, w_in, lb, onorm_g, w_out):
    bsz, seq, _ = hx.shape
    ctx_len = hc.shape[1]
    yx, yc = hx @ w_in, hc @ w_in
    nh = HG_HEADS
    streams = lambda y: [(y, HG_KEY, j * nh, 1) for j in range(5)]
    small = [(lb.reshape(1, D_MODEL), (1, HG_KEY), lambda b, h: (0, h)),
             (onorm_g.reshape(1, HG_VAL), (1, HG_VAL), lambda b, h: (0, 0))]
    ox, oc = _gla_scan_call('hgrn2', nh, HG_KEY, HG_VAL, streams(yx), streams(yc), small, bsz, seq, ctx_len)
    return ox @ w_out, oc @ w_out


def gla_mixer_pallas(hx, hc, w_in, gk_w2, gk_b, onorm_g, w_out):
    bsz, seq, _ = hx.shape
    ctx_len = hc.shape[1]
    rows = seq // GRID_W
    kd, vd, rk = GLA_KEY_DIM, GLA_VAL_DIM, GLA_GATE_RANK
    nh = GLA_HEADS
    w_in_p = jnp.pad(w_in, ((0, 0), (0, LANES - 2 * rk)))
    yx = raster_to_colmajor(hx, rows) @ w_in_p
    yc = hc @ w_in_p
    w2f = jnp.pad(gk_w2[0], ((0, LANES - rk), (0, 0)))
    w2b = jnp.pad(gk_w2[1], ((rk, LANES - 2 * rk), (0, 0)))
    streams = lambda y: [(y, GLA_HK, 0, 1), (y, GLA_HK, kd // GLA_HK, 1), (y, GLA_HV, 2 * kd // GLA_HV, 1),
                         (y, GLA_HV, (2 * kd + vd) // GLA_HV, 1), (y, LANES, (2 * kd + 2 * vd) // LANES, 0)]
    small = [(w2f, (LANES, GLA_HK), lambda b, h: (0, h)), (w2b, (LANES, GLA_HK), lambda b, h: (0, h)),
             (gk_b[0].reshape(1, kd), (1, GLA_HK), lambda b, h: (0, h)),
             (gk_b[1].reshape(1, kd), (1, GLA_HK), lambda b, h: (0, h)),
             (onorm_g.reshape(1, GLA_HV), (1, GLA_HV), lambda b, h: (0, 0))]
    ox, oc = _gla_scan_call('gla', nh, GLA_HK, GLA_HV, streams(yx), streams(yc), small, bsz, seq, ctx_len)
    return colmajor_to_raster(ox @ w_out, rows), oc @ w_out


MB_GROUP_W = MB_INNER // MB_GROUPS


def _ssd_scan_body(seq, ctx_len, *refs):
    chunk = MB_CHUNK
    gw, hpg = MB_GROUP_W, MB_HPG
    (x_x, b_x, c_x, z_x, cl_x, rt_x, x_c, b_c, c_c, z_c, cl_c, rt_c, dskip_ref, gain_ref,
     o_x, o_c, xf_s, xb_s, sf_s, sb_s) = refs
    segs = ((x_c, b_c, c_c, z_c, cl_c, rt_c, o_c, ctx_len, 0),
            (x_x, b_x, c_x, z_x, cl_x, rt_x, o_x, seq, ctx_len))
    xs_s, state_s = (xf_s, xb_s), (sf_s, sb_s)
    end_row = (chunk - 1, 0)

    row = lax.broadcasted_iota(jnp.int32, (chunk, chunk), 0)
    col = lax.broadcasted_iota(jnp.int32, (chunk, chunk), 1)
    masks = (row >= col, row <= col)
    head_of_lane = lax.broadcasted_iota(jnp.int32, (1, gw), 1) // MB_HEADDIM

    def rows_of(i, off=0):
        return pl.ds(pl.multiple_of(off + i * chunk, chunk), chunk)

    def bcast_heads(c4):
        out = c4[:, hpg - 1:hpg]
        for hh in range(hpg - 2, -1, -1):
            out = jnp.where(head_of_lane == hh, c4[:, hh:hh + 1], out)
        return out

    def intra_pass():
        for seg in segs:
            x_r, b_r, c_r, _, cl_r, rt_r, o_r, length, off = seg

            def body(i, carry):
                rows, srows = rows_of(i), rows_of(i, off)
                cols, rt = cl_r[i], rt_r[i]
                x = x_r[rows, :]
                cb = _dot_nt(c_r[rows, :].astype(BF16), b_r[rows, :].astype(BF16))
                xh = [jnp.where(head_of_lane == hh, x, 0.0).astype(BF16) for hh in range(hpg)]
                y = jnp.zeros((chunk, gw), F32)
                for d in range(2):
                    cum = cols[:, d * hpg:(d + 1) * hpg]
                    dt = cols[:, (2 + d) * hpg:(3 + d) * hpg]
                    e = end_row[d]
                    scale = jnp.exp(cum[e:e + 1, :] - cum) * dt
                    xs_s[d][srows, :] = (x * bcast_heads(scale)).astype(BF16)
                    for hh in range(hpg):
                        j = d * hpg + hh
                        seg_decay = jnp.exp(jnp.minimum(cols[:, j:j + 1] - rt[j:j + 1, :], 0.0))
                        w = jnp.where(masks[d], cb * seg_decay * rt[2 * hpg + j:2 * hpg + j + 1, :], 0.0)
                        y = y + jnp.dot(w.astype(BF16), xh[hh], preferred_element_type=F32)
                o_r[rows, :] = y
                return carry

            lax.fori_loop(0, length // chunk, body, 0)

    def inter(seg, d, i):
        _, b_r, c_r, _, cl_r, _, _, _, off = seg
        rows, srows = rows_of(i), rows_of(i, off)
        cum = cl_r[i][:, d * hpg:(d + 1) * hpg]
        state = state_s[d][...]
        y = jnp.dot(c_r[rows, :].astype(BF16), state.astype(BF16), preferred_element_type=F32)
        e = end_row[d]
        decay = bcast_heads(jnp.exp(cum[e:e + 1, :]))
        state_s[d][...] = state * decay + _dot_tn(b_r[rows, :].astype(BF16), xs_s[d][srows, :])
        return y * bcast_heads(jnp.exp(cum))

    def finish(seg, i, y):
        x_r, z_r = seg[0], seg[3]
        rows = rows_of(i)
        y = (y + dskip_ref[...] * x_r[rows, :]) * jax.nn.silu(z_r[rows, :])
        return y * lax.rsqrt(jnp.mean(y * y, axis=-1, keepdims=True) + NORM_EPS) * gain_ref[...]

    def inter_pass():
        sf_s[...] = jnp.zeros_like(sf_s)
        sb_s[...] = jnp.zeros_like(sb_s)
        for seg in segs:
            o_r, length = seg[6], seg[7]
            n = length // chunk

            def first_half(i, carry):
                j = n - 1 - i
                o_r[rows_of(i), :] += inter(seg, 0, i)
                o_r[rows_of(j), :] += inter(seg, 1, j)
                return carry

            def second_half(i, carry):
                j = n - 1 - i
                o_r[rows_of(i), :] = finish(seg, i, o_r[rows_of(i), :] + inter(seg, 0, i))
                o_r[rows_of(j), :] = finish(seg, j, o_r[rows_of(j), :] + inter(seg, 1, j))
                return carry

            lax.fori_loop(0, n // 2, first_half, 0, unroll=2)
            lax.fori_loop(n // 2, n, second_half, 0, unroll=2)

    intra_pass()
    inter_pass()


def _ssd_head_tables(dt_raw, dt_bias, a_log):
    bsz, length, _ = dt_raw.shape
    n = length // MB_CHUNK
    a = -jnp.exp(a_log.astype(F32)).reshape(2, MB_GROUPS, MB_HPG)
    dt = jax.nn.softplus(dt_raw.astype(F32).reshape(bsz, length, 2, MB_GROUPS, MB_HPG)
                         + dt_bias.reshape(2, MB_GROUPS, MB_HPG))
    dta = (dt * a).reshape(bsz, n, MB_CHUNK, 2, MB_GROUPS, MB_HPG)
    dtc = dt.reshape(bsz, n, MB_CHUNK, 2, MB_GROUPS, MB_HPG)
    cum_f = jnp.cumsum(dta[:, :, :, 0], axis=2)
    cum_b = jnp.flip(jnp.cumsum(jnp.flip(dta[:, :, :, 1], axis=2), axis=2), axis=2)
    cols = jnp.concatenate([cum_f, cum_b, dtc[:, :, :, 0], dtc[:, :, :, 1]], axis=-1)
    cols = cols.transpose(0, 3, 1, 2, 4)
    return cols, jnp.swapaxes(cols, -1, -2)


def mamba2_mixer_pallas(hx, hc, w_in, conv_w, conv_b, dt_bias, a_log, d_skip, norm_g, w_out):
    bsz, seq, _ = hx.shape
    ctx_len = hc.shape[1]
    chunk, gw = MB_CHUNK, MB_GROUP_W

    def project(h):
        y = h @ w_in
        xbc = jax.nn.silu(dwconv_centered(y[..., MB_INNER:MB_INNER + MB_CONV_DIM], conv_w, conv_b))
        cols, rows_t = _ssd_head_tables(y[..., MB_INNER + MB_CONV_DIM:], dt_bias, a_log)
        return y, xbc, cols, rows_t

    yx, xbcx, clx, rtx = project(hx)
    yc, xbcc, clc, rtc = project(hc)

    def stream_specs(length):
        n = length // chunk
        bw = MB_INNER // MB_STATE
        return [pl.BlockSpec((None, length, gw), lambda b, g: (b, 0, g)),
                pl.BlockSpec((None, length, MB_STATE), lambda b, g: (b, 0, bw + g)),
                pl.BlockSpec((None, length, MB_STATE), lambda b, g: (b, 0, bw + MB_GROUPS + g)),
                pl.BlockSpec((None, length, gw), lambda b, g: (b, 0, g)),
                pl.BlockSpec((None, None, n, chunk, 4 * MB_HPG), lambda b, g: (b, g, 0, 0, 0)),
                pl.BlockSpec((None, None, n, 4 * MB_HPG, chunk), lambda b, g: (b, g, 0, 0, 0))]

    tot = seq + ctx_len
    ox, oc = pl.pallas_call(
        lambda *refs: _ssd_scan_body(seq, ctx_len, *refs),
        grid=(bsz, MB_GROUPS),
        in_specs=stream_specs(seq) + stream_specs(ctx_len) + [
            pl.BlockSpec((1, gw), lambda b, g: (0, g)), pl.BlockSpec((1, gw), lambda b, g: (0, g))],
        out_specs=[pl.BlockSpec((None, seq, gw), lambda b, g: (b, 0, g)),
                   pl.BlockSpec((None, ctx_len, gw), lambda b, g: (b, 0, g))],
        out_shape=[jax.ShapeDtypeStruct((bsz, seq, MB_INNER), F32),
                   jax.ShapeDtypeStruct((bsz, ctx_len, MB_INNER), F32)],
        scratch_shapes=[pltpu.VMEM((tot, gw), BF16), pltpu.VMEM((tot, gw), BF16),
                        pltpu.VMEM((MB_STATE, gw), F32), pltpu.VMEM((MB_STATE, gw), F32)],
        compiler_params=pltpu.CompilerParams(dimension_semantics=("arbitrary", "arbitrary"),
                                             vmem_limit_bytes=VMEM_LIMIT_BYTES),
        name="ssd_scan",
    )(xbcx, xbcx, xbcx, yx, clx, rtx, xbcc, xbcc, xbcc, yc, clc, rtc,
      jnp.repeat(d_skip, MB_HEADDIM).reshape(1, MB_INNER), norm_g.reshape(1, MB_INNER))
    return ox @ w_out, oc @ w_out


HY_BLOCK = 256
HY_FREQ_ROWS = 64


def _split_bf16(a):
    hi = a.astype(BF16)
    return hi, (a - hi.astype(F32)).astype(BF16)


def _dot_split(w_hi, w_lo, x):
    x_hi, x_lo = _split_bf16(x)
    d = lambda a, b: jnp.dot(a, b, preferred_element_type=F32)
    return d(w_hi, x_hi) + d(w_hi, x_lo) + d(w_lo, x_hi)


def _hyena_dft_matrices():
    p = HY_BLOCK
    f = jnp.arange(p, dtype=F32)[:, None] + 0.5
    b = jnp.arange(p, dtype=F32)[None, :]
    k = jnp.round(f * b * 2.0).astype(jnp.int32) % (4 * p)
    ang = k.astype(F32) * (2.0 * math.pi / (4 * p))
    fwd = jnp.concatenate([jnp.cos(ang), -jnp.sin(ang)], axis=0)
    inv = jnp.concatenate([jnp.cos(ang).T, -jnp.sin(ang).T], axis=1) / p
    return fwd, inv


def _hyena_filter_spectra(hf, hb, fwd):
    length, d = hf.shape
    p = HY_BLOCK
    nb = length // p
    h2 = jnp.concatenate([jnp.zeros((1, d), F32), jnp.flip(hb[1:], axis=0), hf], axis=0)
    blocks = h2.reshape(2 * nb, p, d)
    spec = jnp.einsum('fj,kjd->kfd', fwd, blocks, precision=lax.Precision.HIGHEST)
    s_re, s_im = spec[:, :p], spec[:, p:]
    sign = jnp.where(jnp.arange(p) % 2 == 0, 1.0, -1.0).astype(F32)[None, :, None]
    return s_re[1:] - sign * s_im[:-1], s_im[1:] + sign * s_re[:-1]


def _hyena_conv_body(nb, u_ref, x_ref, bias_ref, gre_ref, gim_ref, wfh_ref, wfl_ref, wih_ref, wil_ref,
                     o_ref, ure_s, uim_s, y_s):
    p, fr = HY_BLOCK, HY_FREQ_ROWS

    def rows_of(i):
        return pl.ds(pl.multiple_of(i * p, p), p)

    def forward(j, carry):
        spec = _dot_split(wfh_ref[...], wfl_ref[...], u_ref[rows_of(j), :])
        ure_s[j] = spec[:p]
        uim_s[j] = spec[p:]
        return carry

    lax.fori_loop(0, nb, forward, 0)

    def out_block(i, carry):
        for t in range(p // fr):
            r = pl.ds(t * fr, fr)

            def acc_body(j, acc):
                a_re, a_im = acc
                k = i - j + nb - 1
                g_re, g_im = gre_ref[k, r, :], gim_ref[k, r, :]
                u_re, u_im = ure_s[j, r, :], uim_s[j, r, :]
                return a_re + g_re * u_re - g_im * u_im, a_im + g_re * u_im + g_im * u_re

            zero = jnp.zeros((fr, u_ref.shape[-1]), F32)
            a_re, a_im = lax.fori_loop(0, nb, acc_body, (zero, zero), unroll=min(nb, 4))
            y_s[pl.ds(t * fr, fr), :] = a_re
            y_s[pl.ds(p + t * fr, fr), :] = a_im
        y = _dot_split(wih_ref[...], wil_ref[...], y_s[...])
        u = u_ref[rows_of(i), :]
        o_ref[rows_of(i), :] = x_ref[rows_of(i), :] * (y + u * bias_ref[...])
        return carry

    lax.fori_loop(0, nb, out_block, 0)


def _hyena_conv(u_arr, u_blk, x_arr, x_blk, bias, g_re, g_im, mats):
    bsz, length, _ = u_arr.shape
    d = bias.shape[-1]
    p = HY_BLOCK
    nb = length // p
    nseg = 2 * nb - 1
    const = lambda shape: pl.BlockSpec(shape, lambda c, b: (0, 0))
    return pl.pallas_call(
        lambda *refs: _hyena_conv_body(nb, *refs),
        grid=(d // LANES, bsz),
        in_specs=[pl.BlockSpec((None, length, LANES), lambda c, b: (b, 0, u_blk + c)),
                  pl.BlockSpec((None, length, LANES), lambda c, b: (b, 0, x_blk + c)),
                  pl.BlockSpec((1, LANES), lambda c, b: (0, c)),
                  pl.BlockSpec((nseg, p, LANES), lambda c, b: (0, 0, c)),
                  pl.BlockSpec((nseg, p, LANES), lambda c, b: (0, 0, c)),
                  const((2 * p, p)), const((2 * p, p)), const((p, 2 * p)), const((p, 2 * p))],
        out_specs=pl.BlockSpec((None, length, LANES), lambda c, b: (b, 0, c)),
        out_shape=jax.ShapeDtypeStruct((bsz, length, d), F32),
        scratch_shapes=[pltpu.VMEM((nb, p, LANES), F32), pltpu.VMEM((nb, p, LANES), F32),
                        pltpu.VMEM((2 * p, LANES), F32)],
        compiler_params=pltpu.CompilerParams(dimension_semantics=("arbitrary", "arbitrary"),
                                             vmem_limit_bytes=VMEM_LIMIT_BYTES),
        name="hyena_conv",
    )(u_arr, x_arr, bias.reshape(1, d), g_re, g_im, *mats)


def hyena_mixer_pallas(hx, hc, w_in, b_in, short_w, short_b, f_w1, f_b1, f_w2, f_b2, f_w3, f_b3, f_w4,
                       f_freq, f_bias, w_out, b_out):
    fwd, inv = _hyena_dft_matrices()
    mats = _split_bf16(fwd) + _split_bf16(inv)
    nblk = D_MODEL // LANES

    def run(h):
        length = h.shape[1]
        u = dwconv_centered(h @ w_in + b_in, short_w, short_b).astype(F32)
        filt = hyena_filters(length, f_w1, f_b1, f_w2, f_b2, f_w3, f_b3, f_w4, f_freq)
        g0 = _hyena_filter_spectra(filt[:, 0, 0], filt[:, 0, 1], fwd)
        g1 = _hyena_filter_spectra(filt[:, 1, 0], filt[:, 1, 1], fwd)
        z = _hyena_conv(u, 2 * nblk, u, 0, f_bias[0], g0[0], g0[1], mats)
        z = _hyena_conv(z, 0, u, nblk, f_bias[1], g1[0], g1[1], mats)
        return z @ w_out + b_out

    return run(hx), run(hc)


MOE_ROW_TILE = 1024
MOE_COL_TILE = 512


def _moe_combine_body(t_len, idx_ref, gate_ref, ye_ref, x_ref, g2_ref, o_ref):
    e = pl.program_id(2)
    cap = idx_ref.shape[-1]
    rt = min(MOE_ROW_TILE, t_len)

    @pl.when(e == 0)
    def _():
        o_ref[...] = jnp.zeros_like(o_ref)

    yg = (ye_ref[...] * gate_ref[...]).astype(BF16)
    idx = idx_ref[...]

    def body(r, carry):
        r0 = pl.multiple_of(r * rt, rt)
        tok = lax.broadcasted_iota(jnp.int32, (rt, cap), 0) + r0
        onehot = jnp.where(tok == idx, 1.0, 0.0).astype(BF16)
        o_ref[pl.ds(r0, rt), :] += jnp.dot(onehot, yg, preferred_element_type=F32)
        return carry

    lax.fori_loop(0, t_len // rt, body, 0)

    @pl.when(e == pl.num_programs(2) - 1)
    def _():
        o_ref[...] = x_ref[...] + g2_ref[...] * o_ref[...]


def expert_choice_ffn_residual(x, g2, h, router_w, w_gate, w_up, w_down):
    bsz, t_len, d = h.shape
    cap = max(1, EC_CAPACITY_FACTOR * t_len // N_EXPERTS)
    aff = jax.nn.softmax(h.astype(F32) @ router_w.astype(F32), axis=-1)
    gate, idx = lax.top_k(jnp.swapaxes(aff, 1, 2), cap)
    xg = jax.vmap(lambda hb, ib: hb[ib])(h, idx)
    hid = jax.nn.silu(jnp.einsum('becd,edf->becf', xg, w_gate)) * jnp.einsum('becd,edf->becf', xg, w_up)
    ye = jnp.einsum('becf,efd->becd', hid, w_down)
    ct = MOE_COL_TILE
    g2 = jnp.broadcast_to(g2.reshape(-1, 1, d), (bsz, 1, d))
    return pl.pallas_call(
        lambda *refs: _moe_combine_body(t_len, *refs),
        grid=(bsz, d // ct, N_EXPERTS),
        in_specs=[pl.BlockSpec((None, None, 1, cap), lambda b, c, e: (b, e, 0, 0)),
                  pl.BlockSpec((None, None, cap, 1), lambda b, c, e: (b, e, 0, 0)),
                  pl.BlockSpec((None, None, cap, ct), lambda b, c, e: (b, e, 0, c)),
                  pl.BlockSpec((None, t_len, ct), lambda b, c, e: (b, 0, c)),
                  pl.BlockSpec((None, 1, ct), lambda b, c, e: (b, 0, c))],
        out_specs=pl.BlockSpec((None, t_len, ct), lambda b, c, e: (b, 0, c)),
        out_shape=jax.ShapeDtypeStruct((bsz, t_len, d), F32),
        compiler_params=pltpu.CompilerParams(dimension_semantics=("arbitrary", "arbitrary", "arbitrary"),
                                             vmem_limit_bytes=VMEM_LIMIT_BYTES),
        name="moe_combine",
    )(idx.reshape(bsz, N_EXPERTS, 1, cap), gate.reshape(bsz, N_EXPERTS, cap, 1), ye, x, g2)


def _final_norm_body(x_ref, g_ref, o_ref):
    xf = x_ref[...]
    o_ref[...] = xf * lax.rsqrt(jnp.mean(xf * xf, axis=-1, keepdims=True) + NORM_EPS) * g_ref[...]


def final_rmsnorm(x, g):
    b, l, d = x.shape
    rows = b * l
    tm = 512
    out = pl.pallas_call(
        _final_norm_body,
        grid=(rows // tm,),
        in_specs=[pl.BlockSpec((tm, d), lambda i: (i, 0)), pl.BlockSpec((1, d), lambda i: (0, 0))],
        out_specs=pl.BlockSpec((tm, d), lambda i: (i, 0)),
        out_shape=jax.ShapeDtypeStruct((rows, d), F32),
    )(x.reshape(rows, d), g.reshape(1, d))
    return out.reshape(b, l, d)


def kernel(x, c, ctx, c_ctx, ada_w, ada_b, norm1_g, norm2_g,
           hg_w_in, hg_lb, hg_onorm_g, hg_w_out,
           hy_w_in, hy_b_in, hy_short_w, hy_short_b, hy_f_w1, hy_f_b1, hy_f_w2, hy_f_b2,
           hy_f_w3, hy_f_b3, hy_f_w4, hy_f_freq, hy_f_bias, hy_w_out, hy_b_out,
           mb_w_in, mb_conv_w, mb_conv_b, mb_dt_bias, mb_a_log, mb_d, mb_norm_g, mb_w_out,
           gla_w_in, gla_gk_w2, gla_gk_b, gla_onorm_g, gla_w_out,
           router_w, moe_w_gate, moe_w_up, moe_w_down, final_norm_g):
    for i in range(DEPTH):
        last = i == DEPTH - 1
        sh1, sc1, g1, sh2, sc2, g2 = adaln(c, ada_w[i], ada_b[i])
        csh1, csc1, cg1, csh2, csc2, cg2 = adaln(c_ctx, ada_w[i], ada_b[i])
        hx = modulate(rmsnorm(x, norm1_g[i]), sh1, sc1)
        hc = modulate(rmsnorm(ctx, norm1_g[i]), csh1, csc1)
        kind, j = i % N_MIXERS, i // N_MIXERS
        if kind == 0:
            yx, yc = hgrn2_mixer_pallas(hx, hc, hg_w_in[j], hgrn_lower_bound(hg_lb, i), hg_onorm_g[j], hg_w_out[j])
        elif kind == 1:
            yx, yc = hyena_mixer_pallas(hx, hc, hy_w_in[j], hy_b_in[j], hy_short_w[j], hy_short_b[j],
                                 hy_f_w1[j], hy_f_b1[j], hy_f_w2[j], hy_f_b2[j], hy_f_w3[j], hy_f_b3[j],
                                 hy_f_w4[j], hy_f_freq[j], hy_f_bias[j], hy_w_out[j], hy_b_out[j])
        elif kind == 2:
            yx, yc = mamba2_mixer_pallas(hx, hc, mb_w_in[j], mb_conv_w[j], mb_conv_b[j], mb_dt_bias[j],
                                  mb_a_log[j], mb_d[j], mb_norm_g[j], mb_w_out[j])
        else:
            yx, yc = gla_mixer_pallas(hx, hc, gla_w_in[j], gla_gk_w2[j], gla_gk_b[j], gla_onorm_g[j], gla_w_out[j])
        x = x + g1 * yx
        x = expert_choice_ffn_residual(x, g2, modulate(rmsnorm(x, norm2_g[i]), sh2, sc2),
                                       router_w[i], moe_w_gate[i], moe_w_up[i], moe_w_down[i])
        if not last:
            ctx = ctx + cg1 * yc
            ctx = expert_choice_ffn_residual(ctx, cg2, modulate(rmsnorm(ctx, norm2_g[i]), csh2, csc2),
                                             router_w[i], moe_w_gate[i], moe_w_up[i], moe_w_down[i])
    return final_rmsnorm(x, final_norm_g)
```

```python
import math
import jax, jax.numpy as jnp
from jax import lax
from jax.experimental import pallas as pl
from jax.experimental.pallas import tpu as pltpu

D_MODEL = 1024
BATCH = 8
SEQ = 4096
DEPTH = 4

F32 = jnp.float32
GRID_W = 64
CTX_LEN = 256
N_MIXERS = 4
NORM_EPS = 1e-6

HG_HEADS = 8
HG_KEY = D_MODEL // HG_HEADS
HG_VAL = D_MODEL // HG_HEADS
HG_CHUNK = 32

HY_ORDER = 2
HY_SHORT = 3
HY_EMB = 33
HY_FILTER_W = 64
HY_DECAY_TARGET = 1e-2
HY_DECAY_HI_PCT = 0.3
HY_DECAY_LO_PCT = 1.5

MB_INNER = 2 * D_MODEL
MB_HEADDIM = 64
MB_HEADS = MB_INNER // MB_HEADDIM
MB_GROUPS = 8
MB_HPG = MB_HEADS // MB_GROUPS
MB_STATE = 128
MB_CONV = 5
MB_CHUNK = 64
MB_CONV_DIM = MB_INNER + 2 * MB_GROUPS * MB_STATE
MB_IN = MB_INNER + MB_CONV_DIM + 2 * MB_HEADS

GLA_HEADS = 4
GLA_KEY_DIM = D_MODEL // 2
GLA_VAL_DIM = D_MODEL
GLA_HK = GLA_KEY_DIM // GLA_HEADS
GLA_HV = GLA_VAL_DIM // GLA_HEADS
GLA_GATE_RANK = 16
GLA_GATE_NORM = 16.0
GLA_CHUNK = 32
GLA_IN = 2 * GLA_KEY_DIM + 2 * GLA_VAL_DIM + 2 * GLA_GATE_RANK

N_EXPERTS = 16
EC_CAPACITY_FACTOR = 2
EXPERT_FF = 2048


def rmsnorm(x, g):
    xf = x.astype(F32)
    return xf * lax.rsqrt(jnp.mean(xf * xf, axis=-1, keepdims=True) + NORM_EPS) * g


def adaln(cond, w, b):
    m = jax.nn.silu(cond.astype(F32)) @ w + b
    return jnp.split(m[..., None, :], 6, axis=-1)


def modulate(h, shift, scale):
    return h * (1.0 + scale) + shift


def heads(t, n):
    b, l, _ = t.shape
    return t.reshape(b, l, n, -1).transpose(0, 2, 1, 3)


def dwconv_centered(x, w, b):
    k_w = w.shape[0]
    pad = k_w // 2
    length = x.shape[1]
    xp = jnp.pad(x, ((0, 0), (pad, pad), (0, 0)))
    return sum(xp[:, j:j + length] * w[j] for j in range(k_w)) + b


def raster_to_colmajor(h, rows):
    b, l, d = h.shape
    return h.reshape(b, rows, GRID_W, d).transpose(0, 2, 1, 3).reshape(b, l, d)


def colmajor_to_raster(h, rows):
    b, l, d = h.shape
    return h.reshape(b, GRID_W, rows, d).transpose(0, 2, 1, 3).reshape(b, l, d)


def chunk_gla(q, k, v, g, s0, chunk):
    bsz, nh, length, kd = q.shape
    vd = v.shape[-1]
    n = length // chunk

    def split_chunks(t):
        return jnp.moveaxis(t.astype(F32).reshape(bsz, nh, n, chunk, t.shape[-1]), 2, 0)

    qs, ks, vs, gs = split_chunks(q), split_chunks(k), split_chunks(v), split_chunks(g)
    cum = jnp.cumsum(gs, axis=3)
    lower = jnp.tril(jnp.ones((chunk, chunk), bool))[:, :, None]

    def step(state, inp):
        qc, kc, vc, gc = inp
        rel = jnp.where(lower, gc[:, :, :, None, :] - gc[:, :, None, :, :], -jnp.inf)
        att = jnp.einsum('bhtk,bhsk,bhtsk->bhts', qc, kc, jnp.exp(rel))
        g_end = gc[:, :, -1:, :]
        o = att @ vc + jnp.einsum('bhtk,bhkv->bhtv', qc * jnp.exp(gc), state)
        state = jnp.exp(g_end[:, :, 0, :, None]) * state + jnp.einsum('bhsk,bhsv->bhkv', kc * jnp.exp(g_end - gc), vc)
        return state, o

    state, o = lax.scan(step, s0.astype(F32), (qs, ks, vs, cum))
    return jnp.moveaxis(o, 0, 2).reshape(bsz, nh, length, vd), state


def bidir_gla(lat, ctx, chunk):
    q, kf, kb, v, gf, gb = lat
    qc, kfc, kbc, vc, gfc, gbc = ctx
    s0 = jnp.zeros(q.shape[:2] + (q.shape[-1], v.shape[-1]), F32)
    oc_f, s_f = chunk_gla(qc, kfc, vc, gfc, s0, chunk)
    ox_f, _ = chunk_gla(q, kf, v, gf, s_f, chunk)
    r = lambda t: jnp.flip(t, axis=2)
    oc_b, s_b = chunk_gla(r(qc), r(kbc), r(vc), r(gbc), s0, chunk)
    ox_b, _ = chunk_gla(r(q), r(kb), r(v), r(gb), s_b, chunk)
    return ox_f + r(ox_b), oc_f + r(oc_b)


def gated_head_out(o, og, gain, w_out):
    b, nh, l, vd = o.shape
    o = rmsnorm(o.transpose(0, 2, 1, 3), gain).reshape(b, l, nh * vd)
    return (o * jax.nn.silu(og)) @ w_out


def hgrn_lower_bound(lb_param, layer):
    return jnp.cumsum(jax.nn.softmax(lb_param.astype(F32), axis=0), axis=0)[layer]


def hgrn2_mixer(hx, hc, w_in, lb, onorm_g, w_out):
    def project(h):
        q, ff, fb, inp, og = jnp.split(h @ w_in, 5, axis=-1)
        f_fwd = lb + (1.0 - lb) * jax.nn.sigmoid(ff.astype(F32))
        f_bwd = lb + (1.0 - lb) * jax.nn.sigmoid(fb.astype(F32))
        hd = lambda t: heads(t, HG_HEADS)
        streams = (hd(jax.nn.silu(q) * HG_KEY ** -0.5), hd(1.0 - f_fwd), hd(1.0 - f_bwd), hd(inp),
                   hd(jnp.log(f_fwd)), hd(jnp.log(f_bwd)))
        return streams, og

    lat, ogx = project(hx)
    ctxs, ogc = project(hc)
    ox, oc = bidir_gla(lat, ctxs, HG_CHUNK)
    return gated_head_out(ox, ogx, onorm_g, w_out), gated_head_out(oc, ogc, onorm_g, w_out)


def hyena_pos_features(length):
    t = jnp.linspace(0.0, 1.0, length, dtype=F32)[:, None]
    bands = (HY_EMB - 1) // 2
    f = jnp.linspace(1e-4, bands - 1, bands, dtype=F32)[None, :]
    w = 2.0 * math.pi * jnp.arange(length, dtype=F32)[:, None] / length
    return jnp.concatenate([t, jnp.cos(f * w), -jnp.sin(f * w)], axis=-1), t


def hyena_window(t):
    max_decay = math.log(HY_DECAY_TARGET) / HY_DECAY_HI_PCT
    min_decay = math.log(HY_DECAY_TARGET) / HY_DECAY_LO_PCT
    deltas = jnp.abs(jnp.linspace(min_decay, max_decay, D_MODEL, dtype=F32))
    return jnp.exp(-t * deltas[None, :])


def hyena_filters(length, w1, b1, w2, b2, w3, b3, w4, freq):
    z, t = hyena_pos_features(length)
    a = jnp.sin(freq * (z @ w1 + b1))
    a = jnp.sin(freq * (a @ w2 + b2))
    a = jnp.sin(freq * (a @ w3 + b3))
    h = (a @ w4).astype(F32).reshape(length, HY_ORDER, 2, D_MODEL) * hyena_window(t)[:, None, None, :]
    return h / jnp.sum(jnp.abs(h), axis=(0, 2), keepdims=True)


def twosided_fftconv(u, h_fwd, h_bwd):
    length = u.shape[1]
    k2 = jnp.concatenate([h_fwd, jnp.zeros_like(h_fwd[:1]), h_bwd[:0:-1]], axis=0)
    spec = jnp.fft.rfft(u, n=2 * length, axis=1) * jnp.fft.rfft(k2, axis=0)[None]
    return jnp.fft.irfft(spec, n=2 * length, axis=1)[:, :length]


def hyena_mixer(hx, hc, w_in, b_in, short_w, short_b, f_w1, f_b1, f_w2, f_b2, f_w3, f_b3, f_w4,
                f_freq, f_bias, w_out, b_out):
    def run(h):
        length = h.shape[1]
        u = dwconv_centered(h @ w_in + b_in, short_w, short_b).astype(F32)
        x1, x2, v = jnp.split(u, 3, axis=-1)
        filt = hyena_filters(length, f_w1, f_b1, f_w2, f_b2, f_w3, f_b3, f_w4, f_freq)
        z = v
        for o, gate in enumerate((x1, x2)):
            z = gate * (twosided_fftconv(z, filt[:, o, 0], filt[:, o, 1]) + z * f_bias[o])
        return z @ w_out + b_out

    return run(hx), run(hc)


def chunk_ssd(xs, bm, cm, dt, a, s0, chunk):
    bsz, length = xs.shape[:2]
    n = length // chunk

    def split_chunks(t):
        return jnp.moveaxis(t.astype(F32).reshape((bsz, n, chunk) + t.shape[2:]), 1, 0)

    xcs, bcs, ccs, dts = split_chunks(xs), split_chunks(bm), split_chunks(cm), split_chunks(dt)
    cum = jnp.cumsum(dts * a.astype(F32), axis=2)
    lower = jnp.tril(jnp.ones((chunk, chunk), bool))[:, :, None, None]

    def step(state, inp):
        xc, bc, cc, dtc, lc = inp
        seg = jnp.exp(jnp.where(lower, lc[:, :, None] - lc[:, None, :], -jnp.inf))
        w = jnp.einsum('btgn,bsgn->btsg', cc, bc)[..., None] * seg * dtc[:, None]
        y = jnp.einsum('btsgh,bsghp->btghp', w, xc)
        y = y + jnp.einsum('btgn,bghnp->btghp', cc, state) * jnp.exp(lc)[..., None]
        l_end = lc[:, -1]
        state = jnp.exp(l_end)[..., None, None] * state + jnp.einsum(
            'bsgn,bsgh,bsghp->bghnp', bc, jnp.exp(l_end[:, None] - lc) * dtc, xc)
        return state, y

    state, y = lax.scan(step, s0.astype(F32), (xcs, bcs, ccs, dts, cum))
    return jnp.moveaxis(y, 0, 1).reshape(xs.shape), state


def mamba2_mixer(hx, hc, w_in, conv_w, conv_b, dt_bias, a_log, d_skip, norm_g, w_out):
    a = -jnp.exp(a_log.astype(F32)).reshape(2, MB_GROUPS, MB_HPG)

    def project(h):
        bsz, length, _ = h.shape
        z, xbc, dt_raw = jnp.split(h @ w_in, [MB_INNER, MB_INNER + MB_CONV_DIM], axis=-1)
        xbc = jax.nn.silu(dwconv_centered(xbc, conv_w, conv_b))
        xs, bm, cm = jnp.split(xbc, [MB_INNER, MB_INNER + MB_GROUPS * MB_STATE], axis=-1)
        xs = xs.reshape(bsz, length, MB_GROUPS, MB_HPG, MB_HEADDIM)
        bm = bm.reshape(bsz, length, MB_GROUPS, MB_STATE)
        cm = cm.reshape(bsz, length, MB_GROUPS, MB_STATE)
        dt = jax.nn.softplus(dt_raw.astype(F32).reshape(bsz, length, 2, MB_GROUPS, MB_HPG)
                             + dt_bias.reshape(2, MB_GROUPS, MB_HPG))
        return z, xs, bm, cm, dt

    zx, xx, bx, cx, dtx = project(hx)
    zc, xc, bc, cc, dtc = project(hc)
    s0 = jnp.zeros((hx.shape[0], MB_GROUPS, MB_HPG, MB_STATE, MB_HEADDIM), F32)
    yc_f, s_f = chunk_ssd(xc, bc, cc, dtc[:, :, 0], a[0], s0, MB_CHUNK)
    yx_f, _ = chunk_ssd(xx, bx, cx, dtx[:, :, 0], a[0], s_f, MB_CHUNK)
    r = lambda t: jnp.flip(t, axis=1)
    yc_b, s_b = chunk_ssd(r(xc), r(bc), r(cc), r(dtc[:, :, 1]), a[1], s0, MB_CHUNK)
    yx_b, _ = chunk_ssd(r(xx), r(bx), r(cx), r(dtx[:, :, 1]), a[1], s_b, MB_CHUNK)

    def finish(y, xs, z):
        bsz, length = y.shape[:2]
        y = y + d_skip.reshape(MB_GROUPS, MB_HPG)[..., None] * xs
        y = y.reshape(bsz, length, MB_INNER) * jax.nn.silu(z)
        y = rmsnorm(y.reshape(bsz, length, MB_GROUPS, MB_INNER // MB_GROUPS),
                    norm_g.reshape(MB_GROUPS, MB_INNER // MB_GROUPS))
        return y.reshape(bsz, length, MB_INNER) @ w_out

    return finish(yx_f + r(yx_b), xx, zx), finish(yc_f + r(yc_b), xc, zc)


def gla_mixer(hx, hc, w_in, gk_w2, gk_b, onorm_g, w_out):
    rows = hx.shape[1] // GRID_W
    kd, vd, rk = GLA_KEY_DIM, GLA_VAL_DIM, GLA_GATE_RANK
    cuts = [kd, 2 * kd, 2 * kd + vd, 2 * kd + 2 * vd, 2 * kd + 2 * vd + rk]

    def project(h):
        q, k, v, og, rf, rb = jnp.split(h @ w_in, cuts, axis=-1)
        gf = jax.nn.log_sigmoid((rf @ gk_w2[0] + gk_b[0]).astype(F32)) / GLA_GATE_NORM
        gb = jax.nn.log_sigmoid((rb @ gk_w2[1] + gk_b[1]).astype(F32)) / GLA_GATE_NORM
        hd = lambda t: heads(t, GLA_HEADS)
        kh = hd(k)
        return (hd(q * GLA_HK ** -0.5), kh, kh, hd(v), hd(gf), hd(gb)), og

    lat, ogx = project(raster_to_colmajor(hx, rows))
    ctxs, ogc = project(hc)
    ox, oc = bidir_gla(lat, ctxs, GLA_CHUNK)
    yx = colmajor_to_raster(gated_head_out(ox, ogx, onorm_g, w_out), rows)
    return yx, gated_head_out(oc, ogc, onorm_g, w_out)


def expert_choice_ffn(h, router_w, w_gate, w_up, w_down):
    bsz, t_len, d = h.shape
    cap = max(1, EC_CAPACITY_FACTOR * t_len // N_EXPERTS)
    aff = jax.nn.softmax(h.astype(F32) @ router_w.astype(F32), axis=-1)
    gate, idx = lax.top_k(jnp.swapaxes(aff, 1, 2), cap)
    xg = jax.vmap(lambda hb, ib: hb[ib])(h, idx)
    hid = jax.nn.silu(jnp.einsum('becd,edf->becf', xg, w_gate)) * jnp.einsum('becd,edf->becf', xg, w_up)
    ye = jnp.einsum('becf,efd->becd', hid, w_down) * gate[..., None]
    return jax.vmap(lambda ib, yb: jnp.zeros((t_len, d), yb.dtype).at[ib.reshape(-1)].add(yb.reshape(-1, d)))(idx, ye)


BF16 = jnp.bfloat16
LANES = 128
VMEM_LIMIT_BYTES = 56 * 2 ** 20
SCAN_CHUNK = 128
SCAN_FAST_RANGE = 80.0
SCAN_UNROLL = 2


def _cumsum_rows(tri_bf16, g):
    hi = g.astype(BF16)
    r1 = g - hi.astype(F32)
    mid = r1.astype(BF16)
    lo = (r1 - mid.astype(F32)).astype(BF16)
    d = lambda a: jnp.dot(tri_bf16, a, preferred_element_type=F32)
    return d(hi) + d(mid) + d(lo)


def _log_sigmoid(z):
    return jnp.minimum(z, 0.0) - jnp.log1p(jnp.exp(-jnp.abs(z)))


def _dot_nt(a, b):
    return lax.dot_general(a, b, (((1,), (1,)), ((), ())), preferred_element_type=F32)


def _dot_tn(a, b):
    return lax.dot_general(a, b, (((0,), (0,)), ((), ())), preferred_element_type=F32)


def _gla_scan_body(mode, seq, ctx_len, kdim, vdim, *refs):
    chunk = SCAN_CHUNK
    if mode == 'hgrn2':
        (q_x, ff_x, fb_x, v_x, og_x, q_c, ff_c, fb_c, v_c, og_c, lb_ref, gain_ref,
         o_x, o_c, cf_s, cb_s, sf_s, sb_s, kt_s, gt_s, qf_s, qb_s, of_s, ob_s, kf_s, kb_s) = refs
    else:
        (q_x, k_x, v_x, og_x, r_x, q_c, k_c, v_c, og_c, r_c, w2f_ref, w2b_ref, bf_ref, bb_ref, gain_ref,
         o_x, o_c, cf_s, cb_s, sf_s, sb_s, kt_s, gt_s, qf_s, qb_s, of_s, ob_s) = refs
    if mode == 'hgrn2':
        segs = (((ff_c, fb_c), q_c, None, v_c, og_c, o_c, ctx_len, 0),
                ((ff_x, fb_x), q_x, None, v_x, og_x, o_x, seq, ctx_len))
    else:
        segs = (((r_c,), q_c, k_c, v_c, og_c, o_c, ctx_len, 0),
                ((r_x,), q_x, k_x, v_x, og_x, o_x, seq, ctx_len))

    row = lax.broadcasted_iota(jnp.int32, (chunk, chunk), 0)
    col = lax.broadcasted_iota(jnp.int32, (chunk, chunk), 1)
    masks = (row >= col, row <= col)
    tris = tuple(m.astype(BF16) for m in masks)
    cum_s, state_s, qin_s, kout_s = (cf_s, cb_s), (sf_s, sb_s), (qf_s, qb_s), (of_s, ob_s)
    end_row = (chunk - 1, 0)
    mid_row = chunk // 2

    def rows_of(i, off=0):
        return pl.ds(pl.multiple_of(off + i * chunk, chunk), chunk)

    def gate_pass(seg, bound):
        gsrc, _, _, _, _, _, length, off = seg

        def body(i, bound):
            rows, srows = rows_of(i), rows_of(i, off)
            if mode == 'hgrn2':
                lb = lb_ref[...]
                f_f = lb + (1.0 - lb) * jax.nn.sigmoid(gsrc[0][rows, :])
                f_b = lb + (1.0 - lb) * jax.nn.sigmoid(gsrc[1][rows, :])
                g_f, g_b = jnp.log(f_f), jnp.log(f_b)
                kf_s[srows, :] = 1.0 - f_f
                kb_s[srows, :] = 1.0 - f_b
            else:
                r = gsrc[0][rows, :].astype(BF16)
                z_f = jnp.dot(r, w2f_ref[...].astype(BF16), preferred_element_type=F32) + bf_ref[...]
                z_b = jnp.dot(r, w2b_ref[...].astype(BF16), preferred_element_type=F32) + bb_ref[...]
                g_f = _log_sigmoid(z_f) / GLA_GATE_NORM
                g_b = _log_sigmoid(z_b) / GLA_GATE_NORM
            c_f = _cumsum_rows(tris[0], g_f)
            c_b = _cumsum_rows(tris[1], g_b)
            cf_s[srows, :] = c_f
            cb_s[srows, :] = c_b
            for c in (c_f, c_b):
                mid = c[mid_row:mid_row + 1, :]
                spread = jnp.maximum(jnp.abs(c[0:1, :] - mid), jnp.abs(c[chunk - 1:chunk, :] - mid))
                bound = jnp.maximum(bound, spread)
            return bound

        return lax.fori_loop(0, length // chunk, body, bound, unroll=SCAN_UNROLL)

    def intra(seg, direction, i, exact):
        _, q_r, k_r, v_r, _, _, _, off = seg
        rows, srows = rows_of(i), rows_of(i, off)
        if mode == 'hgrn2':
            qq = jax.nn.silu(q_r[rows, :]) * HG_KEY ** -0.5
            k = (kf_s if direction == 0 else kb_s)[srows, :]
        else:
            qq = q_r[rows, :] * GLA_HK ** -0.5
            k = k_r[rows, :]
        cum = cum_s[direction][srows, :]
        e = end_row[direction]
        cum_end = cum[e:e + 1, :]
        q_in = (qq * jnp.exp(cum)).astype(BF16)
        qin_s[direction][srows, :] = q_in
        kout_s[direction][srows, :] = (k * jnp.exp(cum_end - cum)).astype(BF16)
        if exact:
            kt_s[...] = k
            gt_s[...] = cum

            def col_body(s, att):
                ks = kt_s[pl.ds(s, 1), :]
                gs = gt_s[pl.ds(s, 1), :]
                p = qq * ks * jnp.exp(jnp.minimum(cum - gs, 0.0))
                return att + jnp.where(col == s, jnp.sum(p, axis=-1, keepdims=True), 0.0)

            att = lax.fori_loop(0, chunk, col_body, jnp.zeros((chunk, chunk), F32))
        else:
            rel = cum - cum[mid_row:mid_row + 1, :]
            att = _dot_nt((qq * jnp.exp(rel)).astype(BF16), (k * jnp.exp(-rel)).astype(BF16))
        att = jnp.where(masks[direction], att, 0.0).astype(BF16)
        return jnp.dot(att, v_r[rows, :].astype(BF16), preferred_element_type=F32)

    def intra_pass(exact):
        for seg in segs:
            o_r, length = seg[5], seg[6]

            def body(i, carry):
                o_r[rows_of(i), :] = intra(seg, 0, i, exact) + intra(seg, 1, i, exact)
                return carry

            lax.fori_loop(0, length // chunk, body, 0, unroll=1 if exact else SCAN_UNROLL)

    def inter(seg, direction, i):
        v_r, off = seg[3], seg[7]
        rows, srows = rows_of(i), rows_of(i, off)
        s_ref = state_s[direction]
        state = s_ref[...]
        o = _dot_nt(qin_s[direction][srows, :], state.astype(BF16))
        e = pl.ds(pl.multiple_of(off + i * chunk, chunk) + end_row[direction], 1)
        decay = jnp.exp(cum_s[direction][e, :])
        s_ref[...] = state * decay + _dot_tn(v_r[rows, :].astype(BF16), kout_s[direction][srows, :])
        return o

    def finish(seg, i, o):
        o = o * lax.rsqrt(jnp.mean(o * o, axis=-1, keepdims=True) + NORM_EPS) * gain_ref[...]
        return o * jax.nn.silu(seg[4][rows_of(i), :])

    def inter_pass():
        sf_s[...] = jnp.zeros_like(sf_s)
        sb_s[...] = jnp.zeros_like(sb_s)
        for seg in segs:
            o_r, length = seg[5], seg[6]
            n = length // chunk

            def first_half(i, carry):
                j = n - 1 - i
                o_r[rows_of(i), :] += inter(seg, 0, i)
                o_r[rows_of(j), :] += inter(seg, 1, j)
                return carry

            def second_half(i, carry):
                j = n - 1 - i
                o_r[rows_of(i), :] = finish(seg, i, o_r[rows_of(i), :] + inter(seg, 0, i))
                o_r[rows_of(j), :] = finish(seg, j, o_r[rows_of(j), :] + inter(seg, 1, j))
                return carry

            lax.fori_loop(0, n // 2, first_half, 0, unroll=min(SCAN_UNROLL, n // 2))
            lax.fori_loop(n // 2, n, second_half, 0, unroll=min(SCAN_UNROLL, n // 2))

    bound = jnp.zeros((1, kdim), F32)
    for seg in segs:
        bound = gate_pass(seg, bound)
    fast = jnp.max(bound) <= SCAN_FAST_RANGE

    @pl.when(fast)
    def _():
        intra_pass(False)

    @pl.when(jnp.logical_not(fast))
    def _():
        intra_pass(True)

    inter_pass()


def _gla_scan_call(mode, nheads, kdim, vdim, lat_in, ctx_in, small_in, bsz, seq, ctx_len):
    def stream_spec(length, width, first, stride):
        return pl.BlockSpec((None, length, width), lambda b, h: (b, 0, first + stride * h))

    assert seq % (2 * SCAN_CHUNK) == 0 and ctx_len % (2 * SCAN_CHUNK) == 0
    in_specs = [stream_spec(seq, w, f, s) for _, w, f, s in lat_in]
    in_specs += [stream_spec(ctx_len, w, f, s) for _, w, f, s in ctx_in]
    in_specs += [pl.BlockSpec(blk, imap) for _, blk, imap in small_in]
    args = [a for a, _, _, _ in lat_in] + [a for a, _, _, _ in ctx_in] + [a for a, _, _ in small_in]
    tot = seq + ctx_len
    scratch = [pltpu.VMEM((tot, kdim), F32), pltpu.VMEM((tot, kdim), F32),
               pltpu.VMEM((vdim, kdim), F32), pltpu.VMEM((vdim, kdim), F32),
               pltpu.VMEM((SCAN_CHUNK, kdim), F32), pltpu.VMEM((SCAN_CHUNK, kdim), F32),
               pltpu.VMEM((tot, kdim), BF16), pltpu.VMEM((tot, kdim), BF16),
               pltpu.VMEM((tot, kdim), BF16), pltpu.VMEM((tot, kdim), BF16)]
    if mode == 'hgrn2':
        scratch += [pltpu.VMEM((tot, kdim), F32), pltpu.VMEM((tot, kdim), F32)]
    return pl.pallas_call(
        lambda *refs: _gla_scan_body(mode, seq, ctx_len, kdim, vdim, *refs),
        grid=(bsz, nheads),
        in_specs=in_specs,
        out_specs=[pl.BlockSpec((None, seq, vdim), lambda b, h: (b, 0, h)),
                   pl.BlockSpec((None, ctx_len, vdim), lambda b, h: (b, 0, h))],
        out_shape=[jax.ShapeDtypeStruct((bsz, seq, nheads * vdim), F32),
                   jax.ShapeDtypeStruct((bsz, ctx_len, nheads * vdim), F32)],
        scratch_shapes=scratch,
        compiler_params=pltpu.CompilerParams(dimension_semantics=("arbitrary", "arbitrary"),
                                             vmem_limit_bytes=VMEM_LIMIT_BYTES),
        name=f"{mode}_scan",
    )(*args)


def hgrn2_mixer_pallas(hx, hc, w_in, lb, onorm_g, w_out):
    bsz, seq, _ = hx.shape
    ctx_len = hc.shape[1]
    yx, yc = hx @ w_in, hc @ w_in
    nh = HG_HEADS
    streams = lambda y: [(y, HG_KEY, j * nh, 1) for j in range(5)]
    small = [(lb.reshape(1, D_MODEL), (1, HG_KEY), lambda b, h: (0, h)),
             (onorm_g.reshape(1, HG_VAL), (1, HG_VAL), lambda b, h: (0, 0))]
    ox, oc = _gla_scan_call('hgrn2', nh, HG_KEY, HG_VAL, streams(yx), streams(yc), small, bsz, seq, ctx_len)
    return ox @ w_out, oc @ w_out


def gla_mixer_pallas(hx, hc, w_in, gk_w2, gk_b, onorm_g, w_out):
    bsz, seq, _ = hx.shape
    ctx_len = hc.shape[1]
    rows = seq // GRID_W
    kd, vd, rk = GLA_KEY_DIM, GLA_VAL_DIM, GLA_GATE_RANK
    nh = GLA_HEADS
    w_in_p = jnp.pad(w_in, ((0, 0), (0, LANES - 2 * rk)))
    yx = raster_to_colmajor(hx, rows) @ w_in_p
    yc = hc @ w_in_p
    w2f = jnp.pad(gk_w2[0], ((0, LANES - rk), (0, 0)))
    w2b = jnp.pad(gk_w2[1], ((rk, LANES - 2 * rk), (0, 0)))
    streams = lambda y: [(y, GLA_HK, 0, 1), (y, GLA_HK, kd // GLA_HK, 1), (y, GLA_HV, 2 * kd // GLA_HV, 1),
                         (y, GLA_HV, (2 * kd + vd) // GLA_HV, 1), (y, LANES, (2 * kd + 2 * vd) // LANES, 0)]
    small = [(w2f, (LANES, GLA_HK), lambda b, h: (0, h)), (w2b, (LANES, GLA_HK), lambda b, h: (0, h)),
             (gk_b[0].reshape(1, kd), (1, GLA_HK), lambda b, h: (0, h)),
             (gk_b[1].reshape(1, kd), (1, GLA_HK), lambda b, h: (0, h)),
             (onorm_g.reshape(1, GLA_HV), (1, GLA_HV), lambda b, h: (0, 0))]
    ox, oc = _gla_scan_call('gla', nh, GLA_HK, GLA_HV, streams(yx), streams(yc), small, bsz, seq, ctx_len)
    return colmajor_to_raster(ox @ w_out, rows), oc @ w_out


MB_GROUP_W = MB_INNER // MB_GROUPS


def _ssd_scan_body(seq, ctx_len, *refs):
    chunk = MB_CHUNK
    gw, hpg = MB_GROUP_W, MB_HPG
    (x_x, b_x, c_x, z_x, cl_x, rt_x, x_c, b_c, c_c, z_c, cl_c, rt_c, dskip_ref, gain_ref,
     o_x, o_c, xf_s, xb_s, sf_s, sb_s) = refs
    segs = ((x_c, b_c, c_c, z_c, cl_c, rt_c, o_c, ctx_len, 0),
            (x_x, b_x, c_x, z_x, cl_x, rt_x, o_x, seq, ctx_len))
    xs_s, state_s = (xf_s, xb_s), (sf_s, sb_s)
    end_row = (chunk - 1, 0)

    row = lax.broadcasted_iota(jnp.int32, (chunk, chunk), 0)
    col = lax.broadcasted_iota(jnp.int32, (chunk, chunk), 1)
    masks = (row >= col, row <= col)
    head_of_lane = lax.broadcasted_iota(jnp.int32, (1, gw), 1) // MB_HEADDIM

    def rows_of(i, off=0):
        return pl.ds(pl.multiple_of(off + i * chunk, chunk), chunk)

    def bcast_heads(c4):
        out = c4[:, hpg - 1:hpg]
        for hh in range(hpg - 2, -1, -1):
            out = jnp.where(head_of_lane == hh, c4[:, hh:hh + 1], out)
        return out

    def intra_pass():
        for seg in segs:
            x_r, b_r, c_r, _, cl_r, rt_r, o_r, length, off = seg

            def body(i, carry):
                rows, srows = rows_of(i), rows_of(i, off)
                cols, rt = cl_r[i], rt_r[i]
                x = x_r[rows, :]
                cb = _dot_nt(c_r[rows, :].astype(BF16), b_r[rows, :].astype(BF16))
                xh = [jnp.where(head_of_lane == hh, x, 0.0).astype(BF16) for hh in range(hpg)]
                y = jnp.zeros((chunk, gw), F32)
                for d in range(2):
                    cum = cols[:, d * hpg:(d + 1) * hpg]
                    dt = cols[:, (2 + d) * hpg:(3 + d) * hpg]
                    e = end_row[d]
                    scale = jnp.exp(cum[e:e + 1, :] - cum) * dt
                    xs_s[d][srows, :] = (x * bcast_heads(scale)).astype(BF16)
                    for hh in range(hpg):
                        j = d * hpg + hh
                        seg_decay = jnp.exp(jnp.minimum(cols[:, j:j + 1] - rt[j:j + 1, :], 0.0))
                        w = jnp.where(masks[d], cb * seg_decay * rt[2 * hpg + j:2 * hpg + j + 1, :], 0.0)
                        y = y + jnp.dot(w.astype(BF16), xh[hh], preferred_element_type=F32)
                o_r[rows, :] = y
                return carry

            lax.fori_loop(0, length // chunk, body, 0, unroll=2)

    def inter(seg, d, i):
        _, b_r, c_r, _, cl_r, _, _, _, off = seg
        rows, srows = rows_of(i), rows_of(i, off)
        cum = cl_r[i][:, d * hpg:(d + 1) * hpg]
        state = state_s[d][...]
        y = jnp.dot(c_r[rows, :].astype(BF16), state.astype(BF16), preferred_element_type=F32)
        e = end_row[d]
        decay = bcast_heads(jnp.exp(cum[e:e + 1, :]))
        state_s[d][...] = state * decay + _dot_tn(b_r[rows, :].astype(BF16), xs_s[d][srows, :])
        return y * bcast_heads(jnp.exp(cum))

    def finish(seg, i, y):
        x_r, z_r = seg[0], seg[3]
        rows = rows_of(i)
        y = (y + dskip_ref[...] * x_r[rows, :]) * jax.nn.silu(z_r[rows, :])
        return y * lax.rsqrt(jnp.mean(y * y, axis=-1, keepdims=True) + NORM_EPS) * gain_ref[...]

    def inter_pass():
        sf_s[...] = jnp.zeros_like(sf_s)
        sb_s[...] = jnp.zeros_like(sb_s)
        for seg in segs:
            o_r, length = seg[6], seg[7]
            n = length // chunk

            def first_half(i, carry):
                j = n - 1 - i
                o_r[rows_of(i), :] += inter(seg, 0, i)
                o_r[rows_of(j), :] += inter(seg, 1, j)
                return carry

            def second_half(i, carry):
                j = n - 1 - i
                o_r[rows_of(i), :] = finish(seg, i, o_r[rows_of(i), :] + inter(seg, 0, i))
                o_r[rows_of(j), :] = finish(seg, j, o_r[rows_of(j), :] + inter(seg, 1, j))
                return carry

            lax.fori_loop(0, n // 2, first_half, 0, unroll=2)
            lax.fori_loop(n // 2, n, second_half, 0, unroll=2)

    intra_pass()
    inter_pass()


def _ssd_head_tables(dt_raw, dt_bias, a_log):
    bsz, length, _ = dt_raw.shape
    n = length // MB_CHUNK
    a = -jnp.exp(a_log.astype(F32)).reshape(2, MB_GROUPS, MB_HPG)
    dt = jax.nn.softplus(dt_raw.astype(F32).reshape(bsz, length, 2, MB_GROUPS, MB_HPG)
                         + dt_bias.reshape(2, MB_GROUPS, MB_HPG))
    dta = (dt * a).reshape(bsz, n, MB_CHUNK, 2, MB_GROUPS, MB_HPG)
    dtc = dt.reshape(bsz, n, MB_CHUNK, 2, MB_GROUPS, MB_HPG)
    cum_f = jnp.cumsum(dta[:, :, :, 0], axis=2)
    cum_b = jnp.flip(jnp.cumsum(jnp.flip(dta[:, :, :, 1], axis=2), axis=2), axis=2)
    cols = jnp.concatenate([cum_f, cum_b, dtc[:, :, :, 0], dtc[:, :, :, 1]], axis=-1)
    cols = cols.transpose(0, 3, 1, 2, 4)
    return cols, jnp.swapaxes(cols, -1, -2)


def mamba2_mixer_pallas(hx, hc, w_in, conv_w, conv_b, dt_bias, a_log, d_skip, norm_g, w_out):
    bsz, seq, _ = hx.shape
    ctx_len = hc.shape[1]
    chunk, gw = MB_CHUNK, MB_GROUP_W

    def project(h):
        y = h @ w_in
        xbc = jax.nn.silu(dwconv_centered(y[..., MB_INNER:MB_INNER + MB_CONV_DIM], conv_w, conv_b))
        cols, rows_t = _ssd_head_tables(y[..., MB_INNER + MB_CONV_DIM:], dt_bias, a_log)
        return y, xbc, cols, rows_t

    yx, xbcx, clx, rtx = project(hx)
    yc, xbcc, clc, rtc = project(hc)

    def stream_specs(length):
        n = length // chunk
        bw = MB_INNER // MB_STATE
        return [pl.BlockSpec((None, length, gw), lambda b, g: (b, 0, g)),
                pl.BlockSpec((None, length, MB_STATE), lambda b, g: (b, 0, bw + g)),
                pl.BlockSpec((None, length, MB_STATE), lambda b, g: (b, 0, bw + MB_GROUPS + g)),
                pl.BlockSpec((None, length, gw), lambda b, g: (b, 0, g)),
                pl.BlockSpec((None, None, n, chunk, 4 * MB_HPG), lambda b, g: (b, g, 0, 0, 0)),
                pl.BlockSpec((None, None, n, 4 * MB_HPG, chunk), lambda b, g: (b, g, 0, 0, 0))]

    tot = seq + ctx_len
    ox, oc = pl.pallas_call(
        lambda *refs: _ssd_scan_body(seq, ctx_len, *refs),
        grid=(bsz, MB_GROUPS),
        in_specs=stream_specs(seq) + stream_specs(ctx_len) + [
            pl.BlockSpec((1, gw), lambda b, g: (0, g)), pl.BlockSpec((1, gw), lambda b, g: (0, g))],
        out_specs=[pl.BlockSpec((None, seq, gw), lambda b, g: (b, 0, g)),
                   pl.BlockSpec((None, ctx_len, gw), lambda b, g: (b, 0, g))],
        out_shape=[jax.ShapeDtypeStruct((bsz, seq, MB_INNER), F32),
                   jax.ShapeDtypeStruct((bsz, ctx_len, MB_INNER), F32)],
        scratch_shapes=[pltpu.VMEM((tot, gw), BF16), pltpu.VMEM((tot, gw), BF16),
                        pltpu.VMEM((MB_STATE, gw), F32), pltpu.VMEM((MB_STATE, gw), F32)],
        compiler_params=pltpu.CompilerParams(dimension_semantics=("arbitrary", "arbitrary"),
                                             vmem_limit_bytes=VMEM_LIMIT_BYTES),
        name="ssd_scan",
    )(xbcx, xbcx, xbcx, yx, clx, rtx, xbcc, xbcc, xbcc, yc, clc, rtc,
      jnp.repeat(d_skip, MB_HEADDIM).reshape(1, MB_INNER), norm_g.reshape(1, MB_INNER))
    return ox @ w_out, oc @ w_out


HY_BLOCK = 256
HY_FREQ_ROWS = 64


def _split_bf16(a):
    hi = a.astype(BF16)
    return hi, (a - hi.astype(F32)).astype(BF16)


def _dot_split(w_hi, w_lo, x):
    x_hi, x_lo = _split_bf16(x)
    d = lambda a, b: jnp.dot(a, b, preferred_element_type=F32)
    return d(w_hi, x_hi) + d(w_hi, x_lo) + d(w_lo, x_hi)


def _hyena_dft_matrices():
    p = HY_BLOCK
    f = jnp.arange(p, dtype=F32)[:, None] + 0.5
    b = jnp.arange(p, dtype=F32)[None, :]
    k = jnp.round(f * b * 2.0).astype(jnp.int32) % (4 * p)
    ang = k.astype(F32) * (2.0 * math.pi / (4 * p))
    fwd = jnp.concatenate([jnp.cos(ang), -jnp.sin(ang)], axis=0)
    inv = jnp.concatenate([jnp.cos(ang).T, -jnp.sin(ang).T], axis=1) / p
    return fwd, inv


def _hyena_filter_spectra(hf, hb, fwd):
    length, d = hf.shape
    p = HY_BLOCK
    nb = length // p
    h2 = jnp.concatenate([jnp.zeros((1, d), F32), jnp.flip(hb[1:], axis=0), hf], axis=0)
    blocks = h2.reshape(2 * nb, p, d)
    spec = jnp.einsum('fj,kjd->kfd', fwd, blocks, precision=lax.Precision.HIGHEST)
    s_re, s_im = spec[:, :p], spec[:, p:]
    sign = jnp.where(jnp.arange(p) % 2 == 0, 1.0, -1.0).astype(F32)[None, :, None]
    return s_re[1:] - sign * s_im[:-1], s_im[1:] + sign * s_re[:-1]


def _hyena_conv_body(nb, u_ref, x_ref, bias_ref, gre_ref, gim_ref, wfh_ref, wfl_ref, wih_ref, wil_ref,
                     o_ref, ure_s, uim_s, y_s):
    p, fr = HY_BLOCK, HY_FREQ_ROWS

    def rows_of(i):
        return pl.ds(pl.multiple_of(i * p, p), p)

    def forward(j, carry):
        spec = _dot_split(wfh_ref[...], wfl_ref[...], u_ref[rows_of(j), :])
        ure_s[j] = spec[:p]
        uim_s[j] = spec[p:]
        return carry

    lax.fori_loop(0, nb, forward, 0)

    def out_block(i, carry):
        for t in range(p // fr):
            r = pl.ds(t * fr, fr)

            def acc_body(j, acc):
                a_re, a_im = acc
                k = i - j + nb - 1
                g_re, g_im = gre_ref[k, r, :], gim_ref[k, r, :]
                u_re, u_im = ure_s[j, r, :], uim_s[j, r, :]
                return a_re + g_re * u_re - g_im * u_im, a_im + g_re * u_im + g_im * u_re

            zero = jnp.zeros((fr, u_ref.shape[-1]), F32)
            a_re, a_im = lax.fori_loop(0, nb, acc_body, (zero, zero), unroll=min(nb, 4))
            y_s[pl.ds(t * fr, fr), :] = a_re
            y_s[pl.ds(p + t * fr, fr), :] = a_im
        y = _dot_split(wih_ref[...], wil_ref[...], y_s[...])
        u = u_ref[rows_of(i), :]
        o_ref[rows_of(i), :] = x_ref[rows_of(i), :] * (y + u * bias_ref[...])
        return carry

    lax.fori_loop(0, nb, out_block, 0)


def _hyena_conv(u_arr, u_blk, x_arr, x_blk, bias, g_re, g_im, mats):
    bsz, length, _ = u_arr.shape
    d = bias.shape[-1]
    p = HY_BLOCK
    nb = length // p
    nseg = 2 * nb - 1
    const = lambda shape: pl.BlockSpec(shape, lambda c, b: (0, 0))
    return pl.pallas_call(
        lambda *refs: _hyena_conv_body(nb, *refs),
        grid=(d // LANES, bsz),
        in_specs=[pl.BlockSpec((None, length, LANES), lambda c, b: (b, 0, u_blk + c)),
                  pl.BlockSpec((None, length, LANES), lambda c, b: (b, 0, x_blk + c)),
                  pl.BlockSpec((1, LANES), lambda c, b: (0, c)),
                  pl.BlockSpec((nseg, p, LANES), lambda c, b: (0, 0, c)),
                  pl.BlockSpec((nseg, p, LANES), lambda c, b: (0, 0, c)),
                  const((2 * p, p)), const((2 * p, p)), const((p, 2 * p)), const((p, 2 * p))],
        out_specs=pl.BlockSpec((None, length, LANES), lambda c, b: (b, 0, c)),
        out_shape=jax.ShapeDtypeStruct((bsz, length, d), F32),
        scratch_shapes=[pltpu.VMEM((nb, p, LANES), F32), pltpu.VMEM((nb, p, LANES), F32),
                        pltpu.VMEM((2 * p, LANES), F32)],
        compiler_params=pltpu.CompilerParams(dimension_semantics=("arbitrary", "arbitrary"),
                                             vmem_limit_bytes=VMEM_LIMIT_BYTES),
        name="hyena_conv",
    )(u_arr, x_arr, bias.reshape(1, d), g_re, g_im, *mats)


def hyena_mixer_pallas(hx, hc, w_in, b_in, short_w, short_b, f_w1, f_b1, f_w2, f_b2, f_w3, f_b3, f_w4,
                       f_freq, f_bias, w_out, b_out):
    fwd, inv = _hyena_dft_matrices()
    mats = _split_bf16(fwd) + _split_bf16(inv)
    nblk = D_MODEL // LANES

    def taps(length):
        z, t = hyena_pos_features(length)
        a = jnp.sin(f_freq * (z @ f_w1 + f_b1))
        a = jnp.sin(f_freq * (a @ f_w2 + f_b2))
        a = jnp.sin(f_freq * (a @ f_w3 + f_b3))
        mw = (a @ f_w4).astype(F32) * jnp.tile(hyena_window(t), (1, 2 * HY_ORDER))
        norm = jnp.sum(jnp.abs(mw), axis=0).reshape(HY_ORDER, 2, D_MODEL).sum(axis=1)
        return mw, 1.0 / norm

    def spectra(mw, inv_norm, o):
        hf = mw[:, (2 * o) * D_MODEL:(2 * o + 1) * D_MODEL]
        hb = mw[:, (2 * o + 1) * D_MODEL:(2 * o + 2) * D_MODEL]
        g_re, g_im = _hyena_filter_spectra(hf, hb, fwd)
        return g_re * inv_norm[o], g_im * inv_norm[o]

    def run(h):
        length = h.shape[1]
        u = dwconv_centered(h @ w_in + b_in, short_w, short_b).astype(F32)
        mw, inv_norm = taps(length)
        g0, g1 = spectra(mw, inv_norm, 0), spectra(mw, inv_norm, 1)
        z = _hyena_conv(u, 2 * nblk, u, 0, f_bias[0], g0[0], g0[1], mats)
        z = _hyena_conv(z, 0, u, nblk, f_bias[1], g1[0], g1[1], mats)
        return z @ w_out + b_out

    return run(hx), run(hc)


MOE_ROW_TILE = 1024
MOE_COL_TILE = 512


def _moe_combine_body(t_len, idx_ref, gate_ref, ye_ref, x_ref, g2_ref, o_ref):
    e = pl.program_id(2)
    cap = idx_ref.shape[-1]
    rt = min(MOE_ROW_TILE, t_len)

    @pl.when(e == 0)
    def _():
        o_ref[...] = jnp.zeros_like(o_ref)

    yg = (ye_ref[...] * gate_ref[...]).astype(BF16)
    idx = idx_ref[...]

    def body(r, carry):
        r0 = pl.multiple_of(r * rt, rt)
        tok = lax.broadcasted_iota(jnp.int32, (rt, cap), 0) + r0
        onehot = jnp.where(tok == idx, 1.0, 0.0).astype(BF16)
        o_ref[pl.ds(r0, rt), :] += jnp.dot(onehot, yg, preferred_element_type=F32)
        return carry

    lax.fori_loop(0, t_len // rt, body, 0)

    @pl.when(e == pl.num_programs(2) - 1)
    def _():
        o_ref[...] = x_ref[...] + g2_ref[...] * o_ref[...]


def expert_choice_ffn_residual(x, g2, h, router_w, w_gate, w_up, w_down):
    bsz, t_len, d = h.shape
    cap = max(1, EC_CAPACITY_FACTOR * t_len // N_EXPERTS)
    aff = jax.nn.softmax(h.astype(F32) @ router_w.astype(F32), axis=-1)
    gate, idx = lax.top_k(jnp.swapaxes(aff, 1, 2), cap)
    xg = jax.vmap(lambda hb, ib: hb[ib])(h.astype(BF16), idx)
    pre_g = jnp.einsum('becd,edf->becf', xg, w_gate, preferred_element_type=BF16)
    pre_u = jnp.einsum('becd,edf->becf', xg, w_up, preferred_element_type=BF16)
    hid = (jax.nn.silu(pre_g.astype(F32)) * pre_u.astype(F32)).astype(BF16)
    ye = jnp.einsum('becf,efd->becd', hid, w_down, preferred_element_type=F32)
    ct = MOE_COL_TILE
    g2 = jnp.broadcast_to(g2.reshape(-1, 1, d), (bsz, 1, d))
    return pl.pallas_call(
        lambda *refs: _moe_combine_body(t_len, *refs),
        grid=(bsz, d // ct, N_EXPERTS),
        in_specs=[pl.BlockSpec((None, None, 1, cap), lambda b, c, e: (b, e, 0, 0)),
                  pl.BlockSpec((None, None, cap, 1), lambda b, c, e: (b, e, 0, 0)),
                  pl.BlockSpec((None, None, cap, ct), lambda b, c, e: (b, e, 0, c)),
                  pl.BlockSpec((None, t_len, ct), lambda b, c, e: (b, 0, c)),
                  pl.BlockSpec((None, 1, ct), lambda b, c, e: (b, 0, c))],
        out_specs=pl.BlockSpec((None, t_len, ct), lambda b, c, e: (b, 0, c)),
        out_shape=jax.ShapeDtypeStruct((bsz, t_len, d), F32),
        compiler_params=pltpu.CompilerParams(dimension_semantics=("arbitrary", "arbitrary", "arbitrary"),
                                             vmem_limit_bytes=VMEM_LIMIT_BYTES),
        name="moe_combine",
    )(idx.reshape(bsz, N_EXPERTS, 1, cap), gate.reshape(bsz, N_EXPERTS, cap, 1), ye, x, g2)


def _final_norm_body(x_ref, g_ref, o_ref):
    xf = x_ref[...]
    o_ref[...] = xf * lax.rsqrt(jnp.mean(xf * xf, axis=-1, keepdims=True) + NORM_EPS) * g_ref[...]


def final_rmsnorm(x, g):
    b, l, d = x.shape
    rows = b * l
    tm = 512
    out = pl.pallas_call(
        _final_norm_body,
        grid=(rows // tm,),
        in_specs=[pl.BlockSpec((tm, d), lambda i: (i, 0)), pl.BlockSpec((1, d), lambda i: (0, 0))],
        out_specs=pl.BlockSpec((tm, d), lambda i: (i, 0)),
        out_shape=jax.ShapeDtypeStruct((rows, d), F32),
    )(x.reshape(rows, d), g.reshape(1, d))
    return out.reshape(b, l, d)


def kernel(x, c, ctx, c_ctx, ada_w, ada_b, norm1_g, norm2_g,
           hg_w_in, hg_lb, hg_onorm_g, hg_w_out,
           hy_w_in, hy_b_in, hy_short_w, hy_short_b, hy_f_w1, hy_f_b1, hy_f_w2, hy_f_b2,
           hy_f_w3, hy_f_b3, hy_f_w4, hy_f_freq, hy_f_bias, hy_w_out, hy_b_out,
           mb_w_in, mb_conv_w, mb_conv_b, mb_dt_bias, mb_a_log, mb_d, mb_norm_g, mb_w_out,
           gla_w_in, gla_gk_w2, gla_gk_b, gla_onorm_g, gla_w_out,
           router_w, moe_w_gate, moe_w_up, moe_w_down, final_norm_g):
    for i in range(DEPTH):
        last = i == DEPTH - 1
        sh1, sc1, g1, sh2, sc2, g2 = adaln(c, ada_w[i], ada_b[i])
        csh1, csc1, cg1, csh2, csc2, cg2 = adaln(c_ctx, ada_w[i], ada_b[i])
        hx = modulate(rmsnorm(x, norm1_g[i]), sh1, sc1)
        hc = modulate(rmsnorm(ctx, norm1_g[i]), csh1, csc1)
        kind, j = i % N_MIXERS, i // N_MIXERS
        if kind == 0:
            yx, yc = hgrn2_mixer_pallas(hx, hc, hg_w_in[j], hgrn_lower_bound(hg_lb, i), hg_onorm_g[j], hg_w_out[j])
        elif kind == 1:
            yx, yc = hyena_mixer_pallas(hx, hc, hy_w_in[j], hy_b_in[j], hy_short_w[j], hy_short_b[j],
                                 hy_f_w1[j], hy_f_b1[j], hy_f_w2[j], hy_f_b2[j], hy_f_w3[j], hy_f_b3[j],
                                 hy_f_w4[j], hy_f_freq[j], hy_f_bias[j], hy_w_out[j], hy_b_out[j])
        elif kind == 2:
            yx, yc = mamba2_mixer_pallas(hx, hc, mb_w_in[j], mb_conv_w[j], mb_conv_b[j], mb_dt_bias[j],
                                  mb_a_log[j], mb_d[j], mb_norm_g[j], mb_w_out[j])
        else:
            yx, yc = gla_mixer_pallas(hx, hc, gla_w_in[j], gla_gk_w2[j], gla_gk_b[j], gla_onorm_g[j], gla_w_out[j])
        x = x + g1 * yx
        x = expert_choice_ffn_residual(x, g2, modulate(rmsnorm(x, norm2_g[i]), sh2, sc2),
                                       router_w[i], moe_w_gate[i], moe_w_up[i], moe_w_down[i])
        if not last:
            ctx = ctx + cg1 * yc
            ctx = expert_choice_ffn_residual(ctx, cg2, modulate(rmsnorm(ctx, norm2_g[i]), csh2, csc2),
                                             router_w[i], moe_w_gate[i], moe_w_up[i], moe_w_down[i])
    return final_rmsnorm(x, final_norm_g)
```

```python
import math
import jax, jax.numpy as jnp
from jax import lax
from jax.experimental import pallas as pl
from jax.experimental.pallas import tpu as pltpu

D_MODEL = 1024
BATCH = 8
SEQ = 4096
DEPTH = 4

F32 = jnp.float32
GRID_W = 64
CTX_LEN = 256
N_MIXERS = 4
NORM_EPS = 1e-6

HG_HEADS = 8
HG_KEY = D_MODEL // HG_HEADS
HG_VAL = D_MODEL // HG_HEADS
HG_CHUNK = 32

HY_ORDER = 2
HY_SHORT = 3
HY_EMB = 33
HY_FILTER_W = 64
HY_DECAY_TARGET = 1e-2
HY_DECAY_HI_PCT = 0.3
HY_DECAY_LO_PCT = 1.5

MB_INNER = 2 * D_MODEL
MB_HEADDIM = 64
MB_HEADS = MB_INNER // MB_HEADDIM
MB_GROUPS = 8
MB_HPG = MB_HEADS // MB_GROUPS
MB_STATE = 128
MB_CONV = 5
MB_CHUNK = 64
MB_CONV_DIM = MB_INNER + 2 * MB_GROUPS * MB_STATE
MB_IN = MB_INNER + MB_CONV_DIM + 2 * MB_HEADS

GLA_HEADS = 4
GLA_KEY_DIM = D_MODEL // 2
GLA_VAL_DIM = D_MODEL
GLA_HK = GLA_KEY_DIM // GLA_HEADS
GLA_HV = GLA_VAL_DIM // GLA_HEADS
GLA_GATE_RANK = 16
GLA_GATE_NORM = 16.0
GLA_CHUNK = 32
GLA_IN = 2 * GLA_KEY_DIM + 2 * GLA_VAL_DIM + 2 * GLA_GATE_RANK

N_EXPERTS = 16
EC_CAPACITY_FACTOR = 2
EXPERT_FF = 2048


def rmsnorm(x, g):
    xf = x.astype(F32)
    return xf * lax.rsqrt(jnp.mean(xf * xf, axis=-1, keepdims=True) + NORM_EPS) * g


def adaln(cond, w, b):
    m = jax.nn.silu(cond.astype(F32)) @ w + b
    return jnp.split(m[..., None, :], 6, axis=-1)


def modulate(h, shift, scale):
    return h * (1.0 + scale) + shift


def heads(t, n):
    b, l, _ = t.shape
    return t.reshape(b, l, n, -1).transpose(0, 2, 1, 3)


def dwconv_centered(x, w, b):
    k_w = w.shape[0]
    pad = k_w // 2
    length = x.shape[1]
    xp = jnp.pad(x, ((0, 0), (pad, pad), (0, 0)))
    return sum(xp[:, j:j + length] * w[j] for j in range(k_w)) + b


def raster_to_colmajor(h, rows):
    b, l, d = h.shape
    return h.reshape(b, rows, GRID_W, d).transpose(0, 2, 1, 3).reshape(b, l, d)


def colmajor_to_raster(h, rows):
    b, l, d = h.shape
    return h.reshape(b, GRID_W, rows, d).transpose(0, 2, 1, 3).reshape(b, l, d)


def chunk_gla(q, k, v, g, s0, chunk):
    bsz, nh, length, kd = q.shape
    vd = v.shape[-1]
    n = length // chunk

    def split_chunks(t):
        return jnp.moveaxis(t.astype(F32).reshape(bsz, nh, n, chunk, t.shape[-1]), 2, 0)

    qs, ks, vs, gs = split_chunks(q), split_chunks(k), split_chunks(v), split_chunks(g)
    cum = jnp.cumsum(gs, axis=3)
    lower = jnp.tril(jnp.ones((chunk, chunk), bool))[:, :, None]

    def step(state, inp):
        qc, kc, vc, gc = inp
        rel = jnp.where(lower, gc[:, :, :, None, :] - gc[:, :, None, :, :], -jnp.inf)
        att = jnp.einsum('bhtk,bhsk,bhtsk->bhts', qc, kc, jnp.exp(rel))
        g_end = gc[:, :, -1:, :]
        o = att @ vc + jnp.einsum('bhtk,bhkv->bhtv', qc * jnp.exp(gc), state)
        state = jnp.exp(g_end[:, :, 0, :, None]) * state + jnp.einsum('bhsk,bhsv->bhkv', kc * jnp.exp(g_end - gc), vc)
        return state, o

    state, o = lax.scan(step, s0.astype(F32), (qs, ks, vs, cum))
    return jnp.moveaxis(o, 0, 2).reshape(bsz, nh, length, vd), state


def bidir_gla(lat, ctx, chunk):
    q, kf, kb, v, gf, gb = lat
    qc, kfc, kbc, vc, gfc, gbc = ctx
    s0 = jnp.zeros(q.shape[:2] + (q.shape[-1], v.shape[-1]), F32)
    oc_f, s_f = chunk_gla(qc, kfc, vc, gfc, s0, chunk)
    ox_f, _ = chunk_gla(q, kf, v, gf, s_f, chunk)
    r = lambda t: jnp.flip(t, axis=2)
    oc_b, s_b = chunk_gla(r(qc), r(kbc), r(vc), r(gbc), s0, chunk)
    ox_b, _ = chunk_gla(r(q), r(kb), r(v), r(gb), s_b, chunk)
    return ox_f + r(ox_b), oc_f + r(oc_b)


def gated_head_out(o, og, gain, w_out):
    b, nh, l, vd = o.shape
    o = rmsnorm(o.transpose(0, 2, 1, 3), gain).reshape(b, l, nh * vd)
    return (o * jax.nn.silu(og)) @ w_out


def hgrn_lower_bound(lb_param, layer):
    return jnp.cumsum(jax.nn.softmax(lb_param.astype(F32), axis=0), axis=0)[layer]


def hgrn2_mixer(hx, hc, w_in, lb, onorm_g, w_out):
    def project(h):
        q, ff, fb, inp, og = jnp.split(h @ w_in, 5, axis=-1)
        f_fwd = lb + (1.0 - lb) * jax.nn.sigmoid(ff.astype(F32))
        f_bwd = lb + (1.0 - lb) * jax.nn.sigmoid(fb.astype(F32))
        hd = lambda t: heads(t, HG_HEADS)
        streams = (hd(jax.nn.silu(q) * HG_KEY ** -0.5), hd(1.0 - f_fwd), hd(1.0 - f_bwd), hd(inp),
                   hd(jnp.log(f_fwd)), hd(jnp.log(f_bwd)))
        return streams, og

    lat, ogx = project(hx)
    ctxs, ogc = project(hc)
    ox, oc = bidir_gla(lat, ctxs, HG_CHUNK)
    return gated_head_out(ox, ogx, onorm_g, w_out), gated_head_out(oc, ogc, onorm_g, w_out)


def hyena_pos_features(length):
    t = jnp.linspace(0.0, 1.0, length, dtype=F32)[:, None]
    bands = (HY_EMB - 1) // 2
    f = jnp.linspace(1e-4, bands - 1, bands, dtype=F32)[None, :]
    w = 2.0 * math.pi * jnp.arange(length, dtype=F32)[:, None] / length
    return jnp.concatenate([t, jnp.cos(f * w), -jnp.sin(f * w)], axis=-1), t


def hyena_window(t):
    max_decay = math.log(HY_DECAY_TARGET) / HY_DECAY_HI_PCT
    min_decay = math.log(HY_DECAY_TARGET) / HY_DECAY_LO_PCT
    deltas = jnp.abs(jnp.linspace(min_decay, max_decay, D_MODEL, dtype=F32))
    return jnp.exp(-t * deltas[None, :])


def hyena_filters(length, w1, b1, w2, b2, w3, b3, w4, freq):
    z, t = hyena_pos_features(length)
    a = jnp.sin(freq * (z @ w1 + b1))
    a = jnp.sin(freq * (a @ w2 + b2))
    a = jnp.sin(freq * (a @ w3 + b3))
    h = (a @ w4).astype(F32).reshape(length, HY_ORDER, 2, D_MODEL) * hyena_window(t)[:, None, None, :]
    return h / jnp.sum(jnp.abs(h), axis=(0, 2), keepdims=True)


def twosided_fftconv(u, h_fwd, h_bwd):
    length = u.shape[1]
    k2 = jnp.concatenate([h_fwd, jnp.zeros_like(h_fwd[:1]), h_bwd[:0:-1]], axis=0)
    spec = jnp.fft.rfft(u, n=2 * length, axis=1) * jnp.fft.rfft(k2, axis=0)[None]
    return jnp.fft.irfft(spec, n=2 * length, axis=1)[:, :length]


def hyena_mixer(hx, hc, w_in, b_in, short_w, short_b, f_w1, f_b1, f_w2, f_b2, f_w3, f_b3, f_w4,
                f_freq, f_bias, w_out, b_out):
    def run(h):
        length = h.shape[1]
        u = dwconv_centered(h @ w_in + b_in, short_w, short_b).astype(F32)
        x1, x2, v = jnp.split(u, 3, axis=-1)
        filt = hyena_filters(length, f_w1, f_b1, f_w2, f_b2, f_w3, f_b3, f_w4, f_freq)
        z = v
        for o, gate in enumerate((x1, x2)):
            z = gate * (twosided_fftconv(z, filt[:, o, 0], filt[:, o, 1]) + z * f_bias[o])
        return z @ w_out + b_out

    return run(hx), run(hc)


def chunk_ssd(xs, bm, cm, dt, a, s0, chunk):
    bsz, length = xs.shape[:2]
    n = length // chunk

    def split_chunks(t):
        return jnp.moveaxis(t.astype(F32).reshape((bsz, n, chunk) + t.shape[2:]), 1, 0)

    xcs, bcs, ccs, dts = split_chunks(xs), split_chunks(bm), split_chunks(cm), split_chunks(dt)
    cum = jnp.cumsum(dts * a.astype(F32), axis=2)
    lower = jnp.tril(jnp.ones((chunk, chunk), bool))[:, :, None, None]

    def step(state, inp):
        xc, bc, cc, dtc, lc = inp
        seg = jnp.exp(jnp.where(lower, lc[:, :, None] - lc[:, None, :], -jnp.inf))
        w = jnp.einsum('btgn,bsgn->btsg', cc, bc)[..., None] * seg * dtc[:, None]
        y = jnp.einsum('btsgh,bsghp->btghp', w, xc)
        y = y + jnp.einsum('btgn,bghnp->btghp', cc, state) * jnp.exp(lc)[..., None]
        l_end = lc[:, -1]
        state = jnp.exp(l_end)[..., None, None] * state + jnp.einsum(
            'bsgn,bsgh,bsghp->bghnp', bc, jnp.exp(l_end[:, None] - lc) * dtc, xc)
        return state, y

    state, y = lax.scan(step, s0.astype(F32), (xcs, bcs, ccs, dts, cum))
    return jnp.moveaxis(y, 0, 1).reshape(xs.shape), state


def mamba2_mixer(hx, hc, w_in, conv_w, conv_b, dt_bias, a_log, d_skip, norm_g, w_out):
    a = -jnp.exp(a_log.astype(F32)).reshape(2, MB_GROUPS, MB_HPG)

    def project(h):
        bsz, length, _ = h.shape
        z, xbc, dt_raw = jnp.split(h @ w_in, [MB_INNER, MB_INNER + MB_CONV_DIM], axis=-1)
        xbc = jax.nn.silu(dwconv_centered(xbc, conv_w, conv_b))
        xs, bm, cm = jnp.split(xbc, [MB_INNER, MB_INNER + MB_GROUPS * MB_STATE], axis=-1)
        xs = xs.reshape(bsz, length, MB_GROUPS, MB_HPG, MB_HEADDIM)
        bm = bm.reshape(bsz, length, MB_GROUPS, MB_STATE)
        cm = cm.reshape(bsz, length, MB_GROUPS, MB_STATE)
        dt = jax.nn.softplus(dt_raw.astype(F32).reshape(bsz, length, 2, MB_GROUPS, MB_HPG)
                             + dt_bias.reshape(2, MB_GROUPS, MB_HPG))
        return z, xs, bm, cm, dt

    zx, xx, bx, cx, dtx = project(hx)
    zc, xc, bc, cc, dtc = project(hc)
    s0 = jnp.zeros((hx.shape[0], MB_GROUPS, MB_HPG, MB_STATE, MB_HEADDIM), F32)
    yc_f, s_f = chunk_ssd(xc, bc, cc, dtc[:, :, 0], a[0], s0, MB_CHUNK)
    yx_f, _ = chunk_ssd(xx, bx, cx, dtx[:, :, 0], a[0], s_f, MB_CHUNK)
    r = lambda t: jnp.flip(t, axis=1)
    yc_b, s_b = chunk_ssd(r(xc), r(bc), r(cc), r(dtc[:, :, 1]), a[1], s0, MB_CHUNK)
    yx_b, _ = chunk_ssd(r(xx), r(bx), r(cx), r(dtx[:, :, 1]), a[1], s_b, MB_CHUNK)

    def finish(y, xs, z):
        bsz, length = y.shape[:2]
        y = y + d_skip.reshape(MB_GROUPS, MB_HPG)[..., None] * xs
        y = y.reshape(bsz, length, MB_INNER) * jax.nn.silu(z)
        y = rmsnorm(y.reshape(bsz, length, MB_GROUPS, MB_INNER // MB_GROUPS),
                    norm_g.reshape(MB_GROUPS, MB_INNER // MB_GROUPS))
        return y.reshape(bsz, length, MB_INNER) @ w_out

    return finish(yx_f + r(yx_b), xx, zx), finish(yc_f + r(yc_b), xc, zc)


def gla_mixer(hx, hc, w_in, gk_w2, gk_b, onorm_g, w_out):
    rows = hx.shape[1] // GRID_W
    kd, vd, rk = GLA_KEY_DIM, GLA_VAL_DIM, GLA_GATE_RANK
    cuts = [kd, 2 * kd, 2 * kd + vd, 2 * kd + 2 * vd, 2 * kd + 2 * vd + rk]

    def project(h):
        q, k, v, og, rf, rb = jnp.split(h @ w_in, cuts, axis=-1)
        gf = jax.nn.log_sigmoid((rf @ gk_w2[0] + gk_b[0]).astype(F32)) / GLA_GATE_NORM
        gb = jax.nn.log_sigmoid((rb @ gk_w2[1] + gk_b[1]).astype(F32)) / GLA_GATE_NORM
        hd = lambda t: heads(t, GLA_HEADS)
        kh = hd(k)
        return (hd(q * GLA_HK ** -0.5), kh, kh, hd(v), hd(gf), hd(gb)), og

    lat, ogx = project(raster_to_colmajor(hx, rows))
    ctxs, ogc = project(hc)
    ox, oc = bidir_gla(lat, ctxs, GLA_CHUNK)
    yx = colmajor_to_raster(gated_head_out(ox, ogx, onorm_g, w_out), rows)
    return yx, gated_head_out(oc, ogc, onorm_g, w_out)


def expert_choice_ffn(h, router_w, w_gate, w_up, w_down):
    bsz, t_len, d = h.shape
    cap = max(1, EC_CAPACITY_FACTOR * t_len // N_EXPERTS)
    aff = jax.nn.softmax(h.astype(F32) @ router_w.astype(F32), axis=-1)
    gate, idx = lax.top_k(jnp.swapaxes(aff, 1, 2), cap)
    xg = jax.vmap(lambda hb, ib: hb[ib])(h, idx)
    hid = jax.nn.silu(jnp.einsum('becd,edf->becf', xg, w_gate)) * jnp.einsum('becd,edf->becf', xg, w_up)
    ye = jnp.einsum('becf,efd->becd', hid, w_down) * gate[..., None]
    return jax.vmap(lambda ib, yb: jnp.zeros((t_len, d), yb.dtype).at[ib.reshape(-1)].add(yb.reshape(-1, d)))(idx, ye)


BF16 = jnp.bfloat16
LANES = 128
SUBLANES = 8
VMEM_LIMIT_BYTES = 56 * 2 ** 20
SCAN_CHUNK = 128
SCAN_FAST_RANGE = 80.0
SCAN_UNROLL = 2


def _cumsum_rows(tri_bf16, g):
    hi = g.astype(BF16)
    r1 = g - hi.astype(F32)
    mid = r1.astype(BF16)
    lo = (r1 - mid.astype(F32)).astype(BF16)
    d = lambda a: jnp.dot(tri_bf16, a, preferred_element_type=F32)
    return d(hi) + d(mid) + d(lo)


def _log_sigmoid(z):
    return jnp.minimum(z, 0.0) - jnp.log1p(jnp.exp(-jnp.abs(z)))


def _dot_nt(a, b):
    return lax.dot_general(a, b, (((1,), (1,)), ((), ())), preferred_element_type=F32)


def _dot_tn(a, b):
    return lax.dot_general(a, b, (((0,), (0,)), ((), ())), preferred_element_type=F32)


def _gla_scan_body(mode, seq, ctx_len, kdim, vdim, *refs):
    chunk = SCAN_CHUNK
    if mode == 'hgrn2':
        (q_x, ff_x, fb_x, v_x, og_x, q_c, ff_c, fb_c, v_c, og_c, lb_ref, gain_ref,
         o_x, o_c, cf_s, cb_s, sf_s, sb_s, kt_s, gt_s, qf_s, qb_s, of_s, ob_s, kf_s, kb_s) = refs
    else:
        (q_x, k_x, v_x, og_x, r_x, q_c, k_c, v_c, og_c, r_c, w2f_ref, w2b_ref, bf_ref, bb_ref, gain_ref,
         o_x, o_c, cf_s, cb_s, sf_s, sb_s, kt_s, gt_s, qf_s, qb_s, of_s, ob_s) = refs
    if mode == 'hgrn2':
        segs = (((ff_c, fb_c), q_c, None, v_c, og_c, o_c, ctx_len, 0),
                ((ff_x, fb_x), q_x, None, v_x, og_x, o_x, seq, ctx_len))
    else:
        segs = (((r_c,), q_c, k_c, v_c, og_c, o_c, ctx_len, 0),
                ((r_x,), q_x, k_x, v_x, og_x, o_x, seq, ctx_len))

    row = lax.broadcasted_iota(jnp.int32, (chunk, chunk), 0)
    col = lax.broadcasted_iota(jnp.int32, (chunk, chunk), 1)
    masks = (row >= col, row <= col)
    tris = tuple(m.astype(BF16) for m in masks)
    cum_s, state_s, qin_s, kout_s = (cf_s, cb_s), (sf_s, sb_s), (qf_s, qb_s), (of_s, ob_s)
    end_row = (chunk - 1, 0)
    mid_row = chunk // 2

    def rows_of(i, off=0):
        return pl.ds(pl.multiple_of(off + i * chunk, chunk), chunk)

    def gate_pass(seg, bound):
        gsrc, _, _, _, _, _, length, off = seg

        def body(i, bound):
            rows, srows = rows_of(i), rows_of(i, off)
            if mode == 'hgrn2':
                lb = lb_ref[...]
                f_f = lb + (1.0 - lb) * jax.nn.sigmoid(gsrc[0][rows, :])
                f_b = lb + (1.0 - lb) * jax.nn.sigmoid(gsrc[1][rows, :])
                g_f, g_b = jnp.log(f_f), jnp.log(f_b)
                kf_s[srows, :] = 1.0 - f_f
                kb_s[srows, :] = 1.0 - f_b
            else:
                r = gsrc[0][rows, :].astype(BF16)
                z_f = jnp.dot(r, w2f_ref[...].astype(BF16), preferred_element_type=F32) + bf_ref[...]
                z_b = jnp.dot(r, w2b_ref[...].astype(BF16), preferred_element_type=F32) + bb_ref[...]
                g_f = _log_sigmoid(z_f) / GLA_GATE_NORM
                g_b = _log_sigmoid(z_b) / GLA_GATE_NORM
            c_f = _cumsum_rows(tris[0], g_f)
            c_b = _cumsum_rows(tris[1], g_b)
            cf_s[srows, :] = c_f
            cb_s[srows, :] = c_b
            for c in (c_f, c_b):
                mid = c[mid_row:mid_row + 1, :]
                spread = jnp.maximum(jnp.abs(c[0:1, :] - mid), jnp.abs(c[chunk - 1:chunk, :] - mid))
                bound = jnp.maximum(bound, spread)
            return bound

        return lax.fori_loop(0, length // chunk, body, bound, unroll=SCAN_UNROLL)

    def intra(seg, direction, i, exact):
        _, q_r, k_r, v_r, _, _, _, off = seg
        rows, srows = rows_of(i), rows_of(i, off)
        if mode == 'hgrn2':
            qq = jax.nn.silu(q_r[rows, :]) * HG_KEY ** -0.5
            k = (kf_s if direction == 0 else kb_s)[srows, :]
        else:
            qq = q_r[rows, :] * GLA_HK ** -0.5
            k = k_r[rows, :]
        cum = cum_s[direction][srows, :]
        e = end_row[direction]
        cum_end = cum[e:e + 1, :]
        q_in = (qq * jnp.exp(cum)).astype(BF16)
        qin_s[direction][srows, :] = q_in
        kout_s[direction][srows, :] = (k * jnp.exp(cum_end - cum)).astype(BF16)
        if exact:
            kt_s[...] = k
            gt_s[...] = cum

            def col_body(s, att):
                ks = kt_s[pl.ds(s, 1), :]
                gs = gt_s[pl.ds(s, 1), :]
                p = qq * ks * jnp.exp(jnp.minimum(cum - gs, 0.0))
                return att + jnp.where(col == s, jnp.sum(p, axis=-1, keepdims=True), 0.0)

            att = lax.fori_loop(0, chunk, col_body, jnp.zeros((chunk, chunk), F32))
        else:
            rel = cum - cum[mid_row:mid_row + 1, :]
            att = _dot_nt((qq * jnp.exp(rel)).astype(BF16), (k * jnp.exp(-rel)).astype(BF16))
        att = jnp.where(masks[direction], att, 0.0).astype(BF16)
        return jnp.dot(att, v_r[rows, :].astype(BF16), preferred_element_type=F32)

    def intra_pass(exact):
        for seg in segs:
            o_r, length = seg[5], seg[6]

            def body(i, carry):
                o_r[rows_of(i), :] = intra(seg, 0, i, exact) + intra(seg, 1, i, exact)
                return carry

            lax.fori_loop(0, length // chunk, body, 0, unroll=1 if exact else SCAN_UNROLL)

    def inter(seg, direction, i):
        v_r, off = seg[3], seg[7]
        rows, srows = rows_of(i), rows_of(i, off)
        s_ref = state_s[direction]
        state = s_ref[...]
        o = _dot_nt(qin_s[direction][srows, :], state.astype(BF16))
        e = pl.ds(pl.multiple_of(off + i * chunk, chunk) + end_row[direction], 1)
        decay = jnp.exp(cum_s[direction][e, :])
        s_ref[...] = state * decay + _dot_tn(v_r[rows, :].astype(BF16), kout_s[direction][srows, :])
        return o

    def finish(seg, i, o):
        o = o * lax.rsqrt(jnp.mean(o * o, axis=-1, keepdims=True) + NORM_EPS) * gain_ref[...]
        return o * jax.nn.silu(seg[4][rows_of(i), :])

    def inter_pass():
        sf_s[...] = jnp.zeros_like(sf_s)
        sb_s[...] = jnp.zeros_like(sb_s)
        for seg in segs:
            o_r, length = seg[5], seg[6]
            n = length // chunk

            def first_half(i, carry):
                j = n - 1 - i
                o_r[rows_of(i), :] += inter(seg, 0, i)
                o_r[rows_of(j), :] += inter(seg, 1, j)
                return carry

            def second_half(i, carry):
                j = n - 1 - i
                o_r[rows_of(i), :] = finish(seg, i, o_r[rows_of(i), :] + inter(seg, 0, i))
                o_r[rows_of(j), :] = finish(seg, j, o_r[rows_of(j), :] + inter(seg, 1, j))
                return carry

            lax.fori_loop(0, n // 2, first_half, 0, unroll=min(SCAN_UNROLL, n // 2))
            lax.fori_loop(n // 2, n, second_half, 0, unroll=min(SCAN_UNROLL, n // 2))

    bound = jnp.zeros((1, kdim), F32)
    for seg in segs:
        bound = gate_pass(seg, bound)
    fast = jnp.max(bound) <= SCAN_FAST_RANGE

    @pl.when(fast)
    def _():
        intra_pass(False)

    @pl.when(jnp.logical_not(fast))
    def _():
        intra_pass(True)

    inter_pass()


def _gla_scan_call(mode, nheads, kdim, vdim, lat_in, ctx_in, small_in, bsz, seq, ctx_len):
    def stream_spec(length, width, first, stride):
        return pl.BlockSpec((None, length, width), lambda b, h: (b, 0, first + stride * h))

    assert seq % (2 * SCAN_CHUNK) == 0 and ctx_len % (2 * SCAN_CHUNK) == 0
    in_specs = [stream_spec(seq, w, f, s) for _, w, f, s in lat_in]
    in_specs += [stream_spec(ctx_len, w, f, s) for _, w, f, s in ctx_in]
    in_specs += [pl.BlockSpec(blk, imap) for _, blk, imap in small_in]
    args = [a for a, _, _, _ in lat_in] + [a for a, _, _, _ in ctx_in] + [a for a, _, _ in small_in]
    tot = seq + ctx_len
    scratch = [pltpu.VMEM((tot, kdim), F32), pltpu.VMEM((tot, kdim), F32),
               pltpu.VMEM((vdim, kdim), F32), pltpu.VMEM((vdim, kdim), F32),
               pltpu.VMEM((SCAN_CHUNK, kdim), F32), pltpu.VMEM((SCAN_CHUNK, kdim), F32),
               pltpu.VMEM((tot, kdim), BF16), pltpu.VMEM((tot, kdim), BF16),
               pltpu.VMEM((tot, kdim), BF16), pltpu.VMEM((tot, kdim), BF16)]
    if mode == 'hgrn2':
        scratch += [pltpu.VMEM((tot, kdim), F32), pltpu.VMEM((tot, kdim), F32)]
    return pl.pallas_call(
        lambda *refs: _gla_scan_body(mode, seq, ctx_len, kdim, vdim, *refs),
        grid=(bsz, nheads),
        in_specs=in_specs,
        out_specs=[pl.BlockSpec((None, seq, vdim), lambda b, h: (b, 0, h)),
                   pl.BlockSpec((None, ctx_len, vdim), lambda b, h: (b, 0, h))],
        out_shape=[jax.ShapeDtypeStruct((bsz, seq, nheads * vdim), F32),
                   jax.ShapeDtypeStruct((bsz, ctx_len, nheads * vdim), F32)],
        scratch_shapes=scratch,
        compiler_params=pltpu.CompilerParams(dimension_semantics=("arbitrary", "arbitrary"),
                                             vmem_limit_bytes=VMEM_LIMIT_BYTES),
        name=f"{mode}_scan",
    )(*args)


def hgrn2_mixer_pallas(hx, hc, w_in, lb, onorm_g, w_out):
    bsz, seq, _ = hx.shape
    ctx_len = hc.shape[1]
    yx, yc = hx @ w_in, hc @ w_in
    nh = HG_HEADS
    streams = lambda y: [(y, HG_KEY, j * nh, 1) for j in range(5)]
    small = [(lb.reshape(1, D_MODEL), (1, HG_KEY), lambda b, h: (0, h)),
             (onorm_g.reshape(1, HG_VAL), (1, HG_VAL), lambda b, h: (0, 0))]
    ox, oc = _gla_scan_call('hgrn2', nh, HG_KEY, HG_VAL, streams(yx), streams(yc), small, bsz, seq, ctx_len)
    return ox @ w_out, oc @ w_out


def gla_mixer_pallas(hx, hc, w_in, gk_w2, gk_b, onorm_g, w_out):
    bsz, seq, _ = hx.shape
    ctx_len = hc.shape[1]
    rows = seq // GRID_W
    kd, vd, rk = GLA_KEY_DIM, GLA_VAL_DIM, GLA_GATE_RANK
    nh = GLA_HEADS
    w_in_p = jnp.pad(w_in, ((0, 0), (0, LANES - 2 * rk)))
    yx = raster_to_colmajor(hx, rows) @ w_in_p
    yc = hc @ w_in_p
    w2f = jnp.pad(gk_w2[0], ((0, LANES - rk), (0, 0)))
    w2b = jnp.pad(gk_w2[1], ((rk, LANES - 2 * rk), (0, 0)))
    streams = lambda y: [(y, GLA_HK, 0, 1), (y, GLA_HK, kd // GLA_HK, 1), (y, GLA_HV, 2 * kd // GLA_HV, 1),
                         (y, GLA_HV, (2 * kd + vd) // GLA_HV, 1), (y, LANES, (2 * kd + 2 * vd) // LANES, 0)]
    small = [(w2f, (LANES, GLA_HK), lambda b, h: (0, h)), (w2b, (LANES, GLA_HK), lambda b, h: (0, h)),
             (gk_b[0].reshape(1, kd), (1, GLA_HK), lambda b, h: (0, h)),
             (gk_b[1].reshape(1, kd), (1, GLA_HK), lambda b, h: (0, h)),
             (onorm_g.reshape(1, GLA_HV), (1, GLA_HV), lambda b, h: (0, 0))]
    ox, oc = _gla_scan_call('gla', nh, GLA_HK, GLA_HV, streams(yx), streams(yc), small, bsz, seq, ctx_len)
    return colmajor_to_raster(ox @ w_out, rows), oc @ w_out


MB_GROUP_W = MB_INNER // MB_GROUPS


def _ssd_scan_body(seq, ctx_len, *refs):
    chunk = MB_CHUNK
    gw, hpg = MB_GROUP_W, MB_HPG
    (x_x, b_x, c_x, z_x, cl_x, rt_x, x_c, b_c, c_c, z_c, cl_c, rt_c,
     wx_ref, wb_ref, wc_ref, bx_ref, bb_ref, bc_ref, dskip_ref, gain_ref,
     o_x, o_c, xf_s, xb_s, sf_s, sb_s, pad_s, xc_s, bc_s, cc_s) = refs
    segs = ((x_c, b_c, c_c, z_c, cl_c, rt_c, o_c, ctx_len, 0),
            (x_x, b_x, c_x, z_x, cl_x, rt_x, o_x, seq, ctx_len))
    xs_s, state_s = (xf_s, xb_s), (sf_s, sb_s)
    end_row = (chunk - 1, 0)

    def conv_silu(raw_ref, w_ref, bias_ref, dst_ref, length, off):
        width = raw_ref.shape[-1]
        halo = SUBLANES
        pad = MB_CONV // 2
        pad_s[0:halo, 0:width] = jnp.zeros((halo, width), F32)
        pad_s[pl.ds(halo + length, halo), 0:width] = jnp.zeros((halo, width), F32)

        def stage(i, carry):
            pad_s[pl.ds(pl.multiple_of(halo + i * chunk, SUBLANES), chunk), 0:width] = raw_ref[rows_of(i), :]
            return carry

        lax.fori_loop(0, length // chunk, stage, 0)

        def body(i, carry):
            rows_in = chunk + 2 * halo
            win = pad_s[pl.ds(pl.multiple_of(i * chunk, chunk), rows_in), 0:width]
            acc = jnp.broadcast_to(bias_ref[...], (chunk, width))
            for j in range(MB_CONV):
                lo = halo - pad + j
                acc = acc + pltpu.roll(win, rows_in - lo, axis=0)[0:chunk, :] * w_ref[j:j + 1, :]
            dst_ref[rows_of(i, off), :] = jax.nn.silu(acc).astype(dst_ref.dtype)
            return carry

        lax.fori_loop(0, length // chunk, body, 0)

    row = lax.broadcasted_iota(jnp.int32, (chunk, chunk), 0)
    col = lax.broadcasted_iota(jnp.int32, (chunk, chunk), 1)
    masks = (row >= col, row <= col)
    head_of_lane = lax.broadcasted_iota(jnp.int32, (1, gw), 1) // MB_HEADDIM

    def rows_of(i, off=0):
        return pl.ds(pl.multiple_of(off + i * chunk, chunk), chunk)

    def bcast_heads(c4):
        out = c4[:, hpg - 1:hpg]
        for hh in range(hpg - 2, -1, -1):
            out = jnp.where(head_of_lane == hh, c4[:, hh:hh + 1], out)
        return out

    def intra_pass():
        for seg in segs:
            x_r, b_r, c_r, _, cl_r, rt_r, o_r, length, off = seg

            def body(i, carry):
                rows, srows = rows_of(i), rows_of(i, off)
                cols, rt = cl_r[i], rt_r[i]
                x = xc_s[srows, :]
                cb = _dot_nt(cc_s[srows, :], bc_s[srows, :])
                xh = [jnp.where(head_of_lane == hh, x, 0.0).astype(BF16) for hh in range(hpg)]
                y = jnp.zeros((chunk, gw), F32)
                for d in range(2):
                    cum = cols[:, d * hpg:(d + 1) * hpg]
                    dt = cols[:, (2 + d) * hpg:(3 + d) * hpg]
                    e = end_row[d]
                    scale = jnp.exp(cum[e:e + 1, :] - cum) * dt
                    xs_s[d][srows, :] = (x * bcast_heads(scale)).astype(BF16)
                    for hh in range(hpg):
                        j = d * hpg + hh
                        seg_decay = jnp.exp(jnp.minimum(cols[:, j:j + 1] - rt[j:j + 1, :], 0.0))
                        w = jnp.where(masks[d], cb * seg_decay * rt[2 * hpg + j:2 * hpg + j + 1, :], 0.0)
                        y = y + jnp.dot(w.astype(BF16), xh[hh], preferred_element_type=F32)
                o_r[rows, :] = y
                return carry

            lax.fori_loop(0, length // chunk, body, 0, unroll=2)

    def inter(seg, d, i):
        cl_r, off = seg[4], seg[8]
        srows = rows_of(i, off)
        cum = cl_r[i][:, d * hpg:(d + 1) * hpg]
        state = state_s[d][...]
        y = jnp.dot(cc_s[srows, :], state.astype(BF16), preferred_element_type=F32)
        e = end_row[d]
        decay = bcast_heads(jnp.exp(cum[e:e + 1, :]))
        state_s[d][...] = state * decay + _dot_tn(bc_s[srows, :], xs_s[d][srows, :])
        return y * bcast_heads(jnp.exp(cum))

    def finish(seg, i, y):
        z_r, off = seg[3], seg[8]
        y = (y + dskip_ref[...] * xc_s[rows_of(i, off), :]) * jax.nn.silu(z_r[rows_of(i), :])
        return y * lax.rsqrt(jnp.mean(y * y, axis=-1, keepdims=True) + NORM_EPS) * gain_ref[...]

    def inter_pass():
        sf_s[...] = jnp.zeros_like(sf_s)
        sb_s[...] = jnp.zeros_like(sb_s)
        for seg in segs:
            o_r, length = seg[6], seg[7]
            n = length // chunk

            def first_half(i, carry):
                j = n - 1 - i
                o_r[rows_of(i), :] += inter(seg, 0, i)
                o_r[rows_of(j), :] += inter(seg, 1, j)
                return carry

            def second_half(i, carry):
                j = n - 1 - i
                o_r[rows_of(i), :] = finish(seg, i, o_r[rows_of(i), :] + inter(seg, 0, i))
                o_r[rows_of(j), :] = finish(seg, j, o_r[rows_of(j), :] + inter(seg, 1, j))
                return carry

            lax.fori_loop(0, n // 2, first_half, 0, unroll=2)
            lax.fori_loop(n // 2, n, second_half, 0, unroll=2)

    for seg in segs:
        conv_silu(seg[0], wx_ref, bx_ref, xc_s, seg[7], seg[8])
        conv_silu(seg[1], wb_ref, bb_ref, bc_s, seg[7], seg[8])
        conv_silu(seg[2], wc_ref, bc_ref, cc_s, seg[7], seg[8])
    intra_pass()
    inter_pass()


def _ssd_head_tables(dt_raw, dt_bias, a_log):
    bsz, length, _ = dt_raw.shape
    n = length // MB_CHUNK
    a = -jnp.exp(a_log.astype(F32)).reshape(2, MB_GROUPS, MB_HPG)
    dt = jax.nn.softplus(dt_raw.astype(F32).reshape(bsz, length, 2, MB_GROUPS, MB_HPG)
                         + dt_bias.reshape(2, MB_GROUPS, MB_HPG))
    dta = (dt * a).reshape(bsz, n, MB_CHUNK, 2, MB_GROUPS, MB_HPG)
    dtc = dt.reshape(bsz, n, MB_CHUNK, 2, MB_GROUPS, MB_HPG)
    cum_f = jnp.cumsum(dta[:, :, :, 0], axis=2)
    cum_b = jnp.flip(jnp.cumsum(jnp.flip(dta[:, :, :, 1], axis=2), axis=2), axis=2)
    cols = jnp.concatenate([cum_f, cum_b, dtc[:, :, :, 0], dtc[:, :, :, 1]], axis=-1)
    cols = cols.transpose(0, 3, 1, 2, 4)
    return cols, jnp.swapaxes(cols, -1, -2)


def mamba2_mixer_pallas(hx, hc, w_in, conv_w, conv_b, dt_bias, a_log, d_skip, norm_g, w_out):
    bsz, seq, _ = hx.shape
    ctx_len = hc.shape[1]
    chunk, gw = MB_CHUNK, MB_GROUP_W

    def project(h):
        z = h @ w_in[:, :MB_INNER]
        xbc = h @ w_in[:, MB_INNER:MB_INNER + MB_CONV_DIM]
        cols, rows_t = _ssd_head_tables(h @ w_in[:, MB_INNER + MB_CONV_DIM:], dt_bias, a_log)
        return z, xbc, cols, rows_t

    yx, xbcx, clx, rtx = project(hx)
    yc, xbcc, clc, rtc = project(hc)
    conv_b2 = conv_b.reshape(1, MB_CONV_DIM)

    def conv_specs(rows):
        bw = MB_INNER // MB_STATE
        return [pl.BlockSpec((rows, gw), lambda b, g: (0, g)),
                pl.BlockSpec((rows, MB_STATE), lambda b, g: (0, bw + g)),
                pl.BlockSpec((rows, MB_STATE), lambda b, g: (0, bw + MB_GROUPS + g))]

    def stream_specs(length):
        n = length // chunk
        bw = MB_INNER // MB_STATE
        return [pl.BlockSpec((None, length, gw), lambda b, g: (b, 0, g)),
                pl.BlockSpec((None, length, MB_STATE), lambda b, g: (b, 0, bw + g)),
                pl.BlockSpec((None, length, MB_STATE), lambda b, g: (b, 0, bw + MB_GROUPS + g)),
                pl.BlockSpec((None, length, gw), lambda b, g: (b, 0, g)),
                pl.BlockSpec((None, None, n, chunk, 4 * MB_HPG), lambda b, g: (b, g, 0, 0, 0)),
                pl.BlockSpec((None, None, n, 4 * MB_HPG, chunk), lambda b, g: (b, g, 0, 0, 0))]

    tot = seq + ctx_len
    ox, oc = pl.pallas_call(
        lambda *refs: _ssd_scan_body(seq, ctx_len, *refs),
        grid=(bsz, MB_GROUPS),
        in_specs=stream_specs(seq) + stream_specs(ctx_len) + conv_specs(MB_CONV) + conv_specs(1) + [
            pl.BlockSpec((1, gw), lambda b, g: (0, g)), pl.BlockSpec((1, gw), lambda b, g: (0, g))],
        out_specs=[pl.BlockSpec((None, seq, gw), lambda b, g: (b, 0, g)),
                   pl.BlockSpec((None, ctx_len, gw), lambda b, g: (b, 0, g))],
        out_shape=[jax.ShapeDtypeStruct((bsz, seq, MB_INNER), F32),
                   jax.ShapeDtypeStruct((bsz, ctx_len, MB_INNER), F32)],
        scratch_shapes=[pltpu.VMEM((tot, gw), BF16), pltpu.VMEM((tot, gw), BF16),
                        pltpu.VMEM((MB_STATE, gw), F32), pltpu.VMEM((MB_STATE, gw), F32),
                        pltpu.VMEM((seq + 2 * SUBLANES, gw), F32), pltpu.VMEM((tot, gw), F32),
                        pltpu.VMEM((tot, MB_STATE), BF16), pltpu.VMEM((tot, MB_STATE), BF16)],
        compiler_params=pltpu.CompilerParams(dimension_semantics=("arbitrary", "arbitrary"),
                                             vmem_limit_bytes=VMEM_LIMIT_BYTES),
        name="ssd_scan",
    )(xbcx, xbcx, xbcx, yx, clx, rtx, xbcc, xbcc, xbcc, yc, clc, rtc,
      conv_w, conv_w, conv_w, conv_b2, conv_b2, conv_b2,
      jnp.repeat(d_skip, MB_HEADDIM).reshape(1, MB_INNER), norm_g.reshape(1, MB_INNER))
    return ox @ w_out, oc @ w_out


HY_BLOCK = 256
HY_FREQ_ROWS = 64


def _split_bf16(a):
    hi = a.astype(BF16)
    return hi, (a - hi.astype(F32)).astype(BF16)


def _dot_split(w_hi, w_lo, x):
    x_hi, x_lo = _split_bf16(x)
    d = lambda a, b: jnp.dot(a, b, preferred_element_type=F32)
    return d(w_hi, x_hi) + d(w_hi, x_lo) + d(w_lo, x_hi)


def _hyena_dft_matrices():
    p = HY_BLOCK
    f = jnp.arange(p, dtype=F32)[:, None] + 0.5
    b = jnp.arange(p, dtype=F32)[None, :]
    k = jnp.round(f * b * 2.0).astype(jnp.int32) % (4 * p)
    ang = k.astype(F32) * (2.0 * math.pi / (4 * p))
    fwd = jnp.concatenate([jnp.cos(ang), -jnp.sin(ang)], axis=0)
    inv = jnp.concatenate([jnp.cos(ang).T, -jnp.sin(ang).T], axis=1) / p
    return fwd, inv


def _hyena_filter_spectra(hf, hb, fwd):
    length, d = hf.shape
    p = HY_BLOCK
    nb = length // p
    h2 = jnp.concatenate([jnp.zeros((1, d), F32), jnp.flip(hb[1:], axis=0), hf], axis=0)
    blocks = h2.reshape(2 * nb, p, d)
    spec = jnp.einsum('fj,kjd->kfd', fwd, blocks, precision=lax.Precision.HIGHEST)
    s_re, s_im = spec[:, :p], spec[:, p:]
    sign = jnp.where(jnp.arange(p) % 2 == 0, 1.0, -1.0).astype(F32)[None, :, None]
    return s_re[1:] - sign * s_im[:-1], s_im[1:] + sign * s_re[:-1]


def _hyena_conv_body(nb, u_ref, x_ref, bias_ref, gre_ref, gim_ref, wfh_ref, wfl_ref, wih_ref, wil_ref,
                     o_ref, ure_s, uim_s, y_s):
    p, fr = HY_BLOCK, HY_FREQ_ROWS

    def rows_of(i):
        return pl.ds(pl.multiple_of(i * p, p), p)

    def forward(j, carry):
        spec = _dot_split(wfh_ref[...], wfl_ref[...], u_ref[rows_of(j), :])
        ure_s[j] = spec[:p]
        uim_s[j] = spec[p:]
        return carry

    lax.fori_loop(0, nb, forward, 0)

    def out_block(i, carry):
        for t in range(p // fr):
            r = pl.ds(t * fr, fr)

            def acc_body(j, acc):
                a_re, a_im = acc
                k = i - j + nb - 1
                g_re, g_im = gre_ref[k, r, :], gim_ref[k, r, :]
                u_re, u_im = ure_s[j, r, :], uim_s[j, r, :]
                return a_re + g_re * u_re - g_im * u_im, a_im + g_re * u_im + g_im * u_re

            zero = jnp.zeros((fr, u_ref.shape[-1]), F32)
            a_re, a_im = lax.fori_loop(0, nb, acc_body, (zero, zero), unroll=min(nb, 4))
            y_s[pl.ds(t * fr, fr), :] = a_re
            y_s[pl.ds(p + t * fr, fr), :] = a_im
        y = _dot_split(wih_ref[...], wil_ref[...], y_s[...])
        u = u_ref[rows_of(i), :]
        o_ref[rows_of(i), :] = x_ref[rows_of(i), :] * (y + u * bias_ref[...])
        return carry

    lax.fori_loop(0, nb, out_block, 0)


def _hyena_conv(u_arr, u_blk, x_arr, x_blk, bias, g_re, g_im, mats):
    bsz, length, _ = u_arr.shape
    d = bias.shape[-1]
    p = HY_BLOCK
    nb = length // p
    nseg = 2 * nb - 1
    const = lambda shape: pl.BlockSpec(shape, lambda c, b: (0, 0))
    return pl.pallas_call(
        lambda *refs: _hyena_conv_body(nb, *refs),
        grid=(d // LANES, bsz),
        in_specs=[pl.BlockSpec((None, length, LANES), lambda c, b: (b, 0, u_blk + c)),
                  pl.BlockSpec((None, length, LANES), lambda c, b: (b, 0, x_blk + c)),
                  pl.BlockSpec((1, LANES), lambda c, b: (0, c)),
                  pl.BlockSpec((nseg, p, LANES), lambda c, b: (0, 0, c)),
                  pl.BlockSpec((nseg, p, LANES), lambda c, b: (0, 0, c)),
                  const((2 * p, p)), const((2 * p, p)), const((p, 2 * p)), const((p, 2 * p))],
        out_specs=pl.BlockSpec((None, length, LANES), lambda c, b: (b, 0, c)),
        out_shape=jax.ShapeDtypeStruct((bsz, length, d), F32),
        scratch_shapes=[pltpu.VMEM((nb, p, LANES), F32), pltpu.VMEM((nb, p, LANES), F32),
                        pltpu.VMEM((2 * p, LANES), F32)],
        compiler_params=pltpu.CompilerParams(dimension_semantics=("arbitrary", "arbitrary"),
                                             vmem_limit_bytes=VMEM_LIMIT_BYTES),
        name="hyena_conv",
    )(u_arr, x_arr, bias.reshape(1, d), g_re, g_im, *mats)


def hyena_mixer_pallas(hx, hc, w_in, b_in, short_w, short_b, f_w1, f_b1, f_w2, f_b2, f_w3, f_b3, f_w4,
                       f_freq, f_bias, w_out, b_out):
    fwd, inv = _hyena_dft_matrices()
    mats = _split_bf16(fwd) + _split_bf16(inv)
    nblk = D_MODEL // LANES

    def taps(length):
        z, t = hyena_pos_features(length)
        a = jnp.sin(f_freq * (z @ f_w1 + f_b1))
        a = jnp.sin(f_freq * (a @ f_w2 + f_b2))
        a = jnp.sin(f_freq * (a @ f_w3 + f_b3))
        mw = (a @ f_w4).astype(F32) * jnp.tile(hyena_window(t), (1, 2 * HY_ORDER))
        norm = jnp.sum(jnp.abs(mw), axis=0).reshape(HY_ORDER, 2, D_MODEL).sum(axis=1)
        return mw, 1.0 / norm

    def spectra(mw, inv_norm, o):
        hf = mw[:, (2 * o) * D_MODEL:(2 * o + 1) * D_MODEL]
        hb = mw[:, (2 * o + 1) * D_MODEL:(2 * o + 2) * D_MODEL]
        g_re, g_im = _hyena_filter_spectra(hf, hb, fwd)
        return g_re * inv_norm[o], g_im * inv_norm[o]

    def run(h):
        length = h.shape[1]
        u = dwconv_centered(h @ w_in + b_in, short_w, short_b).astype(F32)
        mw, inv_norm = taps(length)
        g0, g1 = spectra(mw, inv_norm, 0), spectra(mw, inv_norm, 1)
        z = _hyena_conv(u, 2 * nblk, u, 0, f_bias[0], g0[0], g0[1], mats)
        z = _hyena_conv(z, 0, u, nblk, f_bias[1], g1[0], g1[1], mats)
        return z @ w_out + b_out

    return run(hx), run(hc)


MOE_ROW_TILE = 1024
MOE_COL_TILE = 512


def _moe_combine_body(t_len, idx_ref, gate_ref, ye_ref, x_ref, g2_ref, o_ref):
    e = pl.program_id(2)
    cap = idx_ref.shape[-1]
    rt = min(MOE_ROW_TILE, t_len)

    @pl.when(e == 0)
    def _():
        o_ref[...] = jnp.zeros_like(o_ref)

    yg = (ye_ref[...] * gate_ref[...]).astype(BF16)
    idx = idx_ref[...]

    def body(r, carry):
        r0 = pl.multiple_of(r * rt, rt)
        tok = lax.broadcasted_iota(jnp.int32, (rt, cap), 0) + r0
        onehot = jnp.where(tok == idx, 1.0, 0.0).astype(BF16)
        o_ref[pl.ds(r0, rt), :] += jnp.dot(onehot, yg, preferred_element_type=F32)
        return carry

    lax.fori_loop(0, t_len // rt, body, 0)

    @pl.when(e == pl.num_programs(2) - 1)
    def _():
        o_ref[...] = x_ref[...] + g2_ref[...] * o_ref[...]


def expert_choice_ffn_residual(x, g2, h, router_w, w_gate, w_up, w_down):
    bsz, t_len, d = h.shape
    cap = max(1, EC_CAPACITY_FACTOR * t_len // N_EXPERTS)
    aff = jax.nn.softmax(h.astype(F32) @ router_w.astype(F32), axis=-1)
    gate, idx = lax.top_k(jnp.swapaxes(aff, 1, 2), cap)
    xg = jax.vmap(lambda hb, ib: hb[ib])(h.astype(BF16), idx)
    pre_g = jnp.einsum('becd,edf->becf', xg, w_gate, preferred_element_type=BF16)
    pre_u = jnp.einsum('becd,edf->becf', xg, w_up, preferred_element_type=BF16)
    hid = (jax.nn.silu(pre_g.astype(F32)) * pre_u.astype(F32)).astype(BF16)
    ye = jnp.einsum('becf,efd->becd', hid, w_down, preferred_element_type=F32)
    ct = MOE_COL_TILE if t_len > MOE_ROW_TILE else d
    g2 = jnp.broadcast_to(g2.reshape(-1, 1, d), (bsz, 1, d))
    return pl.pallas_call(
        lambda *refs: _moe_combine_body(t_len, *refs),
        grid=(bsz, d // ct, N_EXPERTS),
        in_specs=[pl.BlockSpec((None, None, 1, cap), lambda b, c, e: (b, e, 0, 0)),
                  pl.BlockSpec((None, None, cap, 1), lambda b, c, e: (b, e, 0, 0)),
                  pl.BlockSpec((None, None, cap, ct), lambda b, c, e: (b, e, 0, c)),
                  pl.BlockSpec((None, t_len, ct), lambda b, c, e: (b, 0, c)),
                  pl.BlockSpec((None, 1, ct), lambda b, c, e: (b, 0, c))],
        out_specs=pl.BlockSpec((None, t_len, ct), lambda b, c, e: (b, 0, c)),
        out_shape=jax.ShapeDtypeStruct((bsz, t_len, d), F32),
        compiler_params=pltpu.CompilerParams(dimension_semantics=("arbitrary", "arbitrary", "arbitrary"),
                                             vmem_limit_bytes=VMEM_LIMIT_BYTES),
        name="moe_combine",
    )(idx.reshape(bsz, N_EXPERTS, 1, cap), gate.reshape(bsz, N_EXPERTS, cap, 1), ye, x, g2)


def _final_norm_body(x_ref, g_ref, o_ref):
    xf = x_ref[...]
    o_ref[...] = xf * lax.rsqrt(jnp.mean(xf * xf, axis=-1, keepdims=True) + NORM_EPS) * g_ref[...]


def final_rmsnorm(x, g):
    b, l, d = x.shape
    rows = b * l
    tm = 512
    out = pl.pallas_call(
        _final_norm_body,
        grid=(rows // tm,),
        in_specs=[pl.BlockSpec((tm, d), lambda i: (i, 0)), pl.BlockSpec((1, d), lambda i: (0, 0))],
        out_specs=pl.BlockSpec((tm, d), lambda i: (i, 0)),
        out_shape=jax.ShapeDtypeStruct((rows, d), F32),
    )(x.reshape(rows, d), g.reshape(1, d))
    return out.reshape(b, l, d)


def kernel(x, c, ctx, c_ctx, ada_w, ada_b, norm1_g, norm2_g,
           hg_w_in, hg_lb, hg_onorm_g, hg_w_out,
           hy_w_in, hy_b_in, hy_short_w, hy_short_b, hy_f_w1, hy_f_b1, hy_f_w2, hy_f_b2,
           hy_f_w3, hy_f_b3, hy_f_w4, hy_f_freq, hy_f_bias, hy_w_out, hy_b_out,
           mb_w_in, mb_conv_w, mb_conv_b, mb_dt_bias, mb_a_log, mb_d, mb_norm_g, mb_w_out,
           gla_w_in, gla_gk_w2, gla_gk_b, gla_onorm_g, gla_w_out,
           router_w, moe_w_gate, moe_w_up, moe_w_down, final_norm_g):
    for i in range(DEPTH):
        last = i == DEPTH - 1
        sh1, sc1, g1, sh2, sc2, g2 = adaln(c, ada_w[i], ada_b[i])
        csh1, csc1, cg1, csh2, csc2, cg2 = adaln(c_ctx, ada_w[i], ada_b[i])
        hx = modulate(rmsnorm(x, norm1_g[i]), sh1, sc1)
        hc = modulate(rmsnorm(ctx, norm1_g[i]), csh1, csc1)
        kind, j = i % N_MIXERS, i // N_MIXERS
        if kind == 0:
            yx, yc = hgrn2_mixer_pallas(hx, hc, hg_w_in[j], hgrn_lower_bound(hg_lb, i), hg_onorm_g[j], hg_w_out[j])
        elif kind == 1:
            yx, yc = hyena_mixer_pallas(hx, hc, hy_w_in[j], hy_b_in[j], hy_short_w[j], hy_short_b[j],
                                 hy_f_w1[j], hy_f_b1[j], hy_f_w2[j], hy_f_b2[j], hy_f_w3[j], hy_f_b3[j],
                                 hy_f_w4[j], hy_f_freq[j], hy_f_bias[j], hy_w_out[j], hy_b_out[j])
        elif kind == 2:
            yx, yc = mamba2_mixer_pallas(hx, hc, mb_w_in[j], mb_conv_w[j], mb_conv_b[j], mb_dt_bias[j],
                                  mb_a_log[j], mb_d[j], mb_norm_g[j], mb_w_out[j])
        else:
            yx, yc = gla_mixer_pallas(hx, hc, gla_w_in[j], gla_gk_w2[j], gla_gk_b[j], gla_onorm_g[j], gla_w_out[j])
        x = x + g1 * yx
        x = expert_choice_ffn_residual(x, g2, modulate(rmsnorm(x, norm2_g[i]), sh2, sc2),
                                       router_w[i], moe_w_gate[i], moe_w_up[i], moe_w_down[i])
        if not last:
            ctx = ctx + cg1 * yc
            ctx = expert_choice_ffn_residual(ctx, cg2, modulate(rmsnorm(ctx, norm2_g[i]), csh2, csc2),
                                             router_w[i], moe_w_gate[i], moe_w_up[i], moe_w_down[i])
    return final_rmsnorm(x, final_norm_g)
```

```python
import math
import jax, jax.numpy as jnp
from jax import lax
from jax.experimental import pallas as pl
from jax.experimental.pallas import tpu as pltpu

D_MODEL = 1024
BATCH = 8
SEQ = 4096
DEPTH = 4

F32 = jnp.float32
GRID_W = 64
CTX_LEN = 256
N_MIXERS = 4
NORM_EPS = 1e-6

HG_HEADS = 8
HG_KEY = D_MODEL // HG_HEADS
HG_VAL = D_MODEL // HG_HEADS
HG_CHUNK = 32

HY_ORDER = 2
HY_SHORT = 3
HY_EMB = 33
HY_FILTER_W = 64
HY_DECAY_TARGET = 1e-2
HY_DECAY_HI_PCT = 0.3
HY_DECAY_LO_PCT = 1.5

MB_INNER = 2 * D_MODEL
MB_HEADDIM = 64
MB_HEADS = MB_INNER // MB_HEADDIM
MB_GROUPS = 8
MB_HPG = MB_HEADS // MB_GROUPS
MB_STATE = 128
MB_CONV = 5
MB_CHUNK = 64
MB_CONV_DIM = MB_INNER + 2 * MB_GROUPS * MB_STATE
MB_IN = MB_INNER + MB_CONV_DIM + 2 * MB_HEADS

GLA_HEADS = 4
GLA_KEY_DIM = D_MODEL // 2
GLA_VAL_DIM = D_MODEL
GLA_HK = GLA_KEY_DIM // GLA_HEADS
GLA_HV = GLA_VAL_DIM // GLA_HEADS
GLA_GATE_RANK = 16
GLA_GATE_NORM = 16.0
GLA_CHUNK = 32
GLA_IN = 2 * GLA_KEY_DIM + 2 * GLA_VAL_DIM + 2 * GLA_GATE_RANK

N_EXPERTS = 16
EC_CAPACITY_FACTOR = 2
EXPERT_FF = 2048


def rmsnorm(x, g):
    xf = x.astype(F32)
    return xf * lax.rsqrt(jnp.mean(xf * xf, axis=-1, keepdims=True) + NORM_EPS) * g


def adaln(cond, w, b):
    m = jax.nn.silu(cond.astype(F32)) @ w + b
    return jnp.split(m[..., None, :], 6, axis=-1)


def modulate(h, shift, scale):
    return h * (1.0 + scale) + shift


def heads(t, n):
    b, l, _ = t.shape
    return t.reshape(b, l, n, -1).transpose(0, 2, 1, 3)


def dwconv_centered(x, w, b):
    k_w = w.shape[0]
    pad = k_w // 2
    length = x.shape[1]
    xp = jnp.pad(x, ((0, 0), (pad, pad), (0, 0)))
    return sum(xp[:, j:j + length] * w[j] for j in range(k_w)) + b


def raster_to_colmajor(h, rows):
    b, l, d = h.shape
    return h.reshape(b, rows, GRID_W, d).transpose(0, 2, 1, 3).reshape(b, l, d)


def colmajor_to_raster(h, rows):
    b, l, d = h.shape
    return h.reshape(b, GRID_W, rows, d).transpose(0, 2, 1, 3).reshape(b, l, d)


def chunk_gla(q, k, v, g, s0, chunk):
    bsz, nh, length, kd = q.shape
    vd = v.shape[-1]
    n = length // chunk

    def split_chunks(t):
        return jnp.moveaxis(t.astype(F32).reshape(bsz, nh, n, chunk, t.shape[-1]), 2, 0)

    qs, ks, vs, gs = split_chunks(q), split_chunks(k), split_chunks(v), split_chunks(g)
    cum = jnp.cumsum(gs, axis=3)
    lower = jnp.tril(jnp.ones((chunk, chunk), bool))[:, :, None]

    def step(state, inp):
        qc, kc, vc, gc = inp
        rel = jnp.where(lower, gc[:, :, :, None, :] - gc[:, :, None, :, :], -jnp.inf)
        att = jnp.einsum('bhtk,bhsk,bhtsk->bhts', qc, kc, jnp.exp(rel))
        g_end = gc[:, :, -1:, :]
        o = att @ vc + jnp.einsum('bhtk,bhkv->bhtv', qc * jnp.exp(gc), state)
        state = jnp.exp(g_end[:, :, 0, :, None]) * state + jnp.einsum('bhsk,bhsv->bhkv', kc * jnp.exp(g_end - gc), vc)
        return state, o

    state, o = lax.scan(step, s0.astype(F32), (qs, ks, vs, cum))
    return jnp.moveaxis(o, 0, 2).reshape(bsz, nh, length, vd), state


def bidir_gla(lat, ctx, chunk):
    q, kf, kb, v, gf, gb = lat
    qc, kfc, kbc, vc, gfc, gbc = ctx
    s0 = jnp.zeros(q.shape[:2] + (q.shape[-1], v.shape[-1]), F32)
    oc_f, s_f = chunk_gla(qc, kfc, vc, gfc, s0, chunk)
    ox_f, _ = chunk_gla(q, kf, v, gf, s_f, chunk)
    r = lambda t: jnp.flip(t, axis=2)
    oc_b, s_b = chunk_gla(r(qc), r(kbc), r(vc), r(gbc), s0, chunk)
    ox_b, _ = chunk_gla(r(q), r(kb), r(v), r(gb), s_b, chunk)
    return ox_f + r(ox_b), oc_f + r(oc_b)


def gated_head_out(o, og, gain, w_out):
    b, nh, l, vd = o.shape
    o = rmsnorm(o.transpose(0, 2, 1, 3), gain).reshape(b, l, nh * vd)
    return (o * jax.nn.silu(og)) @ w_out


def hgrn_lower_bound(lb_param, layer):
    return jnp.cumsum(jax.nn.softmax(lb_param.astype(F32), axis=0), axis=0)[layer]


def hgrn2_mixer(hx, hc, w_in, lb, onorm_g, w_out):
    def project(h):
        q, ff, fb, inp, og = jnp.split(h @ w_in, 5, axis=-1)
        f_fwd = lb + (1.0 - lb) * jax.nn.sigmoid(ff.astype(F32))
        f_bwd = lb + (1.0 - lb) * jax.nn.sigmoid(fb.astype(F32))
        hd = lambda t: heads(t, HG_HEADS)
        streams = (hd(jax.nn.silu(q) * HG_KEY ** -0.5), hd(1.0 - f_fwd), hd(1.0 - f_bwd), hd(inp),
                   hd(jnp.log(f_fwd)), hd(jnp.log(f_bwd)))
        return streams, og

    lat, ogx = project(hx)
    ctxs, ogc = project(hc)
    ox, oc = bidir_gla(lat, ctxs, HG_CHUNK)
    return gated_head_out(ox, ogx, onorm_g, w_out), gated_head_out(oc, ogc, onorm_g, w_out)


def hyena_pos_features(length):
    t = jnp.linspace(0.0, 1.0, length, dtype=F32)[:, None]
    bands = (HY_EMB - 1) // 2
    f = jnp.linspace(1e-4, bands - 1, bands, dtype=F32)[None, :]
    w = 2.0 * math.pi * jnp.arange(length, dtype=F32)[:, None] / length
    return jnp.concatenate([t, jnp.cos(f * w), -jnp.sin(f * w)], axis=-1), t


def hyena_window(t):
    max_decay = math.log(HY_DECAY_TARGET) / HY_DECAY_HI_PCT
    min_decay = math.log(HY_DECAY_TARGET) / HY_DECAY_LO_PCT
    deltas = jnp.abs(jnp.linspace(min_decay, max_decay, D_MODEL, dtype=F32))
    return jnp.exp(-t * deltas[None, :])


def hyena_filters(length, w1, b1, w2, b2, w3, b3, w4, freq):
    z, t = hyena_pos_features(length)
    a = jnp.sin(freq * (z @ w1 + b1))
    a = jnp.sin(freq * (a @ w2 + b2))
    a = jnp.sin(freq * (a @ w3 + b3))
    h = (a @ w4).astype(F32).reshape(length, HY_ORDER, 2, D_MODEL) * hyena_window(t)[:, None, None, :]
    return h / jnp.sum(jnp.abs(h), axis=(0, 2), keepdims=True)


def twosided_fftconv(u, h_fwd, h_bwd):
    length = u.shape[1]
    k2 = jnp.concatenate([h_fwd, jnp.zeros_like(h_fwd[:1]), h_bwd[:0:-1]], axis=0)
    spec = jnp.fft.rfft(u, n=2 * length, axis=1) * jnp.fft.rfft(k2, axis=0)[None]
    return jnp.fft.irfft(spec, n=2 * length, axis=1)[:, :length]


def hyena_mixer(hx, hc, w_in, b_in, short_w, short_b, f_w1, f_b1, f_w2, f_b2, f_w3, f_b3, f_w4,
                f_freq, f_bias, w_out, b_out):
    def run(h):
        length = h.shape[1]
        u = dwconv_centered(h @ w_in + b_in, short_w, short_b).astype(F32)
        x1, x2, v = jnp.split(u, 3, axis=-1)
        filt = hyena_filters(length, f_w1, f_b1, f_w2, f_b2, f_w3, f_b3, f_w4, f_freq)
        z = v
        for o, gate in enumerate((x1, x2)):
            z = gate * (twosided_fftconv(z, filt[:, o, 0], filt[:, o, 1]) + z * f_bias[o])
        return z @ w_out + b_out

    return run(hx), run(hc)


def chunk_ssd(xs, bm, cm, dt, a, s0, chunk):
    bsz, length = xs.shape[:2]
    n = length // chunk

    def split_chunks(t):
        return jnp.moveaxis(t.astype(F32).reshape((bsz, n, chunk) + t.shape[2:]), 1, 0)

    xcs, bcs, ccs, dts = split_chunks(xs), split_chunks(bm), split_chunks(cm), split_chunks(dt)
    cum = jnp.cumsum(dts * a.astype(F32), axis=2)
    lower = jnp.tril(jnp.ones((chunk, chunk), bool))[:, :, None, None]

    def step(state, inp):
        xc, bc, cc, dtc, lc = inp
        seg = jnp.exp(jnp.where(lower, lc[:, :, None] - lc[:, None, :], -jnp.inf))
        w = jnp.einsum('btgn,bsgn->btsg', cc, bc)[..., None] * seg * dtc[:, None]
        y = jnp.einsum('btsgh,bsghp->btghp', w, xc)
        y = y + jnp.einsum('btgn,bghnp->btghp', cc, state) * jnp.exp(lc)[..., None]
        l_end = lc[:, -1]
        state = jnp.exp(l_end)[..., None, None] * state + jnp.einsum(
            'bsgn,bsgh,bsghp->bghnp', bc, jnp.exp(l_end[:, None] - lc) * dtc, xc)
        return state, y

    state, y = lax.scan(step, s0.astype(F32), (xcs, bcs, ccs, dts, cum))
    return jnp.moveaxis(y, 0, 1).reshape(xs.shape), state


def mamba2_mixer(hx, hc, w_in, conv_w, conv_b, dt_bias, a_log, d_skip, norm_g, w_out):
    a = -jnp.exp(a_log.astype(F32)).reshape(2, MB_GROUPS, MB_HPG)

    def project(h):
        bsz, length, _ = h.shape
        z, xbc, dt_raw = jnp.split(h @ w_in, [MB_INNER, MB_INNER + MB_CONV_DIM], axis=-1)
        xbc = jax.nn.silu(dwconv_centered(xbc, conv_w, conv_b))
        xs, bm, cm = jnp.split(xbc, [MB_INNER, MB_INNER + MB_GROUPS * MB_STATE], axis=-1)
        xs = xs.reshape(bsz, length, MB_GROUPS, MB_HPG, MB_HEADDIM)
        bm = bm.reshape(bsz, length, MB_GROUPS, MB_STATE)
        cm = cm.reshape(bsz, length, MB_GROUPS, MB_STATE)
        dt = jax.nn.softplus(dt_raw.astype(F32).reshape(bsz, length, 2, MB_GROUPS, MB_HPG)
                             + dt_bias.reshape(2, MB_GROUPS, MB_HPG))
        return z, xs, bm, cm, dt

    zx, xx, bx, cx, dtx = project(hx)
    zc, xc, bc, cc, dtc = project(hc)
    s0 = jnp.zeros((hx.shape[0], MB_GROUPS, MB_HPG, MB_STATE, MB_HEADDIM), F32)
    yc_f, s_f = chunk_ssd(xc, bc, cc, dtc[:, :, 0], a[0], s0, MB_CHUNK)
    yx_f, _ = chunk_ssd(xx, bx, cx, dtx[:, :, 0], a[0], s_f, MB_CHUNK)
    r = lambda t: jnp.flip(t, axis=1)
    yc_b, s_b = chunk_ssd(r(xc), r(bc), r(cc), r(dtc[:, :, 1]), a[1], s0, MB_CHUNK)
    yx_b, _ = chunk_ssd(r(xx), r(bx), r(cx), r(dtx[:, :, 1]), a[1], s_b, MB_CHUNK)

    def finish(y, xs, z):
        bsz, length = y.shape[:2]
        y = y + d_skip.reshape(MB_GROUPS, MB_HPG)[..., None] * xs
        y = y.reshape(bsz, length, MB_INNER) * jax.nn.silu(z)
        y = rmsnorm(y.reshape(bsz, length, MB_GROUPS, MB_INNER // MB_GROUPS),
                    norm_g.reshape(MB_GROUPS, MB_INNER // MB_GROUPS))
        return y.reshape(bsz, length, MB_INNER) @ w_out

    return finish(yx_f + r(yx_b), xx, zx), finish(yc_f + r(yc_b), xc, zc)


def gla_mixer(hx, hc, w_in, gk_w2, gk_b, onorm_g, w_out):
    rows = hx.shape[1] // GRID_W
    kd, vd, rk = GLA_KEY_DIM, GLA_VAL_DIM, GLA_GATE_RANK
    cuts = [kd, 2 * kd, 2 * kd + vd, 2 * kd + 2 * vd, 2 * kd + 2 * vd + rk]

    def project(h):
        q, k, v, og, rf, rb = jnp.split(h @ w_in, cuts, axis=-1)
        gf = jax.nn.log_sigmoid((rf @ gk_w2[0] + gk_b[0]).astype(F32)) / GLA_GATE_NORM
        gb = jax.nn.log_sigmoid((rb @ gk_w2[1] + gk_b[1]).astype(F32)) / GLA_GATE_NORM
        hd = lambda t: heads(t, GLA_HEADS)
        kh = hd(k)
        return (hd(q * GLA_HK ** -0.5), kh, kh, hd(v), hd(gf), hd(gb)), og

    lat, ogx = project(raster_to_colmajor(hx, rows))
    ctxs, ogc = project(hc)
    ox, oc = bidir_gla(lat, ctxs, GLA_CHUNK)
    yx = colmajor_to_raster(gated_head_out(ox, ogx, onorm_g, w_out), rows)
    return yx, gated_head_out(oc, ogc, onorm_g, w_out)


def expert_choice_ffn(h, router_w, w_gate, w_up, w_down):
    bsz, t_len, d = h.shape
    cap = max(1, EC_CAPACITY_FACTOR * t_len // N_EXPERTS)
    aff = jax.nn.softmax(h.astype(F32) @ router_w.astype(F32), axis=-1)
    gate, idx = lax.top_k(jnp.swapaxes(aff, 1, 2), cap)
    xg = jax.vmap(lambda hb, ib: hb[ib])(h, idx)
    hid = jax.nn.silu(jnp.einsum('becd,edf->becf', xg, w_gate)) * jnp.einsum('becd,edf->becf', xg, w_up)
    ye = jnp.einsum('becf,efd->becd', hid, w_down) * gate[..., None]
    return jax.vmap(lambda ib, yb: jnp.zeros((t_len, d), yb.dtype).at[ib.reshape(-1)].add(yb.reshape(-1, d)))(idx, ye)


BF16 = jnp.bfloat16
LANES = 128
SUBLANES = 8
VMEM_LIMIT_BYTES = 56 * 2 ** 20
SCAN_CHUNK = 128
SCAN_FAST_RANGE = 80.0
SCAN_UNROLL = 8


def _cumsum_rows(tri_bf16, g):
    hi = g.astype(BF16)
    r1 = g - hi.astype(F32)
    mid = r1.astype(BF16)
    lo = (r1 - mid.astype(F32)).astype(BF16)
    d = lambda a: jnp.dot(tri_bf16, a, preferred_element_type=F32)
    return d(hi) + d(mid) + d(lo)


def _log_sigmoid(z):
    return jnp.minimum(z, 0.0) - jnp.log1p(jnp.exp(-jnp.abs(z)))


def _dot_nt(a, b):
    return lax.dot_general(a, b, (((1,), (1,)), ((), ())), preferred_element_type=F32)


def _dot_tn(a, b):
    return lax.dot_general(a, b, (((0,), (0,)), ((), ())), preferred_element_type=F32)


def _gla_scan_body(mode, seq, ctx_len, kdim, vdim, *refs):
    chunk = SCAN_CHUNK
    if mode == 'hgrn2':
        (q_x, ff_x, fb_x, v_x, og_x, q_c, ff_c, fb_c, v_c, og_c, lb_ref, gain_ref,
         o_x, o_c, cf_s, cb_s, sf_s, sb_s, kt_s, gt_s, qf_s, qb_s, of_s, ob_s, kf_s, kb_s) = refs
    else:
        (q_x, k_x, v_x, og_x, r_x, q_c, k_c, v_c, og_c, r_c, w2f_ref, w2b_ref, bf_ref, bb_ref, gain_ref,
         o_x, o_c, cf_s, cb_s, sf_s, sb_s, kt_s, gt_s, qf_s, qb_s, of_s, ob_s) = refs
    if mode == 'hgrn2':
        segs = (((ff_c, fb_c), q_c, None, v_c, og_c, o_c, ctx_len, 0),
                ((ff_x, fb_x), q_x, None, v_x, og_x, o_x, seq, ctx_len))
    else:
        segs = (((r_c,), q_c, k_c, v_c, og_c, o_c, ctx_len, 0),
                ((r_x,), q_x, k_x, v_x, og_x, o_x, seq, ctx_len))

    row = lax.broadcasted_iota(jnp.int32, (chunk, chunk), 0)
    col = lax.broadcasted_iota(jnp.int32, (chunk, chunk), 1)
    masks = (row >= col, row <= col)
    tris = tuple(m.astype(BF16) for m in masks)
    cum_s, state_s, qin_s, kout_s = (cf_s, cb_s), (sf_s, sb_s), (qf_s, qb_s), (of_s, ob_s)
    end_row = (chunk - 1, 0)
    mid_row = chunk // 2

    def rows_of(i, off=0):
        return pl.ds(pl.multiple_of(off + i * chunk, chunk), chunk)

    def gate_pass(seg, bound):
        gsrc, _, _, _, _, _, length, off = seg

        def body(i, bound):
            rows, srows = rows_of(i), rows_of(i, off)
            if mode == 'hgrn2':
                lb = lb_ref[...]
                f_f = lb + (1.0 - lb) * jax.nn.sigmoid(gsrc[0][rows, :])
                f_b = lb + (1.0 - lb) * jax.nn.sigmoid(gsrc[1][rows, :])
                g_f, g_b = jnp.log(f_f), jnp.log(f_b)
                kf_s[srows, :] = 1.0 - f_f
                kb_s[srows, :] = 1.0 - f_b
            else:
                r = gsrc[0][rows, :].astype(BF16)
                z_f = jnp.dot(r, w2f_ref[...].astype(BF16), preferred_element_type=F32) + bf_ref[...]
                z_b = jnp.dot(r, w2b_ref[...].astype(BF16), preferred_element_type=F32) + bb_ref[...]
                g_f = _log_sigmoid(z_f) / GLA_GATE_NORM
                g_b = _log_sigmoid(z_b) / GLA_GATE_NORM
            c_f = _cumsum_rows(tris[0], g_f)
            c_b = _cumsum_rows(tris[1], g_b)
            cf_s[srows, :] = c_f
            cb_s[srows, :] = c_b
            for c in (c_f, c_b):
                mid = c[mid_row:mid_row + 1, :]
                spread = jnp.maximum(jnp.abs(c[0:1, :] - mid), jnp.abs(c[chunk - 1:chunk, :] - mid))
                bound = jnp.maximum(bound, spread)
            return bound

        return lax.fori_loop(0, length // chunk, body, bound, unroll=SCAN_UNROLL)

    def intra(seg, direction, i, exact):
        _, q_r, k_r, v_r, _, _, _, off = seg
        rows, srows = rows_of(i), rows_of(i, off)
        if mode == 'hgrn2':
            qq = jax.nn.silu(q_r[rows, :]) * HG_KEY ** -0.5
            k = (kf_s if direction == 0 else kb_s)[srows, :]
        else:
            qq = q_r[rows, :] * GLA_HK ** -0.5
            k = k_r[rows, :]
        cum = cum_s[direction][srows, :]
        e = end_row[direction]
        cum_end = cum[e:e + 1, :]
        q_in = (qq * jnp.exp(cum)).astype(BF16)
        qin_s[direction][srows, :] = q_in
        kout_s[direction][srows, :] = (k * jnp.exp(cum_end - cum)).astype(BF16)
        if exact:
            kt_s[...] = k
            gt_s[...] = cum

            def col_body(s, att):
                ks = kt_s[pl.ds(s, 1), :]
                gs = gt_s[pl.ds(s, 1), :]
                p = qq * ks * jnp.exp(jnp.minimum(cum - gs, 0.0))
                return att + jnp.where(col == s, jnp.sum(p, axis=-1, keepdims=True), 0.0)

            att = lax.fori_loop(0, chunk, col_body, jnp.zeros((chunk, chunk), F32))
        else:
            rel = cum - cum[mid_row:mid_row + 1, :]
            att = _dot_nt((qq * jnp.exp(rel)).astype(BF16), (k * jnp.exp(-rel)).astype(BF16))
        att = jnp.where(masks[direction], att, 0.0).astype(BF16)
        return jnp.dot(att, v_r[rows, :].astype(BF16), preferred_element_type=F32)

    def intra_pass(exact):
        for seg in segs:
            o_r, length = seg[5], seg[6]

            def body(i, carry):
                o_r[rows_of(i), :] = intra(seg, 0, i, exact) + intra(seg, 1, i, exact)
                return carry

            lax.fori_loop(0, length // chunk, body, 0, unroll=1 if exact else SCAN_UNROLL)

    def inter(seg, direction, i):
        v_r, off = seg[3], seg[7]
        rows, srows = rows_of(i), rows_of(i, off)
        s_ref = state_s[direction]
        state = s_ref[...]
        o = _dot_nt(qin_s[direction][srows, :], state.astype(BF16))
        e = pl.ds(pl.multiple_of(off + i * chunk, chunk) + end_row[direction], 1)
        decay = jnp.exp(cum_s[direction][e, :])
        s_ref[...] = state * decay + _dot_tn(v_r[rows, :].astype(BF16), kout_s[direction][srows, :])
        return o

    def finish(seg, i, o):
        o = o * lax.rsqrt(jnp.mean(o * o, axis=-1, keepdims=True) + NORM_EPS) * gain_ref[...]
        return o * jax.nn.silu(seg[4][rows_of(i), :])

    def inter_pass():
        sf_s[...] = jnp.zeros_like(sf_s)
        sb_s[...] = jnp.zeros_like(sb_s)
        for seg in segs:
            o_r, length = seg[5], seg[6]
            n = length // chunk

            def first_half(i, carry):
                j = n - 1 - i
                o_r[rows_of(i), :] += inter(seg, 0, i)
                o_r[rows_of(j), :] += inter(seg, 1, j)
                return carry

            def second_half(i, carry):
                j = n - 1 - i
                o_r[rows_of(i), :] = finish(seg, i, o_r[rows_of(i), :] + inter(seg, 0, i))
                o_r[rows_of(j), :] = finish(seg, j, o_r[rows_of(j), :] + inter(seg, 1, j))
                return carry

            lax.fori_loop(0, n // 2, first_half, 0, unroll=min(SCAN_UNROLL, n // 2))
            lax.fori_loop(n // 2, n, second_half, 0, unroll=min(SCAN_UNROLL, n // 2))

    bound = jnp.zeros((1, kdim), F32)
    for seg in segs:
        bound = gate_pass(seg, bound)
    fast = jnp.max(bound) <= SCAN_FAST_RANGE

    @pl.when(fast)
    def _():
        intra_pass(False)

    @pl.when(jnp.logical_not(fast))
    def _():
        intra_pass(True)

    inter_pass()


def _gla_scan_call(mode, nheads, kdim, vdim, lat_in, ctx_in, small_in, bsz, seq, ctx_len):
    def stream_spec(length, width, first, stride):
        return pl.BlockSpec((None, length, width), lambda b, h: (b, 0, first + stride * h))

    assert seq % (2 * SCAN_CHUNK) == 0 and ctx_len % (2 * SCAN_CHUNK) == 0
    in_specs = [stream_spec(seq, w, f, s) for _, w, f, s in lat_in]
    in_specs += [stream_spec(ctx_len, w, f, s) for _, w, f, s in ctx_in]
    in_specs += [pl.BlockSpec(blk, imap) for _, blk, imap in small_in]
    args = [a for a, _, _, _ in lat_in] + [a for a, _, _, _ in ctx_in] + [a for a, _, _ in small_in]
    tot = seq + ctx_len
    scratch = [pltpu.VMEM((tot, kdim), F32), pltpu.VMEM((tot, kdim), F32),
               pltpu.VMEM((vdim, kdim), F32), pltpu.VMEM((vdim, kdim), F32),
               pltpu.VMEM((SCAN_CHUNK, kdim), F32), pltpu.VMEM((SCAN_CHUNK, kdim), F32),
               pltpu.VMEM((tot, kdim), BF16), pltpu.VMEM((tot, kdim), BF16),
               pltpu.VMEM((tot, kdim), BF16), pltpu.VMEM((tot, kdim), BF16)]
    if mode == 'hgrn2':
        scratch += [pltpu.VMEM((tot, kdim), F32), pltpu.VMEM((tot, kdim), F32)]
    return pl.pallas_call(
        lambda *refs: _gla_scan_body(mode, seq, ctx_len, kdim, vdim, *refs),
        grid=(bsz, nheads),
        in_specs=in_specs,
        out_specs=[pl.BlockSpec((None, seq, vdim), lambda b, h: (b, 0, h)),
                   pl.BlockSpec((None, ctx_len, vdim), lambda b, h: (b, 0, h))],
        out_shape=[jax.ShapeDtypeStruct((bsz, seq, nheads * vdim), F32),
                   jax.ShapeDtypeStruct((bsz, ctx_len, nheads * vdim), F32)],
        scratch_shapes=scratch,
        compiler_params=pltpu.CompilerParams(dimension_semantics=("arbitrary", "arbitrary"),
                                             vmem_limit_bytes=VMEM_LIMIT_BYTES),
        name=f"{mode}_scan",
    )(*args)


def hgrn2_mixer_pallas(hx, hc, w_in, lb, onorm_g, w_out):
    bsz, seq, _ = hx.shape
    ctx_len = hc.shape[1]
    yx, yc = hx @ w_in, hc @ w_in
    nh = HG_HEADS
    streams = lambda y: [(y, HG_KEY, j * nh, 1) for j in range(5)]
    small = [(lb.reshape(1, D_MODEL), (1, HG_KEY), lambda b, h: (0, h)),
             (onorm_g.reshape(1, HG_VAL), (1, HG_VAL), lambda b, h: (0, 0))]
    ox, oc = _gla_scan_call('hgrn2', nh, HG_KEY, HG_VAL, streams(yx), streams(yc), small, bsz, seq, ctx_len)
    return ox @ w_out, oc @ w_out


def gla_mixer_pallas(hx, hc, w_in, gk_w2, gk_b, onorm_g, w_out):
    bsz, seq, _ = hx.shape
    ctx_len = hc.shape[1]
    rows = seq // GRID_W
    kd, vd, rk = GLA_KEY_DIM, GLA_VAL_DIM, GLA_GATE_RANK
    nh = GLA_HEADS
    w_in_p = jnp.pad(w_in, ((0, 0), (0, LANES - 2 * rk)))
    yx = raster_to_colmajor(hx, rows) @ w_in_p
    yc = hc @ w_in_p
    w2f = jnp.pad(gk_w2[0], ((0, LANES - rk), (0, 0)))
    w2b = jnp.pad(gk_w2[1], ((rk, LANES - 2 * rk), (0, 0)))
    streams = lambda y: [(y, GLA_HK, 0, 1), (y, GLA_HK, kd // GLA_HK, 1), (y, GLA_HV, 2 * kd // GLA_HV, 1),
                         (y, GLA_HV, (2 * kd + vd) // GLA_HV, 1), (y, LANES, (2 * kd + 2 * vd) // LANES, 0)]
    small = [(w2f, (LANES, GLA_HK), lambda b, h: (0, h)), (w2b, (LANES, GLA_HK), lambda b, h: (0, h)),
             (gk_b[0].reshape(1, kd), (1, GLA_HK), lambda b, h: (0, h)),
             (gk_b[1].reshape(1, kd), (1, GLA_HK), lambda b, h: (0, h)),
             (onorm_g.reshape(1, GLA_HV), (1, GLA_HV), lambda b, h: (0, 0))]
    ox, oc = _gla_scan_call('gla', nh, GLA_HK, GLA_HV, streams(yx), streams(yc), small, bsz, seq, ctx_len)
    return colmajor_to_raster(ox @ w_out, rows), oc @ w_out


MB_GROUP_W = MB_INNER // MB_GROUPS


def _ssd_scan_body(seq, ctx_len, *refs):
    chunk = MB_CHUNK
    gw, hpg = MB_GROUP_W, MB_HPG
    (x_x, b_x, c_x, z_x, cl_x, rt_x, x_c, b_c, c_c, z_c, cl_c, rt_c,
     wx_ref, wb_ref, wc_ref, bx_ref, bb_ref, bc_ref, dskip_ref, gain_ref,
     o_x, o_c, xf_s, xb_s, sf_s, sb_s, pad_s, xc_s, bc_s, cc_s) = refs
    segs = ((x_c, b_c, c_c, z_c, cl_c, rt_c, o_c, ctx_len, 0),
            (x_x, b_x, c_x, z_x, cl_x, rt_x, o_x, seq, ctx_len))
    xs_s, state_s = (xf_s, xb_s), (sf_s, sb_s)
    end_row = (chunk - 1, 0)

    def conv_silu(raw_ref, w_ref, bias_ref, dst_ref, length, off):
        width = raw_ref.shape[-1]
        halo = SUBLANES
        pad = MB_CONV // 2
        pad_s[0:halo, 0:width] = jnp.zeros((halo, width), F32)
        pad_s[pl.ds(halo + length, halo), 0:width] = jnp.zeros((halo, width), F32)

        def stage(i, carry):
            pad_s[pl.ds(pl.multiple_of(halo + i * chunk, SUBLANES), chunk), 0:width] = raw_ref[rows_of(i), :]
            return carry

        lax.fori_loop(0, length // chunk, stage, 0)

        def body(i, carry):
            rows_in = chunk + 2 * halo
            win = pad_s[pl.ds(pl.multiple_of(i * chunk, chunk), rows_in), 0:width]
            acc = jnp.broadcast_to(bias_ref[...], (chunk, width))
            for j in range(MB_CONV):
                lo = halo - pad + j
                acc = acc + pltpu.roll(win, rows_in - lo, axis=0)[0:chunk, :] * w_ref[j:j + 1, :]
            dst_ref[rows_of(i, off), :] = jax.nn.silu(acc).astype(dst_ref.dtype)
            return carry

        lax.fori_loop(0, length // chunk, body, 0)

    row = lax.broadcasted_iota(jnp.int32, (chunk, chunk), 0)
    col = lax.broadcasted_iota(jnp.int32, (chunk, chunk), 1)
    masks = (row >= col, row <= col)
    head_of_lane = lax.broadcasted_iota(jnp.int32, (1, gw), 1) // MB_HEADDIM

    def rows_of(i, off=0):
        return pl.ds(pl.multiple_of(off + i * chunk, chunk), chunk)

    def bcast_heads(c4):
        out = c4[:, hpg - 1:hpg]
        for hh in range(hpg - 2, -1, -1):
            out = jnp.where(head_of_lane == hh, c4[:, hh:hh + 1], out)
        return out

    def intra_pass():
        for seg in segs:
            _, _, _, _, cl_r, rt_r, o_r, length, off = seg

            def body(i, carry):
                rows, srows = rows_of(i), rows_of(i, off)
                cols, rt = cl_r[i], rt_r[i]
                x = xc_s[srows, :]
                cb = _dot_nt(cc_s[srows, :], bc_s[srows, :])
                xh = [jnp.where(head_of_lane == hh, x, 0.0).astype(BF16) for hh in range(hpg)]
                y = jnp.zeros((chunk, gw), F32)
                for d in range(2):
                    cum = cols[:, d * hpg:(d + 1) * hpg]
                    dt = cols[:, (2 + d) * hpg:(3 + d) * hpg]
                    e = end_row[d]
                    scale = jnp.exp(cum[e:e + 1, :] - cum) * dt
                    xs_s[d][srows, :] = (x * bcast_heads(scale)).astype(BF16)
                    for hh in range(hpg):
                        j = d * hpg + hh
                        seg_decay = jnp.exp(jnp.minimum(cols[:, j:j + 1] - rt[j:j + 1, :], 0.0))
                        w = jnp.where(masks[d], cb * seg_decay * rt[2 * hpg + j:2 * hpg + j + 1, :], 0.0)
                        y = y + jnp.dot(w.astype(BF16), xh[hh], preferred_element_type=F32)
                o_r[rows, :] = y
                return carry

            lax.fori_loop(0, length // chunk, body, 0, unroll=2)

    def inter(seg, d, i):
        cl_r, off = seg[4], seg[8]
        srows = rows_of(i, off)
        cum = cl_r[i][:, d * hpg:(d + 1) * hpg]
        state = state_s[d][...]
        y = jnp.dot(cc_s[srows, :], state.astype(BF16), preferred_element_type=F32)
        e = end_row[d]
        decay = bcast_heads(jnp.exp(cum[e:e + 1, :]))
        state_s[d][...] = state * decay + _dot_tn(bc_s[srows, :], xs_s[d][srows, :])
        return y * bcast_heads(jnp.exp(cum))

    def finish(seg, i, y):
        z_r, off = seg[3], seg[8]
        y = (y + dskip_ref[...] * xc_s[rows_of(i, off), :]) * jax.nn.silu(z_r[rows_of(i), :])
        return y * lax.rsqrt(jnp.mean(y * y, axis=-1, keepdims=True) + NORM_EPS) * gain_ref[...]

    def inter_pass():
        sf_s[...] = jnp.zeros_like(sf_s)
        sb_s[...] = jnp.zeros_like(sb_s)
        for seg in segs:
            o_r, length = seg[6], seg[7]
            n = length // chunk

            def first_half(i, carry):
                j = n - 1 - i
                o_r[rows_of(i), :] += inter(seg, 0, i)
                o_r[rows_of(j), :] += inter(seg, 1, j)
                return carry

            def second_half(i, carry):
                j = n - 1 - i
                o_r[rows_of(i), :] = finish(seg, i, o_r[rows_of(i), :] + inter(seg, 0, i))
                o_r[rows_of(j), :] = finish(seg, j, o_r[rows_of(j), :] + inter(seg, 1, j))
                return carry

            lax.fori_loop(0, n // 2, first_half, 0, unroll=2)
            lax.fori_loop(n // 2, n, second_half, 0, unroll=2)

    for seg in segs:
        conv_silu(seg[0], wx_ref, bx_ref, xc_s, seg[7], seg[8])
        conv_silu(seg[1], wb_ref, bb_ref, bc_s, seg[7], seg[8])
        conv_silu(seg[2], wc_ref, bc_ref, cc_s, seg[7], seg[8])
    intra_pass()
    inter_pass()


def _ssd_head_tables(dt_raw, dt_bias, a_log):
    bsz, length, _ = dt_raw.shape
    n = length // MB_CHUNK
    a = -jnp.exp(a_log.astype(F32)).reshape(2, MB_GROUPS, MB_HPG)
    dt = jax.nn.softplus(dt_raw.astype(F32).reshape(bsz, length, 2, MB_GROUPS, MB_HPG)
                         + dt_bias.reshape(2, MB_GROUPS, MB_HPG))
    dta = (dt * a).reshape(bsz, n, MB_CHUNK, 2, MB_GROUPS, MB_HPG)
    dtc = dt.reshape(bsz, n, MB_CHUNK, 2, MB_GROUPS, MB_HPG)
    lower = jnp.tril(jnp.ones((MB_CHUNK, MB_CHUNK), F32))
    cum_f = jnp.einsum('ts,bnsgh->bntgh', lower, dta[:, :, :, 0], precision=lax.Precision.HIGHEST)
    cum_b = jnp.einsum('st,bnsgh->bntgh', lower, dta[:, :, :, 1], precision=lax.Precision.HIGHEST)
    cols = jnp.concatenate([cum_f, cum_b, dtc[:, :, :, 0], dtc[:, :, :, 1]], axis=-1)
    cols = cols.transpose(0, 3, 1, 2, 4)
    return cols, jnp.swapaxes(cols, -1, -2)


def mamba2_mixer_pallas(hx, hc, w_in, conv_w, conv_b, dt_bias, a_log, d_skip, norm_g, w_out):
    bsz, seq, _ = hx.shape
    ctx_len = hc.shape[1]
    chunk, gw = MB_CHUNK, MB_GROUP_W

    def project(h):
        z = h @ w_in[:, :MB_INNER]
        xbc = h @ w_in[:, MB_INNER:MB_INNER + MB_CONV_DIM]
        cols, rows_t = _ssd_head_tables(h @ w_in[:, MB_INNER + MB_CONV_DIM:], dt_bias, a_log)
        return z, xbc, cols, rows_t

    yx, xbcx, clx, rtx = project(hx)
    yc, xbcc, clc, rtc = project(hc)
    conv_b2 = conv_b.reshape(1, MB_CONV_DIM)

    def conv_specs(rows):
        bw = MB_INNER // MB_STATE
        return [pl.BlockSpec((rows, gw), lambda b, g: (0, g)),
                pl.BlockSpec((rows, MB_STATE), lambda b, g: (0, bw + g)),
                pl.BlockSpec((rows, MB_STATE), lambda b, g: (0, bw + MB_GROUPS + g))]

    def stream_specs(length):
        n = length // chunk
        bw = MB_INNER // MB_STATE
        return [pl.BlockSpec((None, length, gw), lambda b, g: (b, 0, g)),
                pl.BlockSpec((None, length, MB_STATE), lambda b, g: (b, 0, bw + g)),
                pl.BlockSpec((None, length, MB_STATE), lambda b, g: (b, 0, bw + MB_GROUPS + g)),
                pl.BlockSpec((None, length, gw), lambda b, g: (b, 0, g)),
                pl.BlockSpec((None, None, n, chunk, 4 * MB_HPG), lambda b, g: (b, g, 0, 0, 0)),
                pl.BlockSpec((None, None, n, 4 * MB_HPG, chunk), lambda b, g: (b, g, 0, 0, 0))]

    tot = seq + ctx_len
    ox, oc = pl.pallas_call(
        lambda *refs: _ssd_scan_body(seq, ctx_len, *refs),
        grid=(bsz, MB_GROUPS),
        in_specs=stream_specs(seq) + stream_specs(ctx_len) + conv_specs(MB_CONV) + conv_specs(1) + [
            pl.BlockSpec((1, gw), lambda b, g: (0, g)), pl.BlockSpec((1, gw), lambda b, g: (0, g))],
        out_specs=[pl.BlockSpec((None, seq, gw), lambda b, g: (b, 0, g)),
                   pl.BlockSpec((None, ctx_len, gw), lambda b, g: (b, 0, g))],
        out_shape=[jax.ShapeDtypeStruct((bsz, seq, MB_INNER), F32),
                   jax.ShapeDtypeStruct((bsz, ctx_len, MB_INNER), F32)],
        scratch_shapes=[pltpu.VMEM((tot, gw), BF16), pltpu.VMEM((tot, gw), BF16),
                        pltpu.VMEM((MB_STATE, gw), F32), pltpu.VMEM((MB_STATE, gw), F32),
                        pltpu.VMEM((seq + 2 * SUBLANES, gw), F32), pltpu.VMEM((tot, gw), F32),
                        pltpu.VMEM((tot, MB_STATE), BF16), pltpu.VMEM((tot, MB_STATE), BF16)],
        compiler_params=pltpu.CompilerParams(dimension_semantics=("arbitrary", "arbitrary"),
                                             vmem_limit_bytes=VMEM_LIMIT_BYTES),
        name="ssd_scan",
    )(xbcx, xbcx, xbcx, yx, clx, rtx, xbcc, xbcc, xbcc, yc, clc, rtc,
      conv_w, conv_w, conv_w, conv_b2, conv_b2, conv_b2,
      jnp.repeat(d_skip, MB_HEADDIM).reshape(1, MB_INNER), norm_g.reshape(1, MB_INNER))
    return ox @ w_out, oc @ w_out


HY_BLOCK = 256
HY_FREQ_ROWS = 64


def _split_bf16(a):
    hi = a.astype(BF16)
    return hi, (a - hi.astype(F32)).astype(BF16)


def _dot_split(w_hi, w_lo, x):
    x_hi, x_lo = _split_bf16(x)
    d = lambda a, b: jnp.dot(a, b, preferred_element_type=F32)
    return d(w_hi, x_hi) + d(w_hi, x_lo) + d(w_lo, x_hi)


def _hyena_dft_matrices():
    p = HY_BLOCK
    f = jnp.arange(p, dtype=F32)[:, None] + 0.5
    b = jnp.arange(p, dtype=F32)[None, :]
    k = jnp.round(f * b * 2.0).astype(jnp.int32) % (4 * p)
    ang = k.astype(F32) * (2.0 * math.pi / (4 * p))
    fwd = jnp.concatenate([jnp.cos(ang), -jnp.sin(ang)], axis=0)
    inv = jnp.concatenate([jnp.cos(ang).T, -jnp.sin(ang).T], axis=1) / p
    return fwd, inv


def _hyena_filter_spectra(hf, hb, fwd):
    length, d = hf.shape
    p = HY_BLOCK
    nb = length // p
    h2 = jnp.concatenate([jnp.zeros((1, d), F32), jnp.flip(hb[1:], axis=0), hf], axis=0)
    blocks = h2.reshape(2 * nb, p, d)
    spec = jnp.einsum('fj,kjd->kfd', fwd, blocks, precision=lax.Precision.HIGHEST)
    s_re, s_im = spec[:, :p], spec[:, p:]
    sign = jnp.where(jnp.arange(p) % 2 == 0, 1.0, -1.0).astype(F32)[None, :, None]
    return s_re[1:] - sign * s_im[:-1], s_im[1:] + sign * s_re[:-1]


def _hyena_conv_body(nb, u_ref, x_ref, bias_ref, gre_ref, gim_ref, wfh_ref, wfl_ref, wih_ref, wil_ref,
                     o_ref, ure_s, uim_s, y_s):
    p, fr = HY_BLOCK, HY_FREQ_ROWS

    def rows_of(i):
        return pl.ds(pl.multiple_of(i * p, p), p)

    def forward(j, carry):
        spec = _dot_split(wfh_ref[...], wfl_ref[...], u_ref[rows_of(j), :])
        ure_s[j] = spec[:p]
        uim_s[j] = spec[p:]
        return carry

    lax.fori_loop(0, nb, forward, 0)

    def out_block(i, carry):
        for t in range(p // fr):
            r = pl.ds(t * fr, fr)

            def acc_body(j, acc):
                a_re, a_im = acc
                k = i - j + nb - 1
                g_re, g_im = gre_ref[k, r, :], gim_ref[k, r, :]
                u_re, u_im = ure_s[j, r, :], uim_s[j, r, :]
                return a_re + g_re * u_re - g_im * u_im, a_im + g_re * u_im + g_im * u_re

            zero = jnp.zeros((fr, u_ref.shape[-1]), F32)
            a_re, a_im = lax.fori_loop(0, nb, acc_body, (zero, zero), unroll=min(nb, 4))
            y_s[pl.ds(t * fr, fr), :] = a_re
            y_s[pl.ds(p + t * fr, fr), :] = a_im
        y = _dot_split(wih_ref[...], wil_ref[...], y_s[...])
        u = u_ref[rows_of(i), :]
        o_ref[rows_of(i), :] = x_ref[rows_of(i), :] * (y + u * bias_ref[...])
        return carry

    lax.fori_loop(0, nb, out_block, 0)


def _hyena_conv(u_arr, u_blk, x_arr, x_blk, bias, g_re, g_im, mats):
    bsz, length, _ = u_arr.shape
    d = bias.shape[-1]
    p = HY_BLOCK
    nb = length // p
    nseg = 2 * nb - 1
    const = lambda shape: pl.BlockSpec(shape, lambda c, b: (0, 0))
    return pl.pallas_call(
        lambda *refs: _hyena_conv_body(nb, *refs),
        grid=(d // LANES, bsz),
        in_specs=[pl.BlockSpec((None, length, LANES), lambda c, b: (b, 0, u_blk + c)),
                  pl.BlockSpec((None, length, LANES), lambda c, b: (b, 0, x_blk + c)),
                  pl.BlockSpec((1, LANES), lambda c, b: (0, c)),
                  pl.BlockSpec((nseg, p, LANES), lambda c, b: (0, 0, c)),
                  pl.BlockSpec((nseg, p, LANES), lambda c, b: (0, 0, c)),
                  const((2 * p, p)), const((2 * p, p)), const((p, 2 * p)), const((p, 2 * p))],
        out_specs=pl.BlockSpec((None, length, LANES), lambda c, b: (b, 0, c)),
        out_shape=jax.ShapeDtypeStruct((bsz, length, d), F32),
        scratch_shapes=[pltpu.VMEM((nb, p, LANES), F32), pltpu.VMEM((nb, p, LANES), F32),
                        pltpu.VMEM((2 * p, LANES), F32)],
        compiler_params=pltpu.CompilerParams(dimension_semantics=("arbitrary", "arbitrary"),
                                             vmem_limit_bytes=VMEM_LIMIT_BYTES),
        name="hyena_conv",
    )(u_arr, x_arr, bias.reshape(1, d), g_re, g_im, *mats)


def hyena_mixer_pallas(hx, hc, w_in, b_in, short_w, short_b, f_w1, f_b1, f_w2, f_b2, f_w3, f_b3, f_w4,
                       f_freq, f_bias, w_out, b_out):
    fwd, inv = _hyena_dft_matrices()
    mats = _split_bf16(fwd) + _split_bf16(inv)
    nblk = D_MODEL // LANES

    def taps(length):
        z, t = hyena_pos_features(length)
        a = jnp.sin(f_freq * (z @ f_w1 + f_b1))
        a = jnp.sin(f_freq * (a @ f_w2 + f_b2))
        a = jnp.sin(f_freq * (a @ f_w3 + f_b3))
        mw = (a @ f_w4).astype(F32) * jnp.tile(hyena_window(t), (1, 2 * HY_ORDER))
        norm = jnp.sum(jnp.abs(mw), axis=0).reshape(HY_ORDER, 2, D_MODEL).sum(axis=1)
        return mw, 1.0 / norm

    def spectra(mw, inv_norm, o):
        hf = mw[:, (2 * o) * D_MODEL:(2 * o + 1) * D_MODEL]
        hb = mw[:, (2 * o + 1) * D_MODEL:(2 * o + 2) * D_MODEL]
        g_re, g_im = _hyena_filter_spectra(hf, hb, fwd)
        return g_re * inv_norm[o], g_im * inv_norm[o]

    def run(h):
        length = h.shape[1]
        u = dwconv_centered(h @ w_in + b_in, short_w, short_b).astype(F32)
        mw, inv_norm = taps(length)
        g0, g1 = spectra(mw, inv_norm, 0), spectra(mw, inv_norm, 1)
        z = _hyena_conv(u, 2 * nblk, u, 0, f_bias[0], g0[0], g0[1], mats)
        z = _hyena_conv(z, 0, u, nblk, f_bias[1], g1[0], g1[1], mats)
        return z @ w_out + b_out

    return run(hx), run(hc)


MOE_ROW_TILE = 2048
MOE_COL_TILE = 512


def _moe_combine_body(t_len, idx_ref, gate_ref, ye_ref, x_ref, g2_ref, o_ref):
    e = pl.program_id(2)
    cap = idx_ref.shape[-1]
    rt = min(MOE_ROW_TILE, t_len)

    @pl.when(e == 0)
    def _():
        o_ref[...] = jnp.zeros_like(o_ref)

    yg = (ye_ref[...] * gate_ref[...]).astype(BF16)
    idx = idx_ref[...]

    def body(r, carry):
        r0 = pl.multiple_of(r * rt, rt)
        tok = lax.broadcasted_iota(jnp.int32, (rt, cap), 0) + r0
        onehot = jnp.where(tok == idx, 1.0, 0.0).astype(BF16)
        o_ref[pl.ds(r0, rt), :] += jnp.dot(onehot, yg, preferred_element_type=F32)
        return carry

    lax.fori_loop(0, t_len // rt, body, 0)

    @pl.when(e == pl.num_programs(2) - 1)
    def _():
        o_ref[...] = x_ref[...] + g2_ref[...] * o_ref[...]


def expert_choice_ffn_residual(x, g2, h, router_w, w_gate, w_up, w_down):
    bsz, t_len, d = h.shape
    cap = max(1, EC_CAPACITY_FACTOR * t_len // N_EXPERTS)
    aff = jax.nn.softmax(h.astype(F32) @ router_w.astype(F32), axis=-1)
    gate, idx = lax.top_k(jnp.swapaxes(aff, 1, 2), cap)
    xg = jax.vmap(lambda hb, ib: hb[ib])(h.astype(BF16), idx)
    pre_g = jnp.einsum('becd,edf->becf', xg, w_gate, preferred_element_type=BF16)
    pre_u = jnp.einsum('becd,edf->becf', xg, w_up, preferred_element_type=BF16)
    hid = (jax.nn.silu(pre_g.astype(F32)) * pre_u.astype(F32)).astype(BF16)
    ye = jnp.einsum('becf,efd->becd', hid, w_down, preferred_element_type=F32)
    ct = MOE_COL_TILE if t_len > MOE_ROW_TILE else d
    g2 = jnp.broadcast_to(g2.reshape(-1, 1, d), (bsz, 1, d))
    return pl.pallas_call(
        lambda *refs: _moe_combine_body(t_len, *refs),
        grid=(bsz, d // ct, N_EXPERTS),
        in_specs=[pl.BlockSpec((None, None, 1, cap), lambda b, c, e: (b, e, 0, 0)),
                  pl.BlockSpec((None, None, cap, 1), lambda b, c, e: (b, e, 0, 0)),
                  pl.BlockSpec((None, None, cap, ct), lambda b, c, e: (b, e, 0, c)),
                  pl.BlockSpec((None, t_len, ct), lambda b, c, e: (b, 0, c)),
                  pl.BlockSpec((None, 1, ct), lambda b, c, e: (b, 0, c))],
        out_specs=pl.BlockSpec((None, t_len, ct), lambda b, c, e: (b, 0, c)),
        out_shape=jax.ShapeDtypeStruct((bsz, t_len, d), F32),
        compiler_params=pltpu.CompilerParams(dimension_semantics=("arbitrary", "arbitrary", "arbitrary"),
                                             vmem_limit_bytes=VMEM_LIMIT_BYTES),
        name="moe_combine",
    )(idx.reshape(bsz, N_EXPERTS, 1, cap), gate.reshape(bsz, N_EXPERTS, cap, 1), ye, x, g2)


def _final_norm_body(x_ref, g_ref, o_ref):
    xf = x_ref[...]
    o_ref[...] = xf * lax.rsqrt(jnp.mean(xf * xf, axis=-1, keepdims=True) + NORM_EPS) * g_ref[...]


def final_rmsnorm(x, g):
    b, l, d = x.shape
    rows = b * l
    tm = 512
    out = pl.pallas_call(
        _final_norm_body,
        grid=(rows // tm,),
        in_specs=[pl.BlockSpec((tm, d), lambda i: (i, 0)), pl.BlockSpec((1, d), lambda i: (0, 0))],
        out_specs=pl.BlockSpec((tm, d), lambda i: (i, 0)),
        out_shape=jax.ShapeDtypeStruct((rows, d), F32),
    )(x.reshape(rows, d), g.reshape(1, d))
    return out.reshape(b, l, d)


def kernel(x, c, ctx, c_ctx, ada_w, ada_b, norm1_g, norm2_g,
           hg_w_in, hg_lb, hg_onorm_g, hg_w_out,
           hy_w_in, hy_b_in, hy_short_w, hy_short_b, hy_f_w1, hy_f_b1, hy_f_w2, hy_f_b2,
           hy_f_w3, hy_f_b3, hy_f_w4, hy_f_freq, hy_f_bias, hy_w_out, hy_b_out,
           mb_w_in, mb_conv_w, mb_conv_b, mb_dt_bias, mb_a_log, mb_d, mb_norm_g, mb_w_out,
           gla_w_in, gla_gk_w2, gla_gk_b, gla_onorm_g, gla_w_out,
           router_w, moe_w_gate, moe_w_up, moe_w_down, final_norm_g):
    for i in range(DEPTH):
        last = i == DEPTH - 1
        sh1, sc1, g1, sh2, sc2, g2 = adaln(c, ada_w[i], ada_b[i])
        csh1, csc1, cg1, csh2, csc2, cg2 = adaln(c_ctx, ada_w[i], ada_b[i])
        hx = modulate(rmsnorm(x, norm1_g[i]), sh1, sc1)
        hc = modulate(rmsnorm(ctx, norm1_g[i]), csh1, csc1)
        kind, j = i % N_MIXERS, i // N_MIXERS
        if kind == 0:
            yx, yc = hgrn2_mixer_pallas(hx, hc, hg_w_in[j], hgrn_lower_bound(hg_lb, i), hg_onorm_g[j], hg_w_out[j])
        elif kind == 1:
            yx, yc = hyena_mixer_pallas(hx, hc, hy_w_in[j], hy_b_in[j], hy_short_w[j], hy_short_b[j],
                                 hy_f_w1[j], hy_f_b1[j], hy_f_w2[j], hy_f_b2[j], hy_f_w3[j], hy_f_b3[j],
                                 hy_f_w4[j], hy_f_freq[j], hy_f_bias[j], hy_w_out[j], hy_b_out[j])
        elif kind == 2:
            yx, yc = mamba2_mixer_pallas(hx, hc, mb_w_in[j], mb_conv_w[j], mb_conv_b[j], mb_dt_bias[j],
                                  mb_a_log[j], mb_d[j], mb_norm_g[j], mb_w_out[j])
        else:
            yx, yc = gla_mixer_pallas(hx, hc, gla_w_in[j], gla_gk_w2[j], gla_gk_b[j], gla_onorm_g[j], gla_w_out[j])
        x = x + g1 * yx
        x = expert_choice_ffn_residual(x, g2, modulate(rmsnorm(x, norm2_g[i]), sh2, sc2),
                                       router_w[i], moe_w_gate[i], moe_w_up[i], moe_w_down[i])
        if not last:
            ctx = ctx + cg1 * yc
            ctx = expert_choice_ffn_residual(ctx, cg2, modulate(rmsnorm(ctx, norm2_g[i]), csh2, csc2),
                                             router_w[i], moe_w_gate[i], moe_w_up[i], moe_w_down[i])
    return final_rmsnorm(x, final_norm_g)
```

```python
import math
import jax, jax.numpy as jnp
from jax import lax
from jax.experimental import pallas as pl
from jax.experimental.pallas import tpu as pltpu

D_MODEL = 1024
BATCH = 8
SEQ = 4096
DEPTH = 4

F32 = jnp.float32
GRID_W = 64
CTX_LEN = 256
N_MIXERS = 4
NORM_EPS = 1e-6

HG_HEADS = 8
HG_KEY = D_MODEL // HG_HEADS
HG_VAL = D_MODEL // HG_HEADS
HG_CHUNK = 32

HY_ORDER = 2
HY_SHORT = 3
HY_EMB = 33
HY_FILTER_W = 64
HY_DECAY_TARGET = 1e-2
HY_DECAY_HI_PCT = 0.3
HY_DECAY_LO_PCT = 1.5

MB_INNER = 2 * D_MODEL
MB_HEADDIM = 64
MB_HEADS = MB_INNER // MB_HEADDIM
MB_GROUPS = 8
MB_HPG = MB_HEADS // MB_GROUPS
MB_STATE = 128
MB_CONV = 5
MB_CHUNK = 64
MB_CONV_DIM = MB_INNER + 2 * MB_GROUPS * MB_STATE
MB_IN = MB_INNER + MB_CONV_DIM + 2 * MB_HEADS

GLA_HEADS = 4
GLA_KEY_DIM = D_MODEL // 2
GLA_VAL_DIM = D_MODEL
GLA_HK = GLA_KEY_DIM // GLA_HEADS
GLA_HV = GLA_VAL_DIM // GLA_HEADS
GLA_GATE_RANK = 16
GLA_GATE_NORM = 16.0
GLA_CHUNK = 32
GLA_IN = 2 * GLA_KEY_DIM + 2 * GLA_VAL_DIM + 2 * GLA_GATE_RANK

N_EXPERTS = 16
EC_CAPACITY_FACTOR = 2
EXPERT_FF = 2048


def rmsnorm(x, g):
    xf = x.astype(F32)
    return xf * lax.rsqrt(jnp.mean(xf * xf, axis=-1, keepdims=True) + NORM_EPS) * g


def adaln(cond, w, b):
    m = jax.nn.silu(cond.astype(F32)) @ w + b
    return jnp.split(m[..., None, :], 6, axis=-1)


def modulate(h, shift, scale):
    return h * (1.0 + scale) + shift


def heads(t, n):
    b, l, _ = t.shape
    return t.reshape(b, l, n, -1).transpose(0, 2, 1, 3)


def dwconv_centered(x, w, b):
    k_w = w.shape[0]
    pad = k_w // 2
    length = x.shape[1]
    xp = jnp.pad(x, ((0, 0), (pad, pad), (0, 0)))
    return sum(xp[:, j:j + length] * w[j] for j in range(k_w)) + b


def raster_to_colmajor(h, rows):
    b, l, d = h.shape
    return h.reshape(b, rows, GRID_W, d).transpose(0, 2, 1, 3).reshape(b, l, d)


def colmajor_to_raster(h, rows):
    b, l, d = h.shape
    return h.reshape(b, GRID_W, rows, d).transpose(0, 2, 1, 3).reshape(b, l, d)


def chunk_gla(q, k, v, g, s0, chunk):
    bsz, nh, length, kd = q.shape
    vd = v.shape[-1]
    n = length // chunk

    def split_chunks(t):
        return jnp.moveaxis(t.astype(F32).reshape(bsz, nh, n, chunk, t.shape[-1]), 2, 0)

    qs, ks, vs, gs = split_chunks(q), split_chunks(k), split_chunks(v), split_chunks(g)
    cum = jnp.cumsum(gs, axis=3)
    lower = jnp.tril(jnp.ones((chunk, chunk), bool))[:, :, None]

    def step(state, inp):
        qc, kc, vc, gc = inp
        rel = jnp.where(lower, gc[:, :, :, None, :] - gc[:, :, None, :, :], -jnp.inf)
        att = jnp.einsum('bhtk,bhsk,bhtsk->bhts', qc, kc, jnp.exp(rel))
        g_end = gc[:, :, -1:, :]
        o = att @ vc + jnp.einsum('bhtk,bhkv->bhtv', qc * jnp.exp(gc), state)
        state = jnp.exp(g_end[:, :, 0, :, None]) * state + jnp.einsum('bhsk,bhsv->bhkv', kc * jnp.exp(g_end - gc), vc)
        return state, o

    state, o = lax.scan(step, s0.astype(F32), (qs, ks, vs, cum))
    return jnp.moveaxis(o, 0, 2).reshape(bsz, nh, length, vd), state


def bidir_gla(lat, ctx, chunk):
    q, kf, kb, v, gf, gb = lat
    qc, kfc, kbc, vc, gfc, gbc = ctx
    s0 = jnp.zeros(q.shape[:2] + (q.shape[-1], v.shape[-1]), F32)
    oc_f, s_f = chunk_gla(qc, kfc, vc, gfc, s0, chunk)
    ox_f, _ = chunk_gla(q, kf, v, gf, s_f, chunk)
    r = lambda t: jnp.flip(t, axis=2)
    oc_b, s_b = chunk_gla(r(qc), r(kbc), r(vc), r(gbc), s0, chunk)
    ox_b, _ = chunk_gla(r(q), r(kb), r(v), r(gb), s_b, chunk)
    return ox_f + r(ox_b), oc_f + r(oc_b)


def gated_head_out(o, og, gain, w_out):
    b, nh, l, vd = o.shape
    o = rmsnorm(o.transpose(0, 2, 1, 3), gain).reshape(b, l, nh * vd)
    return (o * jax.nn.silu(og)) @ w_out


def hgrn_lower_bound(lb_param, layer):
    return jnp.cumsum(jax.nn.softmax(lb_param.astype(F32), axis=0), axis=0)[layer]


def hgrn2_mixer(hx, hc, w_in, lb, onorm_g, w_out):
    def project(h):
        q, ff, fb, inp, og = jnp.split(h @ w_in, 5, axis=-1)
        f_fwd = lb + (1.0 - lb) * jax.nn.sigmoid(ff.astype(F32))
        f_bwd = lb + (1.0 - lb) * jax.nn.sigmoid(fb.astype(F32))
        hd = lambda t: heads(t, HG_HEADS)
        streams = (hd(jax.nn.silu(q) * HG_KEY ** -0.5), hd(1.0 - f_fwd), hd(1.0 - f_bwd), hd(inp),
                   hd(jnp.log(f_fwd)), hd(jnp.log(f_bwd)))
        return streams, og

    lat, ogx = project(hx)
    ctxs, ogc = project(hc)
    ox, oc = bidir_gla(lat, ctxs, HG_CHUNK)
    return gated_head_out(ox, ogx, onorm_g, w_out), gated_head_out(oc, ogc, onorm_g, w_out)


def hyena_pos_features(length):
    t = jnp.linspace(0.0, 1.0, length, dtype=F32)[:, None]
    bands = (HY_EMB - 1) // 2
    f = jnp.linspace(1e-4, bands - 1, bands, dtype=F32)[None, :]
    w = 2.0 * math.pi * jnp.arange(length, dtype=F32)[:, None] / length
    return jnp.concatenate([t, jnp.cos(f * w), -jnp.sin(f * w)], axis=-1), t


def hyena_window(t):
    max_decay = math.log(HY_DECAY_TARGET) / HY_DECAY_HI_PCT
    min_decay = math.log(HY_DECAY_TARGET) / HY_DECAY_LO_PCT
    deltas = jnp.abs(jnp.linspace(min_decay, max_decay, D_MODEL, dtype=F32))
    return jnp.exp(-t * deltas[None, :])


def hyena_filters(length, w1, b1, w2, b2, w3, b3, w4, freq):
    z, t = hyena_pos_features(length)
    a = jnp.sin(freq * (z @ w1 + b1))
    a = jnp.sin(freq * (a @ w2 + b2))
    a = jnp.sin(freq * (a @ w3 + b3))
    h = (a @ w4).astype(F32).reshape(length, HY_ORDER, 2, D_MODEL) * hyena_window(t)[:, None, None, :]
    return h / jnp.sum(jnp.abs(h), axis=(0, 2), keepdims=True)


def twosided_fftconv(u, h_fwd, h_bwd):
    length = u.shape[1]
    k2 = jnp.concatenate([h_fwd, jnp.zeros_like(h_fwd[:1]), h_bwd[:0:-1]], axis=0)
    spec = jnp.fft.rfft(u, n=2 * length, axis=1) * jnp.fft.rfft(k2, axis=0)[None]
    return jnp.fft.irfft(spec, n=2 * length, axis=1)[:, :length]


def hyena_mixer(hx, hc, w_in, b_in, short_w, short_b, f_w1, f_b1, f_w2, f_b2, f_w3, f_b3, f_w4,
                f_freq, f_bias, w_out, b_out):
    def run(h):
        length = h.shape[1]
        u = dwconv_centered(h @ w_in + b_in, short_w, short_b).astype(F32)
        x1, x2, v = jnp.split(u, 3, axis=-1)
        filt = hyena_filters(length, f_w1, f_b1, f_w2, f_b2, f_w3, f_b3, f_w4, f_freq)
        z = v
        for o, gate in enumerate((x1, x2)):
            z = gate * (twosided_fftconv(z, filt[:, o, 0], filt[:, o, 1]) + z * f_bias[o])
        return z @ w_out + b_out

    return run(hx), run(hc)


def chunk_ssd(xs, bm, cm, dt, a, s0, chunk):
    bsz, length = xs.shape[:2]
    n = length // chunk

    def split_chunks(t):
        return jnp.moveaxis(t.astype(F32).reshape((bsz, n, chunk) + t.shape[2:]), 1, 0)

    xcs, bcs, ccs, dts = split_chunks(xs), split_chunks(bm), split_chunks(cm), split_chunks(dt)
    cum = jnp.cumsum(dts * a.astype(F32), axis=2)
    lower = jnp.tril(jnp.ones((chunk, chunk), bool))[:, :, None, None]

    def step(state, inp):
        xc, bc, cc, dtc, lc = inp
        seg = jnp.exp(jnp.where(lower, lc[:, :, None] - lc[:, None, :], -jnp.inf))
        w = jnp.einsum('btgn,bsgn->btsg', cc, bc)[..., None] * seg * dtc[:, None]
        y = jnp.einsum('btsgh,bsghp->btghp', w, xc)
        y = y + jnp.einsum('btgn,bghnp->btghp', cc, state) * jnp.exp(lc)[..., None]
        l_end = lc[:, -1]
        state = jnp.exp(l_end)[..., None, None] * state + jnp.einsum(
            'bsgn,bsgh,bsghp->bghnp', bc, jnp.exp(l_end[:, None] - lc) * dtc, xc)
        return state, y

    state, y = lax.scan(step, s0.astype(F32), (xcs, bcs, ccs, dts, cum))
    return jnp.moveaxis(y, 0, 1).reshape(xs.shape), state


def mamba2_mixer(hx, hc, w_in, conv_w, conv_b, dt_bias, a_log, d_skip, norm_g, w_out):
    a = -jnp.exp(a_log.astype(F32)).reshape(2, MB_GROUPS, MB_HPG)

    def project(h):
        bsz, length, _ = h.shape
        z, xbc, dt_raw = jnp.split(h @ w_in, [MB_INNER, MB_INNER + MB_CONV_DIM], axis=-1)
        xbc = jax.nn.silu(dwconv_centered(xbc, conv_w, conv_b))
        xs, bm, cm = jnp.split(xbc, [MB_INNER, MB_INNER + MB_GROUPS * MB_STATE], axis=-1)
        xs = xs.reshape(bsz, length, MB_GROUPS, MB_HPG, MB_HEADDIM)
        bm = bm.reshape(bsz, length, MB_GROUPS, MB_STATE)
        cm = cm.reshape(bsz, length, MB_GROUPS, MB_STATE)
        dt = jax.nn.softplus(dt_raw.astype(F32).reshape(bsz, length, 2, MB_GROUPS, MB_HPG)
                             + dt_bias.reshape(2, MB_GROUPS, MB_HPG))
        return z, xs, bm, cm, dt

    zx, xx, bx, cx, dtx = project(hx)
    zc, xc, bc, cc, dtc = project(hc)
    s0 = jnp.zeros((hx.shape[0], MB_GROUPS, MB_HPG, MB_STATE, MB_HEADDIM), F32)
    yc_f, s_f = chunk_ssd(xc, bc, cc, dtc[:, :, 0], a[0], s0, MB_CHUNK)
    yx_f, _ = chunk_ssd(xx, bx, cx, dtx[:, :, 0], a[0], s_f, MB_CHUNK)
    r = lambda t: jnp.flip(t, axis=1)
    yc_b, s_b = chunk_ssd(r(xc), r(bc), r(cc), r(dtc[:, :, 1]), a[1], s0, MB_CHUNK)
    yx_b, _ = chunk_ssd(r(xx), r(bx), r(cx), r(dtx[:, :, 1]), a[1], s_b, MB_CHUNK)

    def finish(y, xs, z):
        bsz, length = y.shape[:2]
        y = y + d_skip.reshape(MB_GROUPS, MB_HPG)[..., None] * xs
        y = y.reshape(bsz, length, MB_INNER) * jax.nn.silu(z)
        y = rmsnorm(y.reshape(bsz, length, MB_GROUPS, MB_INNER // MB_GROUPS),
                    norm_g.reshape(MB_GROUPS, MB_INNER // MB_GROUPS))
        return y.reshape(bsz, length, MB_INNER) @ w_out

    return finish(yx_f + r(yx_b), xx, zx), finish(yc_f + r(yc_b), xc, zc)


def gla_mixer(hx, hc, w_in, gk_w2, gk_b, onorm_g, w_out):
    rows = hx.shape[1] // GRID_W
    kd, vd, rk = GLA_KEY_DIM, GLA_VAL_DIM, GLA_GATE_RANK
    cuts = [kd, 2 * kd, 2 * kd + vd, 2 * kd + 2 * vd, 2 * kd + 2 * vd + rk]

    def project(h):
        q, k, v, og, rf, rb = jnp.split(h @ w_in, cuts, axis=-1)
        gf = jax.nn.log_sigmoid((rf @ gk_w2[0] + gk_b[0]).astype(F32)) / GLA_GATE_NORM
        gb = jax.nn.log_sigmoid((rb @ gk_w2[1] + gk_b[1]).astype(F32)) / GLA_GATE_NORM
        hd = lambda t: heads(t, GLA_HEADS)
        kh = hd(k)
        return (hd(q * GLA_HK ** -0.5), kh, kh, hd(v), hd(gf), hd(gb)), og

    lat, ogx = project(raster_to_colmajor(hx, rows))
    ctxs, ogc = project(hc)
    ox, oc = bidir_gla(lat, ctxs, GLA_CHUNK)
    yx = colmajor_to_raster(gated_head_out(ox, ogx, onorm_g, w_out), rows)
    return yx, gated_head_out(oc, ogc, onorm_g, w_out)


def expert_choice_ffn(h, router_w, w_gate, w_up, w_down):
    bsz, t_len, d = h.shape
    cap = max(1, EC_CAPACITY_FACTOR * t_len // N_EXPERTS)
    aff = jax.nn.softmax(h.astype(F32) @ router_w.astype(F32), axis=-1)
    gate, idx = lax.top_k(jnp.swapaxes(aff, 1, 2), cap)
    xg = jax.vmap(lambda hb, ib: hb[ib])(h, idx)
    hid = jax.nn.silu(jnp.einsum('becd,edf->becf', xg, w_gate)) * jnp.einsum('becd,edf->becf', xg, w_up)
    ye = jnp.einsum('becf,efd->becd', hid, w_down) * gate[..., None]
    return jax.vmap(lambda ib, yb: jnp.zeros((t_len, d), yb.dtype).at[ib.reshape(-1)].add(yb.reshape(-1, d)))(idx, ye)


BF16 = jnp.bfloat16
LANES = 128
SUBLANES = 8
VMEM_LIMIT_BYTES = 56 * 2 ** 20
SCAN_CHUNK = 128
SCAN_FAST_RANGE = 80.0
SCAN_UNROLL = 8


def _cumsum_rows(tri_bf16, g):
    hi = g.astype(BF16)
    r1 = g - hi.astype(F32)
    mid = r1.astype(BF16)
    lo = (r1 - mid.astype(F32)).astype(BF16)
    d = lambda a: jnp.dot(tri_bf16, a, preferred_element_type=F32)
    return d(hi) + d(mid) + d(lo)


def _log_sigmoid(z):
    return jnp.minimum(z, 0.0) - jnp.log1p(jnp.exp(-jnp.abs(z)))


def _dot_nt(a, b):
    return lax.dot_general(a, b, (((1,), (1,)), ((), ())), preferred_element_type=F32)


def _dot_tn(a, b):
    return lax.dot_general(a, b, (((0,), (0,)), ((), ())), preferred_element_type=F32)


def _gla_scan_body(mode, seq, ctx_len, kdim, vdim, *refs):
    chunk = SCAN_CHUNK
    if mode == 'hgrn2':
        (q_x, ff_x, fb_x, v_x, og_x, q_c, ff_c, fb_c, v_c, og_c, lb_ref, gain_ref,
         o_x, o_c, cf_s, cb_s, sf_s, sb_s, kt_s, gt_s, qf_s, qb_s, of_s, ob_s, kf_s, kb_s) = refs
    else:
        (q_x, k_x, v_x, og_x, r_x, q_c, k_c, v_c, og_c, r_c, w2f_ref, w2b_ref, bf_ref, bb_ref, gain_ref,
         o_x, o_c, cf_s, cb_s, sf_s, sb_s, kt_s, gt_s, qf_s, qb_s, of_s, ob_s) = refs
    if mode == 'hgrn2':
        segs = (((ff_c, fb_c), q_c, None, v_c, og_c, o_c, ctx_len, 0),
                ((ff_x, fb_x), q_x, None, v_x, og_x, o_x, seq, ctx_len))
    else:
        segs = (((r_c,), q_c, k_c, v_c, og_c, o_c, ctx_len, 0),
                ((r_x,), q_x, k_x, v_x, og_x, o_x, seq, ctx_len))

    row = lax.broadcasted_iota(jnp.int32, (chunk, chunk), 0)
    col = lax.broadcasted_iota(jnp.int32, (chunk, chunk), 1)
    masks = (row >= col, row <= col)
    tris = tuple(m.astype(BF16) for m in masks)
    cum_s, state_s, qin_s, kout_s = (cf_s, cb_s), (sf_s, sb_s), (qf_s, qb_s), (of_s, ob_s)
    end_row = (chunk - 1, 0)
    mid_row = chunk // 2

    def rows_of(i, off=0):
        return pl.ds(pl.multiple_of(off + i * chunk, chunk), chunk)

    def gate_pass(seg, bound):
        gsrc, _, _, _, _, _, length, off = seg

        def body(i, bound):
            rows, srows = rows_of(i), rows_of(i, off)
            if mode == 'hgrn2':
                lb = lb_ref[...]
                f_f = lb + (1.0 - lb) * jax.nn.sigmoid(gsrc[0][rows, :])
                f_b = lb + (1.0 - lb) * jax.nn.sigmoid(gsrc[1][rows, :])
                g_f, g_b = jnp.log(f_f), jnp.log(f_b)
                kf_s[srows, :] = 1.0 - f_f
                kb_s[srows, :] = 1.0 - f_b
            else:
                r = gsrc[0][rows, :].astype(BF16)
                z_f = jnp.dot(r, w2f_ref[...].astype(BF16), preferred_element_type=F32) + bf_ref[...]
                z_b = jnp.dot(r, w2b_ref[...].astype(BF16), preferred_element_type=F32) + bb_ref[...]
                g_f = _log_sigmoid(z_f) / GLA_GATE_NORM
                g_b = _log_sigmoid(z_b) / GLA_GATE_NORM
            c_f = _cumsum_rows(tris[0], g_f)
            c_b = _cumsum_rows(tris[1], g_b)
            cf_s[srows, :] = c_f
            cb_s[srows, :] = c_b
            for c in (c_f, c_b):
                mid = c[mid_row:mid_row + 1, :]
                spread = jnp.maximum(jnp.abs(c[0:1, :] - mid), jnp.abs(c[chunk - 1:chunk, :] - mid))
                bound = jnp.maximum(bound, spread)
            return bound

        return lax.fori_loop(0, length // chunk, body, bound, unroll=SCAN_UNROLL)

    def intra(seg, direction, i, exact):
        _, q_r, k_r, v_r, _, _, _, off = seg
        rows, srows = rows_of(i), rows_of(i, off)
        if mode == 'hgrn2':
            qq = jax.nn.silu(q_r[rows, :]) * HG_KEY ** -0.5
            k = (kf_s if direction == 0 else kb_s)[srows, :]
        else:
            qq = q_r[rows, :] * GLA_HK ** -0.5
            k = k_r[rows, :]
        cum = cum_s[direction][srows, :]
        e = end_row[direction]
        cum_end = cum[e:e + 1, :]
        q_in = (qq * jnp.exp(cum)).astype(BF16)
        qin_s[direction][srows, :] = q_in
        kout_s[direction][srows, :] = (k * jnp.exp(cum_end - cum)).astype(BF16)
        if exact:
            kt_s[...] = k
            gt_s[...] = cum

            def col_body(s, att):
                ks = kt_s[pl.ds(s, 1), :]
                gs = gt_s[pl.ds(s, 1), :]
                p = qq * ks * jnp.exp(jnp.minimum(cum - gs, 0.0))
                return att + jnp.where(col == s, jnp.sum(p, axis=-1, keepdims=True), 0.0)

            att = lax.fori_loop(0, chunk, col_body, jnp.zeros((chunk, chunk), F32))
        else:
            rel = cum - cum[mid_row:mid_row + 1, :]
            att = _dot_nt((qq * jnp.exp(rel)).astype(BF16), (k * jnp.exp(-rel)).astype(BF16))
        att = jnp.where(masks[direction], att, 0.0).astype(BF16)
        return jnp.dot(att, v_r[rows, :].astype(BF16), preferred_element_type=F32)

    def intra_pass(exact):
        for seg in segs:
            o_r, length = seg[5], seg[6]

            def body(i, carry):
                o_r[rows_of(i), :] = intra(seg, 0, i, exact) + intra(seg, 1, i, exact)
                return carry

            lax.fori_loop(0, length // chunk, body, 0, unroll=1 if exact else SCAN_UNROLL)

    def inter(seg, direction, i):
        v_r, off = seg[3], seg[7]
        rows, srows = rows_of(i), rows_of(i, off)
        s_ref = state_s[direction]
        state = s_ref[...]
        o = _dot_nt(qin_s[direction][srows, :], state.astype(BF16))
        e = pl.ds(pl.multiple_of(off + i * chunk, chunk) + end_row[direction], 1)
        decay = jnp.exp(cum_s[direction][e, :])
        s_ref[...] = state * decay + _dot_tn(v_r[rows, :].astype(BF16), kout_s[direction][srows, :])
        return o

    def finish(seg, i, o):
        o = o * lax.rsqrt(jnp.mean(o * o, axis=-1, keepdims=True) + NORM_EPS) * gain_ref[...]
        return o * jax.nn.silu(seg[4][rows_of(i), :])

    def inter_pass():
        sf_s[...] = jnp.zeros_like(sf_s)
        sb_s[...] = jnp.zeros_like(sb_s)
        for seg in segs:
            o_r, length = seg[5], seg[6]
            n = length // chunk

            def first_half(i, carry):
                j = n - 1 - i
                o_r[rows_of(i), :] += inter(seg, 0, i)
                o_r[rows_of(j), :] += inter(seg, 1, j)
                return carry

            def second_half(i, carry):
                j = n - 1 - i
                o_r[rows_of(i), :] = finish(seg, i, o_r[rows_of(i), :] + inter(seg, 0, i))
                o_r[rows_of(j), :] = finish(seg, j, o_r[rows_of(j), :] + inter(seg, 1, j))
                return carry

            lax.fori_loop(0, n // 2, first_half, 0, unroll=min(SCAN_UNROLL, n // 2))
            lax.fori_loop(n // 2, n, second_half, 0, unroll=min(SCAN_UNROLL, n // 2))

    bound = jnp.zeros((1, kdim), F32)
    for seg in segs:
        bound = gate_pass(seg, bound)
    fast = jnp.max(bound) <= SCAN_FAST_RANGE

    @pl.when(fast)
    def _():
        intra_pass(False)

    @pl.when(jnp.logical_not(fast))
    def _():
        intra_pass(True)

    inter_pass()


def _gla_scan_call(mode, nheads, kdim, vdim, lat_in, ctx_in, small_in, bsz, seq, ctx_len):
    def stream_spec(length, width, first, stride):
        return pl.BlockSpec((None, length, width), lambda b, h: (b, 0, first + stride * h))

    assert seq % (2 * SCAN_CHUNK) == 0 and ctx_len % (2 * SCAN_CHUNK) == 0
    in_specs = [stream_spec(seq, w, f, s) for _, w, f, s in lat_in]
    in_specs += [stream_spec(ctx_len, w, f, s) for _, w, f, s in ctx_in]
    in_specs += [pl.BlockSpec(blk, imap) for _, blk, imap in small_in]
    args = [a for a, _, _, _ in lat_in] + [a for a, _, _, _ in ctx_in] + [a for a, _, _ in small_in]
    tot = seq + ctx_len
    scratch = [pltpu.VMEM((tot, kdim), F32), pltpu.VMEM((tot, kdim), F32),
               pltpu.VMEM((vdim, kdim), F32), pltpu.VMEM((vdim, kdim), F32),
               pltpu.VMEM((SCAN_CHUNK, kdim), F32), pltpu.VMEM((SCAN_CHUNK, kdim), F32),
               pltpu.VMEM((tot, kdim), BF16), pltpu.VMEM((tot, kdim), BF16),
               pltpu.VMEM((tot, kdim), BF16), pltpu.VMEM((tot, kdim), BF16)]
    if mode == 'hgrn2':
        scratch += [pltpu.VMEM((tot, kdim), F32), pltpu.VMEM((tot, kdim), F32)]
    return pl.pallas_call(
        lambda *refs: _gla_scan_body(mode, seq, ctx_len, kdim, vdim, *refs),
        grid=(bsz, nheads),
        in_specs=in_specs,
        out_specs=[pl.BlockSpec((None, seq, vdim), lambda b, h: (b, 0, h)),
                   pl.BlockSpec((None, ctx_len, vdim), lambda b, h: (b, 0, h))],
        out_shape=[jax.ShapeDtypeStruct((bsz, seq, nheads * vdim), F32),
                   jax.ShapeDtypeStruct((bsz, ctx_len, nheads * vdim), F32)],
        scratch_shapes=scratch,
        compiler_params=pltpu.CompilerParams(dimension_semantics=("arbitrary", "arbitrary"),
                                             vmem_limit_bytes=VMEM_LIMIT_BYTES),
        name=f"{mode}_scan",
    )(*args)


def hgrn2_mixer_pallas(hx, hc, w_in, lb, onorm_g, w_out):
    bsz, seq, _ = hx.shape
    ctx_len = hc.shape[1]
    yx, yc = hx @ w_in, hc @ w_in
    nh = HG_HEADS
    streams = lambda y: [(y, HG_KEY, j * nh, 1) for j in range(5)]
    small = [(lb.reshape(1, D_MODEL), (1, HG_KEY), lambda b, h: (0, h)),
             (onorm_g.reshape(1, HG_VAL), (1, HG_VAL), lambda b, h: (0, 0))]
    ox, oc = _gla_scan_call('hgrn2', nh, HG_KEY, HG_VAL, streams(yx), streams(yc), small, bsz, seq, ctx_len)
    return ox @ w_out, oc @ w_out


def gla_mixer_pallas(hx, hc, w_in, gk_w2, gk_b, onorm_g, w_out):
    bsz, seq, _ = hx.shape
    ctx_len = hc.shape[1]
    rows = seq // GRID_W
    kd, vd, rk = GLA_KEY_DIM, GLA_VAL_DIM, GLA_GATE_RANK
    nh = GLA_HEADS
    w_in_p = jnp.pad(w_in, ((0, 0), (0, LANES - 2 * rk)))
    yx = raster_to_colmajor(hx, rows) @ w_in_p
    yc = hc @ w_in_p
    w2f = jnp.pad(gk_w2[0], ((0, LANES - rk), (0, 0)))
    w2b = jnp.pad(gk_w2[1], ((rk, LANES - 2 * rk), (0, 0)))
    streams = lambda y: [(y, GLA_HK, 0, 1), (y, GLA_HK, kd // GLA_HK, 1), (y, GLA_HV, 2 * kd // GLA_HV, 1),
                         (y, GLA_HV, (2 * kd + vd) // GLA_HV, 1), (y, LANES, (2 * kd + 2 * vd) // LANES, 0)]
    small = [(w2f, (LANES, GLA_HK), lambda b, h: (0, h)), (w2b, (LANES, GLA_HK), lambda b, h: (0, h)),
             (gk_b[0].reshape(1, kd), (1, GLA_HK), lambda b, h: (0, h)),
             (gk_b[1].reshape(1, kd), (1, GLA_HK), lambda b, h: (0, h)),
             (onorm_g.reshape(1, GLA_HV), (1, GLA_HV), lambda b, h: (0, 0))]
    ox, oc = _gla_scan_call('gla', nh, GLA_HK, GLA_HV, streams(yx), streams(yc), small, bsz, seq, ctx_len)
    return colmajor_to_raster(ox @ w_out, rows), oc @ w_out


MB_GROUP_W = MB_INNER // MB_GROUPS
SSD_CHUNK = 128
SSD_CONV_ROWS = 64


def _ssd_scan_body(seq, ctx_len, *refs):
    chunk = SSD_CHUNK
    gw, hpg = MB_GROUP_W, MB_HPG
    (x_x, b_x, c_x, z_x, cl_x, rt_x, x_c, b_c, c_c, z_c, cl_c, rt_c,
     wx_ref, wb_ref, wc_ref, bx_ref, bb_ref, bc_ref, dskip_ref, gain_ref,
     o_x, o_c, xf_s, xb_s, sf_s, sb_s, pad_s, xc_s, bc_s, cc_s) = refs
    segs = ((x_c, b_c, c_c, z_c, cl_c, rt_c, o_c, ctx_len, 0),
            (x_x, b_x, c_x, z_x, cl_x, rt_x, o_x, seq, ctx_len))
    xs_s, state_s = (xf_s, xb_s), (sf_s, sb_s)
    end_row = (chunk - 1, 0)

    def conv_silu(raw_ref, w_ref, bias_ref, dst_ref, length, off):
        width = raw_ref.shape[-1]
        halo = SUBLANES
        pad = MB_CONV // 2
        pad_s[0:halo, 0:width] = jnp.zeros((halo, width), F32)
        pad_s[pl.ds(halo + length, halo), 0:width] = jnp.zeros((halo, width), F32)

        crow = SSD_CONV_ROWS

        def stage(i, carry):
            src = pl.ds(pl.multiple_of(i * crow, crow), crow)
            pad_s[pl.ds(pl.multiple_of(halo + i * crow, SUBLANES), crow), 0:width] = raw_ref[src, :]
            return carry

        lax.fori_loop(0, length // crow, stage, 0)

        def body(i, carry):
            rows_in = crow + 2 * halo
            win = pad_s[pl.ds(pl.multiple_of(i * crow, crow), rows_in), 0:width]
            acc = jnp.broadcast_to(bias_ref[...], (crow, width))
            for j in range(MB_CONV):
                lo = halo - pad + j
                acc = acc + pltpu.roll(win, rows_in - lo, axis=0)[0:crow, :] * w_ref[j:j + 1, :]
            dst = pl.ds(pl.multiple_of(off + i * crow, crow), crow)
            dst_ref[dst, :] = jax.nn.silu(acc).astype(dst_ref.dtype)
            return carry

        lax.fori_loop(0, length // crow, body, 0)

    row = lax.broadcasted_iota(jnp.int32, (chunk, chunk), 0)
    col = lax.broadcasted_iota(jnp.int32, (chunk, chunk), 1)
    masks = (row >= col, row <= col)
    head_of_lane = lax.broadcasted_iota(jnp.int32, (1, gw), 1) // MB_HEADDIM

    def rows_of(i, off=0):
        return pl.ds(pl.multiple_of(off + i * chunk, chunk), chunk)

    def bcast_heads(c4):
        out = c4[:, hpg - 1:hpg]
        for hh in range(hpg - 2, -1, -1):
            out = jnp.where(head_of_lane == hh, c4[:, hh:hh + 1], out)
        return out

    def intra_pass():
        for seg in segs:
            _, _, _, _, cl_r, rt_r, o_r, length, off = seg

            def body(i, carry):
                rows, srows = rows_of(i), rows_of(i, off)
                cols, rt = cl_r[i], rt_r[i]
                x = xc_s[srows, :]
                cb = _dot_nt(cc_s[srows, :], bc_s[srows, :])
                xh = [jnp.where(head_of_lane == hh, x, 0.0).astype(BF16) for hh in range(hpg)]
                y = jnp.zeros((chunk, gw), F32)
                for d in range(2):
                    cum = cols[:, d * hpg:(d + 1) * hpg]
                    dt = cols[:, (2 + d) * hpg:(3 + d) * hpg]
                    e = end_row[d]
                    scale = jnp.exp(cum[e:e + 1, :] - cum) * dt
                    xs_s[d][srows, :] = (x * bcast_heads(scale)).astype(BF16)
                    for hh in range(hpg):
                        j = d * hpg + hh
                        seg_decay = jnp.exp(jnp.minimum(cols[:, j:j + 1] - rt[j:j + 1, :], 0.0))
                        w = jnp.where(masks[d], cb * seg_decay * rt[2 * hpg + j:2 * hpg + j + 1, :], 0.0)
                        y = y + jnp.dot(w.astype(BF16), xh[hh], preferred_element_type=F32)
                o_r[rows, :] = y
                return carry

            lax.fori_loop(0, length // chunk, body, 0, unroll=2)

    def inter(seg, d, i):
        cl_r, off = seg[4], seg[8]
        srows = rows_of(i, off)
        cum = cl_r[i][:, d * hpg:(d + 1) * hpg]
        state = state_s[d][...]
        y = jnp.dot(cc_s[srows, :], state.astype(BF16), preferred_element_type=F32)
        e = end_row[d]
        decay = bcast_heads(jnp.exp(cum[e:e + 1, :]))
        state_s[d][...] = state * decay + _dot_tn(bc_s[srows, :], xs_s[d][srows, :])
        return y * bcast_heads(jnp.exp(cum))

    def finish(seg, i, y):
        z_r, off = seg[3], seg[8]
        y = (y + dskip_ref[...] * xc_s[rows_of(i, off), :]) * jax.nn.silu(z_r[rows_of(i), :])
        return y * lax.rsqrt(jnp.mean(y * y, axis=-1, keepdims=True) + NORM_EPS) * gain_ref[...]

    def inter_pass():
        sf_s[...] = jnp.zeros_like(sf_s)
        sb_s[...] = jnp.zeros_like(sb_s)
        for seg in segs:
            o_r, length = seg[6], seg[7]
            n = length // chunk

            def first_half(i, carry):
                j = n - 1 - i
                o_r[rows_of(i), :] += inter(seg, 0, i)
                o_r[rows_of(j), :] += inter(seg, 1, j)
                return carry

            def second_half(i, carry):
                j = n - 1 - i
                o_r[rows_of(i), :] = finish(seg, i, o_r[rows_of(i), :] + inter(seg, 0, i))
                o_r[rows_of(j), :] = finish(seg, j, o_r[rows_of(j), :] + inter(seg, 1, j))
                return carry

            lax.fori_loop(0, n // 2, first_half, 0, unroll=2)
            lax.fori_loop(n // 2, n, second_half, 0, unroll=2)

    for seg in segs:
        conv_silu(seg[0], wx_ref, bx_ref, xc_s, seg[7], seg[8])
        conv_silu(seg[1], wb_ref, bb_ref, bc_s, seg[7], seg[8])
        conv_silu(seg[2], wc_ref, bc_ref, cc_s, seg[7], seg[8])
    intra_pass()
    inter_pass()


def _ssd_head_tables(dt_raw, dt_bias, a_log):
    bsz, length, _ = dt_raw.shape
    n = length // SSD_CHUNK
    a = -jnp.exp(a_log.astype(F32)).reshape(2, MB_GROUPS, MB_HPG)
    dt = jax.nn.softplus(dt_raw.astype(F32).reshape(bsz, length, 2, MB_GROUPS, MB_HPG)
                         + dt_bias.reshape(2, MB_GROUPS, MB_HPG))
    dta = (dt * a).reshape(bsz, n, SSD_CHUNK, 2, MB_GROUPS, MB_HPG)
    dtc = dt.reshape(bsz, n, SSD_CHUNK, 2, MB_GROUPS, MB_HPG)
    lower = jnp.tril(jnp.ones((SSD_CHUNK, SSD_CHUNK), F32))
    cum_f = jnp.einsum('ts,bnsgh->bntgh', lower, dta[:, :, :, 0], precision=lax.Precision.HIGHEST)
    cum_b = jnp.einsum('st,bnsgh->bntgh', lower, dta[:, :, :, 1], precision=lax.Precision.HIGHEST)
    cols = jnp.concatenate([cum_f, cum_b, dtc[:, :, :, 0], dtc[:, :, :, 1]], axis=-1)
    cols = cols.transpose(0, 3, 1, 2, 4)
    return cols, jnp.swapaxes(cols, -1, -2)


def mamba2_mixer_pallas(hx, hc, w_in, conv_w, conv_b, dt_bias, a_log, d_skip, norm_g, w_out):
    bsz, seq, _ = hx.shape
    ctx_len = hc.shape[1]
    chunk, gw = SSD_CHUNK, MB_GROUP_W
    assert seq % (2 * chunk) == 0 and ctx_len % (2 * chunk) == 0

    def project(h):
        z = h @ w_in[:, :MB_INNER]
        xbc = h @ w_in[:, MB_INNER:MB_INNER + MB_CONV_DIM]
        cols, rows_t = _ssd_head_tables(h @ w_in[:, MB_INNER + MB_CONV_DIM:], dt_bias, a_log)
        return z, xbc, cols, rows_t

    yx, xbcx, clx, rtx = project(hx)
    yc, xbcc, clc, rtc = project(hc)
    conv_b2 = conv_b.reshape(1, MB_CONV_DIM)

    def conv_specs(rows):
        bw = MB_INNER // MB_STATE
        return [pl.BlockSpec((rows, gw), lambda b, g: (0, g)),
                pl.BlockSpec((rows, MB_STATE), lambda b, g: (0, bw + g)),
                pl.BlockSpec((rows, MB_STATE), lambda b, g: (0, bw + MB_GROUPS + g))]

    def stream_specs(length):
        n = length // chunk
        bw = MB_INNER // MB_STATE
        return [pl.BlockSpec((None, length, gw), lambda b, g: (b, 0, g)),
                pl.BlockSpec((None, length, MB_STATE), lambda b, g: (b, 0, bw + g)),
                pl.BlockSpec((None, length, MB_STATE), lambda b, g: (b, 0, bw + MB_GROUPS + g)),
                pl.BlockSpec((None, length, gw), lambda b, g: (b, 0, g)),
                pl.BlockSpec((None, None, n, chunk, 4 * MB_HPG), lambda b, g: (b, g, 0, 0, 0)),
                pl.BlockSpec((None, None, n, 4 * MB_HPG, chunk), lambda b, g: (b, g, 0, 0, 0))]

    tot = seq + ctx_len
    ox, oc = pl.pallas_call(
        lambda *refs: _ssd_scan_body(seq, ctx_len, *refs),
        grid=(bsz, MB_GROUPS),
        in_specs=stream_specs(seq) + stream_specs(ctx_len) + conv_specs(MB_CONV) + conv_specs(1) + [
            pl.BlockSpec((1, gw), lambda b, g: (0, g)), pl.BlockSpec((1, gw), lambda b, g: (0, g))],
        out_specs=[pl.BlockSpec((None, seq, gw), lambda b, g: (b, 0, g)),
                   pl.BlockSpec((None, ctx_len, gw), lambda b, g: (b, 0, g))],
        out_shape=[jax.ShapeDtypeStruct((bsz, seq, MB_INNER), F32),
                   jax.ShapeDtypeStruct((bsz, ctx_len, MB_INNER), F32)],
        scratch_shapes=[pltpu.VMEM((tot, gw), BF16), pltpu.VMEM((tot, gw), BF16),
                        pltpu.VMEM((MB_STATE, gw), F32), pltpu.VMEM((MB_STATE, gw), F32),
                        pltpu.VMEM((seq + 2 * SUBLANES, gw), F32), pltpu.VMEM((tot, gw), F32),
                        pltpu.VMEM((tot, MB_STATE), BF16), pltpu.VMEM((tot, MB_STATE), BF16)],
        compiler_params=pltpu.CompilerParams(dimension_semantics=("arbitrary", "arbitrary"),
                                             vmem_limit_bytes=VMEM_LIMIT_BYTES),
        name="ssd_scan",
    )(xbcx, xbcx, xbcx, yx, clx, rtx, xbcc, xbcc, xbcc, yc, clc, rtc,
      conv_w, conv_w, conv_w, conv_b2, conv_b2, conv_b2,
      jnp.repeat(d_skip, MB_HEADDIM).reshape(1, MB_INNER), norm_g.reshape(1, MB_INNER))
    return ox @ w_out, oc @ w_out


HY_BLOCK = 256
HY_FREQ_ROWS = 64


def _split_bf16(a):
    hi = a.astype(BF16)
    return hi, (a - hi.astype(F32)).astype(BF16)


def _dot_split(w_hi, w_lo, x):
    x_hi, x_lo = _split_bf16(x)
    d = lambda a, b: jnp.dot(a, b, preferred_element_type=F32)
    return d(w_hi, x_hi) + d(w_hi, x_lo) + d(w_lo, x_hi)


def _hyena_dft_matrices():
    p = HY_BLOCK
    f = jnp.arange(p, dtype=F32)[:, None] + 0.5
    b = jnp.arange(p, dtype=F32)[None, :]
    k = jnp.round(f * b * 2.0).astype(jnp.int32) % (4 * p)
    ang = k.astype(F32) * (2.0 * math.pi / (4 * p))
    fwd = jnp.concatenate([jnp.cos(ang), -jnp.sin(ang)], axis=0)
    inv = jnp.concatenate([jnp.cos(ang).T, -jnp.sin(ang).T], axis=1) / p
    return fwd, inv


def _hyena_filter_spectra(hf, hb, fwd):
    length, d = hf.shape
    p = HY_BLOCK
    nb = length // p
    h2 = jnp.concatenate([jnp.zeros((1, d), F32), jnp.flip(hb[1:], axis=0), hf], axis=0)
    blocks = h2.reshape(2 * nb, p, d)
    spec = jnp.einsum('fj,kjd->kfd', fwd, blocks, precision=lax.Precision.HIGHEST)
    s_re, s_im = spec[:, :p], spec[:, p:]
    sign = jnp.where(jnp.arange(p) % 2 == 0, 1.0, -1.0).astype(F32)[None, :, None]
    return s_re[1:] - sign * s_im[:-1], s_im[1:] + sign * s_re[:-1]


def _hyena_conv_body(nb, u_ref, x_ref, bias_ref, gre_ref, gim_ref, wfh_ref, wfl_ref, wih_ref, wil_ref,
                     o_ref, ure_s, uim_s, y_s):
    p, fr = HY_BLOCK, HY_FREQ_ROWS

    def rows_of(i):
        return pl.ds(pl.multiple_of(i * p, p), p)

    def forward(j, carry):
        spec = _dot_split(wfh_ref[...], wfl_ref[...], u_ref[rows_of(j), :])
        ure_s[j] = spec[:p]
        uim_s[j] = spec[p:]
        return carry

    lax.fori_loop(0, nb, forward, 0, unroll=min(nb, 2))

    def out_block(i, carry):
        for t in range(p // fr):
            r = pl.ds(t * fr, fr)

            def acc_body(j, acc):
                a_re, a_im = acc
                k = i - j + nb - 1
                g_re, g_im = gre_ref[k, r, :], gim_ref[k, r, :]
                u_re, u_im = ure_s[j, r, :], uim_s[j, r, :]
                return a_re + g_re * u_re - g_im * u_im, a_im + g_re * u_im + g_im * u_re

            zero = jnp.zeros((fr, u_ref.shape[-1]), F32)
            a_re, a_im = lax.fori_loop(0, nb, acc_body, (zero, zero), unroll=min(nb, 4))
            y_s[pl.ds(t * fr, fr), :] = a_re
            y_s[pl.ds(p + t * fr, fr), :] = a_im
        y = _dot_split(wih_ref[...], wil_ref[...], y_s[...])
        u = u_ref[rows_of(i), :]
        o_ref[rows_of(i), :] = x_ref[rows_of(i), :] * (y + u * bias_ref[...])
        return carry

    lax.fori_loop(0, nb, out_block, 0)


def _hyena_conv(u_arr, u_blk, x_arr, x_blk, bias, g_re, g_im, mats):
    bsz, length, _ = u_arr.shape
    d = bias.shape[-1]
    p = HY_BLOCK
    nb = length // p
    nseg = 2 * nb - 1
    const = lambda shape: pl.BlockSpec(shape, lambda c, b: (0, 0))
    return pl.pallas_call(
        lambda *refs: _hyena_conv_body(nb, *refs),
        grid=(d // LANES, bsz),
        in_specs=[pl.BlockSpec((None, length, LANES), lambda c, b: (b, 0, u_blk + c)),
                  pl.BlockSpec((None, length, LANES), lambda c, b: (b, 0, x_blk + c)),
                  pl.BlockSpec((1, LANES), lambda c, b: (0, c)),
                  pl.BlockSpec((nseg, p, LANES), lambda c, b: (0, 0, c)),
                  pl.BlockSpec((nseg, p, LANES), lambda c, b: (0, 0, c)),
                  const((2 * p, p)), const((2 * p, p)), const((p, 2 * p)), const((p, 2 * p))],
        out_specs=pl.BlockSpec((None, length, LANES), lambda c, b: (b, 0, c)),
        out_shape=jax.ShapeDtypeStruct((bsz, length, d), F32),
        scratch_shapes=[pltpu.VMEM((nb, p, LANES), F32), pltpu.VMEM((nb, p, LANES), F32),
                        pltpu.VMEM((2 * p, LANES), F32)],
        compiler_params=pltpu.CompilerParams(dimension_semantics=("arbitrary", "arbitrary"),
                                             vmem_limit_bytes=VMEM_LIMIT_BYTES),
        name="hyena_conv",
    )(u_arr, x_arr, bias.reshape(1, d), g_re, g_im, *mats)


def hyena_mixer_pallas(hx, hc, w_in, b_in, short_w, short_b, f_w1, f_b1, f_w2, f_b2, f_w3, f_b3, f_w4,
                       f_freq, f_bias, w_out, b_out):
    fwd, inv = _hyena_dft_matrices()
    mats = _split_bf16(fwd) + _split_bf16(inv)
    nblk = D_MODEL // LANES

    def taps(length):
        z, t = hyena_pos_features(length)
        a = jnp.sin(f_freq * (z @ f_w1 + f_b1))
        a = jnp.sin(f_freq * (a @ f_w2 + f_b2))
        a = jnp.sin(f_freq * (a @ f_w3 + f_b3))
        mw = (a @ f_w4).astype(F32) * jnp.tile(hyena_window(t), (1, 2 * HY_ORDER))
        norm = jnp.sum(jnp.abs(mw), axis=0).reshape(HY_ORDER, 2, D_MODEL).sum(axis=1)
        return mw, 1.0 / norm

    def spectra(mw, inv_norm, o):
        hf = mw[:, (2 * o) * D_MODEL:(2 * o + 1) * D_MODEL]
        hb = mw[:, (2 * o + 1) * D_MODEL:(2 * o + 2) * D_MODEL]
        g_re, g_im = _hyena_filter_spectra(hf, hb, fwd)
        return g_re * inv_norm[o], g_im * inv_norm[o]

    def run(h):
        length = h.shape[1]
        u = dwconv_centered(h @ w_in + b_in, short_w, short_b).astype(F32)
        mw, inv_norm = taps(length)
        g0, g1 = spectra(mw, inv_norm, 0), spectra(mw, inv_norm, 1)
        z = _hyena_conv(u, 2 * nblk, u, 0, f_bias[0], g0[0], g0[1], mats)
        z = _hyena_conv(z, 0, u, nblk, f_bias[1], g1[0], g1[1], mats)
        return z @ w_out + b_out

    return run(hx), run(hc)


MOE_ROW_TILE = 2048
MOE_COL_TILE = 512


def _moe_combine_body(t_len, idx_ref, gate_ref, ye_ref, x_ref, g2_ref, o_ref):
    e = pl.program_id(2)
    cap = idx_ref.shape[-1]
    rt = min(MOE_ROW_TILE, t_len)

    @pl.when(e == 0)
    def _():
        o_ref[...] = jnp.zeros_like(o_ref)

    yg = (ye_ref[...] * gate_ref[...]).astype(BF16)
    idx = idx_ref[...]

    def body(r, carry):
        r0 = pl.multiple_of(r * rt, rt)
        tok = lax.broadcasted_iota(jnp.int32, (rt, cap), 0) + r0
        onehot = jnp.where(tok == idx, 1.0, 0.0).astype(BF16)
        o_ref[pl.ds(r0, rt), :] += jnp.dot(onehot, yg, preferred_element_type=F32)
        return carry

    lax.fori_loop(0, t_len // rt, body, 0)

    @pl.when(e == pl.num_programs(2) - 1)
    def _():
        o_ref[...] = x_ref[...] + g2_ref[...] * o_ref[...]


def expert_choice_ffn_residual(x, g2, h, router_w, w_gate, w_up, w_down):
    bsz, t_len, d = h.shape
    cap = max(1, EC_CAPACITY_FACTOR * t_len // N_EXPERTS)
    aff = jax.nn.softmax(h.astype(F32) @ router_w.astype(F32), axis=-1)
    gate, idx = lax.top_k(jnp.swapaxes(aff, 1, 2), cap)
    xg = jax.vmap(lambda hb, ib: hb[ib])(h.astype(BF16), idx)
    pre_g = jnp.einsum('becd,edf->becf', xg, w_gate, preferred_element_type=BF16)
    pre_u = jnp.einsum('becd,edf->becf', xg, w_up, preferred_element_type=BF16)
    hid = (jax.nn.silu(pre_g.astype(F32)) * pre_u.astype(F32)).astype(BF16)
    ye = jnp.einsum('becf,efd->becd', hid, w_down, preferred_element_type=F32)
    ct = MOE_COL_TILE if t_len > MOE_ROW_TILE else d
    g2 = jnp.broadcast_to(g2.reshape(-1, 1, d), (bsz, 1, d))
    return pl.pallas_call(
        lambda *refs: _moe_combine_body(t_len, *refs),
        grid=(bsz, d // ct, N_EXPERTS),
        in_specs=[pl.BlockSpec((None, None, 1, cap), lambda b, c, e: (b, e, 0, 0)),
                  pl.BlockSpec((None, None, cap, 1), lambda b, c, e: (b, e, 0, 0)),
                  pl.BlockSpec((None, None, cap, ct), lambda b, c, e: (b, e, 0, c)),
                  pl.BlockSpec((None, t_len, ct), lambda b, c, e: (b, 0, c)),
                  pl.BlockSpec((None, 1, ct), lambda b, c, e: (b, 0, c))],
        out_specs=pl.BlockSpec((None, t_len, ct), lambda b, c, e: (b, 0, c)),
        out_shape=jax.ShapeDtypeStruct((bsz, t_len, d), F32),
        compiler_params=pltpu.CompilerParams(dimension_semantics=("arbitrary", "arbitrary", "arbitrary"),
                                             vmem_limit_bytes=VMEM_LIMIT_BYTES),
        name="moe_combine",
    )(idx.reshape(bsz, N_EXPERTS, 1, cap), gate.reshape(bsz, N_EXPERTS, cap, 1), ye, x, g2)


def _final_norm_body(x_ref, g_ref, o_ref):
    xf = x_ref[...]
    o_ref[...] = xf * lax.rsqrt(jnp.mean(xf * xf, axis=-1, keepdims=True) + NORM_EPS) * g_ref[...]


def final_rmsnorm(x, g):
    b, l, d = x.shape
    rows = b * l
    tm = 512
    out = pl.pallas_call(
        _final_norm_body,
        grid=(rows // tm,),
        in_specs=[pl.BlockSpec((tm, d), lambda i: (i, 0)), pl.BlockSpec((1, d), lambda i: (0, 0))],
        out_specs=pl.BlockSpec((tm, d), lambda i: (i, 0)),
        out_shape=jax.ShapeDtypeStruct((rows, d), F32),
    )(x.reshape(rows, d), g.reshape(1, d))
    return out.reshape(b, l, d)


def kernel(x, c, ctx, c_ctx, ada_w, ada_b, norm1_g, norm2_g,
           hg_w_in, hg_lb, hg_onorm_g, hg_w_out,
           hy_w_in, hy_b_in, hy_short_w, hy_short_b, hy_f_w1, hy_f_b1, hy_f_w2, hy_f_b2,
           hy_f_w3, hy_f_b3, hy_f_w4, hy_f_freq, hy_f_bias, hy_w_out, hy_b_out,
           mb_w_in, mb_conv_w, mb_conv_b, mb_dt_bias, mb_a_log, mb_d, mb_norm_g, mb_w_out,
           gla_w_in, gla_gk_w2, gla_gk_b, gla_onorm_g, gla_w_out,
           router_w, moe_w_gate, moe_w_up, moe_w_down, final_norm_g):
    for i in range(DEPTH):
        last = i == DEPTH - 1
        sh1, sc1, g1, sh2, sc2, g2 = adaln(c, ada_w[i], ada_b[i])
        csh1, csc1, cg1, csh2, csc2, cg2 = adaln(c_ctx, ada_w[i], ada_b[i])
        hx = modulate(rmsnorm(x, norm1_g[i]), sh1, sc1)
        hc = modulate(rmsnorm(ctx, norm1_g[i]), csh1, csc1)
        kind, j = i % N_MIXERS, i // N_MIXERS
        if kind == 0:
            yx, yc = hgrn2_mixer_pallas(hx, hc, hg_w_in[j], hgrn_lower_bound(hg_lb, i), hg_onorm_g[j], hg_w_out[j])
        elif kind == 1:
            yx, yc = hyena_mixer_pallas(hx, hc, hy_w_in[j], hy_b_in[j], hy_short_w[j], hy_short_b[j],
                                 hy_f_w1[j], hy_f_b1[j], hy_f_w2[j], hy_f_b2[j], hy_f_w3[j], hy_f_b3[j],
                                 hy_f_w4[j], hy_f_freq[j], hy_f_bias[j], hy_w_out[j], hy_b_out[j])
        elif kind == 2:
            yx, yc = mamba2_mixer_pallas(hx, hc, mb_w_in[j], mb_conv_w[j], mb_conv_b[j], mb_dt_bias[j],
                                  mb_a_log[j], mb_d[j], mb_norm_g[j], mb_w_out[j])
        else:
            yx, yc = gla_mixer_pallas(hx, hc, gla_w_in[j], gla_gk_w2[j], gla_gk_b[j], gla_onorm_g[j], gla_w_out[j])
        x = x + g1 * yx
        x = expert_choice_ffn_residual(x, g2, modulate(rmsnorm(x, norm2_g[i]), sh2, sc2),
                                       router_w[i], moe_w_gate[i], moe_w_up[i], moe_w_down[i])
        if not last:
            ctx = ctx + cg1 * yc
            ctx = expert_choice_ffn_residual(ctx, cg2, modulate(rmsnorm(ctx, norm2_g[i]), csh2, csc2),
                                             router_w[i], moe_w_gate[i], moe_w_up[i], moe_w_down[i])
    return final_rmsnorm(x, final_norm_g)
```

```python
import math
import jax, jax.numpy as jnp
from jax import lax
from jax.experimental import pallas as pl
from jax.experimental.pallas import tpu as pltpu

D_MODEL = 1024
BATCH = 8
SEQ = 4096
DEPTH = 4

F32 = jnp.float32
GRID_W = 64
CTX_LEN = 256
N_MIXERS = 4
NORM_EPS = 1e-6

HG_HEADS = 8
HG_KEY = D_MODEL // HG_HEADS
HG_VAL = D_MODEL // HG_HEADS
HG_CHUNK = 32

HY_ORDER = 2
HY_SHORT = 3
HY_EMB = 33
HY_FILTER_W = 64
HY_DECAY_TARGET = 1e-2
HY_DECAY_HI_PCT = 0.3
HY_DECAY_LO_PCT = 1.5

MB_INNER = 2 * D_MODEL
MB_HEADDIM = 64
MB_HEADS = MB_INNER // MB_HEADDIM
MB_GROUPS = 8
MB_HPG = MB_HEADS // MB_GROUPS
MB_STATE = 128
MB_CONV = 5
MB_CHUNK = 64
MB_CONV_DIM = MB_INNER + 2 * MB_GROUPS * MB_STATE
MB_IN = MB_INNER + MB_CONV_DIM + 2 * MB_HEADS

GLA_HEADS = 4
GLA_KEY_DIM = D_MODEL // 2
GLA_VAL_DIM = D_MODEL
GLA_HK = GLA_KEY_DIM // GLA_HEADS
GLA_HV = GLA_VAL_DIM // GLA_HEADS
GLA_GATE_RANK = 16
GLA_GATE_NORM = 16.0
GLA_CHUNK = 32
GLA_IN = 2 * GLA_KEY_DIM + 2 * GLA_VAL_DIM + 2 * GLA_GATE_RANK

N_EXPERTS = 16
EC_CAPACITY_FACTOR = 2
EXPERT_FF = 2048


def rmsnorm(x, g):
    xf = x.astype(F32)
    return xf * lax.rsqrt(jnp.mean(xf * xf, axis=-1, keepdims=True) + NORM_EPS) * g


def adaln(cond, w, b):
    m = jax.nn.silu(cond.astype(F32)) @ w + b
    return jnp.split(m[..., None, :], 6, axis=-1)


def modulate(h, shift, scale):
    return h * (1.0 + scale) + shift


def heads(t, n):
    b, l, _ = t.shape
    return t.reshape(b, l, n, -1).transpose(0, 2, 1, 3)


def dwconv_centered(x, w, b):
    k_w = w.shape[0]
    pad = k_w // 2
    length = x.shape[1]
    xp = jnp.pad(x, ((0, 0), (pad, pad), (0, 0)))
    return sum(xp[:, j:j + length] * w[j] for j in range(k_w)) + b


def raster_to_colmajor(h, rows):
    b, l, d = h.shape
    return h.reshape(b, rows, GRID_W, d).transpose(0, 2, 1, 3).reshape(b, l, d)


def colmajor_to_raster(h, rows):
    b, l, d = h.shape
    return h.reshape(b, GRID_W, rows, d).transpose(0, 2, 1, 3).reshape(b, l, d)


def chunk_gla(q, k, v, g, s0, chunk):
    bsz, nh, length, kd = q.shape
    vd = v.shape[-1]
    n = length // chunk

    def split_chunks(t):
        return jnp.moveaxis(t.astype(F32).reshape(bsz, nh, n, chunk, t.shape[-1]), 2, 0)

    qs, ks, vs, gs = split_chunks(q), split_chunks(k), split_chunks(v), split_chunks(g)
    cum = jnp.cumsum(gs, axis=3)
    lower = jnp.tril(jnp.ones((chunk, chunk), bool))[:, :, None]

    def step(state, inp):
        qc, kc, vc, gc = inp
        rel = jnp.where(lower, gc[:, :, :, None, :] - gc[:, :, None, :, :], -jnp.inf)
        att = jnp.einsum('bhtk,bhsk,bhtsk->bhts', qc, kc, jnp.exp(rel))
        g_end = gc[:, :, -1:, :]
        o = att @ vc + jnp.einsum('bhtk,bhkv->bhtv', qc * jnp.exp(gc), state)
        state = jnp.exp(g_end[:, :, 0, :, None]) * state + jnp.einsum('bhsk,bhsv->bhkv', kc * jnp.exp(g_end - gc), vc)
        return state, o

    state, o = lax.scan(step, s0.astype(F32), (qs, ks, vs, cum))
    return jnp.moveaxis(o, 0, 2).reshape(bsz, nh, length, vd), state


def bidir_gla(lat, ctx, chunk):
    q, kf, kb, v, gf, gb = lat
    qc, kfc, kbc, vc, gfc, gbc = ctx
    s0 = jnp.zeros(q.shape[:2] + (q.shape[-1], v.shape[-1]), F32)
    oc_f, s_f = chunk_gla(qc, kfc, vc, gfc, s0, chunk)
    ox_f, _ = chunk_gla(q, kf, v, gf, s_f, chunk)
    r = lambda t: jnp.flip(t, axis=2)
    oc_b, s_b = chunk_gla(r(qc), r(kbc), r(vc), r(gbc), s0, chunk)
    ox_b, _ = chunk_gla(r(q), r(kb), r(v), r(gb), s_b, chunk)
    return ox_f + r(ox_b), oc_f + r(oc_b)


def gated_head_out(o, og, gain, w_out):
    b, nh, l, vd = o.shape
    o = rmsnorm(o.transpose(0, 2, 1, 3), gain).reshape(b, l, nh * vd)
    return (o * jax.nn.silu(og)) @ w_out


def hgrn_lower_bound(lb_param, layer):
    return jnp.cumsum(jax.nn.softmax(lb_param.astype(F32), axis=0), axis=0)[layer]


def hgrn2_mixer(hx, hc, w_in, lb, onorm_g, w_out):
    def project(h):
        q, ff, fb, inp, og = jnp.split(h @ w_in, 5, axis=-1)
        f_fwd = lb + (1.0 - lb) * jax.nn.sigmoid(ff.astype(F32))
        f_bwd = lb + (1.0 - lb) * jax.nn.sigmoid(fb.astype(F32))
        hd = lambda t: heads(t, HG_HEADS)
        streams = (hd(jax.nn.silu(q) * HG_KEY ** -0.5), hd(1.0 - f_fwd), hd(1.0 - f_bwd), hd(inp),
                   hd(jnp.log(f_fwd)), hd(jnp.log(f_bwd)))
        return streams, og

    lat, ogx = project(hx)
    ctxs, ogc = project(hc)
    ox, oc = bidir_gla(lat, ctxs, HG_CHUNK)
    return gated_head_out(ox, ogx, onorm_g, w_out), gated_head_out(oc, ogc, onorm_g, w_out)


def hyena_pos_features(length):
    t = jnp.linspace(0.0, 1.0, length, dtype=F32)[:, None]
    bands = (HY_EMB - 1) // 2
    f = jnp.linspace(1e-4, bands - 1, bands, dtype=F32)[None, :]
    w = 2.0 * math.pi * jnp.arange(length, dtype=F32)[:, None] / length
    return jnp.concatenate([t, jnp.cos(f * w), -jnp.sin(f * w)], axis=-1), t


def hyena_window(t):
    max_decay = math.log(HY_DECAY_TARGET) / HY_DECAY_HI_PCT
    min_decay = math.log(HY_DECAY_TARGET) / HY_DECAY_LO_PCT
    deltas = jnp.abs(jnp.linspace(min_decay, max_decay, D_MODEL, dtype=F32))
    return jnp.exp(-t * deltas[None, :])


def hyena_filters(length, w1, b1, w2, b2, w3, b3, w4, freq):
    z, t = hyena_pos_features(length)
    a = jnp.sin(freq * (z @ w1 + b1))
    a = jnp.sin(freq * (a @ w2 + b2))
    a = jnp.sin(freq * (a @ w3 + b3))
    h = (a @ w4).astype(F32).reshape(length, HY_ORDER, 2, D_MODEL) * hyena_window(t)[:, None, None, :]
    return h / jnp.sum(jnp.abs(h), axis=(0, 2), keepdims=True)


def twosided_fftconv(u, h_fwd, h_bwd):
    length = u.shape[1]
    k2 = jnp.concatenate([h_fwd, jnp.zeros_like(h_fwd[:1]), h_bwd[:0:-1]], axis=0)
    spec = jnp.fft.rfft(u, n=2 * length, axis=1) * jnp.fft.rfft(k2, axis=0)[None]
    return jnp.fft.irfft(spec, n=2 * length, axis=1)[:, :length]


def hyena_mixer(hx, hc, w_in, b_in, short_w, short_b, f_w1, f_b1, f_w2, f_b2, f_w3, f_b3, f_w4,
                f_freq, f_bias, w_out, b_out):
    def run(h):
        length = h.shape[1]
        u = dwconv_centered(h @ w_in + b_in, short_w, short_b).astype(F32)
        x1, x2, v = jnp.split(u, 3, axis=-1)
        filt = hyena_filters(length, f_w1, f_b1, f_w2, f_b2, f_w3, f_b3, f_w4, f_freq)
        z = v
        for o, gate in enumerate((x1, x2)):
            z = gate * (twosided_fftconv(z, filt[:, o, 0], filt[:, o, 1]) + z * f_bias[o])
        return z @ w_out + b_out

    return run(hx), run(hc)


def chunk_ssd(xs, bm, cm, dt, a, s0, chunk):
    bsz, length = xs.shape[:2]
    n = length // chunk

    def split_chunks(t):
        return jnp.moveaxis(t.astype(F32).reshape((bsz, n, chunk) + t.shape[2:]), 1, 0)

    xcs, bcs, ccs, dts = split_chunks(xs), split_chunks(bm), split_chunks(cm), split_chunks(dt)
    cum = jnp.cumsum(dts * a.astype(F32), axis=2)
    lower = jnp.tril(jnp.ones((chunk, chunk), bool))[:, :, None, None]

    def step(state, inp):
        xc, bc, cc, dtc, lc = inp
        seg = jnp.exp(jnp.where(lower, lc[:, :, None] - lc[:, None, :], -jnp.inf))
        w = jnp.einsum('btgn,bsgn->btsg', cc, bc)[..., None] * seg * dtc[:, None]
        y = jnp.einsum('btsgh,bsghp->btghp', w, xc)
        y = y + jnp.einsum('btgn,bghnp->btghp', cc, state) * jnp.exp(lc)[..., None]
        l_end = lc[:, -1]
        state = jnp.exp(l_end)[..., None, None] * state + jnp.einsum(
            'bsgn,bsgh,bsghp->bghnp', bc, jnp.exp(l_end[:, None] - lc) * dtc, xc)
        return state, y

    state, y = lax.scan(step, s0.astype(F32), (xcs, bcs, ccs, dts, cum))
    return jnp.moveaxis(y, 0, 1).reshape(xs.shape), state


def mamba2_mixer(hx, hc, w_in, conv_w, conv_b, dt_bias, a_log, d_skip, norm_g, w_out):
    a = -jnp.exp(a_log.astype(F32)).reshape(2, MB_GROUPS, MB_HPG)

    def project(h):
        bsz, length, _ = h.shape
        z, xbc, dt_raw = jnp.split(h @ w_in, [MB_INNER, MB_INNER + MB_CONV_DIM], axis=-1)
        xbc = jax.nn.silu(dwconv_centered(xbc, conv_w, conv_b))
        xs, bm, cm = jnp.split(xbc, [MB_INNER, MB_INNER + MB_GROUPS * MB_STATE], axis=-1)
        xs = xs.reshape(bsz, length, MB_GROUPS, MB_HPG, MB_HEADDIM)
        bm = bm.reshape(bsz, length, MB_GROUPS, MB_STATE)
        cm = cm.reshape(bsz, length, MB_GROUPS, MB_STATE)
        dt = jax.nn.softplus(dt_raw.astype(F32).reshape(bsz, length, 2, MB_GROUPS, MB_HPG)
                             + dt_bias.reshape(2, MB_GROUPS, MB_HPG))
        return z, xs, bm, cm, dt

    zx, xx, bx, cx, dtx = project(hx)
    zc, xc, bc, cc, dtc = project(hc)
    s0 = jnp.zeros((hx.shape[0], MB_GROUPS, MB_HPG, MB_STATE, MB_HEADDIM), F32)
    yc_f, s_f = chunk_ssd(xc, bc, cc, dtc[:, :, 0], a[0], s0, MB_CHUNK)
    yx_f, _ = chunk_ssd(xx, bx, cx, dtx[:, :, 0], a[0], s_f, MB_CHUNK)
    r = lambda t: jnp.flip(t, axis=1)
    yc_b, s_b = chunk_ssd(r(xc), r(bc), r(cc), r(dtc[:, :, 1]), a[1], s0, MB_CHUNK)
    yx_b, _ = chunk_ssd(r(xx), r(bx), r(cx), r(dtx[:, :, 1]), a[1], s_b, MB_CHUNK)

    def finish(y, xs, z):
        bsz, length = y.shape[:2]
        y = y + d_skip.reshape(MB_GROUPS, MB_HPG)[..., None] * xs
        y = y.reshape(bsz, length, MB_INNER) * jax.nn.silu(z)
        y = rmsnorm(y.reshape(bsz, length, MB_GROUPS, MB_INNER // MB_GROUPS),
                    norm_g.reshape(MB_GROUPS, MB_INNER // MB_GROUPS))
        return y.reshape(bsz, length, MB_INNER) @ w_out

    return finish(yx_f + r(yx_b), xx, zx), finish(yc_f + r(yc_b), xc, zc)


def gla_mixer(hx, hc, w_in, gk_w2, gk_b, onorm_g, w_out):
    rows = hx.shape[1] // GRID_W
    kd, vd, rk = GLA_KEY_DIM, GLA_VAL_DIM, GLA_GATE_RANK
    cuts = [kd, 2 * kd, 2 * kd + vd, 2 * kd + 2 * vd, 2 * kd + 2 * vd + rk]

    def project(h):
        q, k, v, og, rf, rb = jnp.split(h @ w_in, cuts, axis=-1)
        gf = jax.nn.log_sigmoid((rf @ gk_w2[0] + gk_b[0]).astype(F32)) / GLA_GATE_NORM
        gb = jax.nn.log_sigmoid((rb @ gk_w2[1] + gk_b[1]).astype(F32)) / GLA_GATE_NORM
        hd = lambda t: heads(t, GLA_HEADS)
        kh = hd(k)
        return (hd(q * GLA_HK ** -0.5), kh, kh, hd(v), hd(gf), hd(gb)), og

    lat, ogx = project(raster_to_colmajor(hx, rows))
    ctxs, ogc = project(hc)
    ox, oc = bidir_gla(lat, ctxs, GLA_CHUNK)
    yx = colmajor_to_raster(gated_head_out(ox, ogx, onorm_g, w_out), rows)
    return yx, gated_head_out(oc, ogc, onorm_g, w_out)


def expert_choice_ffn(h, router_w, w_gate, w_up, w_down):
    bsz, t_len, d = h.shape
    cap = max(1, EC_CAPACITY_FACTOR * t_len // N_EXPERTS)
    aff = jax.nn.softmax(h.astype(F32) @ router_w.astype(F32), axis=-1)
    gate, idx = lax.top_k(jnp.swapaxes(aff, 1, 2), cap)
    xg = jax.vmap(lambda hb, ib: hb[ib])(h, idx)
    hid = jax.nn.silu(jnp.einsum('becd,edf->becf', xg, w_gate)) * jnp.einsum('becd,edf->becf', xg, w_up)
    ye = jnp.einsum('becf,efd->becd', hid, w_down) * gate[..., None]
    return jax.vmap(lambda ib, yb: jnp.zeros((t_len, d), yb.dtype).at[ib.reshape(-1)].add(yb.reshape(-1, d)))(idx, ye)


BF16 = jnp.bfloat16
LANES = 128
SUBLANES = 8
VMEM_LIMIT_BYTES = 56 * 2 ** 20
SCAN_CHUNK = 128
SCAN_FAST_RANGE = 80.0
SCAN_UNROLL = 8


def _cumsum_rows(tri_bf16, g):
    hi = g.astype(BF16)
    r1 = g - hi.astype(F32)
    mid = r1.astype(BF16)
    lo = (r1 - mid.astype(F32)).astype(BF16)
    d = lambda a: jnp.dot(tri_bf16, a, preferred_element_type=F32)
    return d(hi) + d(mid) + d(lo)


def _log_sigmoid(z):
    return jnp.minimum(z, 0.0) - jnp.log1p(jnp.exp(-jnp.abs(z)))


def _dot_nt(a, b):
    return lax.dot_general(a, b, (((1,), (1,)), ((), ())), preferred_element_type=F32)


def _dot_tn(a, b):
    return lax.dot_general(a, b, (((0,), (0,)), ((), ())), preferred_element_type=F32)


def _gla_scan_body(mode, seq, ctx_len, kdim, vdim, *refs):
    chunk = SCAN_CHUNK
    if mode == 'hgrn2':
        (q_x, ff_x, fb_x, v_x, og_x, q_c, ff_c, fb_c, v_c, og_c, lb_ref, gain_ref,
         o_x, o_c, cf_s, cb_s, sf_s, sb_s, kt_s, gt_s, qf_s, qb_s, of_s, ob_s, kf_s, kb_s) = refs
    else:
        (q_x, k_x, v_x, og_x, r_x, q_c, k_c, v_c, og_c, r_c, w2f_ref, w2b_ref, bf_ref, bb_ref, gain_ref,
         o_x, o_c, cf_s, cb_s, sf_s, sb_s, kt_s, gt_s, qf_s, qb_s, of_s, ob_s) = refs
    if mode == 'hgrn2':
        segs = (((ff_c, fb_c), q_c, None, v_c, og_c, o_c, ctx_len, 0),
                ((ff_x, fb_x), q_x, None, v_x, og_x, o_x, seq, ctx_len))
    else:
        segs = (((r_c,), q_c, k_c, v_c, og_c, o_c, ctx_len, 0),
                ((r_x,), q_x, k_x, v_x, og_x, o_x, seq, ctx_len))

    row = lax.broadcasted_iota(jnp.int32, (chunk, chunk), 0)
    col = lax.broadcasted_iota(jnp.int32, (chunk, chunk), 1)
    masks = (row >= col, row <= col)
    tris = tuple(m.astype(BF16) for m in masks)
    cum_s, state_s, qin_s, kout_s = (cf_s, cb_s), (sf_s, sb_s), (qf_s, qb_s), (of_s, ob_s)
    end_row = (chunk - 1, 0)
    mid_row = chunk // 2

    def rows_of(i, off=0):
        return pl.ds(pl.multiple_of(off + i * chunk, chunk), chunk)

    def gate_pass(seg, bound):
        gsrc, _, _, _, _, _, length, off = seg

        def body(i, bound):
            rows, srows = rows_of(i), rows_of(i, off)
            if mode == 'hgrn2':
                lb = lb_ref[...]
                f_f = lb + (1.0 - lb) * jax.nn.sigmoid(gsrc[0][rows, :])
                f_b = lb + (1.0 - lb) * jax.nn.sigmoid(gsrc[1][rows, :])
                g_f, g_b = jnp.log(f_f), jnp.log(f_b)
                kf_s[srows, :] = 1.0 - f_f
                kb_s[srows, :] = 1.0 - f_b
            else:
                r = gsrc[0][rows, :].astype(BF16)
                z_f = jnp.dot(r, w2f_ref[...].astype(BF16), preferred_element_type=F32) + bf_ref[...]
                z_b = jnp.dot(r, w2b_ref[...].astype(BF16), preferred_element_type=F32) + bb_ref[...]
                g_f = _log_sigmoid(z_f) / GLA_GATE_NORM
                g_b = _log_sigmoid(z_b) / GLA_GATE_NORM
            c_f = _cumsum_rows(tris[0], g_f)
            c_b = _cumsum_rows(tris[1], g_b)
            cf_s[srows, :] = c_f
            cb_s[srows, :] = c_b
            for c in (c_f, c_b):
                mid = c[mid_row:mid_row + 1, :]
                spread = jnp.maximum(jnp.abs(c[0:1, :] - mid), jnp.abs(c[chunk - 1:chunk, :] - mid))
                bound = jnp.maximum(bound, spread)
            return bound

        return lax.fori_loop(0, length // chunk, body, bound, unroll=SCAN_UNROLL)

    def intra(seg, direction, i, exact):
        _, q_r, k_r, v_r, _, _, _, off = seg
        rows, srows = rows_of(i), rows_of(i, off)
        if mode == 'hgrn2':
            qq = jax.nn.silu(q_r[rows, :]) * HG_KEY ** -0.5
            k = (kf_s if direction == 0 else kb_s)[srows, :]
        else:
            qq = q_r[rows, :] * GLA_HK ** -0.5
            k = k_r[rows, :]
        cum = cum_s[direction][srows, :]
        e = end_row[direction]
        cum_end = cum[e:e + 1, :]
        q_in = (qq * jnp.exp(cum)).astype(BF16)
        qin_s[direction][srows, :] = q_in
        kout_s[direction][srows, :] = (k * jnp.exp(cum_end - cum)).astype(BF16)
        if exact:
            kt_s[...] = k
            gt_s[...] = cum

            def col_body(s, att):
                ks = kt_s[pl.ds(s, 1), :]
                gs = gt_s[pl.ds(s, 1), :]
                p = qq * ks * jnp.exp(jnp.minimum(cum - gs, 0.0))
                return att + jnp.where(col == s, jnp.sum(p, axis=-1, keepdims=True), 0.0)

            att = lax.fori_loop(0, chunk, col_body, jnp.zeros((chunk, chunk), F32))
        else:
            rel = cum - cum[mid_row:mid_row + 1, :]
            att = _dot_nt((qq * jnp.exp(rel)).astype(BF16), (k * jnp.exp(-rel)).astype(BF16))
        att = jnp.where(masks[direction], att, 0.0).astype(BF16)
        return jnp.dot(att, v_r[rows, :].astype(BF16), preferred_element_type=F32)

    def intra_pass(exact):
        for seg in segs:
            o_r, length = seg[5], seg[6]

            def body(i, carry):
                o_r[rows_of(i), :] = intra(seg, 0, i, exact) + intra(seg, 1, i, exact)
                return carry

            lax.fori_loop(0, length // chunk, body, 0, unroll=1 if exact else SCAN_UNROLL)

    def inter(seg, direction, i):
        v_r, off = seg[3], seg[7]
        rows, srows = rows_of(i), rows_of(i, off)
        s_ref = state_s[direction]
        state = s_ref[...]
        o = _dot_nt(qin_s[direction][srows, :], state.astype(BF16))
        e = pl.ds(pl.multiple_of(off + i * chunk, chunk) + end_row[direction], 1)
        decay = jnp.exp(cum_s[direction][e, :])
        s_ref[...] = state * decay + _dot_tn(v_r[rows, :].astype(BF16), kout_s[direction][srows, :])
        return o

    def finish(seg, i, o):
        o = o * lax.rsqrt(jnp.mean(o * o, axis=-1, keepdims=True) + NORM_EPS) * gain_ref[...]
        return o * jax.nn.silu(seg[4][rows_of(i), :])

    def inter_pass():
        sf_s[...] = jnp.zeros_like(sf_s)
        sb_s[...] = jnp.zeros_like(sb_s)
        for seg in segs:
            o_r, length = seg[5], seg[6]
            n = length // chunk

            def first_half(i, carry):
                j = n - 1 - i
                o_r[rows_of(i), :] += inter(seg, 0, i)
                o_r[rows_of(j), :] += inter(seg, 1, j)
                return carry

            def second_half(i, carry):
                j = n - 1 - i
                o_r[rows_of(i), :] = finish(seg, i, o_r[rows_of(i), :] + inter(seg, 0, i))
                o_r[rows_of(j), :] = finish(seg, j, o_r[rows_of(j), :] + inter(seg, 1, j))
                return carry

            lax.fori_loop(0, n // 2, first_half, 0, unroll=min(SCAN_UNROLL, n // 2))
            lax.fori_loop(n // 2, n, second_half, 0, unroll=min(SCAN_UNROLL, n // 2))

    bound = jnp.zeros((1, kdim), F32)
    for seg in segs:
        bound = gate_pass(seg, bound)
    fast = jnp.max(bound) <= SCAN_FAST_RANGE

    @pl.when(fast)
    def _():
        intra_pass(False)

    @pl.when(jnp.logical_not(fast))
    def _():
        intra_pass(True)

    inter_pass()


def _gla_scan_call(mode, nheads, kdim, vdim, lat_in, ctx_in, small_in, bsz, seq, ctx_len):
    def stream_spec(length, width, first, stride):
        return pl.BlockSpec((None, length, width), lambda b, h: (b, 0, first + stride * h))

    assert seq % (2 * SCAN_CHUNK) == 0 and ctx_len % (2 * SCAN_CHUNK) == 0
    in_specs = [stream_spec(seq, w, f, s) for _, w, f, s in lat_in]
    in_specs += [stream_spec(ctx_len, w, f, s) for _, w, f, s in ctx_in]
    in_specs += [pl.BlockSpec(blk, imap) for _, blk, imap in small_in]
    args = [a for a, _, _, _ in lat_in] + [a for a, _, _, _ in ctx_in] + [a for a, _, _ in small_in]
    tot = seq + ctx_len
    scratch = [pltpu.VMEM((tot, kdim), F32), pltpu.VMEM((tot, kdim), F32),
               pltpu.VMEM((vdim, kdim), F32), pltpu.VMEM((vdim, kdim), F32),
               pltpu.VMEM((SCAN_CHUNK, kdim), F32), pltpu.VMEM((SCAN_CHUNK, kdim), F32),
               pltpu.VMEM((tot, kdim), BF16), pltpu.VMEM((tot, kdim), BF16),
               pltpu.VMEM((tot, kdim), BF16), pltpu.VMEM((tot, kdim), BF16)]
    if mode == 'hgrn2':
        scratch += [pltpu.VMEM((tot, kdim), F32), pltpu.VMEM((tot, kdim), F32)]
    return pl.pallas_call(
        lambda *refs: _gla_scan_body(mode, seq, ctx_len, kdim, vdim, *refs),
        grid=(bsz, nheads),
        in_specs=in_specs,
        out_specs=[pl.BlockSpec((None, seq, vdim), lambda b, h: (b, 0, h)),
                   pl.BlockSpec((None, ctx_len, vdim), lambda b, h: (b, 0, h))],
        out_shape=[jax.ShapeDtypeStruct((bsz, seq, nheads * vdim), F32),
                   jax.ShapeDtypeStruct((bsz, ctx_len, nheads * vdim), F32)],
        scratch_shapes=scratch,
        compiler_params=pltpu.CompilerParams(dimension_semantics=("arbitrary", "arbitrary"),
                                             vmem_limit_bytes=VMEM_LIMIT_BYTES),
        name=f"{mode}_scan",
    )(*args)


def hgrn2_mixer_pallas(hx, hc, w_in, lb, onorm_g, w_out):
    bsz, seq, _ = hx.shape
    ctx_len = hc.shape[1]
    yx, yc = hx @ w_in, hc @ w_in
    nh = HG_HEADS
    streams = lambda y: [(y, HG_KEY, j * nh, 1) for j in range(5)]
    small = [(lb.reshape(1, D_MODEL), (1, HG_KEY), lambda b, h: (0, h)),
             (onorm_g.reshape(1, HG_VAL), (1, HG_VAL), lambda b, h: (0, 0))]
    ox, oc = _gla_scan_call('hgrn2', nh, HG_KEY, HG_VAL, streams(yx), streams(yc), small, bsz, seq, ctx_len)
    return ox @ w_out, oc @ w_out


def gla_mixer_pallas(hx, hc, w_in, gk_w2, gk_b, onorm_g, w_out):
    bsz, seq, _ = hx.shape
    ctx_len = hc.shape[1]
    rows = seq // GRID_W
    kd, vd, rk = GLA_KEY_DIM, GLA_VAL_DIM, GLA_GATE_RANK
    nh = GLA_HEADS
    w_in_p = jnp.pad(w_in, ((0, 0), (0, LANES - 2 * rk)))
    yx = raster_to_colmajor(hx, rows) @ w_in_p
    yc = hc @ w_in_p
    w2f = jnp.pad(gk_w2[0], ((0, LANES - rk), (0, 0)))
    w2b = jnp.pad(gk_w2[1], ((rk, LANES - 2 * rk), (0, 0)))
    streams = lambda y: [(y, GLA_HK, 0, 1), (y, GLA_HK, kd // GLA_HK, 1), (y, GLA_HV, 2 * kd // GLA_HV, 1),
                         (y, GLA_HV, (2 * kd + vd) // GLA_HV, 1), (y, LANES, (2 * kd + 2 * vd) // LANES, 0)]
    small = [(w2f, (LANES, GLA_HK), lambda b, h: (0, h)), (w2b, (LANES, GLA_HK), lambda b, h: (0, h)),
             (gk_b[0].reshape(1, kd), (1, GLA_HK), lambda b, h: (0, h)),
             (gk_b[1].reshape(1, kd), (1, GLA_HK), lambda b, h: (0, h)),
             (onorm_g.reshape(1, GLA_HV), (1, GLA_HV), lambda b, h: (0, 0))]
    ox, oc = _gla_scan_call('gla', nh, GLA_HK, GLA_HV, streams(yx), streams(yc), small, bsz, seq, ctx_len)
    return colmajor_to_raster(ox @ w_out, rows), oc @ w_out


MB_GROUP_W = MB_INNER // MB_GROUPS
SSD_CHUNK = 128
SSD_CONV_ROWS = 64


def _ssd_scan_body(seq, ctx_len, *refs):
    chunk = SSD_CHUNK
    gw, hpg = MB_GROUP_W, MB_HPG
    (x_x, b_x, c_x, z_x, cl_x, rt_x, x_c, b_c, c_c, z_c, cl_c, rt_c,
     wx_ref, wb_ref, wc_ref, bx_ref, bb_ref, bc_ref, dskip_ref, gain_ref,
     o_x, o_c, xf_s, xb_s, sf_s, sb_s, pad_s, xc_s, bc_s, cc_s) = refs
    segs = ((x_c, b_c, c_c, z_c, cl_c, rt_c, o_c, ctx_len, 0),
            (x_x, b_x, c_x, z_x, cl_x, rt_x, o_x, seq, ctx_len))
    xs_s, state_s = (xf_s, xb_s), (sf_s, sb_s)
    end_row = (chunk - 1, 0)

    def conv_silu(raw_ref, w_ref, bias_ref, dst_ref, length, off):
        width = raw_ref.shape[-1]
        halo = SUBLANES
        pad = MB_CONV // 2
        pad_s[0:halo, 0:width] = jnp.zeros((halo, width), F32)
        pad_s[pl.ds(halo + length, halo), 0:width] = jnp.zeros((halo, width), F32)

        crow = SSD_CONV_ROWS

        def stage(i, carry):
            src = pl.ds(pl.multiple_of(i * crow, crow), crow)
            pad_s[pl.ds(pl.multiple_of(halo + i * crow, SUBLANES), crow), 0:width] = raw_ref[src, :]
            return carry

        lax.fori_loop(0, length // crow, stage, 0)

        def body(i, carry):
            rows_in = crow + 2 * halo
            win = pad_s[pl.ds(pl.multiple_of(i * crow, crow), rows_in), 0:width]
            acc = jnp.broadcast_to(bias_ref[...], (crow, width))
            for j in range(MB_CONV):
                lo = halo - pad + j
                acc = acc + pltpu.roll(win, rows_in - lo, axis=0)[0:crow, :] * w_ref[j:j + 1, :]
            dst = pl.ds(pl.multiple_of(off + i * crow, crow), crow)
            dst_ref[dst, :] = jax.nn.silu(acc).astype(dst_ref.dtype)
            return carry

        lax.fori_loop(0, length // crow, body, 0)

    row = lax.broadcasted_iota(jnp.int32, (chunk, chunk), 0)
    col = lax.broadcasted_iota(jnp.int32, (chunk, chunk), 1)
    masks = (row >= col, row <= col)
    head_of_lane = lax.broadcasted_iota(jnp.int32, (1, gw), 1) // MB_HEADDIM

    def rows_of(i, off=0):
        return pl.ds(pl.multiple_of(off + i * chunk, chunk), chunk)

    def bcast_heads(c4):
        out = c4[:, hpg - 1:hpg]
        for hh in range(hpg - 2, -1, -1):
            out = jnp.where(head_of_lane == hh, c4[:, hh:hh + 1], out)
        return out

    def intra_pass():
        for seg in segs:
            _, _, _, _, cl_r, rt_r, o_r, length, off = seg

            def body(i, carry):
                rows, srows = rows_of(i), rows_of(i, off)
                cols, rt = cl_r[i], rt_r[i]
                x = xc_s[srows, :]
                cb = _dot_nt(cc_s[srows, :], bc_s[srows, :])
                xh = [jnp.where(head_of_lane == hh, x, 0.0).astype(BF16) for hh in range(hpg)]
                y = jnp.zeros((chunk, gw), F32)
                for d in range(2):
                    cum = cols[:, d * hpg:(d + 1) * hpg]
                    dt = cols[:, (2 + d) * hpg:(3 + d) * hpg]
                    e = end_row[d]
                    scale = jnp.exp(cum[e:e + 1, :] - cum) * dt
                    xs_s[d][srows, :] = (x * bcast_heads(scale)).astype(BF16)
                    for hh in range(hpg):
                        j = d * hpg + hh
                        seg_decay = jnp.exp(jnp.minimum(cols[:, j:j + 1] - rt[j:j + 1, :], 0.0))
                        w = jnp.where(masks[d], cb * seg_decay * rt[2 * hpg + j:2 * hpg + j + 1, :], 0.0)
                        y = y + jnp.dot(w.astype(BF16), xh[hh], preferred_element_type=F32)
                o_r[rows, :] = y
                return carry

            lax.fori_loop(0, length // chunk, body, 0, unroll=2)

    def inter(seg, d, i):
        cl_r, off = seg[4], seg[8]
        srows = rows_of(i, off)
        cum = cl_r[i][:, d * hpg:(d + 1) * hpg]
        state = state_s[d][...]
        y = jnp.dot(cc_s[srows, :], state.astype(BF16), preferred_element_type=F32)
        e = end_row[d]
        decay = bcast_heads(jnp.exp(cum[e:e + 1, :]))
        state_s[d][...] = state * decay + _dot_tn(bc_s[srows, :], xs_s[d][srows, :])
        return y * bcast_heads(jnp.exp(cum))

    def finish(seg, i, y):
        z_r, off = seg[3], seg[8]
        y = (y + dskip_ref[...] * xc_s[rows_of(i, off), :]) * jax.nn.silu(z_r[rows_of(i), :])
        return y * lax.rsqrt(jnp.mean(y * y, axis=-1, keepdims=True) + NORM_EPS) * gain_ref[...]

    def inter_pass():
        sf_s[...] = jnp.zeros_like(sf_s)
        sb_s[...] = jnp.zeros_like(sb_s)
        for seg in segs:
            o_r, length = seg[6], seg[7]
            n = length // chunk

            def first_half(i, carry):
                j = n - 1 - i
                o_r[rows_of(i), :] += inter(seg, 0, i)
                o_r[rows_of(j), :] += inter(seg, 1, j)
                return carry

            def second_half(i, carry):
                j = n - 1 - i
                o_r[rows_of(i), :] = finish(seg, i, o_r[rows_of(i), :] + inter(seg, 0, i))
                o_r[rows_of(j), :] = finish(seg, j, o_r[rows_of(j), :] + inter(seg, 1, j))
                return carry

            lax.fori_loop(0, n // 2, first_half, 0, unroll=2)
            lax.fori_loop(n // 2, n, second_half, 0, unroll=2)

    for seg in segs:
        conv_silu(seg[0], wx_ref, bx_ref, xc_s, seg[7], seg[8])
        conv_silu(seg[1], wb_ref, bb_ref, bc_s, seg[7], seg[8])
        conv_silu(seg[2], wc_ref, bc_ref, cc_s, seg[7], seg[8])
    intra_pass()
    inter_pass()


def _ssd_head_tables(dt_raw, dt_bias, a_log):
    bsz, length, _ = dt_raw.shape
    n = length // SSD_CHUNK
    a = -jnp.exp(a_log.astype(F32)).reshape(2, MB_GROUPS, MB_HPG)
    dt = jax.nn.softplus(dt_raw.astype(F32).reshape(bsz, length, 2, MB_GROUPS, MB_HPG)
                         + dt_bias.reshape(2, MB_GROUPS, MB_HPG))
    dta = (dt * a).reshape(bsz, n, SSD_CHUNK, 2, MB_GROUPS, MB_HPG)
    dtc = dt.reshape(bsz, n, SSD_CHUNK, 2, MB_GROUPS, MB_HPG)
    lower = jnp.tril(jnp.ones((SSD_CHUNK, SSD_CHUNK), F32))
    cum_f = jnp.einsum('ts,bnsgh->bntgh', lower, dta[:, :, :, 0], precision=lax.Precision.HIGHEST)
    cum_b = jnp.einsum('st,bnsgh->bntgh', lower, dta[:, :, :, 1], precision=lax.Precision.HIGHEST)
    cols = jnp.concatenate([cum_f, cum_b, dtc[:, :, :, 0], dtc[:, :, :, 1]], axis=-1)
    cols = cols.transpose(0, 3, 1, 2, 4)
    return cols, jnp.swapaxes(cols, -1, -2)


def mamba2_mixer_pallas(hx, hc, w_in, conv_w, conv_b, dt_bias, a_log, d_skip, norm_g, w_out):
    bsz, seq, _ = hx.shape
    ctx_len = hc.shape[1]
    chunk, gw = SSD_CHUNK, MB_GROUP_W
    assert seq % (2 * chunk) == 0 and ctx_len % (2 * chunk) == 0

    def project(h):
        z = h @ w_in[:, :MB_INNER]
        xbc = h @ w_in[:, MB_INNER:MB_INNER + MB_CONV_DIM]
        cols, rows_t = _ssd_head_tables(h @ w_in[:, MB_INNER + MB_CONV_DIM:], dt_bias, a_log)
        return z, xbc, cols, rows_t

    yx, xbcx, clx, rtx = project(hx)
    yc, xbcc, clc, rtc = project(hc)
    conv_b2 = conv_b.reshape(1, MB_CONV_DIM)

    def conv_specs(rows):
        bw = MB_INNER // MB_STATE
        return [pl.BlockSpec((rows, gw), lambda b, g: (0, g)),
                pl.BlockSpec((rows, MB_STATE), lambda b, g: (0, bw + g)),
                pl.BlockSpec((rows, MB_STATE), lambda b, g: (0, bw + MB_GROUPS + g))]

    def stream_specs(length):
        n = length // chunk
        bw = MB_INNER // MB_STATE
        return [pl.BlockSpec((None, length, gw), lambda b, g: (b, 0, g)),
                pl.BlockSpec((None, length, MB_STATE), lambda b, g: (b, 0, bw + g)),
                pl.BlockSpec((None, length, MB_STATE), lambda b, g: (b, 0, bw + MB_GROUPS + g)),
                pl.BlockSpec((None, length, gw), lambda b, g: (b, 0, g)),
                pl.BlockSpec((None, None, n, chunk, 4 * MB_HPG), lambda b, g: (b, g, 0, 0, 0)),
                pl.BlockSpec((None, None, n, 4 * MB_HPG, chunk), lambda b, g: (b, g, 0, 0, 0))]

    tot = seq + ctx_len
    ox, oc = pl.pallas_call(
        lambda *refs: _ssd_scan_body(seq, ctx_len, *refs),
        grid=(bsz, MB_GROUPS),
        in_specs=stream_specs(seq) + stream_specs(ctx_len) + conv_specs(MB_CONV) + conv_specs(1) + [
            pl.BlockSpec((1, gw), lambda b, g: (0, g)), pl.BlockSpec((1, gw), lambda b, g: (0, g))],
        out_specs=[pl.BlockSpec((None, seq, gw), lambda b, g: (b, 0, g)),
                   pl.BlockSpec((None, ctx_len, gw), lambda b, g: (b, 0, g))],
        out_shape=[jax.ShapeDtypeStruct((bsz, seq, MB_INNER), F32),
                   jax.ShapeDtypeStruct((bsz, ctx_len, MB_INNER), F32)],
        scratch_shapes=[pltpu.VMEM((tot, gw), BF16), pltpu.VMEM((tot, gw), BF16),
                        pltpu.VMEM((MB_STATE, gw), F32), pltpu.VMEM((MB_STATE, gw), F32),
                        pltpu.VMEM((seq + 2 * SUBLANES, gw), F32), pltpu.VMEM((tot, gw), F32),
                        pltpu.VMEM((tot, MB_STATE), BF16), pltpu.VMEM((tot, MB_STATE), BF16)],
        compiler_params=pltpu.CompilerParams(dimension_semantics=("arbitrary", "arbitrary"),
                                             vmem_limit_bytes=VMEM_LIMIT_BYTES),
        name="ssd_scan",
    )(xbcx, xbcx, xbcx, yx, clx, rtx, xbcc, xbcc, xbcc, yc, clc, rtc,
      conv_w, conv_w, conv_w, conv_b2, conv_b2, conv_b2,
      jnp.repeat(d_skip, MB_HEADDIM).reshape(1, MB_INNER), norm_g.reshape(1, MB_INNER))
    return ox @ w_out, oc @ w_out


HY_BLOCK = 256
HY_FREQ_ROWS = 64


def _split_bf16(a):
    hi = a.astype(BF16)
    return hi, (a - hi.astype(F32)).astype(BF16)


def _dot_split(w_hi, w_lo, x):
    x_hi, x_lo = _split_bf16(x)
    d = lambda a, b: jnp.dot(a, b, preferred_element_type=F32)
    return d(w_hi, x_hi) + d(w_hi, x_lo) + d(w_lo, x_hi)


def _hyena_dft_matrices():
    p = HY_BLOCK
    f = jnp.arange(p, dtype=F32)[:, None] + 0.5
    b = jnp.arange(p, dtype=F32)[None, :]
    k = jnp.round(f * b * 2.0).astype(jnp.int32) % (4 * p)
    ang = k.astype(F32) * (2.0 * math.pi / (4 * p))
    fwd = jnp.concatenate([jnp.cos(ang), -jnp.sin(ang)], axis=0)
    inv = jnp.concatenate([jnp.cos(ang).T, -jnp.sin(ang).T], axis=1) / p
    return fwd, inv


def _hyena_filter_spectra(hf, hb, fwd):
    length, d = hf.shape
    p = HY_BLOCK
    nb = length // p
    h2 = jnp.concatenate([jnp.zeros((1, d), F32), jnp.flip(hb[1:], axis=0), hf], axis=0)
    blocks = h2.reshape(2 * nb, p, d)
    spec = jnp.einsum('fj,kjd->kfd', fwd, blocks, precision=lax.Precision.HIGHEST)
    s_re, s_im = spec[:, :p], spec[:, p:]
    sign = jnp.where(jnp.arange(p) % 2 == 0, 1.0, -1.0).astype(F32)[None, :, None]
    return s_re[1:] - sign * s_im[:-1], s_im[1:] + sign * s_re[:-1]


def _hyena_conv_body(nb, conv_u, u_ref, x_ref, bias_ref, gre_ref, gim_ref, wfh_ref, wfl_ref, wih_ref, wil_ref,
                     cwu_ref, cbu_ref, cwx_ref, cbx_ref, o_ref, ure_s, uim_s, y_s, pad_s, uc_s, xc_s):
    p, fr = HY_BLOCK, HY_FREQ_ROWS
    length = nb * p

    def rows_of(i):
        return pl.ds(pl.multiple_of(i * p, p), p)

    def short_conv(raw_ref, w_ref, b_ref, dst_ref):
        halo, pad, crow = SUBLANES, HY_SHORT // 2, SSD_CONV_ROWS
        pad_s[0:halo, :] = jnp.zeros((halo, pad_s.shape[-1]), F32)
        pad_s[pl.ds(halo + length, halo), :] = jnp.zeros((halo, pad_s.shape[-1]), F32)

        def stage(i, carry):
            src = pl.ds(pl.multiple_of(i * crow, crow), crow)
            pad_s[pl.ds(pl.multiple_of(halo + i * crow, SUBLANES), crow), :] = raw_ref[src, :]
            return carry

        lax.fori_loop(0, length // crow, stage, 0)

        def body(i, carry):
            rows_in = crow + 2 * halo
            win = pad_s[pl.ds(pl.multiple_of(i * crow, crow), rows_in), :]
            acc = jnp.broadcast_to(b_ref[...], (crow, win.shape[-1]))
            for j in range(HY_SHORT):
                lo = halo - pad + j
                acc = acc + pltpu.roll(win, rows_in - lo, axis=0)[0:crow, :] * w_ref[j:j + 1, :]
            dst_ref[pl.ds(pl.multiple_of(i * crow, crow), crow), :] = acc
            return carry

        lax.fori_loop(0, length // crow, body, 0, unroll=2)

    if conv_u:
        short_conv(u_ref, cwu_ref, cbu_ref, uc_s)
    u_src = uc_s if conv_u else u_ref
    short_conv(x_ref, cwx_ref, cbx_ref, xc_s)

    def forward(j, carry):
        spec = _dot_split(wfh_ref[...], wfl_ref[...], u_src[rows_of(j), :])
        ure_s[j] = spec[:p]
        uim_s[j] = spec[p:]
        return carry

    lax.fori_loop(0, nb, forward, 0, unroll=min(nb, 2))

    def out_block(i, carry):
        for t in range(p // fr):
            r = pl.ds(t * fr, fr)

            def acc_body(j, acc):
                a_re, a_im = acc
                k = i - j + nb - 1
                g_re, g_im = gre_ref[k, r, :], gim_ref[k, r, :]
                u_re, u_im = ure_s[j, r, :], uim_s[j, r, :]
                return a_re + g_re * u_re - g_im * u_im, a_im + g_re * u_im + g_im * u_re

            zero = jnp.zeros((fr, u_ref.shape[-1]), F32)
            a_re, a_im = lax.fori_loop(0, nb, acc_body, (zero, zero), unroll=min(nb, 4))
            y_s[pl.ds(t * fr, fr), :] = a_re
            y_s[pl.ds(p + t * fr, fr), :] = a_im
        y = _dot_split(wih_ref[...], wil_ref[...], y_s[...])
        u = u_src[rows_of(i), :]
        o_ref[rows_of(i), :] = xc_s[rows_of(i), :] * (y + u * bias_ref[...])
        return carry

    lax.fori_loop(0, nb, out_block, 0)


def _hyena_conv(u_arr, u_blk, x_arr, x_blk, bias, g_re, g_im, mats, short_w, short_b, conv_u):
    bsz, length, _ = u_arr.shape
    d = bias.shape[-1]
    p = HY_BLOCK
    nb = length // p
    nseg = 2 * nb - 1
    const = lambda shape: pl.BlockSpec(shape, lambda c, b: (0, 0))
    cu_blk = u_blk if conv_u else x_blk
    return pl.pallas_call(
        lambda *refs: _hyena_conv_body(nb, conv_u, *refs),
        grid=(d // LANES, bsz),
        in_specs=[pl.BlockSpec((None, length, LANES), lambda c, b: (b, 0, u_blk + c)),
                  pl.BlockSpec((None, length, LANES), lambda c, b: (b, 0, x_blk + c)),
                  pl.BlockSpec((1, LANES), lambda c, b: (0, c)),
                  pl.BlockSpec((nseg, p, LANES), lambda c, b: (0, 0, c)),
                  pl.BlockSpec((nseg, p, LANES), lambda c, b: (0, 0, c)),
                  const((2 * p, p)), const((2 * p, p)), const((p, 2 * p)), const((p, 2 * p)),
                  pl.BlockSpec((HY_SHORT, LANES), lambda c, b: (0, cu_blk + c)),
                  pl.BlockSpec((1, LANES), lambda c, b: (0, cu_blk + c)),
                  pl.BlockSpec((HY_SHORT, LANES), lambda c, b: (0, x_blk + c)),
                  pl.BlockSpec((1, LANES), lambda c, b: (0, x_blk + c))],
        out_specs=pl.BlockSpec((None, length, LANES), lambda c, b: (b, 0, c)),
        out_shape=jax.ShapeDtypeStruct((bsz, length, d), F32),
        scratch_shapes=[pltpu.VMEM((nb, p, LANES), F32), pltpu.VMEM((nb, p, LANES), F32),
                        pltpu.VMEM((2 * p, LANES), F32),
                        pltpu.VMEM((length + 2 * SUBLANES, LANES), F32),
                        pltpu.VMEM((length, LANES), F32), pltpu.VMEM((length, LANES), F32)],
        compiler_params=pltpu.CompilerParams(dimension_semantics=("arbitrary", "arbitrary"),
                                             vmem_limit_bytes=VMEM_LIMIT_BYTES),
        name="hyena_conv",
    )(u_arr, x_arr, bias.reshape(1, d), g_re, g_im, *mats, short_w, short_b, short_w, short_b)


def hyena_mixer_pallas(hx, hc, w_in, b_in, short_w, short_b, f_w1, f_b1, f_w2, f_b2, f_w3, f_b3, f_w4,
                       f_freq, f_bias, w_out, b_out):
    fwd, inv = _hyena_dft_matrices()
    mats = _split_bf16(fwd) + _split_bf16(inv)
    nblk = D_MODEL // LANES

    def taps(length):
        z, t = hyena_pos_features(length)
        a = jnp.sin(f_freq * (z @ f_w1 + f_b1))
        a = jnp.sin(f_freq * (a @ f_w2 + f_b2))
        a = jnp.sin(f_freq * (a @ f_w3 + f_b3))
        mw = (a @ f_w4).astype(F32) * jnp.tile(hyena_window(t), (1, 2 * HY_ORDER))
        norm = jnp.sum(jnp.abs(mw), axis=0).reshape(HY_ORDER, 2, D_MODEL).sum(axis=1)
        return mw, 1.0 / norm

    def spectra(mw, inv_norm, o):
        hf = mw[:, (2 * o) * D_MODEL:(2 * o + 1) * D_MODEL]
        hb = mw[:, (2 * o + 1) * D_MODEL:(2 * o + 2) * D_MODEL]
        g_re, g_im = _hyena_filter_spectra(hf, hb, fwd)
        return g_re * inv_norm[o], g_im * inv_norm[o]

    def run(h):
        length = h.shape[1]
        raw = (h @ w_in + b_in).astype(F32)
        sb = short_b.reshape(1, -1)
        mw, inv_norm = taps(length)
        g0, g1 = spectra(mw, inv_norm, 0), spectra(mw, inv_norm, 1)
        z = _hyena_conv(raw, 2 * nblk, raw, 0, f_bias[0], g0[0], g0[1], mats, short_w, sb, True)
        z = _hyena_conv(z, 0, raw, nblk, f_bias[1], g1[0], g1[1], mats, short_w, sb, False)
        return z @ w_out + b_out

    return run(hx), run(hc)


MOE_ROW_TILE = 2048
MOE_COL_TILE = 512


def _moe_combine_body(t_len, idx_ref, gate_ref, ye_ref, x_ref, g2_ref, o_ref):
    e = pl.program_id(2)
    cap = idx_ref.shape[-1]
    rt = min(MOE_ROW_TILE, t_len)

    @pl.when(e == 0)
    def _():
        o_ref[...] = jnp.zeros_like(o_ref)

    yg = (ye_ref[...] * gate_ref[...]).astype(BF16)
    idx = idx_ref[...]

    def body(r, carry):
        r0 = pl.multiple_of(r * rt, rt)
        tok = lax.broadcasted_iota(jnp.int32, (rt, cap), 0) + r0
        onehot = jnp.where(tok == idx, 1.0, 0.0).astype(BF16)
        o_ref[pl.ds(r0, rt), :] += jnp.dot(onehot, yg, preferred_element_type=F32)
        return carry

    lax.fori_loop(0, t_len // rt, body, 0)

    @pl.when(e == pl.num_programs(2) - 1)
    def _():
        o_ref[...] = x_ref[...] + g2_ref[...] * o_ref[...]


def expert_choice_ffn_residual(x, g2, h, router_w, w_gate, w_up, w_down):
    bsz, t_len, d = h.shape
    cap = max(1, EC_CAPACITY_FACTOR * t_len // N_EXPERTS)
    aff = jax.nn.softmax(h.astype(F32) @ router_w.astype(F32), axis=-1)
    gate, idx = lax.top_k(jnp.swapaxes(aff, 1, 2), cap)
    xg = jax.vmap(lambda hb, ib: hb[ib])(h.astype(BF16), idx)
    pre_g = jnp.einsum('becd,edf->becf', xg, w_gate, preferred_element_type=BF16)
    pre_u = jnp.einsum('becd,edf->becf', xg, w_up, preferred_element_type=BF16)
    hid = (jax.nn.silu(pre_g.astype(F32)) * pre_u.astype(F32)).astype(BF16)
    ye = jnp.einsum('becf,efd->becd', hid, w_down, preferred_element_type=F32)
    ct = MOE_COL_TILE if t_len > MOE_ROW_TILE else d
    g2 = jnp.broadcast_to(g2.reshape(-1, 1, d), (bsz, 1, d))
    return pl.pallas_call(
        lambda *refs: _moe_combine_body(t_len, *refs),
        grid=(bsz, d // ct, N_EXPERTS),
        in_specs=[pl.BlockSpec((None, None, 1, cap), lambda b, c, e: (b, e, 0, 0)),
                  pl.BlockSpec((None, None, cap, 1), lambda b, c, e: (b, e, 0, 0)),
                  pl.BlockSpec((None, None, cap, ct), lambda b, c, e: (b, e, 0, c)),
                  pl.BlockSpec((None, t_len, ct), lambda b, c, e: (b, 0, c)),
                  pl.BlockSpec((None, 1, ct), lambda b, c, e: (b, 0, c))],
        out_specs=pl.BlockSpec((None, t_len, ct), lambda b, c, e: (b, 0, c)),
        out_shape=jax.ShapeDtypeStruct((bsz, t_len, d), F32),
        compiler_params=pltpu.CompilerParams(dimension_semantics=("arbitrary", "arbitrary", "arbitrary"),
                                             vmem_limit_bytes=VMEM_LIMIT_BYTES),
        name="moe_combine",
    )(idx.reshape(bsz, N_EXPERTS, 1, cap), gate.reshape(bsz, N_EXPERTS, cap, 1), ye, x, g2)


def _final_norm_body(x_ref, g_ref, o_ref):
    xf = x_ref[...]
    o_ref[...] = xf * lax.rsqrt(jnp.mean(xf * xf, axis=-1, keepdims=True) + NORM_EPS) * g_ref[...]


def final_rmsnorm(x, g):
    b, l, d = x.shape
    rows = b * l
    tm = 512
    out = pl.pallas_call(
        _final_norm_body,
        grid=(rows // tm,),
        in_specs=[pl.BlockSpec((tm, d), lambda i: (i, 0)), pl.BlockSpec((1, d), lambda i: (0, 0))],
        out_specs=pl.BlockSpec((tm, d), lambda i: (i, 0)),
        out_shape=jax.ShapeDtypeStruct((rows, d), F32),
    )(x.reshape(rows, d), g.reshape(1, d))
    return out.reshape(b, l, d)


def kernel(x, c, ctx, c_ctx, ada_w, ada_b, norm1_g, norm2_g,
           hg_w_in, hg_lb, hg_onorm_g, hg_w_out,
           hy_w_in, hy_b_in, hy_short_w, hy_short_b, hy_f_w1, hy_f_b1, hy_f_w2, hy_f_b2,
           hy_f_w3, hy_f_b3, hy_f_w4, hy_f_freq, hy_f_bias, hy_w_out, hy_b_out,
           mb_w_in, mb_conv_w, mb_conv_b, mb_dt_bias, mb_a_log, mb_d, mb_norm_g, mb_w_out,
           gla_w_in, gla_gk_w2, gla_gk_b, gla_onorm_g, gla_w_out,
           router_w, moe_w_gate, moe_w_up, moe_w_down, final_norm_g):
    for i in range(DEPTH):
        last = i == DEPTH - 1
        sh1, sc1, g1, sh2, sc2, g2 = adaln(c, ada_w[i], ada_b[i])
        csh1, csc1, cg1, csh2, csc2, cg2 = adaln(c_ctx, ada_w[i], ada_b[i])
        hx = modulate(rmsnorm(x, norm1_g[i]), sh1, sc1)
        hc = modulate(rmsnorm(ctx, norm1_g[i]), csh1, csc1)
        kind, j = i % N_MIXERS, i // N_MIXERS
        if kind == 0:
            yx, yc = hgrn2_mixer_pallas(hx, hc, hg_w_in[j], hgrn_lower_bound(hg_lb, i), hg_onorm_g[j], hg_w_out[j])
        elif kind == 1:
            yx, yc = hyena_mixer_pallas(hx, hc, hy_w_in[j], hy_b_in[j], hy_short_w[j], hy_short_b[j],
                                 hy_f_w1[j], hy_f_b1[j], hy_f_w2[j], hy_f_b2[j], hy_f_w3[j], hy_f_b3[j],
                                 hy_f_w4[j], hy_f_freq[j], hy_f_bias[j], hy_w_out[j], hy_b_out[j])
        elif kind == 2:
            yx, yc = mamba2_mixer_pallas(hx, hc, mb_w_in[j], mb_conv_w[j], mb_conv_b[j], mb_dt_bias[j],
                                  mb_a_log[j], mb_d[j], mb_norm_g[j], mb_w_out[j])
        else:
            yx, yc = gla_mixer_pallas(hx, hc, gla_w_in[j], gla_gk_w2[j], gla_gk_b[j], gla_onorm_g[j], gla_w_out[j])
        x = x + g1 * yx
        x = expert_choice_ffn_residual(x, g2, modulate(rmsnorm(x, norm2_g[i]), sh2, sc2),
                                       router_w[i], moe_w_gate[i], moe_w_up[i], moe_w_down[i])
        if not last:
            ctx = ctx + cg1 * yc
            ctx = expert_choice_ffn_residual(ctx, cg2, modulate(rmsnorm(ctx, norm2_g[i]), csh2, csc2),
                                             router_w[i], moe_w_gate[i], moe_w_up[i], moe_w_down[i])
    return final_rmsnorm(x, final_norm_g)
```

```python
import math
import jax, jax.numpy as jnp
from jax import lax
from jax.experimental import pallas as pl
from jax.experimental.pallas import tpu as pltpu

D_MODEL = 1024
BATCH = 8
SEQ = 4096
DEPTH = 4

F32 = jnp.float32
GRID_W = 64
CTX_LEN = 256
N_MIXERS = 4
NORM_EPS = 1e-6

HG_HEADS = 8
HG_KEY = D_MODEL // HG_HEADS
HG_VAL = D_MODEL // HG_HEADS
HG_CHUNK = 32

HY_ORDER = 2
HY_SHORT = 3
HY_EMB = 33
HY_FILTER_W = 64
HY_DECAY_TARGET = 1e-2
HY_DECAY_HI_PCT = 0.3
HY_DECAY_LO_PCT = 1.5

MB_INNER = 2 * D_MODEL
MB_HEADDIM = 64
MB_HEADS = MB_INNER // MB_HEADDIM
MB_GROUPS = 8
MB_HPG = MB_HEADS // MB_GROUPS
MB_STATE = 128
MB_CONV = 5
MB_CHUNK = 64
MB_CONV_DIM = MB_INNER + 2 * MB_GROUPS * MB_STATE
MB_IN = MB_INNER + MB_CONV_DIM + 2 * MB_HEADS

GLA_HEADS = 4
GLA_KEY_DIM = D_MODEL // 2
GLA_VAL_DIM = D_MODEL
GLA_HK = GLA_KEY_DIM // GLA_HEADS
GLA_HV = GLA_VAL_DIM // GLA_HEADS
GLA_GATE_RANK = 16
GLA_GATE_NORM = 16.0
GLA_CHUNK = 32
GLA_IN = 2 * GLA_KEY_DIM + 2 * GLA_VAL_DIM + 2 * GLA_GATE_RANK

N_EXPERTS = 16
EC_CAPACITY_FACTOR = 2
EXPERT_FF = 2048


def rmsnorm(x, g):
    xf = x.astype(F32)
    return xf * lax.rsqrt(jnp.mean(xf * xf, axis=-1, keepdims=True) + NORM_EPS) * g


def adaln(cond, w, b):
    m = jax.nn.silu(cond.astype(F32)) @ w + b
    return jnp.split(m[..., None, :], 6, axis=-1)


def modulate(h, shift, scale):
    return h * (1.0 + scale) + shift


def heads(t, n):
    b, l, _ = t.shape
    return t.reshape(b, l, n, -1).transpose(0, 2, 1, 3)


def dwconv_centered(x, w, b):
    k_w = w.shape[0]
    pad = k_w // 2
    length = x.shape[1]
    xp = jnp.pad(x, ((0, 0), (pad, pad), (0, 0)))
    return sum(xp[:, j:j + length] * w[j] for j in range(k_w)) + b


def raster_to_colmajor(h, rows):
    b, l, d = h.shape
    return h.reshape(b, rows, GRID_W, d).transpose(0, 2, 1, 3).reshape(b, l, d)


def colmajor_to_raster(h, rows):
    b, l, d = h.shape
    return h.reshape(b, GRID_W, rows, d).transpose(0, 2, 1, 3).reshape(b, l, d)


def chunk_gla(q, k, v, g, s0, chunk):
    bsz, nh, length, kd = q.shape
    vd = v.shape[-1]
    n = length // chunk

    def split_chunks(t):
        return jnp.moveaxis(t.astype(F32).reshape(bsz, nh, n, chunk, t.shape[-1]), 2, 0)

    qs, ks, vs, gs = split_chunks(q), split_chunks(k), split_chunks(v), split_chunks(g)
    cum = jnp.cumsum(gs, axis=3)
    lower = jnp.tril(jnp.ones((chunk, chunk), bool))[:, :, None]

    def step(state, inp):
        qc, kc, vc, gc = inp
        rel = jnp.where(lower, gc[:, :, :, None, :] - gc[:, :, None, :, :], -jnp.inf)
        att = jnp.einsum('bhtk,bhsk,bhtsk->bhts', qc, kc, jnp.exp(rel))
        g_end = gc[:, :, -1:, :]
        o = att @ vc + jnp.einsum('bhtk,bhkv->bhtv', qc * jnp.exp(gc), state)
        state = jnp.exp(g_end[:, :, 0, :, None]) * state + jnp.einsum('bhsk,bhsv->bhkv', kc * jnp.exp(g_end - gc), vc)
        return state, o

    state, o = lax.scan(step, s0.astype(F32), (qs, ks, vs, cum))
    return jnp.moveaxis(o, 0, 2).reshape(bsz, nh, length, vd), state


def bidir_gla(lat, ctx, chunk):
    q, kf, kb, v, gf, gb = lat
    qc, kfc, kbc, vc, gfc, gbc = ctx
    s0 = jnp.zeros(q.shape[:2] + (q.shape[-1], v.shape[-1]), F32)
    oc_f, s_f = chunk_gla(qc, kfc, vc, gfc, s0, chunk)
    ox_f, _ = chunk_gla(q, kf, v, gf, s_f, chunk)
    r = lambda t: jnp.flip(t, axis=2)
    oc_b, s_b = chunk_gla(r(qc), r(kbc), r(vc), r(gbc), s0, chunk)
    ox_b, _ = chunk_gla(r(q), r(kb), r(v), r(gb), s_b, chunk)
    return ox_f + r(ox_b), oc_f + r(oc_b)


def gated_head_out(o, og, gain, w_out):
    b, nh, l, vd = o.shape
    o = rmsnorm(o.transpose(0, 2, 1, 3), gain).reshape(b, l, nh * vd)
    return (o * jax.nn.silu(og)) @ w_out


def hgrn_lower_bound(lb_param, layer):
    return jnp.cumsum(jax.nn.softmax(lb_param.astype(F32), axis=0), axis=0)[layer]


def hgrn2_mixer(hx, hc, w_in, lb, onorm_g, w_out):
    def project(h):
        q, ff, fb, inp, og = jnp.split(h @ w_in, 5, axis=-1)
        f_fwd = lb + (1.0 - lb) * jax.nn.sigmoid(ff.astype(F32))
        f_bwd = lb + (1.0 - lb) * jax.nn.sigmoid(fb.astype(F32))
        hd = lambda t: heads(t, HG_HEADS)
        streams = (hd(jax.nn.silu(q) * HG_KEY ** -0.5), hd(1.0 - f_fwd), hd(1.0 - f_bwd), hd(inp),
                   hd(jnp.log(f_fwd)), hd(jnp.log(f_bwd)))
        return streams, og

    lat, ogx = project(hx)
    ctxs, ogc = project(hc)
    ox, oc = bidir_gla(lat, ctxs, HG_CHUNK)
    return gated_head_out(ox, ogx, onorm_g, w_out), gated_head_out(oc, ogc, onorm_g, w_out)


def hyena_pos_features(length):
    t = jnp.linspace(0.0, 1.0, length, dtype=F32)[:, None]
    bands = (HY_EMB - 1) // 2
    f = jnp.linspace(1e-4, bands - 1, bands, dtype=F32)[None, :]
    w = 2.0 * math.pi * jnp.arange(length, dtype=F32)[:, None] / length
    return jnp.concatenate([t, jnp.cos(f * w), -jnp.sin(f * w)], axis=-1), t


def hyena_window(t):
    max_decay = math.log(HY_DECAY_TARGET) / HY_DECAY_HI_PCT
    min_decay = math.log(HY_DECAY_TARGET) / HY_DECAY_LO_PCT
    deltas = jnp.abs(jnp.linspace(min_decay, max_decay, D_MODEL, dtype=F32))
    return jnp.exp(-t * deltas[None, :])


def hyena_filters(length, w1, b1, w2, b2, w3, b3, w4, freq):
    z, t = hyena_pos_features(length)
    a = jnp.sin(freq * (z @ w1 + b1))
    a = jnp.sin(freq * (a @ w2 + b2))
    a = jnp.sin(freq * (a @ w3 + b3))
    h = (a @ w4).astype(F32).reshape(length, HY_ORDER, 2, D_MODEL) * hyena_window(t)[:, None, None, :]
    return h / jnp.sum(jnp.abs(h), axis=(0, 2), keepdims=True)


def twosided_fftconv(u, h_fwd, h_bwd):
    length = u.shape[1]
    k2 = jnp.concatenate([h_fwd, jnp.zeros_like(h_fwd[:1]), h_bwd[:0:-1]], axis=0)
    spec = jnp.fft.rfft(u, n=2 * length, axis=1) * jnp.fft.rfft(k2, axis=0)[None]
    return jnp.fft.irfft(spec, n=2 * length, axis=1)[:, :length]


def hyena_mixer(hx, hc, w_in, b_in, short_w, short_b, f_w1, f_b1, f_w2, f_b2, f_w3, f_b3, f_w4,
                f_freq, f_bias, w_out, b_out):
    def run(h):
        length = h.shape[1]
        u = dwconv_centered(h @ w_in + b_in, short_w, short_b).astype(F32)
        x1, x2, v = jnp.split(u, 3, axis=-1)
        filt = hyena_filters(length, f_w1, f_b1, f_w2, f_b2, f_w3, f_b3, f_w4, f_freq)
        z = v
        for o, gate in enumerate((x1, x2)):
            z = gate * (twosided_fftconv(z, filt[:, o, 0], filt[:, o, 1]) + z * f_bias[o])
        return z @ w_out + b_out

    return run(hx), run(hc)


def chunk_ssd(xs, bm, cm, dt, a, s0, chunk):
    bsz, length = xs.shape[:2]
    n = length // chunk

    def split_chunks(t):
        return jnp.moveaxis(t.astype(F32).reshape((bsz, n, chunk) + t.shape[2:]), 1, 0)

    xcs, bcs, ccs, dts = split_chunks(xs), split_chunks(bm), split_chunks(cm), split_chunks(dt)
    cum = jnp.cumsum(dts * a.astype(F32), axis=2)
    lower = jnp.tril(jnp.ones((chunk, chunk), bool))[:, :, None, None]

    def step(state, inp):
        xc, bc, cc, dtc, lc = inp
        seg = jnp.exp(jnp.where(lower, lc[:, :, None] - lc[:, None, :], -jnp.inf))
        w = jnp.einsum('btgn,bsgn->btsg', cc, bc)[..., None] * seg * dtc[:, None]
        y = jnp.einsum('btsgh,bsghp->btghp', w, xc)
        y = y + jnp.einsum('btgn,bghnp->btghp', cc, state) * jnp.exp(lc)[..., None]
        l_end = lc[:, -1]
        state = jnp.exp(l_end)[..., None, None] * state + jnp.einsum(
            'bsgn,bsgh,bsghp->bghnp', bc, jnp.exp(l_end[:, None] - lc) * dtc, xc)
        return state, y

    state, y = lax.scan(step, s0.astype(F32), (xcs, bcs, ccs, dts, cum))
    return jnp.moveaxis(y, 0, 1).reshape(xs.shape), state


def mamba2_mixer(hx, hc, w_in, conv_w, conv_b, dt_bias, a_log, d_skip, norm_g, w_out):
    a = -jnp.exp(a_log.astype(F32)).reshape(2, MB_GROUPS, MB_HPG)

    def project(h):
        bsz, length, _ = h.shape
        z, xbc, dt_raw = jnp.split(h @ w_in, [MB_INNER, MB_INNER + MB_CONV_DIM], axis=-1)
        xbc = jax.nn.silu(dwconv_centered(xbc, conv_w, conv_b))
        xs, bm, cm = jnp.split(xbc, [MB_INNER, MB_INNER + MB_GROUPS * MB_STATE], axis=-1)
        xs = xs.reshape(bsz, length, MB_GROUPS, MB_HPG, MB_HEADDIM)
        bm = bm.reshape(bsz, length, MB_GROUPS, MB_STATE)
        cm = cm.reshape(bsz, length, MB_GROUPS, MB_STATE)
        dt = jax.nn.softplus(dt_raw.astype(F32).reshape(bsz, length, 2, MB_GROUPS, MB_HPG)
                             + dt_bias.reshape(2, MB_GROUPS, MB_HPG))
        return z, xs, bm, cm, dt

    zx, xx, bx, cx, dtx = project(hx)
    zc, xc, bc, cc, dtc = project(hc)
    s0 = jnp.zeros((hx.shape[0], MB_GROUPS, MB_HPG, MB_STATE, MB_HEADDIM), F32)
    yc_f, s_f = chunk_ssd(xc, bc, cc, dtc[:, :, 0], a[0], s0, MB_CHUNK)
    yx_f, _ = chunk_ssd(xx, bx, cx, dtx[:, :, 0], a[0], s_f, MB_CHUNK)
    r = lambda t: jnp.flip(t, axis=1)
    yc_b, s_b = chunk_ssd(r(xc), r(bc), r(cc), r(dtc[:, :, 1]), a[1], s0, MB_CHUNK)
    yx_b, _ = chunk_ssd(r(xx), r(bx), r(cx), r(dtx[:, :, 1]), a[1], s_b, MB_CHUNK)

    def finish(y, xs, z):
        bsz, length = y.shape[:2]
        y = y + d_skip.reshape(MB_GROUPS, MB_HPG)[..., None] * xs
        y = y.reshape(bsz, length, MB_INNER) * jax.nn.silu(z)
        y = rmsnorm(y.reshape(bsz, length, MB_GROUPS, MB_INNER // MB_GROUPS),
                    norm_g.reshape(MB_GROUPS, MB_INNER // MB_GROUPS))
        return y.reshape(bsz, length, MB_INNER) @ w_out

    return finish(yx_f + r(yx_b), xx, zx), finish(yc_f + r(yc_b), xc, zc)


def gla_mixer(hx, hc, w_in, gk_w2, gk_b, onorm_g, w_out):
    rows = hx.shape[1] // GRID_W
    kd, vd, rk = GLA_KEY_DIM, GLA_VAL_DIM, GLA_GATE_RANK
    cuts = [kd, 2 * kd, 2 * kd + vd, 2 * kd + 2 * vd, 2 * kd + 2 * vd + rk]

    def project(h):
        q, k, v, og, rf, rb = jnp.split(h @ w_in, cuts, axis=-1)
        gf = jax.nn.log_sigmoid((rf @ gk_w2[0] + gk_b[0]).astype(F32)) / GLA_GATE_NORM
        gb = jax.nn.log_sigmoid((rb @ gk_w2[1] + gk_b[1]).astype(F32)) / GLA_GATE_NORM
        hd = lambda t: heads(t, GLA_HEADS)
        kh = hd(k)
        return (hd(q * GLA_HK ** -0.5), kh, kh, hd(v), hd(gf), hd(gb)), og

    lat, ogx = project(raster_to_colmajor(hx, rows))
    ctxs, ogc = project(hc)
    ox, oc = bidir_gla(lat, ctxs, GLA_CHUNK)
    yx = colmajor_to_raster(gated_head_out(ox, ogx, onorm_g, w_out), rows)
    return yx, gated_head_out(oc, ogc, onorm_g, w_out)


def expert_choice_ffn(h, router_w, w_gate, w_up, w_down):
    bsz, t_len, d = h.shape
    cap = max(1, EC_CAPACITY_FACTOR * t_len // N_EXPERTS)
    aff = jax.nn.softmax(h.astype(F32) @ router_w.astype(F32), axis=-1)
    gate, idx = lax.top_k(jnp.swapaxes(aff, 1, 2), cap)
    xg = jax.vmap(lambda hb, ib: hb[ib])(h, idx)
    hid = jax.nn.silu(jnp.einsum('becd,edf->becf', xg, w_gate)) * jnp.einsum('becd,edf->becf', xg, w_up)
    ye = jnp.einsum('becf,efd->becd', hid, w_down) * gate[..., None]
    return jax.vmap(lambda ib, yb: jnp.zeros((t_len, d), yb.dtype).at[ib.reshape(-1)].add(yb.reshape(-1, d)))(idx, ye)


BF16 = jnp.bfloat16
LANES = 128
SUBLANES = 8
VMEM_LIMIT_BYTES = 56 * 2 ** 20
SCAN_CHUNK = 128
SCAN_FAST_RANGE = 80.0
SCAN_UNROLL = 8


def _cumsum_rows(tri_bf16, g):
    hi = g.astype(BF16)
    r1 = g - hi.astype(F32)
    mid = r1.astype(BF16)
    lo = (r1 - mid.astype(F32)).astype(BF16)
    d = lambda a: jnp.dot(tri_bf16, a, preferred_element_type=F32)
    return d(hi) + d(mid) + d(lo)


def _log_sigmoid(z):
    return jnp.minimum(z, 0.0) - jnp.log1p(jnp.exp(-jnp.abs(z)))


def _dot_nt(a, b):
    return lax.dot_general(a, b, (((1,), (1,)), ((), ())), preferred_element_type=F32)


def _dot_tn(a, b):
    return lax.dot_general(a, b, (((0,), (0,)), ((), ())), preferred_element_type=F32)


def _gla_scan_body(mode, seq, ctx_len, kdim, vdim, *refs):
    chunk = SCAN_CHUNK
    if mode == 'hgrn2':
        (q_x, ff_x, fb_x, v_x, og_x, q_c, ff_c, fb_c, v_c, og_c, lb_ref, gain_ref,
         o_x, o_c, cf_s, cb_s, sf_s, sb_s, kt_s, gt_s, qf_s, qb_s, of_s, ob_s, kf_s, kb_s) = refs
    else:
        (q_x, k_x, v_x, og_x, r_x, q_c, k_c, v_c, og_c, r_c, w2f_ref, w2b_ref, bf_ref, bb_ref, gain_ref,
         o_x, o_c, cf_s, cb_s, sf_s, sb_s, kt_s, gt_s, qf_s, qb_s, of_s, ob_s) = refs
    if mode == 'hgrn2':
        segs = (((ff_c, fb_c), q_c, None, v_c, og_c, o_c, ctx_len, 0),
                ((ff_x, fb_x), q_x, None, v_x, og_x, o_x, seq, ctx_len))
    else:
        segs = (((r_c,), q_c, k_c, v_c, og_c, o_c, ctx_len, 0),
                ((r_x,), q_x, k_x, v_x, og_x, o_x, seq, ctx_len))

    row = lax.broadcasted_iota(jnp.int32, (chunk, chunk), 0)
    col = lax.broadcasted_iota(jnp.int32, (chunk, chunk), 1)
    masks = (row >= col, row <= col)
    tris = tuple(m.astype(BF16) for m in masks)
    cum_s, state_s, qin_s, kout_s = (cf_s, cb_s), (sf_s, sb_s), (qf_s, qb_s), (of_s, ob_s)
    end_row = (chunk - 1, 0)
    mid_row = chunk // 2

    def rows_of(i, off=0):
        return pl.ds(pl.multiple_of(off + i * chunk, chunk), chunk)

    def gate_pass(seg, bound):
        gsrc, _, _, _, _, _, length, off = seg

        def body(i, bound):
            rows, srows = rows_of(i), rows_of(i, off)
            if mode == 'hgrn2':
                lb = lb_ref[...]
                f_f = lb + (1.0 - lb) * jax.nn.sigmoid(gsrc[0][rows, :])
                f_b = lb + (1.0 - lb) * jax.nn.sigmoid(gsrc[1][rows, :])
                g_f, g_b = jnp.log(f_f), jnp.log(f_b)
                kf_s[srows, :] = 1.0 - f_f
                kb_s[srows, :] = 1.0 - f_b
            else:
                r = gsrc[0][rows, :].astype(BF16)
                z_f = jnp.dot(r, w2f_ref[...].astype(BF16), preferred_element_type=F32) + bf_ref[...]
                z_b = jnp.dot(r, w2b_ref[...].astype(BF16), preferred_element_type=F32) + bb_ref[...]
                g_f = _log_sigmoid(z_f) / GLA_GATE_NORM
                g_b = _log_sigmoid(z_b) / GLA_GATE_NORM
            c_f = _cumsum_rows(tris[0], g_f)
            c_b = _cumsum_rows(tris[1], g_b)
            cf_s[srows, :] = c_f
            cb_s[srows, :] = c_b
            for c in (c_f, c_b):
                mid = c[mid_row:mid_row + 1, :]
                spread = jnp.maximum(jnp.abs(c[0:1, :] - mid), jnp.abs(c[chunk - 1:chunk, :] - mid))
                bound = jnp.maximum(bound, spread)
            return bound

        return lax.fori_loop(0, length // chunk, body, bound, unroll=SCAN_UNROLL)

    def intra(seg, direction, i, exact):
        _, q_r, k_r, v_r, _, _, _, off = seg
        rows, srows = rows_of(i), rows_of(i, off)
        if mode == 'hgrn2':
            qq = jax.nn.silu(q_r[rows, :]) * HG_KEY ** -0.5
            k = (kf_s if direction == 0 else kb_s)[srows, :]
        else:
            qq = q_r[rows, :] * GLA_HK ** -0.5
            k = k_r[rows, :]
        cum = cum_s[direction][srows, :]
        e = end_row[direction]
        cum_end = cum[e:e + 1, :]
        q_in = (qq * jnp.exp(cum)).astype(BF16)
        qin_s[direction][srows, :] = q_in
        kout_s[direction][srows, :] = (k * jnp.exp(cum_end - cum)).astype(BF16)
        if exact:
            kt_s[...] = k
            gt_s[...] = cum

            def col_body(s, att):
                ks = kt_s[pl.ds(s, 1), :]
                gs = gt_s[pl.ds(s, 1), :]
                p = qq * ks * jnp.exp(jnp.minimum(cum - gs, 0.0))
                return att + jnp.where(col == s, jnp.sum(p, axis=-1, keepdims=True), 0.0)

            att = lax.fori_loop(0, chunk, col_body, jnp.zeros((chunk, chunk), F32))
        else:
            rel = cum - cum[mid_row:mid_row + 1, :]
            att = _dot_nt((qq * jnp.exp(rel)).astype(BF16), (k * jnp.exp(-rel)).astype(BF16))
        att = jnp.where(masks[direction], att, 0.0).astype(BF16)
        return jnp.dot(att, v_r[rows, :].astype(BF16), preferred_element_type=F32)

    def intra_pass(exact):
        for seg in segs:
            o_r, length = seg[5], seg[6]

            def body(i, carry):
                o_r[rows_of(i), :] = intra(seg, 0, i, exact) + intra(seg, 1, i, exact)
                return carry

            lax.fori_loop(0, length // chunk, body, 0, unroll=1 if exact else SCAN_UNROLL)

    def inter(seg, direction, i):
        v_r, off = seg[3], seg[7]
        rows, srows = rows_of(i), rows_of(i, off)
        s_ref = state_s[direction]
        state = s_ref[...]
        o = _dot_nt(qin_s[direction][srows, :], state.astype(BF16))
        e = pl.ds(pl.multiple_of(off + i * chunk, chunk) + end_row[direction], 1)
        decay = jnp.exp(cum_s[direction][e, :])
        s_ref[...] = state * decay + _dot_tn(v_r[rows, :].astype(BF16), kout_s[direction][srows, :])
        return o

    def finish(seg, i, o):
        o = o * lax.rsqrt(jnp.mean(o * o, axis=-1, keepdims=True) + NORM_EPS) * gain_ref[...]
        return o * jax.nn.silu(seg[4][rows_of(i), :])

    def inter_pass():
        sf_s[...] = jnp.zeros_like(sf_s)
        sb_s[...] = jnp.zeros_like(sb_s)
        for seg in segs:
            o_r, length = seg[5], seg[6]
            n = length // chunk

            def first_half(i, carry):
                j = n - 1 - i
                o_r[rows_of(i), :] += inter(seg, 0, i)
                o_r[rows_of(j), :] += inter(seg, 1, j)
                return carry

            def second_half(i, carry):
                j = n - 1 - i
                o_r[rows_of(i), :] = finish(seg, i, o_r[rows_of(i), :] + inter(seg, 0, i))
                o_r[rows_of(j), :] = finish(seg, j, o_r[rows_of(j), :] + inter(seg, 1, j))
                return carry

            lax.fori_loop(0, n // 2, first_half, 0, unroll=min(SCAN_UNROLL, n // 2))
            lax.fori_loop(n // 2, n, second_half, 0, unroll=min(SCAN_UNROLL, n // 2))

    bound = jnp.zeros((1, kdim), F32)
    for seg in segs:
        bound = gate_pass(seg, bound)
    fast = jnp.max(bound) <= SCAN_FAST_RANGE

    @pl.when(fast)
    def _():
        intra_pass(False)

    @pl.when(jnp.logical_not(fast))
    def _():
        intra_pass(True)

    inter_pass()


def _gla_scan_call(mode, nheads, kdim, vdim, lat_in, ctx_in, small_in, bsz, seq, ctx_len):
    def stream_spec(length, width, first, stride):
        return pl.BlockSpec((None, length, width), lambda b, h: (b, 0, first + stride * h))

    assert seq % (2 * SCAN_CHUNK) == 0 and ctx_len % (2 * SCAN_CHUNK) == 0
    in_specs = [stream_spec(seq, w, f, s) for _, w, f, s in lat_in]
    in_specs += [stream_spec(ctx_len, w, f, s) for _, w, f, s in ctx_in]
    in_specs += [pl.BlockSpec(blk, imap) for _, blk, imap in small_in]
    args = [a for a, _, _, _ in lat_in] + [a for a, _, _, _ in ctx_in] + [a for a, _, _ in small_in]
    tot = seq + ctx_len
    scratch = [pltpu.VMEM((tot, kdim), F32), pltpu.VMEM((tot, kdim), F32),
               pltpu.VMEM((vdim, kdim), F32), pltpu.VMEM((vdim, kdim), F32),
               pltpu.VMEM((SCAN_CHUNK, kdim), F32), pltpu.VMEM((SCAN_CHUNK, kdim), F32),
               pltpu.VMEM((tot, kdim), BF16), pltpu.VMEM((tot, kdim), BF16),
               pltpu.VMEM((tot, kdim), BF16), pltpu.VMEM((tot, kdim), BF16)]
    if mode == 'hgrn2':
        scratch += [pltpu.VMEM((tot, kdim), F32), pltpu.VMEM((tot, kdim), F32)]
    return pl.pallas_call(
        lambda *refs: _gla_scan_body(mode, seq, ctx_len, kdim, vdim, *refs),
        grid=(bsz, nheads),
        in_specs=in_specs,
        out_specs=[pl.BlockSpec((None, seq, vdim), lambda b, h: (b, 0, h)),
                   pl.BlockSpec((None, ctx_len, vdim), lambda b, h: (b, 0, h))],
        out_shape=[jax.ShapeDtypeStruct((bsz, seq, nheads * vdim), F32),
                   jax.ShapeDtypeStruct((bsz, ctx_len, nheads * vdim), F32)],
        scratch_shapes=scratch,
        compiler_params=pltpu.CompilerParams(dimension_semantics=("arbitrary", "arbitrary"),
                                             vmem_limit_bytes=VMEM_LIMIT_BYTES),
        name=f"{mode}_scan",
    )(*args)


def hgrn2_mixer_pallas(hx, hc, w_in, lb, onorm_g, w_out):
    bsz, seq, _ = hx.shape
    ctx_len = hc.shape[1]
    yx, yc = hx @ w_in, hc @ w_in
    nh = HG_HEADS
    streams = lambda y: [(y, HG_KEY, j * nh, 1) for j in range(5)]
    small = [(lb.reshape(1, D_MODEL), (1, HG_KEY), lambda b, h: (0, h)),
             (onorm_g.reshape(1, HG_VAL), (1, HG_VAL), lambda b, h: (0, 0))]
    ox, oc = _gla_scan_call('hgrn2', nh, HG_KEY, HG_VAL, streams(yx), streams(yc), small, bsz, seq, ctx_len)
    return ox @ w_out, oc @ w_out


def gla_mixer_pallas(hx, hc, w_in, gk_w2, gk_b, onorm_g, w_out):
    bsz, seq, _ = hx.shape
    ctx_len = hc.shape[1]
    rows = seq // GRID_W
    kd, vd, rk = GLA_KEY_DIM, GLA_VAL_DIM, GLA_GATE_RANK
    nh = GLA_HEADS
    w_in_p = jnp.pad(w_in, ((0, 0), (0, LANES - 2 * rk)))
    yx = raster_to_colmajor(hx, rows) @ w_in_p
    yc = hc @ w_in_p
    w2f = jnp.pad(gk_w2[0], ((0, LANES - rk), (0, 0)))
    w2b = jnp.pad(gk_w2[1], ((rk, LANES - 2 * rk), (0, 0)))
    streams = lambda y: [(y, GLA_HK, 0, 1), (y, GLA_HK, kd // GLA_HK, 1), (y, GLA_HV, 2 * kd // GLA_HV, 1),
                         (y, GLA_HV, (2 * kd + vd) // GLA_HV, 1), (y, LANES, (2 * kd + 2 * vd) // LANES, 0)]
    small = [(w2f, (LANES, GLA_HK), lambda b, h: (0, h)), (w2b, (LANES, GLA_HK), lambda b, h: (0, h)),
             (gk_b[0].reshape(1, kd), (1, GLA_HK), lambda b, h: (0, h)),
             (gk_b[1].reshape(1, kd), (1, GLA_HK), lambda b, h: (0, h)),
             (onorm_g.reshape(1, GLA_HV), (1, GLA_HV), lambda b, h: (0, 0))]
    ox, oc = _gla_scan_call('gla', nh, GLA_HK, GLA_HV, streams(yx), streams(yc), small, bsz, seq, ctx_len)
    return colmajor_to_raster(ox @ w_out, rows), oc @ w_out


MB_GROUP_W = MB_INNER // MB_GROUPS
SSD_CHUNK = 128
SSD_CONV_ROWS = 64


def _ssd_scan_body(seq, ctx_len, *refs):
    chunk = SSD_CHUNK
    gw, hpg = MB_GROUP_W, MB_HPG
    (x_x, b_x, c_x, z_x, cl_x, rt_x, x_c, b_c, c_c, z_c, cl_c, rt_c,
     wx_ref, wb_ref, wc_ref, bx_ref, bb_ref, bc_ref, dskip_ref, gain_ref,
     o_x, o_c, xf_s, xb_s, sf_s, sb_s, pad_s, xc_s, bc_s, cc_s) = refs
    segs = ((x_c, b_c, c_c, z_c, cl_c, rt_c, o_c, ctx_len, 0),
            (x_x, b_x, c_x, z_x, cl_x, rt_x, o_x, seq, ctx_len))
    xs_s, state_s = (xf_s, xb_s), (sf_s, sb_s)
    end_row = (chunk - 1, 0)

    def conv_silu(raw_ref, w_ref, bias_ref, dst_ref, length, off):
        width = raw_ref.shape[-1]
        halo = SUBLANES
        pad = MB_CONV // 2
        pad_s[0:halo, 0:width] = jnp.zeros((halo, width), F32)
        pad_s[pl.ds(halo + length, halo), 0:width] = jnp.zeros((halo, width), F32)

        crow = SSD_CONV_ROWS

        def stage(i, carry):
            src = pl.ds(pl.multiple_of(i * crow, crow), crow)
            pad_s[pl.ds(pl.multiple_of(halo + i * crow, SUBLANES), crow), 0:width] = raw_ref[src, :]
            return carry

        lax.fori_loop(0, length // crow, stage, 0)

        def body(i, carry):
            rows_in = crow + 2 * halo
            win = pad_s[pl.ds(pl.multiple_of(i * crow, crow), rows_in), 0:width]
            acc = jnp.broadcast_to(bias_ref[...], (crow, width))
            for j in range(MB_CONV):
                lo = halo - pad + j
                acc = acc + pltpu.roll(win, rows_in - lo, axis=0)[0:crow, :] * w_ref[j:j + 1, :]
            dst = pl.ds(pl.multiple_of(off + i * crow, crow), crow)
            dst_ref[dst, :] = jax.nn.silu(acc).astype(dst_ref.dtype)
            return carry

        lax.fori_loop(0, length // crow, body, 0)

    row = lax.broadcasted_iota(jnp.int32, (chunk, chunk), 0)
    col = lax.broadcasted_iota(jnp.int32, (chunk, chunk), 1)
    masks = (row >= col, row <= col)
    head_of_lane = lax.broadcasted_iota(jnp.int32, (1, gw), 1) // MB_HEADDIM

    def rows_of(i, off=0):
        return pl.ds(pl.multiple_of(off + i * chunk, chunk), chunk)

    def bcast_heads(c4):
        out = c4[:, hpg - 1:hpg]
        for hh in range(hpg - 2, -1, -1):
            out = jnp.where(head_of_lane == hh, c4[:, hh:hh + 1], out)
        return out

    def intra_pass():
        for seg in segs:
            _, _, _, _, cl_r, rt_r, o_r, length, off = seg

            def body(i, carry):
                rows, srows = rows_of(i), rows_of(i, off)
                cols, rt = cl_r[i], rt_r[i]
                x = xc_s[srows, :]
                cb = _dot_nt(cc_s[srows, :], bc_s[srows, :])
                xh = [jnp.where(head_of_lane == hh, x, 0.0).astype(BF16) for hh in range(hpg)]
                y = jnp.zeros((chunk, gw), F32)
                for d in range(2):
                    cum = cols[:, d * hpg:(d + 1) * hpg]
                    dt = cols[:, (2 + d) * hpg:(3 + d) * hpg]
                    e = end_row[d]
                    scale = jnp.exp(cum[e:e + 1, :] - cum) * dt
                    xs_s[d][srows, :] = (x * bcast_heads(scale)).astype(BF16)
                    for hh in range(hpg):
                        j = d * hpg + hh
                        seg_decay = jnp.exp(jnp.minimum(cols[:, j:j + 1] - rt[j:j + 1, :], 0.0))
                        w = jnp.where(masks[d], cb * seg_decay * rt[2 * hpg + j:2 * hpg + j + 1, :], 0.0)
                        y = y + jnp.dot(w.astype(BF16), xh[hh], preferred_element_type=F32)
                o_r[rows, :] = y
                return carry

            lax.fori_loop(0, length // chunk, body, 0, unroll=2)

    def inter(seg, d, i):
        cl_r, off = seg[4], seg[8]
        srows = rows_of(i, off)
        cum = cl_r[i][:, d * hpg:(d + 1) * hpg]
        state = state_s[d][...]
        y = jnp.dot(cc_s[srows, :], state.astype(BF16), preferred_element_type=F32)
        e = end_row[d]
        decay = bcast_heads(jnp.exp(cum[e:e + 1, :]))
        state_s[d][...] = state * decay + _dot_tn(bc_s[srows, :], xs_s[d][srows, :])
        return y * bcast_heads(jnp.exp(cum))

    def finish(seg, i, y):
        z_r, off = seg[3], seg[8]
        y = (y + dskip_ref[...] * xc_s[rows_of(i, off), :]) * jax.nn.silu(z_r[rows_of(i), :])
        return y * lax.rsqrt(jnp.mean(y * y, axis=-1, keepdims=True) + NORM_EPS) * gain_ref[...]

    def inter_pass():
        sf_s[...] = jnp.zeros_like(sf_s)
        sb_s[...] = jnp.zeros_like(sb_s)
        for seg in segs:
            o_r, length = seg[6], seg[7]
            n = length // chunk

            def first_half(i, carry):
                j = n - 1 - i
                o_r[rows_of(i), :] += inter(seg, 0, i)
                o_r[rows_of(j), :] += inter(seg, 1, j)
                return carry

            def second_half(i, carry):
                j = n - 1 - i
                o_r[rows_of(i), :] = finish(seg, i, o_r[rows_of(i), :] + inter(seg, 0, i))
                o_r[rows_of(j), :] = finish(seg, j, o_r[rows_of(j), :] + inter(seg, 1, j))
                return carry

            lax.fori_loop(0, n // 2, first_half, 0, unroll=2)
            lax.fori_loop(n // 2, n, second_half, 0, unroll=2)

    for seg in segs:
        conv_silu(seg[0], wx_ref, bx_ref, xc_s, seg[7], seg[8])
        conv_silu(seg[1], wb_ref, bb_ref, bc_s, seg[7], seg[8])
        conv_silu(seg[2], wc_ref, bc_ref, cc_s, seg[7], seg[8])
    intra_pass()
    inter_pass()


def _ssd_head_tables(dt_raw, dt_bias, a_log):
    bsz, length, _ = dt_raw.shape
    n = length // SSD_CHUNK
    a = -jnp.exp(a_log.astype(F32)).reshape(2, MB_GROUPS, MB_HPG)
    dt = jax.nn.softplus(dt_raw.astype(F32).reshape(bsz, length, 2, MB_GROUPS, MB_HPG)
                         + dt_bias.reshape(2, MB_GROUPS, MB_HPG))
    dta = (dt * a).reshape(bsz, n, SSD_CHUNK, 2, MB_GROUPS, MB_HPG)
    dtc = dt.reshape(bsz, n, SSD_CHUNK, 2, MB_GROUPS, MB_HPG)
    lower = jnp.tril(jnp.ones((SSD_CHUNK, SSD_CHUNK), F32))
    cum_f = jnp.einsum('ts,bnsgh->bntgh', lower, dta[:, :, :, 0], precision=lax.Precision.HIGHEST)
    cum_b = jnp.einsum('st,bnsgh->bntgh', lower, dta[:, :, :, 1], precision=lax.Precision.HIGHEST)
    cols = jnp.concatenate([cum_f, cum_b, dtc[:, :, :, 0], dtc[:, :, :, 1]], axis=-1)
    cols = cols.transpose(0, 3, 1, 2, 4)
    return cols, jnp.swapaxes(cols, -1, -2)


def mamba2_mixer_pallas(hx, hc, w_in, conv_w, conv_b, dt_bias, a_log, d_skip, norm_g, w_out):
    bsz, seq, _ = hx.shape
    ctx_len = hc.shape[1]
    chunk, gw = SSD_CHUNK, MB_GROUP_W
    assert seq % (2 * chunk) == 0 and ctx_len % (2 * chunk) == 0

    def project(h):
        z = h @ w_in[:, :MB_INNER]
        xbc = h @ w_in[:, MB_INNER:MB_INNER + MB_CONV_DIM]
        cols, rows_t = _ssd_head_tables(h @ w_in[:, MB_INNER + MB_CONV_DIM:], dt_bias, a_log)
        return z, xbc, cols, rows_t

    yx, xbcx, clx, rtx = project(hx)
    yc, xbcc, clc, rtc = project(hc)
    conv_b2 = conv_b.reshape(1, MB_CONV_DIM)

    def conv_specs(rows):
        bw = MB_INNER // MB_STATE
        return [pl.BlockSpec((rows, gw), lambda b, g: (0, g)),
                pl.BlockSpec((rows, MB_STATE), lambda b, g: (0, bw + g)),
                pl.BlockSpec((rows, MB_STATE), lambda b, g: (0, bw + MB_GROUPS + g))]

    def stream_specs(length):
        n = length // chunk
        bw = MB_INNER // MB_STATE
        return [pl.BlockSpec((None, length, gw), lambda b, g: (b, 0, g)),
                pl.BlockSpec((None, length, MB_STATE), lambda b, g: (b, 0, bw + g)),
                pl.BlockSpec((None, length, MB_STATE), lambda b, g: (b, 0, bw + MB_GROUPS + g)),
                pl.BlockSpec((None, length, gw), lambda b, g: (b, 0, g)),
                pl.BlockSpec((None, None, n, chunk, 4 * MB_HPG), lambda b, g: (b, g, 0, 0, 0)),
                pl.BlockSpec((None, None, n, 4 * MB_HPG, chunk), lambda b, g: (b, g, 0, 0, 0))]

    tot = seq + ctx_len
    ox, oc = pl.pallas_call(
        lambda *refs: _ssd_scan_body(seq, ctx_len, *refs),
        grid=(bsz, MB_GROUPS),
        in_specs=stream_specs(seq) + stream_specs(ctx_len) + conv_specs(MB_CONV) + conv_specs(1) + [
            pl.BlockSpec((1, gw), lambda b, g: (0, g)), pl.BlockSpec((1, gw), lambda b, g: (0, g))],
        out_specs=[pl.BlockSpec((None, seq, gw), lambda b, g: (b, 0, g)),
                   pl.BlockSpec((None, ctx_len, gw), lambda b, g: (b, 0, g))],
        out_shape=[jax.ShapeDtypeStruct((bsz, seq, MB_INNER), F32),
                   jax.ShapeDtypeStruct((bsz, ctx_len, MB_INNER), F32)],
        scratch_shapes=[pltpu.VMEM((tot, gw), BF16), pltpu.VMEM((tot, gw), BF16),
                        pltpu.VMEM((MB_STATE, gw), F32), pltpu.VMEM((MB_STATE, gw), F32),
                        pltpu.VMEM((seq + 2 * SUBLANES, gw), F32), pltpu.VMEM((tot, gw), F32),
                        pltpu.VMEM((tot, MB_STATE), BF16), pltpu.VMEM((tot, MB_STATE), BF16)],
        compiler_params=pltpu.CompilerParams(dimension_semantics=("arbitrary", "arbitrary"),
                                             vmem_limit_bytes=VMEM_LIMIT_BYTES),
        name="ssd_scan",
    )(xbcx, xbcx, xbcx, yx, clx, rtx, xbcc, xbcc, xbcc, yc, clc, rtc,
      conv_w, conv_w, conv_w, conv_b2, conv_b2, conv_b2,
      jnp.repeat(d_skip, MB_HEADDIM).reshape(1, MB_INNER), norm_g.reshape(1, MB_INNER))
    return ox @ w_out, oc @ w_out


HY_BLOCK = 256
HY_FREQ_ROWS = 64


def _split_bf16(a):
    hi = a.astype(BF16)
    return hi, (a - hi.astype(F32)).astype(BF16)


def _dot_split(w_hi, w_lo, x):
    x_hi, x_lo = _split_bf16(x)
    d = lambda a, b: jnp.dot(a, b, preferred_element_type=F32)
    return d(w_hi, x_hi) + d(w_hi, x_lo) + d(w_lo, x_hi)


def _hyena_dft_matrices():
    p = HY_BLOCK
    f = jnp.arange(p, dtype=F32)[:, None] + 0.5
    b = jnp.arange(p, dtype=F32)[None, :]
    k = jnp.round(f * b * 2.0).astype(jnp.int32) % (4 * p)
    ang = k.astype(F32) * (2.0 * math.pi / (4 * p))
    fwd = jnp.concatenate([jnp.cos(ang), -jnp.sin(ang)], axis=0)
    inv = jnp.concatenate([jnp.cos(ang).T, -jnp.sin(ang).T], axis=1) / p
    return fwd, inv


def _hyena_filter_spectra(hf, hb, fwd):
    length, d = hf.shape
    p = HY_BLOCK
    nb = length // p
    h2 = jnp.concatenate([jnp.zeros((1, d), F32), jnp.flip(hb[1:], axis=0), hf], axis=0)
    blocks = h2.reshape(2 * nb, p, d)
    spec = jnp.einsum('fj,kjd->kfd', fwd, blocks, precision=lax.Precision.HIGHEST)
    s_re, s_im = spec[:, :p], spec[:, p:]
    sign = jnp.where(jnp.arange(p) % 2 == 0, 1.0, -1.0).astype(F32)[None, :, None]
    return s_re[1:] - sign * s_im[:-1], s_im[1:] + sign * s_re[:-1]


def _hyena_conv_body(nb, conv_u, u_ref, x_ref, bias_ref, gre_ref, gim_ref, wfh_ref, wfl_ref, wih_ref, wil_ref,
                     cwu_ref, cbu_ref, cwx_ref, cbx_ref, o_ref, ure_s, uim_s, y_s, pad_s, uc_s, xc_s):
    p, fr = HY_BLOCK, HY_FREQ_ROWS
    length = nb * p

    def rows_of(i):
        return pl.ds(pl.multiple_of(i * p, p), p)

    def short_conv(raw_ref, w_ref, b_ref, dst_ref):
        halo, pad, crow = SUBLANES, HY_SHORT // 2, SSD_CONV_ROWS
        pad_s[0:halo, :] = jnp.zeros((halo, pad_s.shape[-1]), F32)
        pad_s[pl.ds(halo + length, halo), :] = jnp.zeros((halo, pad_s.shape[-1]), F32)

        def stage(i, carry):
            src = pl.ds(pl.multiple_of(i * crow, crow), crow)
            pad_s[pl.ds(pl.multiple_of(halo + i * crow, SUBLANES), crow), :] = raw_ref[src, :]
            return carry

        lax.fori_loop(0, length // crow, stage, 0)

        def body(i, carry):
            rows_in = crow + 2 * halo
            win = pad_s[pl.ds(pl.multiple_of(i * crow, crow), rows_in), :]
            acc = jnp.broadcast_to(b_ref[...], (crow, win.shape[-1]))
            for j in range(HY_SHORT):
                lo = halo - pad + j
                acc = acc + pltpu.roll(win, rows_in - lo, axis=0)[0:crow, :] * w_ref[j:j + 1, :]
            dst_ref[pl.ds(pl.multiple_of(i * crow, crow), crow), :] = acc
            return carry

        lax.fori_loop(0, length // crow, body, 0, unroll=2)

    if conv_u:
        short_conv(u_ref, cwu_ref, cbu_ref, uc_s)
    u_src = uc_s if conv_u else u_ref
    short_conv(x_ref, cwx_ref, cbx_ref, xc_s)

    def forward(j, carry):
        spec = _dot_split(wfh_ref[...], wfl_ref[...], u_src[rows_of(j), :])
        ure_s[j] = spec[:p]
        uim_s[j] = spec[p:]
        return carry

    lax.fori_loop(0, nb, forward, 0, unroll=min(nb, 2))

    def out_block(i, carry):
        for t in range(p // fr):
            r = pl.ds(t * fr, fr)

            def acc_body(j, acc):
                a_re, a_im = acc
                k = i - j + nb - 1
                g_re, g_im = gre_ref[k, r, :], gim_ref[k, r, :]
                u_re, u_im = ure_s[j, r, :], uim_s[j, r, :]
                return a_re + g_re * u_re - g_im * u_im, a_im + g_re * u_im + g_im * u_re

            zero = jnp.zeros((fr, u_ref.shape[-1]), F32)
            a_re, a_im = lax.fori_loop(0, nb, acc_body, (zero, zero), unroll=min(nb, 4))
            y_s[i, pl.ds(t * fr, fr), :] = a_re
            y_s[i, pl.ds(p + t * fr, fr), :] = a_im
        return carry

    lax.fori_loop(0, nb, out_block, 0)

    def inverse(i, carry):
        y = _dot_split(wih_ref[...], wil_ref[...], y_s[i])
        u = u_src[rows_of(i), :]
        o_ref[rows_of(i), :] = xc_s[rows_of(i), :] * (y + u * bias_ref[...])
        return carry

    lax.fori_loop(0, nb, inverse, 0, unroll=min(nb, 2))


def _hyena_conv(u_arr, u_blk, x_arr, x_blk, bias, g_re, g_im, mats, short_w, short_b, conv_u):
    bsz, length, _ = u_arr.shape
    d = bias.shape[-1]
    p = HY_BLOCK
    nb = length // p
    nseg = 2 * nb - 1
    const = lambda shape: pl.BlockSpec(shape, lambda c, b: (0, 0))
    cu_blk = u_blk if conv_u else x_blk
    return pl.pallas_call(
        lambda *refs: _hyena_conv_body(nb, conv_u, *refs),
        grid=(d // LANES, bsz),
        in_specs=[pl.BlockSpec((None, length, LANES), lambda c, b: (b, 0, u_blk + c)),
                  pl.BlockSpec((None, length, LANES), lambda c, b: (b, 0, x_blk + c)),
                  pl.BlockSpec((1, LANES), lambda c, b: (0, c)),
                  pl.BlockSpec((nseg, p, LANES), lambda c, b: (0, 0, c)),
                  pl.BlockSpec((nseg, p, LANES), lambda c, b: (0, 0, c)),
                  const((2 * p, p)), const((2 * p, p)), const((p, 2 * p)), const((p, 2 * p)),
                  pl.BlockSpec((HY_SHORT, LANES), lambda c, b: (0, cu_blk + c)),
                  pl.BlockSpec((1, LANES), lambda c, b: (0, cu_blk + c)),
                  pl.BlockSpec((HY_SHORT, LANES), lambda c, b: (0, x_blk + c)),
                  pl.BlockSpec((1, LANES), lambda c, b: (0, x_blk + c))],
        out_specs=pl.BlockSpec((None, length, LANES), lambda c, b: (b, 0, c)),
        out_shape=jax.ShapeDtypeStruct((bsz, length, d), F32),
        scratch_shapes=[pltpu.VMEM((nb, p, LANES), F32), pltpu.VMEM((nb, p, LANES), F32),
                        pltpu.VMEM((nb, 2 * p, LANES), F32),
                        pltpu.VMEM((length + 2 * SUBLANES, LANES), F32),
                        pltpu.VMEM((length, LANES), F32), pltpu.VMEM((length, LANES), F32)],
        compiler_params=pltpu.CompilerParams(dimension_semantics=("arbitrary", "arbitrary"),
                                             vmem_limit_bytes=VMEM_LIMIT_BYTES),
        name="hyena_conv",
    )(u_arr, x_arr, bias.reshape(1, d), g_re, g_im, *mats, short_w, short_b, short_w, short_b)


def hyena_mixer_pallas(hx, hc, w_in, b_in, short_w, short_b, f_w1, f_b1, f_w2, f_b2, f_w3, f_b3, f_w4,
                       f_freq, f_bias, w_out, b_out):
    fwd, inv = _hyena_dft_matrices()
    mats = _split_bf16(fwd) + _split_bf16(inv)
    nblk = D_MODEL // LANES

    def taps(length):
        z, t = hyena_pos_features(length)
        a = jnp.sin(f_freq * (z @ f_w1 + f_b1))
        a = jnp.sin(f_freq * (a @ f_w2 + f_b2))
        a = jnp.sin(f_freq * (a @ f_w3 + f_b3))
        mw = (a @ f_w4).astype(F32) * jnp.tile(hyena_window(t), (1, 2 * HY_ORDER))
        norm = jnp.sum(jnp.abs(mw), axis=0).reshape(HY_ORDER, 2, D_MODEL).sum(axis=1)
        return mw, 1.0 / norm

    def spectra(mw, inv_norm, o):
        hf = mw[:, (2 * o) * D_MODEL:(2 * o + 1) * D_MODEL]
        hb = mw[:, (2 * o + 1) * D_MODEL:(2 * o + 2) * D_MODEL]
        g_re, g_im = _hyena_filter_spectra(hf, hb, fwd)
        return g_re * inv_norm[o], g_im * inv_norm[o]

    def run(h):
        length = h.shape[1]
        raw = (h @ w_in + b_in).astype(F32)
        sb = short_b.reshape(1, -1)
        mw, inv_norm = taps(length)
        g0, g1 = spectra(mw, inv_norm, 0), spectra(mw, inv_norm, 1)
        z = _hyena_conv(raw, 2 * nblk, raw, 0, f_bias[0], g0[0], g0[1], mats, short_w, sb, True)
        z = _hyena_conv(z, 0, raw, nblk, f_bias[1], g1[0], g1[1], mats, short_w, sb, False)
        return z @ w_out + b_out

    return run(hx), run(hc)


MOE_ROW_TILE = 2048
MOE_COL_TILE = 512


def _moe_combine_body(t_len, idx_ref, gate_ref, ye_ref, x_ref, g2_ref, o_ref):
    e = pl.program_id(2)
    cap = idx_ref.shape[-1]
    rt = min(MOE_ROW_TILE, t_len)

    @pl.when(e == 0)
    def _():
        o_ref[...] = jnp.zeros_like(o_ref)

    yg = (ye_ref[...] * gate_ref[...]).astype(BF16)
    idx = idx_ref[...]

    def body(r, carry):
        r0 = pl.multiple_of(r * rt, rt)
        tok = lax.broadcasted_iota(jnp.int32, (rt, cap), 0) + r0
        onehot = jnp.where(tok == idx, 1.0, 0.0).astype(BF16)
        o_ref[pl.ds(r0, rt), :] += jnp.dot(onehot, yg, preferred_element_type=F32)
        return carry

    lax.fori_loop(0, t_len // rt, body, 0)

    @pl.when(e == pl.num_programs(2) - 1)
    def _():
        o_ref[...] = x_ref[...] + g2_ref[...] * o_ref[...]


def expert_choice_ffn_residual(x, g2, h, router_w, w_gate, w_up, w_down):
    bsz, t_len, d = h.shape
    cap = max(1, EC_CAPACITY_FACTOR * t_len // N_EXPERTS)
    aff = jax.nn.softmax(h.astype(F32) @ router_w.astype(F32), axis=-1)
    gate, idx = lax.top_k(jnp.swapaxes(aff, 1, 2), cap)
    xg = jax.vmap(lambda hb, ib: hb[ib])(h.astype(BF16), idx)
    pre_g = jnp.einsum('becd,edf->becf', xg, w_gate, preferred_element_type=BF16)
    pre_u = jnp.einsum('becd,edf->becf', xg, w_up, preferred_element_type=BF16)
    hid = (jax.nn.silu(pre_g.astype(F32)) * pre_u.astype(F32)).astype(BF16)
    ye = jnp.einsum('becf,efd->becd', hid, w_down, preferred_element_type=F32)
    ct = MOE_COL_TILE if t_len > MOE_ROW_TILE else d
    g2 = jnp.broadcast_to(g2.reshape(-1, 1, d), (bsz, 1, d))
    return pl.pallas_call(
        lambda *refs: _moe_combine_body(t_len, *refs),
        grid=(bsz, d // ct, N_EXPERTS),
        in_specs=[pl.BlockSpec((None, None, 1, cap), lambda b, c, e: (b, e, 0, 0)),
                  pl.BlockSpec((None, None, cap, 1), lambda b, c, e: (b, e, 0, 0)),
                  pl.BlockSpec((None, None, cap, ct), lambda b, c, e: (b, e, 0, c)),
                  pl.BlockSpec((None, t_len, ct), lambda b, c, e: (b, 0, c)),
                  pl.BlockSpec((None, 1, ct), lambda b, c, e: (b, 0, c))],
        out_specs=pl.BlockSpec((None, t_len, ct), lambda b, c, e: (b, 0, c)),
        out_shape=jax.ShapeDtypeStruct((bsz, t_len, d), F32),
        compiler_params=pltpu.CompilerParams(dimension_semantics=("arbitrary", "arbitrary", "arbitrary"),
                                             vmem_limit_bytes=VMEM_LIMIT_BYTES),
        name="moe_combine",
    )(idx.reshape(bsz, N_EXPERTS, 1, cap), gate.reshape(bsz, N_EXPERTS, cap, 1), ye, x, g2)


def _final_norm_body(x_ref, g_ref, o_ref):
    xf = x_ref[...]
    o_ref[...] = xf * lax.rsqrt(jnp.mean(xf * xf, axis=-1, keepdims=True) + NORM_EPS) * g_ref[...]


def final_rmsnorm(x, g):
    b, l, d = x.shape
    rows = b * l
    tm = 512
    out = pl.pallas_call(
        _final_norm_body,
        grid=(rows // tm,),
        in_specs=[pl.BlockSpec((tm, d), lambda i: (i, 0)), pl.BlockSpec((1, d), lambda i: (0, 0))],
        out_specs=pl.BlockSpec((tm, d), lambda i: (i, 0)),
        out_shape=jax.ShapeDtypeStruct((rows, d), F32),
    )(x.reshape(rows, d), g.reshape(1, d))
    return out.reshape(b, l, d)


def kernel(x, c, ctx, c_ctx, ada_w, ada_b, norm1_g, norm2_g,
           hg_w_in, hg_lb, hg_onorm_g, hg_w_out,
           hy_w_in, hy_b_in, hy_short_w, hy_short_b, hy_f_w1, hy_f_b1, hy_f_w2, hy_f_b2,
           hy_f_w3, hy_f_b3, hy_f_w4, hy_f_freq, hy_f_bias, hy_w_out, hy_b_out,
           mb_w_in, mb_conv_w, mb_conv_b, mb_dt_bias, mb_a_log, mb_d, mb_norm_g, mb_w_out,
           gla_w_in, gla_gk_w2, gla_gk_b, gla_onorm_g, gla_w_out,
           router_w, moe_w_gate, moe_w_up, moe_w_down, final_norm_g):
    for i in range(DEPTH):
        last = i == DEPTH - 1
        sh1, sc1, g1, sh2, sc2, g2 = adaln(c, ada_w[i], ada_b[i])
        csh1, csc1, cg1, csh2, csc2, cg2 = adaln(c_ctx, ada_w[i], ada_b[i])
        hx = modulate(rmsnorm(x, norm1_g[i]), sh1, sc1)
        hc = modulate(rmsnorm(ctx, norm1_g[i]), csh1, csc1)
        kind, j = i % N_MIXERS, i // N_MIXERS
        if kind == 0:
            yx, yc = hgrn2_mixer_pallas(hx, hc, hg_w_in[j], hgrn_lower_bound(hg_lb, i), hg_onorm_g[j], hg_w_out[j])
        elif kind == 1:
            yx, yc = hyena_mixer_pallas(hx, hc, hy_w_in[j], hy_b_in[j], hy_short_w[j], hy_short_b[j],
                                 hy_f_w1[j], hy_f_b1[j], hy_f_w2[j], hy_f_b2[j], hy_f_w3[j], hy_f_b3[j],
                                 hy_f_w4[j], hy_f_freq[j], hy_f_bias[j], hy_w_out[j], hy_b_out[j])
        elif kind == 2:
            yx, yc = mamba2_mixer_pallas(hx, hc, mb_w_in[j], mb_conv_w[j], mb_conv_b[j], mb_dt_bias[j],
                                  mb_a_log[j], mb_d[j], mb_norm_g[j], mb_w_out[j])
        else:
            yx, yc = gla_mixer_pallas(hx, hc, gla_w_in[j], gla_gk_w2[j], gla_gk_b[j], gla_onorm_g[j], gla_w_out[j])
        x = x + g1 * yx
        x = expert_choice_ffn_residual(x, g2, modulate(rmsnorm(x, norm2_g[i]), sh2, sc2),
                                       router_w[i], moe_w_gate[i], moe_w_up[i], moe_w_down[i])
        if not last:
            ctx = ctx + cg1 * yc
            ctx = expert_choice_ffn_residual(ctx, cg2, modulate(rmsnorm(ctx, norm2_g[i]), csh2, csc2),
                                             router_w[i], moe_w_gate[i], moe_w_up[i], moe_w_down[i])
    return final_rmsnorm(x, final_norm_g)
```

```python
import math
import jax, jax.numpy as jnp
from jax import lax
from jax.experimental import pallas as pl
from jax.experimental.pallas import tpu as pltpu

D_MODEL = 1024
BATCH = 8
SEQ = 4096
DEPTH = 4

F32 = jnp.float32
GRID_W = 64
CTX_LEN = 256
N_MIXERS = 4
NORM_EPS = 1e-6

HG_HEADS = 8
HG_KEY = D_MODEL // HG_HEADS
HG_VAL = D_MODEL // HG_HEADS
HG_CHUNK = 32

HY_ORDER = 2
HY_SHORT = 3
HY_EMB = 33
HY_FILTER_W = 64
HY_DECAY_TARGET = 1e-2
HY_DECAY_HI_PCT = 0.3
HY_DECAY_LO_PCT = 1.5

MB_INNER = 2 * D_MODEL
MB_HEADDIM = 64
MB_HEADS = MB_INNER // MB_HEADDIM
MB_GROUPS = 8
MB_HPG = MB_HEADS // MB_GROUPS
MB_STATE = 128
MB_CONV = 5
MB_CHUNK = 64
MB_CONV_DIM = MB_INNER + 2 * MB_GROUPS * MB_STATE
MB_IN = MB_INNER + MB_CONV_DIM + 2 * MB_HEADS

GLA_HEADS = 4
GLA_KEY_DIM = D_MODEL // 2
GLA_VAL_DIM = D_MODEL
GLA_HK = GLA_KEY_DIM // GLA_HEADS
GLA_HV = GLA_VAL_DIM // GLA_HEADS
GLA_GATE_RANK = 16
GLA_GATE_NORM = 16.0
GLA_CHUNK = 32
GLA_IN = 2 * GLA_KEY_DIM + 2 * GLA_VAL_DIM + 2 * GLA_GATE_RANK

N_EXPERTS = 16
EC_CAPACITY_FACTOR = 2
EXPERT_FF = 2048


def rmsnorm(x, g):
    xf = x.astype(F32)
    return xf * lax.rsqrt(jnp.mean(xf * xf, axis=-1, keepdims=True) + NORM_EPS) * g


def adaln(cond, w, b):
    m = jax.nn.silu(cond.astype(F32)) @ w + b
    return jnp.split(m[..., None, :], 6, axis=-1)


def modulate(h, shift, scale):
    return h * (1.0 + scale) + shift


def raster_to_colmajor(h, rows):
    b, l, d = h.shape
    return h.reshape(b, rows, GRID_W, d).transpose(0, 2, 1, 3).reshape(b, l, d)


def colmajor_to_raster(h, rows):
    b, l, d = h.shape
    return h.reshape(b, GRID_W, rows, d).transpose(0, 2, 1, 3).reshape(b, l, d)


def hgrn_lower_bound(lb_param, layer):
    return jnp.cumsum(jax.nn.softmax(lb_param.astype(F32), axis=0), axis=0)[layer]


def hyena_pos_features(length):
    t = jnp.linspace(0.0, 1.0, length, dtype=F32)[:, None]
    bands = (HY_EMB - 1) // 2
    f = jnp.linspace(1e-4, bands - 1, bands, dtype=F32)[None, :]
    w = 2.0 * math.pi * jnp.arange(length, dtype=F32)[:, None] / length
    return jnp.concatenate([t, jnp.cos(f * w), -jnp.sin(f * w)], axis=-1), t


def hyena_window(t):
    max_decay = math.log(HY_DECAY_TARGET) / HY_DECAY_HI_PCT
    min_decay = math.log(HY_DECAY_TARGET) / HY_DECAY_LO_PCT
    deltas = jnp.abs(jnp.linspace(min_decay, max_decay, D_MODEL, dtype=F32))
    return jnp.exp(-t * deltas[None, :])


BF16 = jnp.bfloat16
LANES = 128
SUBLANES = 8
VMEM_LIMIT_BYTES = 56 * 2 ** 20
SCAN_CHUNK = 128
SCAN_FAST_RANGE = 80.0
SCAN_UNROLL = 8


def _cumsum_rows(tri_bf16, g):
    hi = g.astype(BF16)
    r1 = g - hi.astype(F32)
    mid = r1.astype(BF16)
    lo = (r1 - mid.astype(F32)).astype(BF16)
    d = lambda a: jnp.dot(tri_bf16, a, preferred_element_type=F32)
    return d(hi) + d(mid) + d(lo)


def _log_sigmoid(z):
    return jnp.minimum(z, 0.0) - jnp.log1p(jnp.exp(-jnp.abs(z)))


def _dot_nt(a, b):
    return lax.dot_general(a, b, (((1,), (1,)), ((), ())), preferred_element_type=F32)


def _dot_tn(a, b):
    return lax.dot_general(a, b, (((0,), (0,)), ((), ())), preferred_element_type=F32)


def _gla_scan_body(mode, seq, ctx_len, kdim, vdim, *refs):
    chunk = SCAN_CHUNK
    if mode == 'hgrn2':
        (q_x, ff_x, fb_x, v_x, og_x, q_c, ff_c, fb_c, v_c, og_c, lb_ref, gain_ref,
         o_x, o_c, cf_s, cb_s, sf_s, sb_s, kt_s, gt_s, qf_s, qb_s, of_s, ob_s, kf_s, kb_s) = refs
    else:
        (q_x, k_x, v_x, og_x, r_x, q_c, k_c, v_c, og_c, r_c, w2f_ref, w2b_ref, bf_ref, bb_ref, gain_ref,
         o_x, o_c, cf_s, cb_s, sf_s, sb_s, kt_s, gt_s, qf_s, qb_s, of_s, ob_s) = refs
    if mode == 'hgrn2':
        segs = (((ff_c, fb_c), q_c, None, v_c, og_c, o_c, ctx_len, 0),
                ((ff_x, fb_x), q_x, None, v_x, og_x, o_x, seq, ctx_len))
    else:
        segs = (((r_c,), q_c, k_c, v_c, og_c, o_c, ctx_len, 0),
                ((r_x,), q_x, k_x, v_x, og_x, o_x, seq, ctx_len))

    row = lax.broadcasted_iota(jnp.int32, (chunk, chunk), 0)
    col = lax.broadcasted_iota(jnp.int32, (chunk, chunk), 1)
    masks = (row >= col, row <= col)
    tris = tuple(m.astype(BF16) for m in masks)
    cum_s, state_s, qin_s, kout_s = (cf_s, cb_s), (sf_s, sb_s), (qf_s, qb_s), (of_s, ob_s)
    end_row = (chunk - 1, 0)
    mid_row = chunk // 2

    def rows_of(i, off=0):
        return pl.ds(pl.multiple_of(off + i * chunk, chunk), chunk)

    def gate_pass(seg, bound):
        gsrc, _, _, _, _, _, length, off = seg

        def body(i, bound):
            rows, srows = rows_of(i), rows_of(i, off)
            if mode == 'hgrn2':
                lb = lb_ref[...]
                f_f = lb + (1.0 - lb) * jax.nn.sigmoid(gsrc[0][rows, :])
                f_b = lb + (1.0 - lb) * jax.nn.sigmoid(gsrc[1][rows, :])
                g_f, g_b = jnp.log(f_f), jnp.log(f_b)
                kf_s[srows, :] = 1.0 - f_f
                kb_s[srows, :] = 1.0 - f_b
            else:
                r = gsrc[0][rows, :].astype(BF16)
                z_f = jnp.dot(r, w2f_ref[...].astype(BF16), preferred_element_type=F32) + bf_ref[...]
                z_b = jnp.dot(r, w2b_ref[...].astype(BF16), preferred_element_type=F32) + bb_ref[...]
                g_f = _log_sigmoid(z_f) / GLA_GATE_NORM
                g_b = _log_sigmoid(z_b) / GLA_GATE_NORM
            c_f = _cumsum_rows(tris[0], g_f)
            c_b = _cumsum_rows(tris[1], g_b)
            cf_s[srows, :] = c_f
            cb_s[srows, :] = c_b
            for c in (c_f, c_b):
                mid = c[mid_row:mid_row + 1, :]
                spread = jnp.maximum(jnp.abs(c[0:1, :] - mid), jnp.abs(c[chunk - 1:chunk, :] - mid))
                bound = jnp.maximum(bound, spread)
            return bound

        return lax.fori_loop(0, length // chunk, body, bound, unroll=SCAN_UNROLL)

    def intra(seg, direction, i, exact):
        _, q_r, k_r, v_r, _, _, _, off = seg
        rows, srows = rows_of(i), rows_of(i, off)
        if mode == 'hgrn2':
            qq = jax.nn.silu(q_r[rows, :]) * HG_KEY ** -0.5
            k = (kf_s if direction == 0 else kb_s)[srows, :]
        else:
            qq = q_r[rows, :] * GLA_HK ** -0.5
            k = k_r[rows, :]
        cum = cum_s[direction][srows, :]
        e = end_row[direction]
        cum_end = cum[e:e + 1, :]
        q_in = (qq * jnp.exp(cum)).astype(BF16)
        qin_s[direction][srows, :] = q_in
        kout_s[direction][srows, :] = (k * jnp.exp(cum_end - cum)).astype(BF16)
        if exact:
            kt_s[...] = k
            gt_s[...] = cum

            def col_body(s, att):
                ks = kt_s[pl.ds(s, 1), :]
                gs = gt_s[pl.ds(s, 1), :]
                p = qq * ks * jnp.exp(jnp.minimum(cum - gs, 0.0))
                return att + jnp.where(col == s, jnp.sum(p, axis=-1, keepdims=True), 0.0)

            att = lax.fori_loop(0, chunk, col_body, jnp.zeros((chunk, chunk), F32))
        else:
            rel = cum - cum[mid_row:mid_row + 1, :]
            att = _dot_nt((qq * jnp.exp(rel)).astype(BF16), (k * jnp.exp(-rel)).astype(BF16))
        att = jnp.where(masks[direction], att, 0.0).astype(BF16)
        return jnp.dot(att, v_r[rows, :].astype(BF16), preferred_element_type=F32)

    def intra_pass(exact):
        for seg in segs:
            o_r, length = seg[5], seg[6]

            def body(i, carry):
                o_r[rows_of(i), :] = intra(seg, 0, i, exact) + intra(seg, 1, i, exact)
                return carry

            lax.fori_loop(0, length // chunk, body, 0, unroll=1 if exact else SCAN_UNROLL)

    def inter(seg, direction, i):
        v_r, off = seg[3], seg[7]
        rows, srows = rows_of(i), rows_of(i, off)
        s_ref = state_s[direction]
        state = s_ref[...]
        o = _dot_nt(qin_s[direction][srows, :], state.astype(BF16))
        e = pl.ds(pl.multiple_of(off + i * chunk, chunk) + end_row[direction], 1)
        decay = jnp.exp(cum_s[direction][e, :])
        s_ref[...] = state * decay + _dot_tn(v_r[rows, :].astype(BF16), kout_s[direction][srows, :])
        return o

    def finish(seg, i, o):
        o = o * lax.rsqrt(jnp.mean(o * o, axis=-1, keepdims=True) + NORM_EPS) * gain_ref[...]
        return o * jax.nn.silu(seg[4][rows_of(i), :])

    def inter_pass():
        sf_s[...] = jnp.zeros_like(sf_s)
        sb_s[...] = jnp.zeros_like(sb_s)
        for seg in segs:
            o_r, length = seg[5], seg[6]
            n = length // chunk

            def first_half(i, carry):
                j = n - 1 - i
                o_r[rows_of(i), :] += inter(seg, 0, i)
                o_r[rows_of(j), :] += inter(seg, 1, j)
                return carry

            def second_half(i, carry):
                j = n - 1 - i
                o_r[rows_of(i), :] = finish(seg, i, o_r[rows_of(i), :] + inter(seg, 0, i))
                o_r[rows_of(j), :] = finish(seg, j, o_r[rows_of(j), :] + inter(seg, 1, j))
                return carry

            lax.fori_loop(0, n // 2, first_half, 0, unroll=min(SCAN_UNROLL, n // 2))
            lax.fori_loop(n // 2, n, second_half, 0, unroll=min(SCAN_UNROLL, n // 2))

    bound = jnp.zeros((1, kdim), F32)
    for seg in segs:
        bound = gate_pass(seg, bound)
    fast = jnp.max(bound) <= SCAN_FAST_RANGE

    @pl.when(fast)
    def _():
        intra_pass(False)

    @pl.when(jnp.logical_not(fast))
    def _():
        intra_pass(True)

    inter_pass()


def _gla_scan_call(mode, nheads, kdim, vdim, lat_in, ctx_in, small_in, bsz, seq, ctx_len):
    def stream_spec(length, width, first, stride):
        return pl.BlockSpec((None, length, width), lambda b, h: (b, 0, first + stride * h))

    assert seq % (2 * SCAN_CHUNK) == 0 and ctx_len % (2 * SCAN_CHUNK) == 0
    in_specs = [stream_spec(seq, w, f, s) for _, w, f, s in lat_in]
    in_specs += [stream_spec(ctx_len, w, f, s) for _, w, f, s in ctx_in]
    in_specs += [pl.BlockSpec(blk, imap) for _, blk, imap in small_in]
    args = [a for a, _, _, _ in lat_in] + [a for a, _, _, _ in ctx_in] + [a for a, _, _ in small_in]
    tot = seq + ctx_len
    scratch = [pltpu.VMEM((tot, kdim), F32), pltpu.VMEM((tot, kdim), F32),
               pltpu.VMEM((vdim, kdim), F32), pltpu.VMEM((vdim, kdim), F32),
               pltpu.VMEM((SCAN_CHUNK, kdim), F32), pltpu.VMEM((SCAN_CHUNK, kdim), F32),
               pltpu.VMEM((tot, kdim), BF16), pltpu.VMEM((tot, kdim), BF16),
               pltpu.VMEM((tot, kdim), BF16), pltpu.VMEM((tot, kdim), BF16)]
    if mode == 'hgrn2':
        scratch += [pltpu.VMEM((tot, kdim), F32), pltpu.VMEM((tot, kdim), F32)]
    return pl.pallas_call(
        lambda *refs: _gla_scan_body(mode, seq, ctx_len, kdim, vdim, *refs),
        grid=(bsz, nheads),
        in_specs=in_specs,
        out_specs=[pl.BlockSpec((None, seq, vdim), lambda b, h: (b, 0, h)),
                   pl.BlockSpec((None, ctx_len, vdim), lambda b, h: (b, 0, h))],
        out_shape=[jax.ShapeDtypeStruct((bsz, seq, nheads * vdim), F32),
                   jax.ShapeDtypeStruct((bsz, ctx_len, nheads * vdim), F32)],
        scratch_shapes=scratch,
        compiler_params=pltpu.CompilerParams(dimension_semantics=("arbitrary", "arbitrary"),
                                             vmem_limit_bytes=VMEM_LIMIT_BYTES),
        name=f"{mode}_scan",
    )(*args)


def hgrn2_mixer_pallas(hx, hc, w_in, lb, onorm_g, w_out):
    bsz, seq, _ = hx.shape
    ctx_len = hc.shape[1]
    yx, yc = hx @ w_in, hc @ w_in
    nh = HG_HEADS
    streams = lambda y: [(y, HG_KEY, j * nh, 1) for j in range(5)]
    small = [(lb.reshape(1, D_MODEL), (1, HG_KEY), lambda b, h: (0, h)),
             (onorm_g.reshape(1, HG_VAL), (1, HG_VAL), lambda b, h: (0, 0))]
    ox, oc = _gla_scan_call('hgrn2', nh, HG_KEY, HG_VAL, streams(yx), streams(yc), small, bsz, seq, ctx_len)
    return ox @ w_out, oc @ w_out


def gla_mixer_pallas(hx, hc, w_in, gk_w2, gk_b, onorm_g, w_out):
    bsz, seq, _ = hx.shape
    ctx_len = hc.shape[1]
    rows = seq // GRID_W
    kd, vd, rk = GLA_KEY_DIM, GLA_VAL_DIM, GLA_GATE_RANK
    nh = GLA_HEADS
    w_in_p = jnp.pad(w_in, ((0, 0), (0, LANES - 2 * rk)))
    yx = raster_to_colmajor(hx, rows) @ w_in_p
    yc = hc @ w_in_p
    w2f = jnp.pad(gk_w2[0], ((0, LANES - rk), (0, 0)))
    w2b = jnp.pad(gk_w2[1], ((rk, LANES - 2 * rk), (0, 0)))
    streams = lambda y: [(y, GLA_HK, 0, 1), (y, GLA_HK, kd // GLA_HK, 1), (y, GLA_HV, 2 * kd // GLA_HV, 1),
                         (y, GLA_HV, (2 * kd + vd) // GLA_HV, 1), (y, LANES, (2 * kd + 2 * vd) // LANES, 0)]
    small = [(w2f, (LANES, GLA_HK), lambda b, h: (0, h)), (w2b, (LANES, GLA_HK), lambda b, h: (0, h)),
             (gk_b[0].reshape(1, kd), (1, GLA_HK), lambda b, h: (0, h)),
             (gk_b[1].reshape(1, kd), (1, GLA_HK), lambda b, h: (0, h)),
             (onorm_g.reshape(1, GLA_HV), (1, GLA_HV), lambda b, h: (0, 0))]
    ox, oc = _gla_scan_call('gla', nh, GLA_HK, GLA_HV, streams(yx), streams(yc), small, bsz, seq, ctx_len)
    return colmajor_to_raster(ox @ w_out, rows), oc @ w_out


MB_GROUP_W = MB_INNER // MB_GROUPS
SSD_CHUNK = 128
SSD_CONV_ROWS = 64


def _ssd_scan_body(seq, ctx_len, *refs):
    chunk = SSD_CHUNK
    gw, hpg = MB_GROUP_W, MB_HPG
    (x_x, b_x, c_x, z_x, cl_x, rt_x, x_c, b_c, c_c, z_c, cl_c, rt_c,
     wx_ref, wb_ref, wc_ref, bx_ref, bb_ref, bc_ref, dskip_ref, gain_ref,
     o_x, o_c, xf_s, xb_s, sf_s, sb_s, pad_s, xc_s, bc_s, cc_s) = refs
    segs = ((x_c, b_c, c_c, z_c, cl_c, rt_c, o_c, ctx_len, 0),
            (x_x, b_x, c_x, z_x, cl_x, rt_x, o_x, seq, ctx_len))
    xs_s, state_s = (xf_s, xb_s), (sf_s, sb_s)
    end_row = (chunk - 1, 0)

    def conv_silu(raw_ref, w_ref, bias_ref, dst_ref, length, off):
        width = raw_ref.shape[-1]
        halo = SUBLANES
        pad = MB_CONV // 2
        pad_s[0:halo, 0:width] = jnp.zeros((halo, width), F32)
        pad_s[pl.ds(halo + length, halo), 0:width] = jnp.zeros((halo, width), F32)

        crow = SSD_CONV_ROWS

        def stage(i, carry):
            src = pl.ds(pl.multiple_of(i * crow, crow), crow)
            pad_s[pl.ds(pl.multiple_of(halo + i * crow, SUBLANES), crow), 0:width] = raw_ref[src, :]
            return carry

        lax.fori_loop(0, length // crow, stage, 0)

        def body(i, carry):
            rows_in = crow + 2 * halo
            win = pad_s[pl.ds(pl.multiple_of(i * crow, crow), rows_in), 0:width]
            acc = jnp.broadcast_to(bias_ref[...], (crow, width))
            for j in range(MB_CONV):
                lo = halo - pad + j
                acc = acc + pltpu.roll(win, rows_in - lo, axis=0)[0:crow, :] * w_ref[j:j + 1, :]
            dst = pl.ds(pl.multiple_of(off + i * crow, crow), crow)
            dst_ref[dst, :] = jax.nn.silu(acc).astype(dst_ref.dtype)
            return carry

        lax.fori_loop(0, length // crow, body, 0)

    row = lax.broadcasted_iota(jnp.int32, (chunk, chunk), 0)
    col = lax.broadcasted_iota(jnp.int32, (chunk, chunk), 1)
    masks = (row >= col, row <= col)
    head_of_lane = lax.broadcasted_iota(jnp.int32, (1, gw), 1) // MB_HEADDIM

    def rows_of(i, off=0):
        return pl.ds(pl.multiple_of(off + i * chunk, chunk), chunk)

    def bcast_heads(c4):
        out = c4[:, hpg - 1:hpg]
        for hh in range(hpg - 2, -1, -1):
            out = jnp.where(head_of_lane == hh, c4[:, hh:hh + 1], out)
        return out

    def intra_pass():
        for seg in segs:
            _, _, _, _, cl_r, rt_r, o_r, length, off = seg

            def body(i, carry):
                rows, srows = rows_of(i), rows_of(i, off)
                cols, rt = cl_r[i], rt_r[i]
                x = xc_s[srows, :]
                cb = _dot_nt(cc_s[srows, :], bc_s[srows, :])
                xh = [jnp.where(head_of_lane == hh, x, 0.0).astype(BF16) for hh in range(hpg)]
                y = jnp.zeros((chunk, gw), F32)
                for d in range(2):
                    cum = cols[:, d * hpg:(d + 1) * hpg]
                    dt = cols[:, (2 + d) * hpg:(3 + d) * hpg]
                    e = end_row[d]
                    scale = jnp.exp(cum[e:e + 1, :] - cum) * dt
                    xs_s[d][srows, :] = (x * bcast_heads(scale)).astype(BF16)
                    for hh in range(hpg):
                        j = d * hpg + hh
                        seg_decay = jnp.exp(jnp.minimum(cols[:, j:j + 1] - rt[j:j + 1, :], 0.0))
                        w = jnp.where(masks[d], cb * seg_decay * rt[2 * hpg + j:2 * hpg + j + 1, :], 0.0)
                        y = y + jnp.dot(w.astype(BF16), xh[hh], preferred_element_type=F32)
                o_r[rows, :] = y
                return carry

            lax.fori_loop(0, length // chunk, body, 0, unroll=2)

    def inter(seg, d, i):
        cl_r, off = seg[4], seg[8]
        srows = rows_of(i, off)
        cum = cl_r[i][:, d * hpg:(d + 1) * hpg]
        state = state_s[d][...]
        y = jnp.dot(cc_s[srows, :], state.astype(BF16), preferred_element_type=F32)
        e = end_row[d]
        decay = bcast_heads(jnp.exp(cum[e:e + 1, :]))
        state_s[d][...] = state * decay + _dot_tn(bc_s[srows, :], xs_s[d][srows, :])
        return y * bcast_heads(jnp.exp(cum))

    def finish(seg, i, y):
        z_r, off = seg[3], seg[8]
        y = (y + dskip_ref[...] * xc_s[rows_of(i, off), :]) * jax.nn.silu(z_r[rows_of(i), :])
        return y * lax.rsqrt(jnp.mean(y * y, axis=-1, keepdims=True) + NORM_EPS) * gain_ref[...]

    def inter_pass():
        sf_s[...] = jnp.zeros_like(sf_s)
        sb_s[...] = jnp.zeros_like(sb_s)
        for seg in segs:
            o_r, length = seg[6], seg[7]
            n = length // chunk

            def first_half(i, carry):
                j = n - 1 - i
                o_r[rows_of(i), :] += inter(seg, 0, i)
                o_r[rows_of(j), :] += inter(seg, 1, j)
                return carry

            def second_half(i, carry):
                j = n - 1 - i
                o_r[rows_of(i), :] = finish(seg, i, o_r[rows_of(i), :] + inter(seg, 0, i))
                o_r[rows_of(j), :] = finish(seg, j, o_r[rows_of(j), :] + inter(seg, 1, j))
                return carry

            lax.fori_loop(0, n // 2, first_half, 0, unroll=2)
            lax.fori_loop(n // 2, n, second_half, 0, unroll=2)

    for seg in segs:
        conv_silu(seg[0], wx_ref, bx_ref, xc_s, seg[7], seg[8])
        conv_silu(seg[1], wb_ref, bb_ref, bc_s, seg[7], seg[8])
        conv_silu(seg[2], wc_ref, bc_ref, cc_s, seg[7], seg[8])
    intra_pass()
    inter_pass()


def _ssd_head_tables(dt_raw, dt_bias, a_log):
    bsz, length, _ = dt_raw.shape
    n = length // SSD_CHUNK
    a = -jnp.exp(a_log.astype(F32)).reshape(2, MB_GROUPS, MB_HPG)
    dt = jax.nn.softplus(dt_raw.astype(F32).reshape(bsz, length, 2, MB_GROUPS, MB_HPG)
                         + dt_bias.reshape(2, MB_GROUPS, MB_HPG))
    dta = (dt * a).reshape(bsz, n, SSD_CHUNK, 2, MB_GROUPS, MB_HPG)
    dtc = dt.reshape(bsz, n, SSD_CHUNK, 2, MB_GROUPS, MB_HPG)
    lower = jnp.tril(jnp.ones((SSD_CHUNK, SSD_CHUNK), F32))
    cum_f = jnp.einsum('ts,bnsgh->bntgh', lower, dta[:, :, :, 0], precision=lax.Precision.HIGHEST)
    cum_b = jnp.einsum('st,bnsgh->bntgh', lower, dta[:, :, :, 1], precision=lax.Precision.HIGHEST)
    cols = jnp.concatenate([cum_f, cum_b, dtc[:, :, :, 0], dtc[:, :, :, 1]], axis=-1)
    cols = cols.transpose(0, 3, 1, 2, 4)
    return cols, jnp.swapaxes(cols, -1, -2)


def mamba2_mixer_pallas(hx, hc, w_in, conv_w, conv_b, dt_bias, a_log, d_skip, norm_g, w_out):
    bsz, seq, _ = hx.shape
    ctx_len = hc.shape[1]
    chunk, gw = SSD_CHUNK, MB_GROUP_W
    assert seq % (2 * chunk) == 0 and ctx_len % (2 * chunk) == 0

    def project(h):
        z = h @ w_in[:, :MB_INNER]
        xbc = h @ w_in[:, MB_INNER:MB_INNER + MB_CONV_DIM]
        cols, rows_t = _ssd_head_tables(h @ w_in[:, MB_INNER + MB_CONV_DIM:], dt_bias, a_log)
        return z, xbc, cols, rows_t

    yx, xbcx, clx, rtx = project(hx)
    yc, xbcc, clc, rtc = project(hc)
    conv_b2 = conv_b.reshape(1, MB_CONV_DIM)

    def conv_specs(rows):
        bw = MB_INNER // MB_STATE
        return [pl.BlockSpec((rows, gw), lambda b, g: (0, g)),
                pl.BlockSpec((rows, MB_STATE), lambda b, g: (0, bw + g)),
                pl.BlockSpec((rows, MB_STATE), lambda b, g: (0, bw + MB_GROUPS + g))]

    def stream_specs(length):
        n = length // chunk
        bw = MB_INNER // MB_STATE
        return [pl.BlockSpec((None, length, gw), lambda b, g: (b, 0, g)),
                pl.BlockSpec((None, length, MB_STATE), lambda b, g: (b, 0, bw + g)),
                pl.BlockSpec((None, length, MB_STATE), lambda b, g: (b, 0, bw + MB_GROUPS + g)),
                pl.BlockSpec((None, length, gw), lambda b, g: (b, 0, g)),
                pl.BlockSpec((None, None, n, chunk, 4 * MB_HPG), lambda b, g: (b, g, 0, 0, 0)),
                pl.BlockSpec((None, None, n, 4 * MB_HPG, chunk), lambda b, g: (b, g, 0, 0, 0))]

    tot = seq + ctx_len
    ox, oc = pl.pallas_call(
        lambda *refs: _ssd_scan_body(seq, ctx_len, *refs),
        grid=(bsz, MB_GROUPS),
        in_specs=stream_specs(seq) + stream_specs(ctx_len) + conv_specs(MB_CONV) + conv_specs(1) + [
            pl.BlockSpec((1, gw), lambda b, g: (0, g)), pl.BlockSpec((1, gw), lambda b, g: (0, g))],
        out_specs=[pl.BlockSpec((None, seq, gw), lambda b, g: (b, 0, g)),
                   pl.BlockSpec((None, ctx_len, gw), lambda b, g: (b, 0, g))],
        out_shape=[jax.ShapeDtypeStruct((bsz, seq, MB_INNER), F32),
                   jax.ShapeDtypeStruct((bsz, ctx_len, MB_INNER), F32)],
        scratch_shapes=[pltpu.VMEM((tot, gw), BF16), pltpu.VMEM((tot, gw), BF16),
                        pltpu.VMEM((MB_STATE, gw), F32), pltpu.VMEM((MB_STATE, gw), F32),
                        pltpu.VMEM((seq + 2 * SUBLANES, gw), F32), pltpu.VMEM((tot, gw), F32),
                        pltpu.VMEM((tot, MB_STATE), BF16), pltpu.VMEM((tot, MB_STATE), BF16)],
        compiler_params=pltpu.CompilerParams(dimension_semantics=("arbitrary", "arbitrary"),
                                             vmem_limit_bytes=VMEM_LIMIT_BYTES),
        name="ssd_scan",
    )(xbcx, xbcx, xbcx, yx, clx, rtx, xbcc, xbcc, xbcc, yc, clc, rtc,
      conv_w, conv_w, conv_w, conv_b2, conv_b2, conv_b2,
      jnp.repeat(d_skip, MB_HEADDIM).reshape(1, MB_INNER), norm_g.reshape(1, MB_INNER))
    return ox @ w_out, oc @ w_out


HY_BLOCK = 256
HY_FREQ_ROWS = 64


def _split_bf16(a):
    hi = a.astype(BF16)
    return hi, (a - hi.astype(F32)).astype(BF16)


def _dot_split(w_hi, w_lo, x):
    x_hi, x_lo = _split_bf16(x)
    d = lambda a, b: jnp.dot(a, b, preferred_element_type=F32)
    return d(w_hi, x_hi) + d(w_hi, x_lo) + d(w_lo, x_hi)


def _hyena_dft_matrices():
    p = HY_BLOCK
    f = jnp.arange(p, dtype=F32)[:, None] + 0.5
    b = jnp.arange(p, dtype=F32)[None, :]
    k = jnp.round(f * b * 2.0).astype(jnp.int32) % (4 * p)
    ang = k.astype(F32) * (2.0 * math.pi / (4 * p))
    fwd = jnp.concatenate([jnp.cos(ang), -jnp.sin(ang)], axis=0)
    inv = jnp.concatenate([jnp.cos(ang).T, -jnp.sin(ang).T], axis=1) / p
    return fwd, inv


def _hyena_filter_spectra(hf, hb, fwd):
    length, d = hf.shape
    p = HY_BLOCK
    nb = length // p
    h2 = jnp.concatenate([jnp.zeros((1, d), F32), jnp.flip(hb[1:], axis=0), hf], axis=0)
    blocks = h2.reshape(2 * nb, p, d)
    spec = jnp.einsum('fj,kjd->kfd', fwd, blocks, precision=lax.Precision.HIGHEST)
    s_re, s_im = spec[:, :p], spec[:, p:]
    sign = jnp.where(jnp.arange(p) % 2 == 0, 1.0, -1.0).astype(F32)[None, :, None]
    return s_re[1:] - sign * s_im[:-1], s_im[1:] + sign * s_re[:-1]


def _hyena_conv_body(nb, conv_u, u_ref, x_ref, bias_ref, gre_ref, gim_ref, wfh_ref, wfl_ref, wih_ref, wil_ref,
                     cwu_ref, cbu_ref, cwx_ref, cbx_ref, o_ref, ure_s, uim_s, y_s, pad_s, uc_s, xc_s):
    p, fr = HY_BLOCK, HY_FREQ_ROWS
    length = nb * p

    def rows_of(i):
        return pl.ds(pl.multiple_of(i * p, p), p)

    def short_conv(raw_ref, w_ref, b_ref, dst_ref):
        halo, pad, crow = SUBLANES, HY_SHORT // 2, SSD_CONV_ROWS
        pad_s[0:halo, :] = jnp.zeros((halo, pad_s.shape[-1]), F32)
        pad_s[pl.ds(halo + length, halo), :] = jnp.zeros((halo, pad_s.shape[-1]), F32)

        def stage(i, carry):
            src = pl.ds(pl.multiple_of(i * crow, crow), crow)
            pad_s[pl.ds(pl.multiple_of(halo + i * crow, SUBLANES), crow), :] = raw_ref[src, :]
            return carry

        lax.fori_loop(0, length // crow, stage, 0)

        def body(i, carry):
            rows_in = crow + 2 * halo
            win = pad_s[pl.ds(pl.multiple_of(i * crow, crow), rows_in), :]
            acc = jnp.broadcast_to(b_ref[...], (crow, win.shape[-1]))
            for j in range(HY_SHORT):
                lo = halo - pad + j
                acc = acc + pltpu.roll(win, rows_in - lo, axis=0)[0:crow, :] * w_ref[j:j + 1, :]
            dst_ref[pl.ds(pl.multiple_of(i * crow, crow), crow), :] = acc
            return carry

        lax.fori_loop(0, length // crow, body, 0, unroll=2)

    if conv_u:
        short_conv(u_ref, cwu_ref, cbu_ref, uc_s)
    u_src = uc_s if conv_u else u_ref
    short_conv(x_ref, cwx_ref, cbx_ref, xc_s)

    def forward(j, carry):
        spec = _dot_split(wfh_ref[...], wfl_ref[...], u_src[rows_of(j), :])
        ure_s[j] = spec[:p]
        uim_s[j] = spec[p:]
        return carry

    lax.fori_loop(0, nb, forward, 0, unroll=min(nb, 2))

    def out_block(i, carry):
        for t in range(p // fr):
            r = pl.ds(t * fr, fr)

            def acc_body(j, acc):
                a_re, a_im = acc
                k = i - j + nb - 1
                g_re, g_im = gre_ref[k, r, :], gim_ref[k, r, :]
                u_re, u_im = ure_s[j, r, :], uim_s[j, r, :]
                return a_re + g_re * u_re - g_im * u_im, a_im + g_re * u_im + g_im * u_re

            zero = jnp.zeros((fr, u_ref.shape[-1]), F32)
            a_re, a_im = lax.fori_loop(0, nb, acc_body, (zero, zero), unroll=min(nb, 4))
            y_s[i, pl.ds(t * fr, fr), :] = a_re
            y_s[i, pl.ds(p + t * fr, fr), :] = a_im
        return carry

    lax.fori_loop(0, nb, out_block, 0)

    def inverse(i, carry):
        y = _dot_split(wih_ref[...], wil_ref[...], y_s[i])
        u = u_src[rows_of(i), :]
        o_ref[rows_of(i), :] = xc_s[rows_of(i), :] * (y + u * bias_ref[...])
        return carry

    lax.fori_loop(0, nb, inverse, 0, unroll=min(nb, 2))


def _hyena_conv(u_arr, u_blk, x_arr, x_blk, bias, g_re, g_im, mats, short_w, short_b, conv_u):
    bsz, length, _ = u_arr.shape
    d = bias.shape[-1]
    p = HY_BLOCK
    nb = length // p
    nseg = 2 * nb - 1
    const = lambda shape: pl.BlockSpec(shape, lambda c, b: (0, 0))
    cu_blk = u_blk if conv_u else x_blk
    return pl.pallas_call(
        lambda *refs: _hyena_conv_body(nb, conv_u, *refs),
        grid=(d // LANES, bsz),
        in_specs=[pl.BlockSpec((None, length, LANES), lambda c, b: (b, 0, u_blk + c)),
                  pl.BlockSpec((None, length, LANES), lambda c, b: (b, 0, x_blk + c)),
                  pl.BlockSpec((1, LANES), lambda c, b: (0, c)),
                  pl.BlockSpec((nseg, p, LANES), lambda c, b: (0, 0, c)),
                  pl.BlockSpec((nseg, p, LANES), lambda c, b: (0, 0, c)),
                  const((2 * p, p)), const((2 * p, p)), const((p, 2 * p)), const((p, 2 * p)),
                  pl.BlockSpec((HY_SHORT, LANES), lambda c, b: (0, cu_blk + c)),
                  pl.BlockSpec((1, LANES), lambda c, b: (0, cu_blk + c)),
                  pl.BlockSpec((HY_SHORT, LANES), lambda c, b: (0, x_blk + c)),
                  pl.BlockSpec((1, LANES), lambda c, b: (0, x_blk + c))],
        out_specs=pl.BlockSpec((None, length, LANES), lambda c, b: (b, 0, c)),
        out_shape=jax.ShapeDtypeStruct((bsz, length, d), F32),
        scratch_shapes=[pltpu.VMEM((nb, p, LANES), F32), pltpu.VMEM((nb, p, LANES), F32),
                        pltpu.VMEM((nb, 2 * p, LANES), F32),
                        pltpu.VMEM((length + 2 * SUBLANES, LANES), F32),
                        pltpu.VMEM((length, LANES), F32), pltpu.VMEM((length, LANES), F32)],
        compiler_params=pltpu.CompilerParams(dimension_semantics=("arbitrary", "arbitrary"),
                                             vmem_limit_bytes=VMEM_LIMIT_BYTES),
        name="hyena_conv",
    )(u_arr, x_arr, bias.reshape(1, d), g_re, g_im, *mats, short_w, short_b, short_w, short_b)


def hyena_mixer_pallas(hx, hc, w_in, b_in, short_w, short_b, f_w1, f_b1, f_w2, f_b2, f_w3, f_b3, f_w4,
                       f_freq, f_bias, w_out, b_out):
    fwd, inv = _hyena_dft_matrices()
    mats = _split_bf16(fwd) + _split_bf16(inv)
    nblk = D_MODEL // LANES

    def taps(length):
        z, t = hyena_pos_features(length)
        a = jnp.sin(f_freq * (z @ f_w1 + f_b1))
        a = jnp.sin(f_freq * (a @ f_w2 + f_b2))
        a = jnp.sin(f_freq * (a @ f_w3 + f_b3))
        mw = (a @ f_w4).astype(F32) * jnp.tile(hyena_window(t), (1, 2 * HY_ORDER))
        norm = jnp.sum(jnp.abs(mw), axis=0).reshape(HY_ORDER, 2, D_MODEL).sum(axis=1)
        return mw, 1.0 / norm

    def spectra(mw, inv_norm, o):
        hf = mw[:, (2 * o) * D_MODEL:(2 * o + 1) * D_MODEL]
        hb = mw[:, (2 * o + 1) * D_MODEL:(2 * o + 2) * D_MODEL]
        g_re, g_im = _hyena_filter_spectra(hf, hb, fwd)
        return g_re * inv_norm[o], g_im * inv_norm[o]

    def run(h):
        length = h.shape[1]
        raw = (h @ w_in + b_in).astype(F32)
        sb = short_b.reshape(1, -1)
        mw, inv_norm = taps(length)
        g0, g1 = spectra(mw, inv_norm, 0), spectra(mw, inv_norm, 1)
        z = _hyena_conv(raw, 2 * nblk, raw, 0, f_bias[0], g0[0], g0[1], mats, short_w, sb, True)
        z = _hyena_conv(z, 0, raw, nblk, f_bias[1], g1[0], g1[1], mats, short_w, sb, False)
        return z @ w_out + b_out

    return run(hx), run(hc)


MOE_ROW_TILE = 2048
MOE_COL_TILE = 512


def _moe_combine_body(t_len, idx_ref, gate_ref, ye_ref, x_ref, g2_ref, o_ref):
    e = pl.program_id(2)
    cap = idx_ref.shape[-1]
    rt = min(MOE_ROW_TILE, t_len)

    @pl.when(e == 0)
    def _():
        o_ref[...] = jnp.zeros_like(o_ref)

    yg = (ye_ref[...] * gate_ref[...]).astype(BF16)
    idx = idx_ref[...]

    def body(r, carry):
        r0 = pl.multiple_of(r * rt, rt)
        tok = lax.broadcasted_iota(jnp.int32, (rt, cap), 0) + r0
        onehot = jnp.where(tok == idx, 1.0, 0.0).astype(BF16)
        o_ref[pl.ds(r0, rt), :] += jnp.dot(onehot, yg, preferred_element_type=F32)
        return carry

    lax.fori_loop(0, t_len // rt, body, 0)

    @pl.when(e == pl.num_programs(2) - 1)
    def _():
        o_ref[...] = x_ref[...] + g2_ref[...] * o_ref[...]


def expert_choice_ffn_residual(x, g2, h, router_w, w_gate, w_up, w_down):
    bsz, t_len, d = h.shape
    cap = max(1, EC_CAPACITY_FACTOR * t_len // N_EXPERTS)
    aff = jax.nn.softmax(h.astype(F32) @ router_w.astype(F32), axis=-1)
    gate, idx = lax.top_k(jnp.swapaxes(aff, 1, 2), cap)
    xg = jax.vmap(lambda hb, ib: hb[ib])(h.astype(BF16), idx)
    pre_g = jnp.einsum('becd,edf->becf', xg, w_gate, preferred_element_type=BF16)
    pre_u = jnp.einsum('becd,edf->becf', xg, w_up, preferred_element_type=BF16)
    hid = (jax.nn.silu(pre_g.astype(F32)) * pre_u.astype(F32)).astype(BF16)
    ye = jnp.einsum('becf,efd->becd', hid, w_down, preferred_element_type=F32)
    ct = MOE_COL_TILE if t_len > MOE_ROW_TILE else d
    g2 = jnp.broadcast_to(g2.reshape(-1, 1, d), (bsz, 1, d))
    return pl.pallas_call(
        lambda *refs: _moe_combine_body(t_len, *refs),
        grid=(bsz, d // ct, N_EXPERTS),
        in_specs=[pl.BlockSpec((None, None, 1, cap), lambda b, c, e: (b, e, 0, 0)),
                  pl.BlockSpec((None, None, cap, 1), lambda b, c, e: (b, e, 0, 0)),
                  pl.BlockSpec((None, None, cap, ct), lambda b, c, e: (b, e, 0, c)),
                  pl.BlockSpec((None, t_len, ct), lambda b, c, e: (b, 0, c)),
                  pl.BlockSpec((None, 1, ct), lambda b, c, e: (b, 0, c))],
        out_specs=pl.BlockSpec((None, t_len, ct), lambda b, c, e: (b, 0, c)),
        out_shape=jax.ShapeDtypeStruct((bsz, t_len, d), F32),
        compiler_params=pltpu.CompilerParams(dimension_semantics=("arbitrary", "arbitrary", "arbitrary"),
                                             vmem_limit_bytes=VMEM_LIMIT_BYTES),
        name="moe_combine",
    )(idx.reshape(bsz, N_EXPERTS, 1, cap), gate.reshape(bsz, N_EXPERTS, cap, 1), ye, x, g2)


def _final_norm_body(x_ref, g_ref, o_ref):
    xf = x_ref[...]
    o_ref[...] = xf * lax.rsqrt(jnp.mean(xf * xf, axis=-1, keepdims=True) + NORM_EPS) * g_ref[...]


def final_rmsnorm(x, g):
    b, l, d = x.shape
    rows = b * l
    tm = 512
    out = pl.pallas_call(
        _final_norm_body,
        grid=(rows // tm,),
        in_specs=[pl.BlockSpec((tm, d), lambda i: (i, 0)), pl.BlockSpec((1, d), lambda i: (0, 0))],
        out_specs=pl.BlockSpec((tm, d), lambda i: (i, 0)),
        out_shape=jax.ShapeDtypeStruct((rows, d), F32),
    )(x.reshape(rows, d), g.reshape(1, d))
    return out.reshape(b, l, d)


def kernel(x, c, ctx, c_ctx, ada_w, ada_b, norm1_g, norm2_g,
           hg_w_in, hg_lb, hg_onorm_g, hg_w_out,
           hy_w_in, hy_b_in, hy_short_w, hy_short_b, hy_f_w1, hy_f_b1, hy_f_w2, hy_f_b2,
           hy_f_w3, hy_f_b3, hy_f_w4, hy_f_freq, hy_f_bias, hy_w_out, hy_b_out,
           mb_w_in, mb_conv_w, mb_conv_b, mb_dt_bias, mb_a_log, mb_d, mb_norm_g, mb_w_out,
           gla_w_in, gla_gk_w2, gla_gk_b, gla_onorm_g, gla_w_out,
           router_w, moe_w_gate, moe_w_up, moe_w_down, final_norm_g):
    for i in range(DEPTH):
        last = i == DEPTH - 1
        sh1, sc1, g1, sh2, sc2, g2 = adaln(c, ada_w[i], ada_b[i])
        csh1, csc1, cg1, csh2, csc2, cg2 = adaln(c_ctx, ada_w[i], ada_b[i])
        hx = modulate(rmsnorm(x, norm1_g[i]), sh1, sc1)
        hc = modulate(rmsnorm(ctx, norm1_g[i]), csh1, csc1)
        kind, j = i % N_MIXERS, i // N_MIXERS
        if kind == 0:
            yx, yc = hgrn2_mixer_pallas(hx, hc, hg_w_in[j], hgrn_lower_bound(hg_lb, i), hg_onorm_g[j], hg_w_out[j])
        elif kind == 1:
            yx, yc = hyena_mixer_pallas(hx, hc, hy_w_in[j], hy_b_in[j], hy_short_w[j], hy_short_b[j],
                                 hy_f_w1[j], hy_f_b1[j], hy_f_w2[j], hy_f_b2[j], hy_f_w3[j], hy_f_b3[j],
                                 hy_f_w4[j], hy_f_freq[j], hy_f_bias[j], hy_w_out[j], hy_b_out[j])
        elif kind == 2:
            yx, yc = mamba2_mixer_pallas(hx, hc, mb_w_in[j], mb_conv_w[j], mb_conv_b[j], mb_dt_bias[j],
                                  mb_a_log[j], mb_d[j], mb_norm_g[j], mb_w_out[j])
        else:
            yx, yc = gla_mixer_pallas(hx, hc, gla_w_in[j], gla_gk_w2[j], gla_gk_b[j], gla_onorm_g[j], gla_w_out[j])
        x = x + g1 * yx
        x = expert_choice_ffn_residual(x, g2, modulate(rmsnorm(x, norm2_g[i]), sh2, sc2),
                                       router_w[i], moe_w_gate[i], moe_w_up[i], moe_w_down[i])
        if not last:
            ctx = ctx + cg1 * yc
            ctx = expert_choice_ffn_residual(ctx, cg2, modulate(rmsnorm(ctx, norm2_g[i]), csh2, csc2),
                                             router_w[i], moe_w_gate[i], moe_w_up[i], moe_w_down[i])
    return final_rmsnorm(x, final_norm_g)
```

```python
import math
import jax, jax.numpy as jnp
from jax import lax
from jax.experimental import pallas as pl
from jax.experimental.pallas import tpu as pltpu

D_MODEL = 1024
BATCH = 8
SEQ = 4096
DEPTH = 4

F32 = jnp.float32
GRID_W = 64
CTX_LEN = 256
N_MIXERS = 4
NORM_EPS = 1e-6

HG_HEADS = 8
HG_KEY = D_MODEL // HG_HEADS
HG_VAL = D_MODEL // HG_HEADS
HG_CHUNK = 32

HY_ORDER = 2
HY_SHORT = 3
HY_EMB = 33
HY_FILTER_W = 64
HY_DECAY_TARGET = 1e-2
HY_DECAY_HI_PCT = 0.3
HY_DECAY_LO_PCT = 1.5

MB_INNER = 2 * D_MODEL
MB_HEADDIM = 64
MB_HEADS = MB_INNER // MB_HEADDIM
MB_GROUPS = 8
MB_HPG = MB_HEADS // MB_GROUPS
MB_STATE = 128
MB_CONV = 5
MB_CHUNK = 64
MB_CONV_DIM = MB_INNER + 2 * MB_GROUPS * MB_STATE
MB_IN = MB_INNER + MB_CONV_DIM + 2 * MB_HEADS

GLA_HEADS = 4
GLA_KEY_DIM = D_MODEL // 2
GLA_VAL_DIM = D_MODEL
GLA_HK = GLA_KEY_DIM // GLA_HEADS
GLA_HV = GLA_VAL_DIM // GLA_HEADS
GLA_GATE_RANK = 16
GLA_GATE_NORM = 16.0
GLA_CHUNK = 32
GLA_IN = 2 * GLA_KEY_DIM + 2 * GLA_VAL_DIM + 2 * GLA_GATE_RANK

N_EXPERTS = 16
EC_CAPACITY_FACTOR = 2
EXPERT_FF = 2048


def rmsnorm(x, g):
    xf = x.astype(F32)
    return xf * lax.rsqrt(jnp.mean(xf * xf, axis=-1, keepdims=True) + NORM_EPS) * g


def adaln(cond, w, b):
    m = jax.nn.silu(cond.astype(F32)) @ w + b
    return jnp.split(m[..., None, :], 6, axis=-1)


def modulate(h, shift, scale):
    return h * (1.0 + scale) + shift


def raster_to_colmajor(h, rows):
    b, l, d = h.shape
    return h.reshape(b, rows, GRID_W, d).transpose(0, 2, 1, 3).reshape(b, l, d)


def colmajor_to_raster(h, rows):
    b, l, d = h.shape
    return h.reshape(b, GRID_W, rows, d).transpose(0, 2, 1, 3).reshape(b, l, d)


def hgrn_lower_bound(lb_param, layer):
    return jnp.cumsum(jax.nn.softmax(lb_param.astype(F32), axis=0), axis=0)[layer]


def hyena_pos_features(length):
    t = jnp.linspace(0.0, 1.0, length, dtype=F32)[:, None]
    bands = (HY_EMB - 1) // 2
    f = jnp.linspace(1e-4, bands - 1, bands, dtype=F32)[None, :]
    w = 2.0 * math.pi * jnp.arange(length, dtype=F32)[:, None] / length
    return jnp.concatenate([t, jnp.cos(f * w), -jnp.sin(f * w)], axis=-1), t


def hyena_window(t):
    max_decay = math.log(HY_DECAY_TARGET) / HY_DECAY_HI_PCT
    min_decay = math.log(HY_DECAY_TARGET) / HY_DECAY_LO_PCT
    deltas = jnp.abs(jnp.linspace(min_decay, max_decay, D_MODEL, dtype=F32))
    return jnp.exp(-t * deltas[None, :])


BF16 = jnp.bfloat16
LANES = 128
SUBLANES = 8
VMEM_LIMIT_BYTES = 56 * 2 ** 20
SCAN_CHUNK = 128
SCAN_FAST_RANGE = 80.0
SCAN_UNROLL = 8


def _cumsum_rows(tri_bf16, g):
    hi = g.astype(BF16)
    r1 = g - hi.astype(F32)
    mid = r1.astype(BF16)
    lo = (r1 - mid.astype(F32)).astype(BF16)
    d = lambda a: jnp.dot(tri_bf16, a, preferred_element_type=F32)
    return d(hi) + d(mid) + d(lo)


def _log_sigmoid(z):
    return jnp.minimum(z, 0.0) - jnp.log1p(jnp.exp(-jnp.abs(z)))


def _dot_nt(a, b):
    return lax.dot_general(a, b, (((1,), (1,)), ((), ())), preferred_element_type=F32)


def _dot_tn(a, b):
    return lax.dot_general(a, b, (((0,), (0,)), ((), ())), preferred_element_type=F32)


def _gla_scan_body(mode, seq, ctx_len, kdim, vdim, *refs):
    chunk = SCAN_CHUNK
    if mode == 'hgrn2':
        (q_x, ff_x, fb_x, v_x, og_x, q_c, ff_c, fb_c, v_c, og_c, lb_ref, gain_ref,
         o_x, o_c, cf_s, cb_s, sf_s, sb_s, kt_s, gt_s, qf_s, qb_s, of_s, ob_s, kf_s, kb_s) = refs
    else:
        (q_x, k_x, v_x, og_x, r_x, q_c, k_c, v_c, og_c, r_c, w2f_ref, w2b_ref, bf_ref, bb_ref, gain_ref,
         o_x, o_c, cf_s, cb_s, sf_s, sb_s, kt_s, gt_s, qf_s, qb_s, of_s, ob_s) = refs
    if mode == 'hgrn2':
        segs = (((ff_c, fb_c), q_c, None, v_c, og_c, o_c, ctx_len, 0),
                ((ff_x, fb_x), q_x, None, v_x, og_x, o_x, seq, ctx_len))
    else:
        segs = (((r_c,), q_c, k_c, v_c, og_c, o_c, ctx_len, 0),
                ((r_x,), q_x, k_x, v_x, og_x, o_x, seq, ctx_len))

    row = lax.broadcasted_iota(jnp.int32, (chunk, chunk), 0)
    col = lax.broadcasted_iota(jnp.int32, (chunk, chunk), 1)
    masks = (row >= col, row <= col)
    tris = tuple(m.astype(BF16) for m in masks)
    cum_s, state_s, qin_s, kout_s = (cf_s, cb_s), (sf_s, sb_s), (qf_s, qb_s), (of_s, ob_s)
    end_row = (chunk - 1, 0)
    mid_row = chunk // 2

    def rows_of(i, off=0):
        return pl.ds(pl.multiple_of(off + i * chunk, chunk), chunk)

    def gate_pass(seg, bound):
        gsrc, _, _, _, _, _, length, off = seg

        def body(i, bound):
            rows, srows = rows_of(i), rows_of(i, off)
            if mode == 'hgrn2':
                lb = lb_ref[...]
                f_f = lb + (1.0 - lb) * jax.nn.sigmoid(gsrc[0][rows, :])
                f_b = lb + (1.0 - lb) * jax.nn.sigmoid(gsrc[1][rows, :])
                g_f, g_b = jnp.log(f_f), jnp.log(f_b)
                kf_s[srows, :] = 1.0 - f_f
                kb_s[srows, :] = 1.0 - f_b
            else:
                r = gsrc[0][rows, :].astype(BF16)
                z_f = jnp.dot(r, w2f_ref[...].astype(BF16), preferred_element_type=F32) + bf_ref[...]
                z_b = jnp.dot(r, w2b_ref[...].astype(BF16), preferred_element_type=F32) + bb_ref[...]
                g_f = _log_sigmoid(z_f) / GLA_GATE_NORM
                g_b = _log_sigmoid(z_b) / GLA_GATE_NORM
            c_f = _cumsum_rows(tris[0], g_f)
            c_b = _cumsum_rows(tris[1], g_b)
            cf_s[srows, :] = c_f
            cb_s[srows, :] = c_b
            for c in (c_f, c_b):
                mid = c[mid_row:mid_row + 1, :]
                spread = jnp.maximum(jnp.abs(c[0:1, :] - mid), jnp.abs(c[chunk - 1:chunk, :] - mid))
                bound = jnp.maximum(bound, spread)
            return bound

        return lax.fori_loop(0, length // chunk, body, bound, unroll=SCAN_UNROLL)

    def intra(seg, direction, i, exact):
        _, q_r, k_r, v_r, _, _, _, off = seg
        rows, srows = rows_of(i), rows_of(i, off)
        if mode == 'hgrn2':
            qq = jax.nn.silu(q_r[rows, :]) * HG_KEY ** -0.5
            k = (kf_s if direction == 0 else kb_s)[srows, :]
        else:
            qq = q_r[rows, :] * GLA_HK ** -0.5
            k = k_r[rows, :]
        cum = cum_s[direction][srows, :]
        e = end_row[direction]
        cum_end = cum[e:e + 1, :]
        q_in = (qq * jnp.exp(cum)).astype(BF16)
        qin_s[direction][srows, :] = q_in
        kout_s[direction][srows, :] = (k * jnp.exp(cum_end - cum)).astype(BF16)
        if exact:
            kt_s[...] = k
            gt_s[...] = cum

            def col_body(s, att):
                ks = kt_s[pl.ds(s, 1), :]
                gs = gt_s[pl.ds(s, 1), :]
                p = qq * ks * jnp.exp(jnp.minimum(cum - gs, 0.0))
                return att + jnp.where(col == s, jnp.sum(p, axis=-1, keepdims=True), 0.0)

            att = lax.fori_loop(0, chunk, col_body, jnp.zeros((chunk, chunk), F32))
        else:
            rel = cum - cum[mid_row:mid_row + 1, :]
            att = _dot_nt((qq * jnp.exp(rel)).astype(BF16), (k * jnp.exp(-rel)).astype(BF16))
        att = jnp.where(masks[direction], att, 0.0).astype(BF16)
        return jnp.dot(att, v_r[rows, :].astype(BF16), preferred_element_type=F32)

    def intra_pass(exact):
        for seg in segs:
            o_r, length = seg[5], seg[6]

            def body(i, carry):
                o_r[rows_of(i), :] = intra(seg, 0, i, exact) + intra(seg, 1, i, exact)
                return carry

            lax.fori_loop(0, length // chunk, body, 0, unroll=1 if exact else SCAN_UNROLL)

    def inter(seg, direction, i):
        v_r, off = seg[3], seg[7]
        rows, srows = rows_of(i), rows_of(i, off)
        s_ref = state_s[direction]
        state = s_ref[...]
        o = _dot_nt(qin_s[direction][srows, :], state.astype(BF16))
        e = pl.ds(pl.multiple_of(off + i * chunk, chunk) + end_row[direction], 1)
        decay = jnp.exp(cum_s[direction][e, :])
        s_ref[...] = state * decay + _dot_tn(v_r[rows, :].astype(BF16), kout_s[direction][srows, :])
        return o

    def finish(seg, i, o):
        o = o * lax.rsqrt(jnp.mean(o * o, axis=-1, keepdims=True) + NORM_EPS) * gain_ref[...]
        return o * jax.nn.silu(seg[4][rows_of(i), :])

    def inter_pass():
        sf_s[...] = jnp.zeros_like(sf_s)
        sb_s[...] = jnp.zeros_like(sb_s)
        for seg in segs:
            o_r, length = seg[5], seg[6]
            n = length // chunk

            def first_half(i, carry):
                j = n - 1 - i
                o_r[rows_of(i), :] += inter(seg, 0, i)
                o_r[rows_of(j), :] += inter(seg, 1, j)
                return carry

            def second_half(i, carry):
                j = n - 1 - i
                o_r[rows_of(i), :] = finish(seg, i, o_r[rows_of(i), :] + inter(seg, 0, i))
                o_r[rows_of(j), :] = finish(seg, j, o_r[rows_of(j), :] + inter(seg, 1, j))
                return carry

            lax.fori_loop(0, n // 2, first_half, 0, unroll=min(SCAN_UNROLL, n // 2))
            lax.fori_loop(n // 2, n, second_half, 0, unroll=min(SCAN_UNROLL, n // 2))

    bound = jnp.zeros((1, kdim), F32)
    for seg in segs:
        bound = gate_pass(seg, bound)
    fast = jnp.max(bound) <= SCAN_FAST_RANGE

    @pl.when(fast)
    def _():
        intra_pass(False)

    @pl.when(jnp.logical_not(fast))
    def _():
        intra_pass(True)

    inter_pass()


def _gla_scan_call(mode, nheads, kdim, vdim, lat_in, ctx_in, small_in, bsz, seq, ctx_len):
    def stream_spec(length, width, first, stride):
        return pl.BlockSpec((None, length, width), lambda b, h: (b, 0, first + stride * h))

    assert seq % (2 * SCAN_CHUNK) == 0 and ctx_len % (2 * SCAN_CHUNK) == 0
    in_specs = [stream_spec(seq, w, f, s) for _, w, f, s in lat_in]
    in_specs += [stream_spec(ctx_len, w, f, s) for _, w, f, s in ctx_in]
    in_specs += [pl.BlockSpec(blk, imap) for _, blk, imap in small_in]
    args = [a for a, _, _, _ in lat_in] + [a for a, _, _, _ in ctx_in] + [a for a, _, _ in small_in]
    tot = seq + ctx_len
    scratch = [pltpu.VMEM((tot, kdim), F32), pltpu.VMEM((tot, kdim), F32),
               pltpu.VMEM((vdim, kdim), F32), pltpu.VMEM((vdim, kdim), F32),
               pltpu.VMEM((SCAN_CHUNK, kdim), F32), pltpu.VMEM((SCAN_CHUNK, kdim), F32),
               pltpu.VMEM((tot, kdim), BF16), pltpu.VMEM((tot, kdim), BF16),
               pltpu.VMEM((tot, kdim), BF16), pltpu.VMEM((tot, kdim), BF16)]
    if mode == 'hgrn2':
        scratch += [pltpu.VMEM((tot, kdim), F32), pltpu.VMEM((tot, kdim), F32)]
    return pl.pallas_call(
        lambda *refs: _gla_scan_body(mode, seq, ctx_len, kdim, vdim, *refs),
        grid=(bsz, nheads),
        in_specs=in_specs,
        out_specs=[pl.BlockSpec((None, seq, vdim), lambda b, h: (b, 0, h)),
                   pl.BlockSpec((None, ctx_len, vdim), lambda b, h: (b, 0, h))],
        out_shape=[jax.ShapeDtypeStruct((bsz, seq, nheads * vdim), F32),
                   jax.ShapeDtypeStruct((bsz, ctx_len, nheads * vdim), F32)],
        scratch_shapes=scratch,
        compiler_params=pltpu.CompilerParams(dimension_semantics=("arbitrary", "arbitrary"),
                                             vmem_limit_bytes=VMEM_LIMIT_BYTES),
        name=f"{mode}_scan",
    )(*args)


def hgrn2_mixer_pallas(hx, hc, w_in, lb, onorm_g, w_out):
    bsz, seq, _ = hx.shape
    ctx_len = hc.shape[1]
    yx, yc = hx @ w_in, hc @ w_in
    nh = HG_HEADS
    streams = lambda y: [(y, HG_KEY, j * nh, 1) for j in range(5)]
    small = [(lb.reshape(1, D_MODEL), (1, HG_KEY), lambda b, h: (0, h)),
             (onorm_g.reshape(1, HG_VAL), (1, HG_VAL), lambda b, h: (0, 0))]
    ox, oc = _gla_scan_call('hgrn2', nh, HG_KEY, HG_VAL, streams(yx), streams(yc), small, bsz, seq, ctx_len)
    return ox @ w_out, oc @ w_out


def gla_mixer_pallas(hx, hc, w_in, gk_w2, gk_b, onorm_g, w_out):
    bsz, seq, _ = hx.shape
    ctx_len = hc.shape[1]
    rows = seq // GRID_W
    kd, vd, rk = GLA_KEY_DIM, GLA_VAL_DIM, GLA_GATE_RANK
    nh = GLA_HEADS
    w_in_p = jnp.pad(w_in, ((0, 0), (0, LANES - 2 * rk)))
    yx = raster_to_colmajor(hx, rows) @ w_in_p
    yc = hc @ w_in_p
    w2f = jnp.pad(gk_w2[0], ((0, LANES - rk), (0, 0)))
    w2b = jnp.pad(gk_w2[1], ((rk, LANES - 2 * rk), (0, 0)))
    streams = lambda y: [(y, GLA_HK, 0, 1), (y, GLA_HK, kd // GLA_HK, 1), (y, GLA_HV, 2 * kd // GLA_HV, 1),
                         (y, GLA_HV, (2 * kd + vd) // GLA_HV, 1), (y, LANES, (2 * kd + 2 * vd) // LANES, 0)]
    small = [(w2f, (LANES, GLA_HK), lambda b, h: (0, h)), (w2b, (LANES, GLA_HK), lambda b, h: (0, h)),
             (gk_b[0].reshape(1, kd), (1, GLA_HK), lambda b, h: (0, h)),
             (gk_b[1].reshape(1, kd), (1, GLA_HK), lambda b, h: (0, h)),
             (onorm_g.reshape(1, GLA_HV), (1, GLA_HV), lambda b, h: (0, 0))]
    ox, oc = _gla_scan_call('gla', nh, GLA_HK, GLA_HV, streams(yx), streams(yc), small, bsz, seq, ctx_len)
    return colmajor_to_raster(ox @ w_out, rows), oc @ w_out


MB_GROUP_W = MB_INNER // MB_GROUPS
SSD_CHUNK = 128
SSD_CONV_ROWS = 64


def _ssd_scan_body(seq, ctx_len, *refs):
    chunk = SSD_CHUNK
    gw, hpg = MB_GROUP_W, MB_HPG
    (x_x, b_x, c_x, z_x, cl_x, rt_x, x_c, b_c, c_c, z_c, cl_c, rt_c,
     wx_ref, wb_ref, wc_ref, bx_ref, bb_ref, bc_ref, dskip_ref, gain_ref,
     o_x, o_c, xf_s, xb_s, sf_s, sb_s, pad_s, xc_s, bc_s, cc_s) = refs
    segs = ((x_c, b_c, c_c, z_c, cl_c, rt_c, o_c, ctx_len, 0),
            (x_x, b_x, c_x, z_x, cl_x, rt_x, o_x, seq, ctx_len))
    xs_s, state_s = (xf_s, xb_s), (sf_s, sb_s)
    end_row = (chunk - 1, 0)

    def conv_silu(raw_ref, w_ref, bias_ref, dst_ref, length, off):
        width = raw_ref.shape[-1]
        halo = SUBLANES
        pad = MB_CONV // 2
        pad_s[0:halo, 0:width] = jnp.zeros((halo, width), F32)
        pad_s[pl.ds(halo + length, halo), 0:width] = jnp.zeros((halo, width), F32)

        crow = SSD_CONV_ROWS

        def stage(i, carry):
            src = pl.ds(pl.multiple_of(i * crow, crow), crow)
            pad_s[pl.ds(pl.multiple_of(halo + i * crow, SUBLANES), crow), 0:width] = raw_ref[src, :]
            return carry

        lax.fori_loop(0, length // crow, stage, 0)

        def body(i, carry):
            rows_in = crow + 2 * halo
            win = pad_s[pl.ds(pl.multiple_of(i * crow, crow), rows_in), 0:width]
            acc = jnp.broadcast_to(bias_ref[...], (crow, width))
            for j in range(MB_CONV):
                lo = halo - pad + j
                acc = acc + pltpu.roll(win, rows_in - lo, axis=0)[0:crow, :] * w_ref[j:j + 1, :]
            dst = pl.ds(pl.multiple_of(off + i * crow, crow), crow)
            dst_ref[dst, :] = jax.nn.silu(acc).astype(dst_ref.dtype)
            return carry

        lax.fori_loop(0, length // crow, body, 0)

    row = lax.broadcasted_iota(jnp.int32, (chunk, chunk), 0)
    col = lax.broadcasted_iota(jnp.int32, (chunk, chunk), 1)
    masks = (row >= col, row <= col)
    head_of_lane = lax.broadcasted_iota(jnp.int32, (1, gw), 1) // MB_HEADDIM

    def rows_of(i, off=0):
        return pl.ds(pl.multiple_of(off + i * chunk, chunk), chunk)

    def bcast_heads(c4):
        out = c4[:, hpg - 1:hpg]
        for hh in range(hpg - 2, -1, -1):
            out = jnp.where(head_of_lane == hh, c4[:, hh:hh + 1], out)
        return out

    def intra_pass():
        for seg in segs:
            _, _, _, _, cl_r, rt_r, o_r, length, off = seg

            def body(i, carry):
                rows, srows = rows_of(i), rows_of(i, off)
                cols, rt = cl_r[i], rt_r[i]
                x = xc_s[srows, :]
                cb = _dot_nt(cc_s[srows, :], bc_s[srows, :])
                xh = [jnp.where(head_of_lane == hh, x, 0.0).astype(BF16) for hh in range(hpg)]
                y = jnp.zeros((chunk, gw), F32)
                for d in range(2):
                    cum = cols[:, d * hpg:(d + 1) * hpg]
                    dt = cols[:, (2 + d) * hpg:(3 + d) * hpg]
                    e = end_row[d]
                    scale = jnp.exp(cum[e:e + 1, :] - cum) * dt
                    xs_s[d][srows, :] = (x * bcast_heads(scale)).astype(BF16)
                    for hh in range(hpg):
                        j = d * hpg + hh
                        seg_decay = jnp.exp(jnp.minimum(cols[:, j:j + 1] - rt[j:j + 1, :], 0.0))
                        w = jnp.where(masks[d], cb * seg_decay * rt[2 * hpg + j:2 * hpg + j + 1, :], 0.0)
                        y = y + jnp.dot(w.astype(BF16), xh[hh], preferred_element_type=F32)
                o_r[rows, :] = y
                return carry

            lax.fori_loop(0, length // chunk, body, 0, unroll=2)

    def inter(seg, d, i):
        cl_r, off = seg[4], seg[8]
        srows = rows_of(i, off)
        cum = cl_r[i][:, d * hpg:(d + 1) * hpg]
        state = state_s[d][...]
        y = jnp.dot(cc_s[srows, :], state.astype(BF16), preferred_element_type=F32)
        e = end_row[d]
        decay = bcast_heads(jnp.exp(cum[e:e + 1, :]))
        state_s[d][...] = state * decay + _dot_tn(bc_s[srows, :], xs_s[d][srows, :])
        return y * bcast_heads(jnp.exp(cum))

    def finish(seg, i, y):
        z_r, off = seg[3], seg[8]
        y = (y + dskip_ref[...] * xc_s[rows_of(i, off), :]) * jax.nn.silu(z_r[rows_of(i), :])
        return y * lax.rsqrt(jnp.mean(y * y, axis=-1, keepdims=True) + NORM_EPS) * gain_ref[...]

    def inter_pass():
        sf_s[...] = jnp.zeros_like(sf_s)
        sb_s[...] = jnp.zeros_like(sb_s)
        for seg in segs:
            o_r, length = seg[6], seg[7]
            n = length // chunk

            def first_half(i, carry):
                j = n - 1 - i
                o_r[rows_of(i), :] += inter(seg, 0, i)
                o_r[rows_of(j), :] += inter(seg, 1, j)
                return carry

            def second_half(i, carry):
                j = n - 1 - i
                o_r[rows_of(i), :] = finish(seg, i, o_r[rows_of(i), :] + inter(seg, 0, i))
                o_r[rows_of(j), :] = finish(seg, j, o_r[rows_of(j), :] + inter(seg, 1, j))
                return carry

            lax.fori_loop(0, n // 2, first_half, 0, unroll=2)
            lax.fori_loop(n // 2, n, second_half, 0, unroll=2)

    for seg in segs:
        conv_silu(seg[0], wx_ref, bx_ref, xc_s, seg[7], seg[8])
        conv_silu(seg[1], wb_ref, bb_ref, bc_s, seg[7], seg[8])
        conv_silu(seg[2], wc_ref, bc_ref, cc_s, seg[7], seg[8])
    intra_pass()
    inter_pass()


def _ssd_head_tables(dt_raw, dt_bias, a_log):
    bsz, length, _ = dt_raw.shape
    n = length // SSD_CHUNK
    a = -jnp.exp(a_log.astype(F32)).reshape(2, MB_GROUPS, MB_HPG)
    dt = jax.nn.softplus(dt_raw.astype(F32).reshape(bsz, length, 2, MB_GROUPS, MB_HPG)
                         + dt_bias.reshape(2, MB_GROUPS, MB_HPG))
    dta = (dt * a).reshape(bsz, n, SSD_CHUNK, 2, MB_GROUPS, MB_HPG)
    dtc = dt.reshape(bsz, n, SSD_CHUNK, 2, MB_GROUPS, MB_HPG)
    lower = jnp.tril(jnp.ones((SSD_CHUNK, SSD_CHUNK), F32))
    cum_f = jnp.einsum('ts,bnsgh->bntgh', lower, dta[:, :, :, 0], precision=lax.Precision.HIGHEST)
    cum_b = jnp.einsum('st,bnsgh->bntgh', lower, dta[:, :, :, 1], precision=lax.Precision.HIGHEST)
    cols = jnp.concatenate([cum_f, cum_b, dtc[:, :, :, 0], dtc[:, :, :, 1]], axis=-1)
    cols = cols.transpose(0, 3, 1, 2, 4)
    return cols, jnp.swapaxes(cols, -1, -2)


def mamba2_mixer_pallas(hx, hc, w_in, conv_w, conv_b, dt_bias, a_log, d_skip, norm_g, w_out):
    bsz, seq, _ = hx.shape
    ctx_len = hc.shape[1]
    chunk, gw = SSD_CHUNK, MB_GROUP_W
    assert seq % (2 * chunk) == 0 and ctx_len % (2 * chunk) == 0

    def project(h):
        z = h @ w_in[:, :MB_INNER]
        xbc = h @ w_in[:, MB_INNER:MB_INNER + MB_CONV_DIM]
        cols, rows_t = _ssd_head_tables(h @ w_in[:, MB_INNER + MB_CONV_DIM:], dt_bias, a_log)
        return z, xbc, cols, rows_t

    yx, xbcx, clx, rtx = project(hx)
    yc, xbcc, clc, rtc = project(hc)
    conv_b2 = conv_b.reshape(1, MB_CONV_DIM)

    def conv_specs(rows):
        bw = MB_INNER // MB_STATE
        return [pl.BlockSpec((rows, gw), lambda b, g: (0, g)),
                pl.BlockSpec((rows, MB_STATE), lambda b, g: (0, bw + g)),
                pl.BlockSpec((rows, MB_STATE), lambda b, g: (0, bw + MB_GROUPS + g))]

    def stream_specs(length):
        n = length // chunk
        bw = MB_INNER // MB_STATE
        return [pl.BlockSpec((None, length, gw), lambda b, g: (b, 0, g)),
                pl.BlockSpec((None, length, MB_STATE), lambda b, g: (b, 0, bw + g)),
                pl.BlockSpec((None, length, MB_STATE), lambda b, g: (b, 0, bw + MB_GROUPS + g)),
                pl.BlockSpec((None, length, gw), lambda b, g: (b, 0, g)),
                pl.BlockSpec((None, None, n, chunk, 4 * MB_HPG), lambda b, g: (b, g, 0, 0, 0)),
                pl.BlockSpec((None, None, n, 4 * MB_HPG, chunk), lambda b, g: (b, g, 0, 0, 0))]

    tot = seq + ctx_len
    ox, oc = pl.pallas_call(
        lambda *refs: _ssd_scan_body(seq, ctx_len, *refs),
        grid=(bsz, MB_GROUPS),
        in_specs=stream_specs(seq) + stream_specs(ctx_len) + conv_specs(MB_CONV) + conv_specs(1) + [
            pl.BlockSpec((1, gw), lambda b, g: (0, g)), pl.BlockSpec((1, gw), lambda b, g: (0, g))],
        out_specs=[pl.BlockSpec((None, seq, gw), lambda b, g: (b, 0, g)),
                   pl.BlockSpec((None, ctx_len, gw), lambda b, g: (b, 0, g))],
        out_shape=[jax.ShapeDtypeStruct((bsz, seq, MB_INNER), F32),
                   jax.ShapeDtypeStruct((bsz, ctx_len, MB_INNER), F32)],
        scratch_shapes=[pltpu.VMEM((tot, gw), BF16), pltpu.VMEM((tot, gw), BF16),
                        pltpu.VMEM((MB_STATE, gw), F32), pltpu.VMEM((MB_STATE, gw), F32),
                        pltpu.VMEM((seq + 2 * SUBLANES, gw), F32), pltpu.VMEM((tot, gw), F32),
                        pltpu.VMEM((tot, MB_STATE), BF16), pltpu.VMEM((tot, MB_STATE), BF16)],
        compiler_params=pltpu.CompilerParams(dimension_semantics=("arbitrary", "arbitrary"),
                                             vmem_limit_bytes=VMEM_LIMIT_BYTES),
        name="ssd_scan",
    )(xbcx, xbcx, xbcx, yx, clx, rtx, xbcc, xbcc, xbcc, yc, clc, rtc,
      conv_w, conv_w, conv_w, conv_b2, conv_b2, conv_b2,
      jnp.repeat(d_skip, MB_HEADDIM).reshape(1, MB_INNER), norm_g.reshape(1, MB_INNER))
    return ox @ w_out, oc @ w_out


HY_BLOCK = 256
HY_FREQ_ROWS = 64


def _split_bf16(a):
    hi = a.astype(BF16)
    return hi, (a - hi.astype(F32)).astype(BF16)


def _dot_split(w_hi, w_lo, x):
    x_hi, x_lo = _split_bf16(x)
    d = lambda a, b: jnp.dot(a, b, preferred_element_type=F32)
    return d(w_hi, x_hi) + d(w_hi, x_lo) + d(w_lo, x_hi)


def _hyena_dft_matrices():
    p = HY_BLOCK
    f = jnp.arange(p, dtype=F32)[:, None] + 0.5
    b = jnp.arange(p, dtype=F32)[None, :]
    k = jnp.round(f * b * 2.0).astype(jnp.int32) % (4 * p)
    ang = k.astype(F32) * (2.0 * math.pi / (4 * p))
    fwd = jnp.concatenate([jnp.cos(ang), -jnp.sin(ang)], axis=0)
    inv = jnp.concatenate([jnp.cos(ang).T, -jnp.sin(ang).T], axis=1) / p
    return fwd, inv


def _hyena_filter_spectra(hf, hb, fwd):
    length, d = hf.shape
    p = HY_BLOCK
    nb = length // p
    h2 = jnp.concatenate([jnp.zeros((1, d), F32), jnp.flip(hb[1:], axis=0), hf], axis=0)
    blocks = h2.reshape(2 * nb, p, d)
    spec = jnp.einsum('fj,kjd->kfd', fwd, blocks, precision=lax.Precision.HIGHEST)
    s_re, s_im = spec[:, :p], spec[:, p:]
    sign = jnp.where(jnp.arange(p) % 2 == 0, 1.0, -1.0).astype(F32)[None, :, None]
    return s_re[1:] - sign * s_im[:-1], s_im[1:] + sign * s_re[:-1]


def _hyena_conv_body(nb, conv_u, u_ref, x_ref, bias_ref, gre_ref, gim_ref, wfh_ref, wfl_ref, wih_ref, wil_ref,
                     cwu_ref, cbu_ref, cwx_ref, cbx_ref, o_ref, ure_s, uim_s, y_s, pad_s, uc_s, xc_s):
    p, fr = HY_BLOCK, HY_FREQ_ROWS
    length = nb * p

    def rows_of(i):
        return pl.ds(pl.multiple_of(i * p, p), p)

    def short_conv(raw_ref, w_ref, b_ref, dst_ref):
        halo, pad, crow = SUBLANES, HY_SHORT // 2, SSD_CONV_ROWS
        pad_s[0:halo, :] = jnp.zeros((halo, pad_s.shape[-1]), F32)
        pad_s[pl.ds(halo + length, halo), :] = jnp.zeros((halo, pad_s.shape[-1]), F32)

        def stage(i, carry):
            src = pl.ds(pl.multiple_of(i * crow, crow), crow)
            pad_s[pl.ds(pl.multiple_of(halo + i * crow, SUBLANES), crow), :] = raw_ref[src, :]
            return carry

        lax.fori_loop(0, length // crow, stage, 0)

        def body(i, carry):
            rows_in = crow + 2 * halo
            win = pad_s[pl.ds(pl.multiple_of(i * crow, crow), rows_in), :]
            acc = jnp.broadcast_to(b_ref[...], (crow, win.shape[-1]))
            for j in range(HY_SHORT):
                lo = halo - pad + j
                acc = acc + pltpu.roll(win, rows_in - lo, axis=0)[0:crow, :] * w_ref[j:j + 1, :]
            dst_ref[pl.ds(pl.multiple_of(i * crow, crow), crow), :] = acc
            return carry

        lax.fori_loop(0, length // crow, body, 0, unroll=2)

    if conv_u:
        short_conv(u_ref, cwu_ref, cbu_ref, uc_s)
    u_src = uc_s if conv_u else u_ref
    short_conv(x_ref, cwx_ref, cbx_ref, xc_s)

    def forward(j, carry):
        spec = _dot_split(wfh_ref[...], wfl_ref[...], u_src[rows_of(j), :])
        ure_s[j] = spec[:p]
        uim_s[j] = spec[p:]
        return carry

    lax.fori_loop(0, nb, forward, 0, unroll=min(nb, 2))

    def out_block(i, carry):
        for t in range(p // fr):
            r = pl.ds(t * fr, fr)

            def acc_body(j, acc):
                a_re, a_im = acc
                k = i - j + nb - 1
                g_re, g_im = gre_ref[k, r, :], gim_ref[k, r, :]
                u_re, u_im = ure_s[j, r, :], uim_s[j, r, :]
                return a_re + g_re * u_re - g_im * u_im, a_im + g_re * u_im + g_im * u_re

            zero = jnp.zeros((fr, u_ref.shape[-1]), F32)
            a_re, a_im = lax.fori_loop(0, nb, acc_body, (zero, zero), unroll=min(nb, 4))
            y_s[i, pl.ds(t * fr, fr), :] = a_re
            y_s[i, pl.ds(p + t * fr, fr), :] = a_im
        return carry

    lax.fori_loop(0, nb, out_block, 0)

    def inverse(i, carry):
        y = _dot_split(wih_ref[...], wil_ref[...], y_s[i])
        u = u_src[rows_of(i), :]
        o_ref[rows_of(i), :] = xc_s[rows_of(i), :] * (y + u * bias_ref[...])
        return carry

    lax.fori_loop(0, nb, inverse, 0, unroll=min(nb, 2))


def _hyena_conv(u_arr, u_blk, x_arr, x_blk, bias, g_re, g_im, mats, short_w, short_b, conv_u):
    bsz, length, _ = u_arr.shape
    d = bias.shape[-1]
    p = HY_BLOCK
    nb = length // p
    nseg = 2 * nb - 1
    const = lambda shape: pl.BlockSpec(shape, lambda c, b: (0, 0))
    cu_blk = u_blk if conv_u else x_blk
    return pl.pallas_call(
        lambda *refs: _hyena_conv_body(nb, conv_u, *refs),
        grid=(d // LANES, bsz),
        in_specs=[pl.BlockSpec((None, length, LANES), lambda c, b: (b, 0, u_blk + c)),
                  pl.BlockSpec((None, length, LANES), lambda c, b: (b, 0, x_blk + c)),
                  pl.BlockSpec((1, LANES), lambda c, b: (0, c)),
                  pl.BlockSpec((nseg, p, LANES), lambda c, b: (0, 0, c)),
                  pl.BlockSpec((nseg, p, LANES), lambda c, b: (0, 0, c)),
                  const((2 * p, p)), const((2 * p, p)), const((p, 2 * p)), const((p, 2 * p)),
                  pl.BlockSpec((HY_SHORT, LANES), lambda c, b: (0, cu_blk + c)),
                  pl.BlockSpec((1, LANES), lambda c, b: (0, cu_blk + c)),
                  pl.BlockSpec((HY_SHORT, LANES), lambda c, b: (0, x_blk + c)),
                  pl.BlockSpec((1, LANES), lambda c, b: (0, x_blk + c))],
        out_specs=pl.BlockSpec((None, length, LANES), lambda c, b: (b, 0, c)),
        out_shape=jax.ShapeDtypeStruct((bsz, length, d), F32),
        scratch_shapes=[pltpu.VMEM((nb, p, LANES), F32), pltpu.VMEM((nb, p, LANES), F32),
                        pltpu.VMEM((nb, 2 * p, LANES), F32),
                        pltpu.VMEM((length + 2 * SUBLANES, LANES), F32),
                        pltpu.VMEM((length, LANES), F32), pltpu.VMEM((length, LANES), F32)],
        compiler_params=pltpu.CompilerParams(dimension_semantics=("arbitrary", "arbitrary"),
                                             vmem_limit_bytes=VMEM_LIMIT_BYTES),
        name="hyena_conv",
    )(u_arr, x_arr, bias.reshape(1, d), g_re, g_im, *mats, short_w, short_b, short_w, short_b)


def hyena_mixer_pallas(hx, hc, w_in, b_in, short_w, short_b, f_w1, f_b1, f_w2, f_b2, f_w3, f_b3, f_w4,
                       f_freq, f_bias, w_out, b_out):
    fwd, inv = _hyena_dft_matrices()
    mats = _split_bf16(fwd) + _split_bf16(inv)
    nblk = D_MODEL // LANES

    def taps(length):
        z, t = hyena_pos_features(length)
        a = jnp.sin(f_freq * (z @ f_w1 + f_b1))
        a = jnp.sin(f_freq * (a @ f_w2 + f_b2))
        a = jnp.sin(f_freq * (a @ f_w3 + f_b3))
        mw = (a @ f_w4).astype(F32) * jnp.tile(hyena_window(t), (1, 2 * HY_ORDER))
        norm = jnp.sum(jnp.abs(mw), axis=0).reshape(HY_ORDER, 2, D_MODEL).sum(axis=1)
        return mw, 1.0 / norm

    def spectra(mw, inv_norm, o):
        hf = mw[:, (2 * o) * D_MODEL:(2 * o + 1) * D_MODEL]
        hb = mw[:, (2 * o + 1) * D_MODEL:(2 * o + 2) * D_MODEL]
        g_re, g_im = _hyena_filter_spectra(hf, hb, fwd)
        return g_re * inv_norm[o], g_im * inv_norm[o]

    def run(h):
        length = h.shape[1]
        raw = (h @ w_in + b_in).astype(F32)
        sb = short_b.reshape(1, -1)
        mw, inv_norm = taps(length)
        g0, g1 = spectra(mw, inv_norm, 0), spectra(mw, inv_norm, 1)
        z = _hyena_conv(raw, 2 * nblk, raw, 0, f_bias[0], g0[0], g0[1], mats, short_w, sb, True)
        z = _hyena_conv(z, 0, raw, nblk, f_bias[1], g1[0], g1[1], mats, short_w, sb, False)
        return z @ w_out + b_out

    return run(hx), run(hc)


MOE_ROW_TILE = 1024
MOE_EXPERT_FOLD = 4
MOE_COL_TILE = 512


def _moe_combine_body(t_len, idx_ref, gate_ref, ye_ref, x_ref, g2_ref, o_ref):
    e = pl.program_id(2)
    cap = idx_ref.shape[-1]
    rt = min(MOE_ROW_TILE, t_len)

    @pl.when(e == 0)
    def _():
        o_ref[...] = jnp.zeros_like(o_ref)

    yg = (ye_ref[...] * gate_ref[...]).astype(BF16)
    idx = idx_ref[...]

    def body(r, carry):
        r0 = pl.multiple_of(r * rt, rt)
        tok = lax.broadcasted_iota(jnp.int32, (rt, cap), 0) + r0
        onehot = jnp.where(tok == idx, 1.0, 0.0).astype(BF16)
        o_ref[pl.ds(r0, rt), :] += jnp.dot(onehot, yg, preferred_element_type=F32)
        return carry

    lax.fori_loop(0, t_len // rt, body, 0)

    @pl.when(e == pl.num_programs(2) - 1)
    def _():
        o_ref[...] = x_ref[...] + g2_ref[...] * o_ref[...]


def expert_choice_ffn_residual(x, g2, h, router_w, w_gate, w_up, w_down):
    bsz, t_len, d = h.shape
    cap = max(1, EC_CAPACITY_FACTOR * t_len // N_EXPERTS)
    aff = jax.nn.softmax(h.astype(F32) @ router_w.astype(F32), axis=-1)
    gate, idx = lax.top_k(jnp.swapaxes(aff, 1, 2), cap)
    xg = jax.vmap(lambda hb, ib: hb[ib])(h.astype(BF16), idx)
    pre_g = jnp.einsum('becd,edf->becf', xg, w_gate, preferred_element_type=BF16)
    pre_u = jnp.einsum('becd,edf->becf', xg, w_up, preferred_element_type=BF16)
    hid = (jax.nn.silu(pre_g.astype(F32)) * pre_u.astype(F32)).astype(BF16)
    ye = jnp.einsum('becf,efd->becd', hid, w_down, preferred_element_type=F32)
    ct = MOE_COL_TILE if t_len > MOE_ROW_TILE else d
    g2 = jnp.broadcast_to(g2.reshape(-1, 1, d), (bsz, 1, d))
    ngrp = N_EXPERTS // MOE_EXPERT_FOLD
    cap = cap * MOE_EXPERT_FOLD
    ye = ye.reshape(bsz, ngrp, cap, d)
    return pl.pallas_call(
        lambda *refs: _moe_combine_body(t_len, *refs),
        grid=(bsz, d // ct, ngrp),
        in_specs=[pl.BlockSpec((None, None, 1, cap), lambda b, c, e: (b, e, 0, 0)),
                  pl.BlockSpec((None, None, cap, 1), lambda b, c, e: (b, e, 0, 0)),
                  pl.BlockSpec((None, None, cap, ct), lambda b, c, e: (b, e, 0, c)),
                  pl.BlockSpec((None, t_len, ct), lambda b, c, e: (b, 0, c)),
                  pl.BlockSpec((None, 1, ct), lambda b, c, e: (b, 0, c))],
        out_specs=pl.BlockSpec((None, t_len, ct), lambda b, c, e: (b, 0, c)),
        out_shape=jax.ShapeDtypeStruct((bsz, t_len, d), F32),
        compiler_params=pltpu.CompilerParams(dimension_semantics=("arbitrary", "arbitrary", "arbitrary"),
                                             vmem_limit_bytes=VMEM_LIMIT_BYTES),
        name="moe_combine",
    )(idx.reshape(bsz, ngrp, 1, cap), gate.reshape(bsz, ngrp, cap, 1), ye, x, g2)


def _final_norm_body(x_ref, g_ref, o_ref):
    xf = x_ref[...]
    o_ref[...] = xf * lax.rsqrt(jnp.mean(xf * xf, axis=-1, keepdims=True) + NORM_EPS) * g_ref[...]


def final_rmsnorm(x, g):
    b, l, d = x.shape
    rows = b * l
    tm = 512
    out = pl.pallas_call(
        _final_norm_body,
        grid=(rows // tm,),
        in_specs=[pl.BlockSpec((tm, d), lambda i: (i, 0)), pl.BlockSpec((1, d), lambda i: (0, 0))],
        out_specs=pl.BlockSpec((tm, d), lambda i: (i, 0)),
        out_shape=jax.ShapeDtypeStruct((rows, d), F32),
    )(x.reshape(rows, d), g.reshape(1, d))
    return out.reshape(b, l, d)


def kernel(x, c, ctx, c_ctx, ada_w, ada_b, norm1_g, norm2_g,
           hg_w_in, hg_lb, hg_onorm_g, hg_w_out,
           hy_w_in, hy_b_in, hy_short_w, hy_short_b, hy_f_w1, hy_f_b1, hy_f_w2, hy_f_b2,
           hy_f_w3, hy_f_b3, hy_f_w4, hy_f_freq, hy_f_bias, hy_w_out, hy_b_out,
           mb_w_in, mb_conv_w, mb_conv_b, mb_dt_bias, mb_a_log, mb_d, mb_norm_g, mb_w_out,
           gla_w_in, gla_gk_w2, gla_gk_b, gla_onorm_g, gla_w_out,
           router_w, moe_w_gate, moe_w_up, moe_w_down, final_norm_g):
    for i in range(DEPTH):
        last = i == DEPTH - 1
        sh1, sc1, g1, sh2, sc2, g2 = adaln(c, ada_w[i], ada_b[i])
        csh1, csc1, cg1, csh2, csc2, cg2 = adaln(c_ctx, ada_w[i], ada_b[i])
        hx = modulate(rmsnorm(x, norm1_g[i]), sh1, sc1)
        hc = modulate(rmsnorm(ctx, norm1_g[i]), csh1, csc1)
        kind, j = i % N_MIXERS, i // N_MIXERS
        if kind == 0:
            yx, yc = hgrn2_mixer_pallas(hx, hc, hg_w_in[j], hgrn_lower_bound(hg_lb, i), hg_onorm_g[j], hg_w_out[j])
        elif kind == 1:
            yx, yc = hyena_mixer_pallas(hx, hc, hy_w_in[j], hy_b_in[j], hy_short_w[j], hy_short_b[j],
                                 hy_f_w1[j], hy_f_b1[j], hy_f_w2[j], hy_f_b2[j], hy_f_w3[j], hy_f_b3[j],
                                 hy_f_w4[j], hy_f_freq[j], hy_f_bias[j], hy_w_out[j], hy_b_out[j])
        elif kind == 2:
            yx, yc = mamba2_mixer_pallas(hx, hc, mb_w_in[j], mb_conv_w[j], mb_conv_b[j], mb_dt_bias[j],
                                  mb_a_log[j], mb_d[j], mb_norm_g[j], mb_w_out[j])
        else:
            yx, yc = gla_mixer_pallas(hx, hc, gla_w_in[j], gla_gk_w2[j], gla_gk_b[j], gla_onorm_g[j], gla_w_out[j])
        x = x + g1 * yx
        x = expert_choice_ffn_residual(x, g2, modulate(rmsnorm(x, norm2_g[i]), sh2, sc2),
                                       router_w[i], moe_w_gate[i], moe_w_up[i], moe_w_down[i])
        if not last:
            ctx = ctx + cg1 * yc
            ctx = expert_choice_ffn_residual(ctx, cg2, modulate(rmsnorm(ctx, norm2_g[i]), csh2, csc2),
                                             router_w[i], moe_w_gate[i], moe_w_up[i], moe_w_down[i])
    return final_rmsnorm(x, final_norm_g)
```
